```python
import numpy as np
import jax
import jax.numpy as jnp
from jax import lax

D_MODEL = 2048
BATCH = 8
SEQ = 2048
DEPTH = 1

HEAD_DIM = 128
N_HEADS = D_MODEL // HEAD_DIM
MOBA_HEADS = N_HEADS // 2
NSA_HEADS = N_HEADS - MOBA_HEADS
NSA_KV_GROUPS = 2
NSA_REP = NSA_HEADS // NSA_KV_GROUPS
MOBA_WIDTH = MOBA_HEADS * HEAD_DIM
NSA_WIDTH = NSA_HEADS * HEAD_DIM
NSA_KV_WIDTH = NSA_KV_GROUPS * HEAD_DIM
NSA_N_GATES = 3 * NSA_HEADS
IN_COLS = 4 * MOBA_WIDTH + 2 * NSA_WIDTH + 6 * NSA_KV_WIDTH + NSA_N_GATES
MOBA_BLOCK = 256
MOBA_TOPK = 3
MOBA_Q_CHUNK = 32
CMP_BLOCK = 32
CMP_STRIDE = 16
CMP_HIDDEN = 2 * HEAD_DIM
SEL_BLOCK = 64
SEL_TOPK = 8
WINDOW = 512
NSA_Q_CHUNK = 64
ROPE_THETA = 10000.0
EPS = 1e-6

kernel_name = 'hybrid_moba_nsa_layer'


def _split_points():
    sizes = [MOBA_WIDTH] * 4 + [NSA_WIDTH] + [NSA_KV_WIDTH] * 6 + [NSA_N_GATES, NSA_WIDTH]
    return [int(v) for v in np.cumsum(sizes)[:-1]]


def rmsnorm(x, g):
    xf = x.astype(jnp.float32)
    y = xf * lax.rsqrt(jnp.mean(xf * xf, axis=-1, keepdims=True) + EPS)
    return (y * g.astype(jnp.float32)).astype(x.dtype)


def rope_tables(positions):
    inv_freq = 1.0 / (ROPE_THETA ** (jnp.arange(0, HEAD_DIM, 2, dtype=jnp.float32) / HEAD_DIM))
    ang = positions.astype(jnp.float32)[..., None] * inv_freq
    return jnp.cos(ang), jnp.sin(ang)


def apply_rope(x, cos, sin):
    c = cos[:, :, None, :]
    s = sin[:, :, None, :]
    x1, x2 = jnp.split(x.astype(jnp.float32), 2, axis=-1)
    return jnp.concatenate([x1 * c - x2 * s, x2 * c + x1 * s], axis=-1).astype(x.dtype)


def masked_softmax(s, mask):
    s = jnp.where(mask, s, -jnp.inf)
    m = jnp.max(s, axis=-1, keepdims=True)
    m = jnp.where(jnp.isfinite(m), m, 0.0)
    e = jnp.where(mask, jnp.exp(s - m), 0.0)
    return e / jnp.maximum(jnp.sum(e, axis=-1, keepdims=True), 1e-30)


def moba_attention(q, k, v):
    B, H, S, dh = q.shape
    n_blk = -(-S // MOBA_BLOCK)
    pad = n_blk * MOBA_BLOCK - S
    kp = jnp.pad(k, ((0, 0), (0, 0), (0, pad), (0, 0)))
    vp = jnp.pad(v, ((0, 0), (0, 0), (0, pad), (0, 0)))
    kb = kp.reshape(B, H, n_blk, MOBA_BLOCK, dh)
    vb = vp.reshape(B, H, n_blk, MOBA_BLOCK, dh)
    k_mean = jnp.mean(kb.astype(jnp.float32), axis=3)
    k_top = min(MOBA_TOPK, n_blk - 1)
    scale = dh ** -0.5
    b_ix = jnp.arange(B)[:, None, None, None]
    h_ix = jnp.arange(H)[None, :, None, None]
    blk_ix = jnp.arange(n_blk)
    C = MOBA_Q_CHUNK

    def chunk(c0):
        qc = lax.dynamic_slice_in_dim(q, c0, C, axis=2)
        q_pos = c0 + jnp.arange(C)
        own = c0 // MOBA_BLOCK
        k_own = lax.dynamic_slice_in_dim(kp, own * MOBA_BLOCK, MOBA_BLOCK, axis=2)
        v_own = lax.dynamic_slice_in_dim(vp, own * MOBA_BLOCK, MOBA_BLOCK, axis=2)
        own_pos = own * MOBA_BLOCK + jnp.arange(MOBA_BLOCK)
        s_own = jnp.einsum('bhqd,bhkd->bhqk', qc, k_own).astype(jnp.float32) * scale
        m_own = jnp.broadcast_to(own_pos[None, :] <= q_pos[:, None], s_own.shape)
        if k_top == 0:
            p = masked_softmax(s_own, m_own).astype(v.dtype)
            return jnp.einsum('bhqk,bhkd->bhqd', p, v_own)
        gate = jnp.einsum('bhqd,bhnd->bhqn', qc.astype(jnp.float32), k_mean)
        fully_past = (blk_ix[None, :] + 1) * MOBA_BLOCK <= q_pos[:, None]
        gate = jnp.where(fully_past, gate, -jnp.inf)
        g_val, g_idx = lax.top_k(gate, k_top)
        k_sel = kb[b_ix, h_ix, g_idx]
        v_sel = vb[b_ix, h_ix, g_idx]
        s_sel = jnp.einsum('bhqd,bhqnld->bhqnl', qc, k_sel).astype(jnp.float32) * scale
        m_sel = jnp.broadcast_to(jnp.isfinite(g_val)[..., None], s_sel.shape)
        n_sel = k_top * MOBA_BLOCK
        s_all = jnp.concatenate([s_sel.reshape(B, H, C, n_sel), s_own], axis=-1)
        m_all = jnp.concatenate([m_sel.reshape(B, H, C, n_sel), m_own], axis=-1)
        p = masked_softmax(s_all, m_all).astype(v.dtype)
        p_sel = p[..., :n_sel].reshape(B, H, C, k_top, MOBA_BLOCK)
        return (jnp.einsum('bhqnl,bhqnld->bhqd', p_sel, v_sel)
                + jnp.einsum('bhqk,bhkd->bhqd', p[..., n_sel:], v_own))

    out = lax.map(chunk, jnp.arange(S // C) * C)
    return out.transpose(1, 2, 0, 3, 4).reshape(B, H, S, dh)


def compress_tokens(kv, pe, w1, w2):
    B, S, G, dh = kv.shape
    n_seg = S // CMP_STRIDE
    r = CMP_BLOCK // CMP_STRIDE
    n_cmp = n_seg - r + 1
    seg = kv.reshape(B, n_seg, CMP_STRIDE, G, dh)
    blocks = jnp.concatenate([seg[:, i:i + n_cmp] for i in range(r)], axis=2)
    blocks = blocks + pe[None, None, :, None, :]
    flat = blocks.transpose(0, 3, 1, 2, 4).reshape(B, G, n_cmp, CMP_BLOCK * dh)
    return jax.nn.silu(flat @ w1) @ w2


def cmp_to_sel_overlap(n_cmp, n_sel_blk):
    cs = np.arange(n_cmp)[:, None] * CMP_STRIDE
    ss = np.arange(n_sel_blk)[None, :] * SEL_BLOCK
    ov = (cs < ss + SEL_BLOCK) & (cs + CMP_BLOCK > ss)
    return jnp.asarray(ov.astype(np.float32))


def nsa_attention(q_rot, q_raw, k_c, v_c, k_s, v_s, k_w, v_w, gates):
    B, G, R, S, dh = q_rot.shape
    scale = dh ** -0.5
    pos = jnp.arange(S)
    n_cmp = k_c.shape[2]
    s_c = jnp.einsum('bgrqd,bgnd->bgrqn', q_raw, k_c).astype(jnp.float32) * scale
    cmp_end = jnp.arange(n_cmp) * CMP_STRIDE + CMP_BLOCK - 1
    p_c = masked_softmax(s_c, cmp_end[None, :] <= pos[:, None])
    o_c = jnp.einsum('bgrqn,bgnd->bgrqd', p_c.astype(v_c.dtype), v_c)
    n_blk = S // SEL_BLOCK
    imp = jnp.einsum('bgrqn,nj->bgqj', p_c, cmp_to_sel_overlap(n_cmp, n_blk))
    blk = jnp.arange(n_blk)[None, :]
    own = (pos // SEL_BLOCK)[:, None]
    forced = (blk == 0) | (blk == own) | (blk == own - 1)
    future = blk * SEL_BLOCK > pos[:, None]
    score = jnp.where(future, -jnp.inf, jnp.where(forced, jnp.inf, imp))
    n_top = min(SEL_TOPK, n_blk)
    s_val, s_idx = lax.top_k(score, n_top)
    s_ok = s_val > -jnp.inf
    ksb = k_s.reshape(B, G, n_blk, SEL_BLOCK, dh)
    vsb = v_s.reshape(B, G, n_blk, SEL_BLOCK, dh)
    kwp = jnp.pad(k_w, ((0, 0), (0, 0), (WINDOW, 0), (0, 0)))
    vwp = jnp.pad(v_w, ((0, 0), (0, 0), (WINDOW, 0), (0, 0)))
    b_ix = jnp.arange(B)[:, None, None, None]
    g_ix = jnp.arange(G)[None, :, None, None]
    C = NSA_Q_CHUNK
    n_keys = n_top * SEL_BLOCK

    def chunk(c0):
        qc = lax.dynamic_slice_in_dim(q_rot, c0, C, axis=3)
        q_pos = c0 + jnp.arange(C)
        idx = lax.dynamic_slice_in_dim(s_idx, c0, C, axis=2)
        ok = lax.dynamic_slice_in_dim(s_ok, c0, C, axis=2)
        k_sel = ksb[b_ix, g_ix, idx]
        v_sel = vsb[b_ix, g_ix, idx]
        s_s = jnp.einsum('bgrqd,bgqnld->bgrqnl', qc, k_sel).astype(jnp.float32) * scale
        key_pos = idx[..., None] * SEL_BLOCK + jnp.arange(SEL_BLOCK)
        m_s = ok[..., None] & (key_pos <= q_pos[None, None, :, None, None])
        p_s = masked_softmax(s_s.reshape(B, G, R, C, n_keys), m_s.reshape(B, G, 1, C, n_keys))
        p_s = p_s.reshape(B, G, R, C, n_top, SEL_BLOCK).astype(v_s.dtype)
        o_s = jnp.einsum('bgrqnl,bgqnld->bgrqd', p_s, v_sel)
        k_win = lax.dynamic_slice_in_dim(kwp, c0, WINDOW + C, axis=2)
        v_win = lax.dynamic_slice_in_dim(vwp, c0, WINDOW + C, axis=2)
        w_pos = c0 - WINDOW + jnp.arange(WINDOW + C)
        m_w = ((w_pos[None, :] <= q_pos[:, None]) & (w_pos[None, :] > q_pos[:, None] - WINDOW)
               & (w_pos[None, :] >= 0))
        s_w = jnp.einsum('bgrqd,bgkd->bgrqk', qc, k_win).astype(jnp.float32) * scale
        p_w = masked_softmax(s_w, m_w).astype(v_w.dtype)
        o_w = jnp.einsum('bgrqk,bgkd->bgrqd', p_w, v_win)
        return o_s, o_w

    o_s, o_w = lax.map(chunk, jnp.arange(S // C) * C)
    o_s = o_s.transpose(1, 2, 3, 0, 4, 5).reshape(B, G, R, S, dh)
    o_w = o_w.transpose(1, 2, 3, 0, 4, 5).reshape(B, G, R, S, dh)
    return gates[..., 0:1] * o_c + gates[..., 1:2] * o_s + gates[..., 2:3] * o_w


def hybrid_layer(x, cos, sin, w_in, g_norm, pe_ck, pe_cv, w_ck1, w_ck2, w_cv1, w_cv2,
                 g_out_moba, g_out_nsa, w_out):
    B, S, _ = x.shape
    G = NSA_KV_GROUPS
    h = rmsnorm(x, g_norm)
    proj = jnp.einsum('bsd,de->bse', h, w_in)
    (m_q, m_k, m_v, m_z, n_q, n_kc, n_vc, n_ks, n_vs, n_kw, n_vw, n_g, n_z) = jnp.split(
        proj, _split_points(), axis=-1)
    mq = apply_rope(m_q.reshape(B, S, MOBA_HEADS, HEAD_DIM), cos, sin).transpose(0, 2, 1, 3)
    mk = apply_rope(m_k.reshape(B, S, MOBA_HEADS, HEAD_DIM), cos, sin).transpose(0, 2, 1, 3)
    mv = m_v.reshape(B, S, MOBA_HEADS, HEAD_DIM).transpose(0, 2, 1, 3)
    o_moba = moba_attention(mq, mk, mv).transpose(0, 2, 1, 3).reshape(B, S, MOBA_WIDTH)
    y_moba = rmsnorm(o_moba * jax.nn.silu(m_z), g_out_moba)
    nq = n_q.reshape(B, S, NSA_HEADS, HEAD_DIM)
    q_rot = apply_rope(nq, cos, sin).reshape(B, S, G, NSA_REP, HEAD_DIM).transpose(0, 2, 3, 1, 4)
    q_raw = nq.reshape(B, S, G, NSA_REP, HEAD_DIM).transpose(0, 2, 3, 1, 4)
    k_c = compress_tokens(n_kc.reshape(B, S, G, HEAD_DIM), pe_ck, w_ck1, w_ck2)
    v_c = compress_tokens(n_vc.reshape(B, S, G, HEAD_DIM), pe_cv, w_cv1, w_cv2)
    k_s = apply_rope(n_ks.reshape(B, S, G, HEAD_DIM), cos, sin).transpose(0, 2, 1, 3)
    v_s = n_vs.reshape(B, S, G, HEAD_DIM).transpose(0, 2, 1, 3)
    k_w = apply_rope(n_kw.reshape(B, S, G, HEAD_DIM), cos, sin).transpose(0, 2, 1, 3)
    v_w = n_vw.reshape(B, S, G, HEAD_DIM).transpose(0, 2, 1, 3)
    gates = jax.nn.sigmoid(n_g.reshape(B, S, G, NSA_REP, 3)).transpose(0, 2, 3, 1, 4)
    o_nsa = nsa_attention(q_rot, q_raw, k_c, v_c, k_s, v_s, k_w, v_w, gates)
    o_nsa = o_nsa.transpose(0, 3, 1, 2, 4).reshape(B, S, NSA_WIDTH)
    y_nsa = rmsnorm(o_nsa * jax.nn.silu(n_z), g_out_nsa)
    y = jnp.concatenate([y_moba, y_nsa], axis=-1)
    return x + jnp.einsum('bse,ed->bsd', y, w_out)


def setup_inputs(seed: int = 0) -> dict:
    key = jax.random.key(seed)
    ks = jax.random.split(key, 13)
    f32 = jnp.float32

    def nrm(k, shape, fan_in):
        return jax.random.normal(k, shape, f32) * (fan_in ** -0.5)

    x = jax.random.normal(ks[0], (BATCH, SEQ, D_MODEL), f32)
    positions = jnp.broadcast_to(jnp.arange(SEQ, dtype=jnp.int32)[None, :], (BATCH, SEQ))
    w_in = nrm(ks[1], (DEPTH, D_MODEL, IN_COLS), D_MODEL)
    g_norm = 1.0 + 0.01 * jax.random.normal(ks[2], (DEPTH, D_MODEL), f32)
    pe_ck = 0.1 * jax.random.normal(ks[3], (DEPTH, CMP_BLOCK, HEAD_DIM), f32)
    pe_cv = 0.1 * jax.random.normal(ks[4], (DEPTH, CMP_BLOCK, HEAD_DIM), f32)
    w_ck1 = nrm(ks[5], (DEPTH, CMP_BLOCK * HEAD_DIM, CMP_HIDDEN), CMP_BLOCK * HEAD_DIM)
    w_ck2 = nrm(ks[6], (DEPTH, CMP_HIDDEN, HEAD_DIM), CMP_HIDDEN)
    w_cv1 = nrm(ks[7], (DEPTH, CMP_BLOCK * HEAD_DIM, CMP_HIDDEN), CMP_BLOCK * HEAD_DIM)
    w_cv2 = nrm(ks[8], (DEPTH, CMP_HIDDEN, HEAD_DIM), CMP_HIDDEN)
    g_out_moba = 1.0 + 0.01 * jax.random.normal(ks[9], (DEPTH, MOBA_WIDTH), f32)
    g_out_nsa = 1.0 + 0.01 * jax.random.normal(ks[10], (DEPTH, NSA_WIDTH), f32)
    w_out = nrm(ks[11], (DEPTH, D_MODEL, D_MODEL), D_MODEL)
    g_final = 1.0 + 0.01 * jax.random.normal(ks[12], (D_MODEL,), f32)
    return {'x': x, 'positions': positions, 'w_in': w_in, 'g_norm': g_norm,
            'pe_ck': pe_ck, 'pe_cv': pe_cv, 'w_ck1': w_ck1, 'w_ck2': w_ck2,
            'w_cv1': w_cv1, 'w_cv2': w_cv2, 'g_out_moba': g_out_moba,
            'g_out_nsa': g_out_nsa, 'w_out': w_out, 'g_final': g_final}


def reference(x, positions, w_in, g_norm, pe_ck, pe_cv, w_ck1, w_ck2, w_cv1, w_cv2,
              g_out_moba, g_out_nsa, w_out, g_final):
    cos, sin = rope_tables(positions)
    for layer in range(DEPTH):
        x = hybrid_layer(x, cos, sin, w_in[layer], g_norm[layer], pe_ck[layer], pe_cv[layer],
                         w_ck1[layer], w_ck2[layer], w_cv1[layer], w_cv2[layer],
                         g_out_moba[layer], g_out_nsa[layer], w_out[layer])
    return rmsnorm(x, g_final)
```

```python
import functools

import numpy as np
import jax
import jax.numpy as jnp
from jax import lax
from jax.experimental import pallas as pl
from jax.experimental.pallas import tpu as pltpu

F32 = jnp.float32
BF16 = jnp.bfloat16

D_MODEL = 2048
HEAD_DIM = 128
MOBA_HEADS = 8
NSA_HEADS = 8
NSA_GROUPS = 2
NSA_REP = 4
MOBA_BLOCK = 256
MOBA_TOPK = 3
CMP_BLOCK = 32
CMP_STRIDE = 16
CMP_HIDDEN = 256
SEL_BLOCK = 64
SEL_TOPK = 8
WINDOW = 512
ROPE_THETA = 10000.0
EPS = 1e-6
SCALE = HEAD_DIM ** -0.5
NEG_BIG = -(2.0 ** 100)

LANES = 128
VMEM_LIMIT = 48 * 1024 * 1024

BLK_MQ, BLK_MK, BLK_MV, BLK_MZ, BLK_NQ, BLK_NZ = 0, 8, 16, 24, 32, 40
BLK_NKS, BLK_NKW, BLK_NKC, BLK_NVC, BLK_NVS, BLK_NVW = 48, 50, 52, 54, 56, 58
N_BLOCKS = 60
ROPE_BLOCKS = tuple(range(0, 16)) + (48, 49, 50, 51)


def _nt_dot(a, b):
    return lax.dot_general(a, b, (((1,), (1,)), ((), ())), preferred_element_type=F32)


def _sigmoid(x):
    return 1.0 / (1.0 + jnp.exp(-x))


def _rope(a, cos, sin_signed):
    return a * cos + pltpu.roll(a, HEAD_DIM // 2, axis=a.ndim - 1) * sin_signed


def _rope_table_kernel(pos_ref, invf_ref, sign_ref, cos_ref, sin_ref):
    ang = pos_ref[...].astype(F32) * invf_ref[...]
    cos_ref[...] = jnp.cos(ang)
    sin_ref[...] = jnp.sin(ang) * sign_ref[...]


def _rope_tables(positions):
    T = positions.size
    tile = min(T, 2048)
    half = HEAD_DIM // 2
    inv_freq = 1.0 / (ROPE_THETA ** (jnp.arange(0, HEAD_DIM, 2, dtype=F32) / HEAD_DIM))
    invf = jnp.concatenate([inv_freq, inv_freq]).reshape(1, HEAD_DIM)
    sign = jnp.concatenate([-jnp.ones((half,), F32), jnp.ones((half,), F32)]).reshape(1, HEAD_DIM)
    return pl.pallas_call(
        _rope_table_kernel,
        grid=(T // tile,),
        in_specs=[pl.BlockSpec((tile, 1), lambda i: (i, 0)),
                  pl.BlockSpec((1, HEAD_DIM), lambda i: (0, 0)),
                  pl.BlockSpec((1, HEAD_DIM), lambda i: (0, 0))],
        out_specs=[pl.BlockSpec((tile, HEAD_DIM), lambda i: (i, 0)),
                   pl.BlockSpec((tile, HEAD_DIM), lambda i: (i, 0))],
        out_shape=[jax.ShapeDtypeStruct((T, HEAD_DIM), F32)] * 2,
        name="rope_tables",
    )(positions.reshape(T, 1), invf, sign)


def _in_proj_kernel(x_ref, g_ref, w_ref, wg_ref, cos_ref, sin_ref, out_ref, gate_ref, h_scr,
                    *, rope_tiles, blocks_per_tile):
    j = pl.program_id(1)

    @pl.when(j == 0)
    def _():
        x = x_ref[...]
        ms = jnp.mean(x * x, axis=-1, keepdims=True)
        h = (x * lax.rsqrt(ms + EPS) * g_ref[...]).astype(BF16)
        h_scr[...] = h
        gates = jnp.dot(h, wg_ref[...], preferred_element_type=F32)
        for g in range(NSA_GROUPS):
            gate_ref[g] = gates[:, g * LANES:(g + 1) * LANES]

    acc = jnp.dot(h_scr[...], w_ref[...], preferred_element_type=F32)
    is_rope = functools.reduce(jnp.logical_or, [j == t for t in rope_tiles])

    @pl.when(is_rope)
    def _():
        cos = cos_ref[...]
        sin = sin_ref[...]
        for c in range(blocks_per_tile):
            out_ref[c] = _rope(acc[:, c * LANES:(c + 1) * LANES], cos, sin).astype(BF16)

    @pl.when(jnp.logical_not(is_rope))
    def _():
        for c in range(blocks_per_tile):
            out_ref[c] = acc[:, c * LANES:(c + 1) * LANES].astype(BF16)


def _in_proj(x2d, g_norm, w_perm, w_gate, cos, sin, *, tm, blocks_per_tile):
    T = x2d.shape[0]
    tn = blocks_per_tile * LANES
    n_tiles = N_BLOCKS // blocks_per_tile
    rope_tiles = tuple(sorted({b // blocks_per_tile for b in ROPE_BLOCKS}))
    assert all((t * blocks_per_tile + c) in ROPE_BLOCKS
               for t in rope_tiles for c in range(blocks_per_tile))
    kern = functools.partial(_in_proj_kernel, rope_tiles=rope_tiles,
                             blocks_per_tile=blocks_per_tile)
    return pl.pallas_call(
        kern,
        grid=(T // tm, n_tiles),
        in_specs=[pl.BlockSpec((tm, D_MODEL), lambda i, j: (i, 0)),
                  pl.BlockSpec((1, D_MODEL), lambda i, j: (0, 0)),
                  pl.BlockSpec((D_MODEL, tn), lambda i, j: (0, j)),
                  pl.BlockSpec((D_MODEL, NSA_GROUPS * LANES), lambda i, j: (0, 0)),
                  pl.BlockSpec((tm, HEAD_DIM), lambda i, j: (i, 0)),
                  pl.BlockSpec((tm, HEAD_DIM), lambda i, j: (i, 0))],
        out_specs=[pl.BlockSpec((blocks_per_tile, tm, LANES), lambda i, j: (j, i, 0)),
                   pl.BlockSpec((NSA_GROUPS, tm, LANES), lambda i, j: (0, i, 0))],
        out_shape=[jax.ShapeDtypeStruct((N_BLOCKS, T, LANES), BF16),
                   jax.ShapeDtypeStruct((NSA_GROUPS, T, LANES), F32)],
        scratch_shapes=[pltpu.VMEM((tm, D_MODEL), BF16)],
        compiler_params=pltpu.CompilerParams(
            dimension_semantics=("parallel", "arbitrary"), vmem_limit_bytes=VMEM_LIMIT),
        name="in_proj",
    )(x2d, g_norm, w_perm, w_gate, cos, sin)


def _compress_kernel(seg_ref, pe_ref, w1_ref, w2_ref, out_ref):
    half = CMP_STRIDE * HEAD_DIM
    seg = seg_ref[0].astype(F32)
    pe = pe_ref[0]
    top = (seg + pe[:, :half]).astype(BF16)
    bot = (seg + pe[:, half:]).astype(BF16)
    a = jnp.dot(top, w1_ref[0, :half, :], preferred_element_type=F32)
    b = jnp.dot(bot, w1_ref[0, half:, :], preferred_element_type=F32)
    rows = a.shape[0]
    h = a + pltpu.roll(b, rows - 1, axis=0)
    hid = h * _sigmoid(h)
    out_ref[0] = jnp.dot(hid.astype(BF16), w2_ref[0], preferred_element_type=F32).astype(BF16)


def _compress(seg, pe, w1, w2):
    _, R, half = seg.shape
    return pl.pallas_call(
        _compress_kernel,
        grid=(2,),
        in_specs=[pl.BlockSpec((1, R, half), lambda c: (c, 0, 0)),
                  pl.BlockSpec((1, 1, 2 * half), lambda c: (c, 0, 0)),
                  pl.BlockSpec((1, 2 * half, CMP_HIDDEN), lambda c: (c, 0, 0)),
                  pl.BlockSpec((1, CMP_HIDDEN, HEAD_DIM), lambda c: (c, 0, 0))],
        out_specs=pl.BlockSpec((1, R, HEAD_DIM), lambda c: (c, 0, 0)),
        out_shape=jax.ShapeDtypeStruct((2, R, HEAD_DIM), BF16),
        compiler_params=pltpu.CompilerParams(
            dimension_semantics=("arbitrary",), vmem_limit_bytes=VMEM_LIMIT),
        name="compress",
    )(seg, pe, w1, w2)


def _select_bias_t(score_t, n_rows, n_keep):
    jrow = lax.broadcasted_iota(jnp.int32, score_t.shape, 0)
    cnt = jnp.zeros(score_t.shape, jnp.int32)
    for jp in range(n_rows):
        row = score_t[jp:jp + 1, :]
        beats = (row > score_t) | ((row == score_t) & (jrow > jp))
        cnt = cnt + beats.astype(jnp.int32)
    return (cnt < n_keep) & (score_t > -jnp.inf)


def _bias_columns(keep_t):
    rows, q = keep_t.shape
    bias_t = jnp.where(keep_t, 0.0, NEG_BIG).astype(F32)
    if rows < LANES:
        bias_t = jnp.concatenate([bias_t, jnp.zeros((LANES - rows, q), F32)], axis=0)
    return bias_t.T.astype(BF16)


def _softmax_pv(s, v):
    m = jnp.max(s, axis=-1, keepdims=True)
    e = jnp.exp(s - m)
    l = jnp.sum(e, axis=-1, keepdims=True)
    return jnp.dot(e.astype(BF16), v, preferred_element_type=F32) / l


def _moba_kernel(q_ref, k_ref, v_ref, onehot_ref, o_ref, kaug_ref, kmean_ref, *, n_blk, k_top):
    i = pl.program_id(2)
    S = k_ref.shape[0]
    tq = q_ref.shape[0]
    nb8 = kmean_ref.shape[0]

    @pl.when(i == 0)
    def _():
        kaug_ref[:, :HEAD_DIM] = k_ref[...]
        kaug_ref[:, HEAD_DIM:] = onehot_ref[...]
        kmean_ref[...] = jnp.zeros(kmean_ref.shape, F32)
        for j in range(n_blk):
            kb = k_ref[j * MOBA_BLOCK:(j + 1) * MOBA_BLOCK, :].astype(F32)
            kmean_ref[j:j + 1, :] = jnp.sum(kb, axis=0, keepdims=True) * (1.0 / MOBA_BLOCK)

    q = q_ref[...]
    gate_t = _nt_dot(kmean_ref[...].astype(BF16), q)
    jrow = lax.broadcasted_iota(jnp.int32, gate_t.shape, 0)
    past = jrow < i
    gate_t = jnp.where(past & jnp.isfinite(gate_t), gate_t, -jnp.inf)
    keep_t = _select_bias_t(gate_t, n_blk, k_top) | (jrow == i)
    q_aug = jnp.concatenate([q, _bias_columns(keep_t)], axis=1)

    s = _nt_dot(q_aug, kaug_ref[...]) * SCALE
    qpos = i * tq + lax.broadcasted_iota(jnp.int32, s.shape, 0)
    kpos = lax.broadcasted_iota(jnp.int32, s.shape, 1)
    s = jnp.where(kpos <= qpos, s, NEG_BIG)
    o_ref[...] = _softmax_pv(s, v_ref[...]).astype(BF16)


def _moba(proj, onehot, *, B, S):
    T = B * S
    tq = MOBA_BLOCK
    nq = S // tq
    n_blk = S // MOBA_BLOCK
    k_top = min(MOBA_TOPK, n_blk - 1)
    nb8 = -(-n_blk // 8) * 8
    kern = functools.partial(_moba_kernel, n_blk=n_blk, k_top=k_top)
    return pl.pallas_call(
        kern,
        grid=(B, MOBA_HEADS, nq),
        in_specs=[pl.BlockSpec((None, tq, LANES), lambda b, h, i: (BLK_MQ + h, b * nq + i, 0)),
                  pl.BlockSpec((None, S, LANES), lambda b, h, i: (BLK_MK + h, b, 0)),
                  pl.BlockSpec((None, S, LANES), lambda b, h, i: (BLK_MV + h, b, 0)),
                  pl.BlockSpec((S, LANES), lambda b, h, i: (0, 0))],
        out_specs=pl.BlockSpec((None, tq, LANES), lambda b, h, i: (h, b * nq + i, 0)),
        out_shape=jax.ShapeDtypeStruct((MOBA_HEADS, T, LANES), BF16),
        scratch_shapes=[pltpu.VMEM((S, 2 * LANES), BF16), pltpu.VMEM((nb8, HEAD_DIM), F32)],
        compiler_params=pltpu.CompilerParams(
            dimension_semantics=("parallel", "parallel", "arbitrary"),
            vmem_limit_bytes=VMEM_LIMIT),
        name="moba_attn",
    )(proj, proj, proj, onehot)


def _nsa_kernel(q_ref, cos_ref, sin_ref, kc_ref, vc_ref, ks_ref, vs_ref, kw_ref, vw_ref,
                gate_ref, onehot_ref, ovt_ref, o_ref, ksaug_ref, *, n_cmp, n_sel_blk, n_top):
    i = pl.program_id(2)
    S = ks_ref.shape[0]
    R, tq, _ = q_ref.shape

    @pl.when(i == 0)
    def _():
        ksaug_ref[:, :HEAD_DIM] = ks_ref[...]
        ksaug_ref[:, HEAD_DIM:] = onehot_ref[...]

    q_raw = q_ref[...]
    q_rot = _rope(q_raw.astype(F32), cos_ref[...][None], sin_ref[...][None]).astype(BF16)

    n_seg = kc_ref.shape[0]
    s_c = (_nt_dot(q_raw.reshape(R * tq, HEAD_DIM), kc_ref[...]) * SCALE).reshape(R, tq, n_seg)
    n_idx = lax.broadcasted_iota(jnp.int32, s_c.shape, 2)
    pos3 = i * tq + lax.broadcasted_iota(jnp.int32, s_c.shape, 1)
    m_c = (n_idx * CMP_STRIDE + CMP_BLOCK - 1 <= pos3) & (n_idx < n_cmp)
    s_c = jnp.where(m_c, s_c, -jnp.inf)
    mx = jnp.max(s_c, axis=-1, keepdims=True)
    mx = jnp.where(jnp.isfinite(mx), mx, 0.0)
    e_c = jnp.where(m_c, jnp.exp(s_c - mx), 0.0)
    p_c = e_c / jnp.maximum(jnp.sum(e_c, axis=-1, keepdims=True), 1e-30)
    o_c = jnp.dot(p_c.reshape(R * tq, n_seg).astype(BF16), vc_ref[...],
                  preferred_element_type=F32).reshape(R, tq, HEAD_DIM)

    p_sum = jnp.sum(p_c, axis=0)
    p_hi = p_sum.astype(BF16)
    p_lo = (p_sum - p_hi.astype(F32)).astype(BF16)
    ovt = ovt_ref[...]
    imp_t = _nt_dot(ovt, p_hi) + _nt_dot(ovt, p_lo)
    jrow = lax.broadcasted_iota(jnp.int32, imp_t.shape, 0)
    posq = i * tq + lax.broadcasted_iota(jnp.int32, imp_t.shape, 1)
    own = posq // SEL_BLOCK
    forced = (jrow == 0) | (jrow == own) | (jrow == own - 1)
    future = jrow * SEL_BLOCK > posq
    score_t = jnp.where(future, -jnp.inf, jnp.where(forced, jnp.inf, imp_t))
    keep_t = _select_bias_t(score_t, n_sel_blk, n_top)
    bias = _bias_columns(keep_t)

    q2 = q_rot.reshape(R * tq, HEAD_DIM)
    q_aug = jnp.concatenate(
        [q_rot, jnp.broadcast_to(bias[None], (R, tq, LANES))], axis=2).reshape(R * tq, 2 * LANES)
    s_s = (_nt_dot(q_aug, ksaug_ref[...]) * SCALE).reshape(R, tq, S)
    qpos = i * tq + lax.broadcasted_iota(jnp.int32, s_s.shape, 1)
    kpos = lax.broadcasted_iota(jnp.int32, s_s.shape, 2)
    s_s = jnp.where(kpos <= qpos, s_s, NEG_BIG).reshape(R * tq, S)
    o_s = _softmax_pv(s_s, vs_ref[...]).reshape(R, tq, HEAD_DIM)

    s_w = (_nt_dot(q2, kw_ref[...]) * SCALE).reshape(R, tq, S)
    s_w = jnp.where((kpos <= qpos) & (kpos > qpos - WINDOW), s_w, NEG_BIG).reshape(R * tq, S)
    o_w = _softmax_pv(s_w, vw_ref[...]).reshape(R, tq, HEAD_DIM)

    gt = _sigmoid(gate_ref[...])
    for r in range(R):
        o = (gt[:, 3 * r:3 * r + 1] * o_c[r] + gt[:, 3 * r + 1:3 * r + 2] * o_s[r]
             + gt[:, 3 * r + 2:3 * r + 3] * o_w[r])
        o_ref[r] = o.astype(BF16)


def _nsa(proj, gates, cos, sin, kvc, onehot, ovt, *, B, S, tq):
    T = B * S
    nq = S // tq
    n_seg = S // CMP_STRIDE
    n_cmp = n_seg - CMP_BLOCK // CMP_STRIDE + 1
    n_sel_blk = S // SEL_BLOCK
    n_top = min(SEL_TOPK, n_sel_blk)
    R = NSA_REP
    kern = functools.partial(_nsa_kernel, n_cmp=n_cmp, n_sel_blk=n_sel_blk, n_top=n_top)
    kv_spec = lambda blk: pl.BlockSpec((None, S, LANES), lambda b, g, i: (blk + g, b, 0))
    return pl.pallas_call(
        kern,
        grid=(B, NSA_GROUPS, nq),
        in_specs=[pl.BlockSpec((R, tq, LANES), lambda b, g, i: (BLK_NQ // R + g, b * nq + i, 0)),
                  pl.BlockSpec((tq, LANES), lambda b, g, i: (b * nq + i, 0)),
                  pl.BlockSpec((tq, LANES), lambda b, g, i: (b * nq + i, 0)),
                  pl.BlockSpec((None, n_seg, LANES), lambda b, g, i: (0, g * B + b, 0)),
                  pl.BlockSpec((None, n_seg, LANES), lambda b, g, i: (1, g * B + b, 0)),
                  kv_spec(BLK_NKS), kv_spec(BLK_NVS), kv_spec(BLK_NKW), kv_spec(BLK_NVW),
                  pl.BlockSpec((None, tq, LANES), lambda b, g, i: (g, b * nq + i, 0)),
                  pl.BlockSpec((S, LANES), lambda b, g, i: (0, 0)),
                  pl.BlockSpec(ovt.shape, lambda b, g, i: (0, 0))],
        out_specs=pl.BlockSpec((R, tq, LANES), lambda b, g, i: (g, b * nq + i, 0)),
        out_shape=jax.ShapeDtypeStruct((NSA_HEADS, T, LANES), BF16),
        scratch_shapes=[pltpu.VMEM((S, 2 * LANES), BF16)],
        compiler_params=pltpu.CompilerParams(
            dimension_semantics=("parallel", "parallel", "arbitrary"),
            vmem_limit_bytes=VMEM_LIMIT),
        name="nsa_attn",
    )(proj, cos, sin, kvc, kvc, proj, proj, proj, proj, gates, onehot, ovt)


def _out_proj_kernel(om_ref, on_ref, zm_ref, zn_ref, gm_ref, gn_ref, x_ref, w_ref, gf_ref,
                     out_ref, y_scr):
    def gated_norm(o_ref, z_ref, g_ref, col0):
        z = z_ref[...].astype(F32)
        a = o_ref[...].astype(F32) * (z * _sigmoid(z))
        ss = jnp.sum(jnp.sum(a * a, axis=0), axis=-1, keepdims=True)
        inv = lax.rsqrt(ss * (1.0 / (a.shape[0] * LANES)) + EPS)
        for h in range(a.shape[0]):
            y = a[h] * inv * g_ref[h]
            y_scr[:, col0 + h * LANES:col0 + (h + 1) * LANES] = y.astype(BF16)

    gated_norm(om_ref, zm_ref, gm_ref, 0)
    gated_norm(on_ref, zn_ref, gn_ref, MOBA_HEADS * LANES)
    r = x_ref[...] + jnp.dot(y_scr[...], w_ref[...], preferred_element_type=F32)
    ms = jnp.mean(r * r, axis=-1, keepdims=True)
    out_ref[...] = r * lax.rsqrt(ms + EPS) * gf_ref[...]


def _out_proj(o_moba, o_nsa, proj, g_moba, g_nsa, x2d, w_out, g_final, *, tm):
    T = x2d.shape[0]
    H = MOBA_HEADS
    return pl.pallas_call(
        _out_proj_kernel,
        grid=(T // tm,),
        in_specs=[pl.BlockSpec((H, tm, LANES), lambda i: (0, i, 0)),
                  pl.BlockSpec((H, tm, LANES), lambda i: (0, i, 0)),
                  pl.BlockSpec((H, tm, LANES), lambda i: (BLK_MZ // H, i, 0)),
                  pl.BlockSpec((H, tm, LANES), lambda i: (BLK_NZ // H, i, 0)),
                  pl.BlockSpec((H, 1, LANES), lambda i: (0, 0, 0)),
                  pl.BlockSpec((H, 1, LANES), lambda i: (0, 0, 0)),
                  pl.BlockSpec((tm, D_MODEL), lambda i: (i, 0)),
                  pl.BlockSpec((D_MODEL, D_MODEL), lambda i: (0, 0)),
                  pl.BlockSpec((1, D_MODEL), lambda i: (0, 0))],
        out_specs=pl.BlockSpec((tm, D_MODEL), lambda i: (i, 0)),
        out_shape=jax.ShapeDtypeStruct((T, D_MODEL), F32),
        scratch_shapes=[pltpu.VMEM((tm, D_MODEL), BF16)],
        compiler_params=pltpu.CompilerParams(
            dimension_semantics=("parallel",), vmem_limit_bytes=VMEM_LIMIT),
        name="out_proj",
    )(o_moba, o_nsa, proj, proj, g_moba, g_nsa, x2d, w_out, g_final)


def _permute_w_in(w_in):
    mw, nw, kw = MOBA_HEADS * HEAD_DIM, NSA_HEADS * HEAD_DIM, NSA_GROUPS * HEAD_DIM
    sizes = [mw] * 4 + [nw] + [kw] * 6 + [3 * NSA_HEADS, nw]
    offs = np.concatenate([[0], np.cumsum(sizes)])
    names = ["mq", "mk", "mv", "mz", "nq", "nkc", "nvc", "nks", "nvs", "nkw", "nvw", "ng", "nz"]
    part = {n: w_in[:, int(offs[k]):int(offs[k + 1])] for k, n in enumerate(names)}
    order = ["mq", "mk", "mv", "mz", "nq", "nz", "nks", "nkw", "nkc", "nvc", "nvs", "nvw"]
    w_perm = jnp.concatenate([part[n] for n in order], axis=1).astype(BF16)
    per_group = 3 * NSA_REP
    wg = part["ng"].reshape(D_MODEL, NSA_GROUPS, per_group)
    wg = jnp.pad(wg, ((0, 0), (0, 0), (0, LANES - per_group)))
    return w_perm, wg.reshape(D_MODEL, NSA_GROUPS * LANES).astype(BF16)


def _block_onehot(S, block):
    ids = np.arange(S)[:, None] // block
    return jnp.asarray((ids == np.arange(LANES)[None, :]).astype(np.float32), dtype=BF16)


def _overlap_t(n_seg, n_cmp, n_sel_blk):
    cs = np.arange(n_seg)[None, :] * CMP_STRIDE
    ss = np.arange(n_sel_blk)[:, None] * SEL_BLOCK
    ov = (cs < ss + SEL_BLOCK) & (cs + CMP_BLOCK > ss) & (np.arange(n_seg)[None, :] < n_cmp)
    return jnp.asarray(ov.astype(np.float32), dtype=BF16)


def _layer(x, cos, sin, w_in, g_norm, pe_ck, pe_cv, w_ck1, w_ck2, w_cv1, w_cv2,
           g_out_moba, g_out_nsa, w_out, g_final, *, nsa_tq, tm_in, tm_out, blocks_per_tile):
    B, S, _ = x.shape
    T = B * S
    x2d = x.reshape(T, D_MODEL)
    w_perm, w_gate = _permute_w_in(w_in)
    proj, gates = _in_proj(x2d, g_norm.reshape(1, D_MODEL), w_perm, w_gate, cos, sin,
                           tm=tm_in, blocks_per_tile=blocks_per_tile)

    n_seg = S // CMP_STRIDE
    seg = proj[BLK_NKC:BLK_NKC + 4].reshape(2, NSA_GROUPS * B * n_seg, CMP_STRIDE * HEAD_DIM)
    pe = jnp.stack([pe_ck.reshape(1, -1), pe_cv.reshape(1, -1)])
    w1 = jnp.stack([w_ck1, w_cv1]).astype(BF16)
    w2 = jnp.stack([w_ck2, w_cv2]).astype(BF16)
    kvc = _compress(seg, pe, w1, w2)

    o_moba = _moba(proj, _block_onehot(S, MOBA_BLOCK), B=B, S=S)
    n_cmp = n_seg - CMP_BLOCK // CMP_STRIDE + 1
    o_nsa = _nsa(proj, gates, cos, sin, kvc, _block_onehot(S, SEL_BLOCK),
                 _overlap_t(n_seg, n_cmp, S // SEL_BLOCK), B=B, S=S, tq=nsa_tq)
    out = _out_proj(o_moba, o_nsa, proj,
                    g_out_moba.reshape(MOBA_HEADS, 1, LANES), g_out_nsa.reshape(NSA_HEADS, 1, LANES),
                    x2d, w_out.astype(BF16), g_final.reshape(1, D_MODEL), tm=tm_out)
    return out.reshape(B, S, D_MODEL)


def kernel(x, positions, w_in, g_norm, pe_ck, pe_cv, w_ck1, w_ck2, w_cv1, w_cv2,
           g_out_moba, g_out_nsa, w_out, g_final):
    assert w_in.shape[0] == 1, "single-layer problem"
    cos, sin = _rope_tables(positions)
    return _layer(x, cos, sin, w_in[0], g_norm[0], pe_ck[0], pe_cv[0], w_ck1[0], w_ck2[0],
                  w_cv1[0], w_cv2[0], g_out_moba[0], g_out_nsa[0], w_out[0], g_final,
                  nsa_tq=128, tm_in=512, tm_out=256, blocks_per_tile=4)
```

```python
import functools

import numpy as np
import jax
import jax.numpy as jnp
from jax import lax
from jax.experimental import pallas as pl
from jax.experimental.pallas import tpu as pltpu

F32 = jnp.float32
BF16 = jnp.bfloat16

D_MODEL = 2048
HEAD_DIM = 128
MOBA_HEADS = 8
NSA_HEADS = 8
NSA_GROUPS = 2
NSA_REP = 4
MOBA_BLOCK = 256
MOBA_TOPK = 3
CMP_BLOCK = 32
CMP_STRIDE = 16
CMP_HIDDEN = 256
SEL_BLOCK = 64
SEL_TOPK = 8
WINDOW = 512
ROPE_THETA = 10000.0
EPS = 1e-6
SCALE = HEAD_DIM ** -0.5
NEG_BIG = -(2.0 ** 100)

LANES = 128
VMEM_LIMIT = 48 * 1024 * 1024
VMEM_LIMIT_BIG = 58 * 1024 * 1024

BLK_MQ, BLK_MK, BLK_MV, BLK_MZ, BLK_NQ, BLK_NZ = 0, 8, 16, 24, 32, 40
BLK_NKS, BLK_NKW, BLK_NKC, BLK_NVC, BLK_NVS, BLK_NVW = 48, 50, 52, 54, 56, 58
N_BLOCKS = 60
ROPE_BLOCKS = tuple(range(0, 16)) + (48, 49, 50, 51)


def _nt_dot(a, b):
    return lax.dot_general(a, b, (((1,), (1,)), ((), ())), preferred_element_type=F32)


def _sigmoid(x):
    return 1.0 / (1.0 + jnp.exp(-x))


def _rope(a, cos, sin_signed):
    return a * cos + pltpu.roll(a, HEAD_DIM // 2, axis=a.ndim - 1) * sin_signed


def _rope_table_kernel(pos_ref, invf_ref, sign_ref, cos_ref, sin_ref):
    ang = pos_ref[...].astype(F32) * invf_ref[...]
    cos_ref[...] = jnp.cos(ang)
    sin_ref[...] = jnp.sin(ang) * sign_ref[...]


def _rope_tables(positions):
    T = positions.size
    tile = min(T, 2048)
    half = HEAD_DIM // 2
    inv_freq = 1.0 / (ROPE_THETA ** (jnp.arange(0, HEAD_DIM, 2, dtype=F32) / HEAD_DIM))
    invf = jnp.concatenate([inv_freq, inv_freq]).reshape(1, HEAD_DIM)
    sign = jnp.concatenate([-jnp.ones((half,), F32), jnp.ones((half,), F32)]).reshape(1, HEAD_DIM)
    return pl.pallas_call(
        _rope_table_kernel,
        grid=(T // tile,),
        in_specs=[pl.BlockSpec((tile, 1), lambda i: (i, 0)),
                  pl.BlockSpec((1, HEAD_DIM), lambda i: (0, 0)),
                  pl.BlockSpec((1, HEAD_DIM), lambda i: (0, 0))],
        out_specs=[pl.BlockSpec((tile, HEAD_DIM), lambda i: (i, 0)),
                   pl.BlockSpec((tile, HEAD_DIM), lambda i: (i, 0))],
        out_shape=[jax.ShapeDtypeStruct((T, HEAD_DIM), F32)] * 2,
        name="rope_tables",
    )(positions.reshape(T, 1), invf, sign)


def _in_proj_kernel(x_ref, g_ref, w_ref, wg_ref, cos_ref, sin_ref, out_ref, gate_ref, h_scr,
                    *, rope_tiles, blocks_per_tile):
    j = pl.program_id(1)

    @pl.when(j == 0)
    def _():
        x = x_ref[...]
        ms = jnp.mean(x * x, axis=-1, keepdims=True)
        h = (x * lax.rsqrt(ms + EPS) * g_ref[...]).astype(BF16)
        h_scr[...] = h
        gates = jnp.dot(h, wg_ref[...], preferred_element_type=F32)
        for g in range(NSA_GROUPS):
            gate_ref[g] = gates[:, g * LANES:(g + 1) * LANES]

    acc = jnp.dot(h_scr[...], w_ref[...], preferred_element_type=F32)
    is_rope = functools.reduce(jnp.logical_or, [j == t for t in rope_tiles])

    @pl.when(is_rope)
    def _():
        cos = cos_ref[...]
        sin = sin_ref[...]
        for c in range(blocks_per_tile):
            out_ref[c] = _rope(acc[:, c * LANES:(c + 1) * LANES], cos, sin).astype(BF16)

    @pl.when(jnp.logical_not(is_rope))
    def _():
        for c in range(blocks_per_tile):
            out_ref[c] = acc[:, c * LANES:(c + 1) * LANES].astype(BF16)


def _in_proj(x2d, g_norm, w_perm, w_gate, cos, sin, *, tm, blocks_per_tile):
    T = x2d.shape[0]
    tn = blocks_per_tile * LANES
    n_tiles = N_BLOCKS // blocks_per_tile
    rope_tiles = tuple(sorted({b // blocks_per_tile for b in ROPE_BLOCKS}))
    assert all((t * blocks_per_tile + c) in ROPE_BLOCKS
               for t in rope_tiles for c in range(blocks_per_tile))
    kern = functools.partial(_in_proj_kernel, rope_tiles=rope_tiles,
                             blocks_per_tile=blocks_per_tile)
    return pl.pallas_call(
        kern,
        grid=(T // tm, n_tiles),
        in_specs=[pl.BlockSpec((tm, D_MODEL), lambda i, j: (i, 0)),
                  pl.BlockSpec((1, D_MODEL), lambda i, j: (0, 0)),
                  pl.BlockSpec((D_MODEL, tn), lambda i, j: (0, j)),
                  pl.BlockSpec((D_MODEL, NSA_GROUPS * LANES), lambda i, j: (0, 0)),
                  pl.BlockSpec((tm, HEAD_DIM), lambda i, j: (i, 0)),
                  pl.BlockSpec((tm, HEAD_DIM), lambda i, j: (i, 0))],
        out_specs=[pl.BlockSpec((blocks_per_tile, tm, LANES), lambda i, j: (j, i, 0)),
                   pl.BlockSpec((NSA_GROUPS, tm, LANES), lambda i, j: (0, i, 0))],
        out_shape=[jax.ShapeDtypeStruct((N_BLOCKS, T, LANES), BF16),
                   jax.ShapeDtypeStruct((NSA_GROUPS, T, LANES), F32)],
        scratch_shapes=[pltpu.VMEM((tm, D_MODEL), BF16)],
        compiler_params=pltpu.CompilerParams(
            dimension_semantics=("parallel", "arbitrary"), vmem_limit_bytes=VMEM_LIMIT),
        name="in_proj",
    )(x2d, g_norm, w_perm, w_gate, cos, sin)


def _compress_kernel(seg_ref, pe_ref, w1_ref, w2_ref, out_ref):
    half = CMP_STRIDE * HEAD_DIM
    seg = seg_ref[0].astype(F32)
    pe = pe_ref[0]
    top = (seg + pe[:, :half]).astype(BF16)
    bot = (seg + pe[:, half:]).astype(BF16)
    a = jnp.dot(top, w1_ref[0, :half, :], preferred_element_type=F32)
    b = jnp.dot(bot, w1_ref[0, half:, :], preferred_element_type=F32)
    rows = a.shape[0]
    h = a + pltpu.roll(b, rows - 1, axis=0)
    hid = h * _sigmoid(h)
    out_ref[0] = jnp.dot(hid.astype(BF16), w2_ref[0], preferred_element_type=F32).astype(BF16)


def _compress(seg, pe, w1, w2):
    _, R, half = seg.shape
    return pl.pallas_call(
        _compress_kernel,
        grid=(2,),
        in_specs=[pl.BlockSpec((1, R, half), lambda c: (c, 0, 0)),
                  pl.BlockSpec((1, 1, 2 * half), lambda c: (c, 0, 0)),
                  pl.BlockSpec((1, 2 * half, CMP_HIDDEN), lambda c: (c, 0, 0)),
                  pl.BlockSpec((1, CMP_HIDDEN, HEAD_DIM), lambda c: (c, 0, 0))],
        out_specs=pl.BlockSpec((1, R, HEAD_DIM), lambda c: (c, 0, 0)),
        out_shape=jax.ShapeDtypeStruct((2, R, HEAD_DIM), BF16),
        compiler_params=pltpu.CompilerParams(
            dimension_semantics=("arbitrary",), vmem_limit_bytes=VMEM_LIMIT),
        name="compress",
    )(seg, pe, w1, w2)


def _select_bias_t(score_t, n_rows, n_keep):
    jrow = lax.broadcasted_iota(jnp.int32, score_t.shape, 0)
    cnt = jnp.zeros(score_t.shape, jnp.int32)
    for jp in range(n_rows):
        row = score_t[jp:jp + 1, :]
        beats = (row > score_t) | ((row == score_t) & (jrow > jp))
        cnt = cnt + beats.astype(jnp.int32)
    return (cnt < n_keep) & (score_t > -jnp.inf)


def _bias_t(keep_t):
    return jnp.where(keep_t, 0.0, NEG_BIG).astype(F32)


def _bias_columns(bias_t):
    rows, q = bias_t.shape
    if rows < LANES:
        bias_t = jnp.concatenate([bias_t, jnp.zeros((LANES - rows, q), F32)], axis=0)
    return bias_t.T.astype(BF16)


ROW_CHUNK = 64
HEADS_PER_PASS = 2


def _softmax_static(s, p_ref, l_ref, *, n_rows, tq, width, q0, k0, window=None):
    n_tiles = width // LANES
    for r0 in range(0, n_rows, ROW_CHUNK):
        rows = slice(r0, r0 + ROW_CHUNK)
        qlo = q0 + r0 % tq
        qhi = qlo + ROW_CHUNK - 1
        kinds = []
        for t in range(n_tiles):
            klo = k0 + t * LANES
            khi = klo + LANES - 1
            none = klo > qhi or (window is not None and khi <= qlo - window)
            full = khi <= qlo and (window is None or klo > qhi - window)
            kinds.append("none" if none else "full" if full else "part")
        mx = None
        masked = {}
        for t, kind in enumerate(kinds):
            if kind == "none":
                continue
            x = s[rows, t * LANES:(t + 1) * LANES]
            if kind == "part":
                qpos = qlo + lax.broadcasted_iota(jnp.int32, x.shape, 0)
                kpos = k0 + t * LANES + lax.broadcasted_iota(jnp.int32, x.shape, 1)
                ok = kpos <= qpos
                if window is not None:
                    ok = ok & (kpos > qpos - window)
                x = jnp.where(ok, x, NEG_BIG)
                masked[t] = x
            mx = x if mx is None else jnp.maximum(mx, x)
        m = jnp.broadcast_to(jnp.max(mx, axis=-1, keepdims=True), mx.shape)
        l = jnp.zeros(mx.shape, F32)
        for t, kind in enumerate(kinds):
            cols = slice(t * LANES, (t + 1) * LANES)
            if kind == "none":
                p_ref[rows, cols] = jnp.zeros((ROW_CHUNK, LANES), BF16)
                continue
            x = masked[t] if kind == "part" else s[rows, cols]
            p = jnp.exp((x - m) * SCALE)
            l = l + p
            p_ref[rows, cols] = p.astype(BF16)
        l_ref[rows, :] = jnp.broadcast_to(jnp.sum(l, axis=-1, keepdims=True), l.shape)


def _moba_kernel(q_ref, k_ref, v_ref, onehot_ref, o_ref, kaug_ref, qaug_ref, kmean_ref,
                 p_scr, l_scr, *, n_blk, k_top):
    S = k_ref.shape[0]
    tq = MOBA_BLOCK
    kaug_ref[:, :HEAD_DIM] = k_ref[...]
    kaug_ref[:, HEAD_DIM:] = onehot_ref[...]
    kmean_ref[...] = jnp.zeros(kmean_ref.shape, F32)
    for j in range(n_blk):
        kb = k_ref[j * MOBA_BLOCK:(j + 1) * MOBA_BLOCK, :].astype(F32)
        kmean_ref[j:j + 1, :] = jnp.sum(kb, axis=0, keepdims=True) * (1.0 / MOBA_BLOCK)

    q = q_ref[...]
    gate_t = _nt_dot(kmean_ref[...].astype(BF16), q)
    jrow = lax.broadcasted_iota(jnp.int32, gate_t.shape, 0)
    own = lax.broadcasted_iota(jnp.int32, gate_t.shape, 1) // MOBA_BLOCK
    gate_t = jnp.where((jrow < own) & jnp.isfinite(gate_t), gate_t, -jnp.inf)
    bias_t = _bias_t(_select_bias_t(gate_t, n_blk, k_top) | (jrow == own))
    qaug_ref[:, :HEAD_DIM] = q
    for c in range(S // tq):
        qaug_ref[c * tq:(c + 1) * tq, HEAD_DIM:] = _bias_columns(bias_t[:, c * tq:(c + 1) * tq])

    for c in range(S // tq):
        w = (c + 1) * tq
        p_c, l_c = p_scr.at[c % 2], l_scr.at[c % 2]
        s = _nt_dot(qaug_ref[c * tq:(c + 1) * tq, :], kaug_ref[:w, :])
        _softmax_static(s, p_c, l_c, n_rows=tq, tq=tq, width=w, q0=c * tq, k0=0)
        o = jnp.dot(p_c[:, :w], v_ref[:w, :], preferred_element_type=F32) / l_c[...]
        o_ref[c * tq:(c + 1) * tq, :] = o.astype(BF16)


def _moba(proj, onehot, *, B, S):
    T = B * S
    tq = MOBA_BLOCK
    n_blk = S // MOBA_BLOCK
    k_top = min(MOBA_TOPK, n_blk - 1)
    nb8 = -(-n_blk // 8) * 8
    kern = functools.partial(_moba_kernel, n_blk=n_blk, k_top=k_top)
    head_spec = lambda blk: pl.BlockSpec((None, S, LANES), lambda b, h: (blk + h, b, 0))
    return pl.pallas_call(
        kern,
        grid=(B, MOBA_HEADS),
        in_specs=[head_spec(BLK_MQ), head_spec(BLK_MK), head_spec(BLK_MV),
                  pl.BlockSpec((S, LANES), lambda b, h: (0, 0))],
        out_specs=head_spec(0),
        out_shape=jax.ShapeDtypeStruct((MOBA_HEADS, T, LANES), BF16),
        scratch_shapes=[pltpu.VMEM((S, 2 * LANES), BF16), pltpu.VMEM((S, 2 * LANES), BF16),
                        pltpu.VMEM((nb8, HEAD_DIM), F32),
                        pltpu.VMEM((2, tq, S), BF16), pltpu.VMEM((2, tq, LANES), F32)],
        compiler_params=pltpu.CompilerParams(
            dimension_semantics=("parallel", "parallel"), vmem_limit_bytes=VMEM_LIMIT),
        name="moba_attn",
    )(proj, proj, proj, onehot)


def _nsa_select_kernel(q_ref, cos_ref, sin_ref, kc_ref, vc_ref, ovt_ref,
                       qrot_ref, bias_ref, oc_ref, *, n_cmp, n_sel_blk, n_top):
    i = pl.program_id(2)
    R, tq, _ = q_ref.shape
    q_raw = q_ref[...]
    qrot_ref[...] = _rope(q_raw.astype(F32), cos_ref[...][None], sin_ref[...][None]).astype(BF16)

    n_seg = kc_ref.shape[0]
    s_c = (_nt_dot(q_raw.reshape(R * tq, HEAD_DIM), kc_ref[...]) * SCALE).reshape(R, tq, n_seg)
    n_idx = lax.broadcasted_iota(jnp.int32, s_c.shape, 2)
    pos3 = i * tq + lax.broadcasted_iota(jnp.int32, s_c.shape, 1)
    m_c = (n_idx * CMP_STRIDE + CMP_BLOCK - 1 <= pos3) & (n_idx < n_cmp)
    s_c = jnp.where(m_c, s_c, -jnp.inf)
    mx = jnp.max(s_c, axis=-1, keepdims=True)
    mx = jnp.where(jnp.isfinite(mx), mx, 0.0)
    e_c = jnp.where(m_c, jnp.exp(s_c - mx), 0.0)
    p_c = e_c / jnp.maximum(jnp.sum(e_c, axis=-1, keepdims=True), 1e-30)
    o_c = jnp.dot(p_c.reshape(R * tq, n_seg).astype(BF16), vc_ref[...],
                  preferred_element_type=F32).reshape(R, tq, HEAD_DIM)
    oc_ref[...] = o_c.astype(BF16)

    p_sum = jnp.sum(p_c, axis=0)
    p_hi = p_sum.astype(BF16)
    p_lo = (p_sum - p_hi.astype(F32)).astype(BF16)
    ovt = ovt_ref[...]
    imp_t = _nt_dot(ovt, p_hi) + _nt_dot(ovt, p_lo)
    jrow = lax.broadcasted_iota(jnp.int32, imp_t.shape, 0)
    posq = i * tq + lax.broadcasted_iota(jnp.int32, imp_t.shape, 1)
    own = posq // SEL_BLOCK
    forced = (jrow == 0) | (jrow == own) | (jrow == own - 1)
    future = jrow * SEL_BLOCK > posq
    score_t = jnp.where(future, -jnp.inf, jnp.where(forced, jnp.inf, imp_t))
    keep_t = _select_bias_t(score_t, n_sel_blk, n_top)
    bias_ref[...] = _bias_columns(_bias_t(keep_t))


def _nsa_select(proj, cos, sin, kvc, ovt, *, B, S, tq):
    T = B * S
    nq = S // tq
    n_seg = S // CMP_STRIDE
    n_cmp = n_seg - CMP_BLOCK // CMP_STRIDE + 1
    n_sel_blk = S // SEL_BLOCK
    R = NSA_REP
    kern = functools.partial(_nsa_select_kernel, n_cmp=n_cmp, n_sel_blk=n_sel_blk,
                             n_top=min(SEL_TOPK, n_sel_blk))
    heads_spec = pl.BlockSpec((R, tq, LANES), lambda b, g, i: (g, b * nq + i, 0))
    return pl.pallas_call(
        kern,
        grid=(B, NSA_GROUPS, nq),
        in_specs=[pl.BlockSpec((R, tq, LANES), lambda b, g, i: (BLK_NQ // R + g, b * nq + i, 0)),
                  pl.BlockSpec((tq, LANES), lambda b, g, i: (b * nq + i, 0)),
                  pl.BlockSpec((tq, LANES), lambda b, g, i: (b * nq + i, 0)),
                  pl.BlockSpec((None, n_seg, LANES), lambda b, g, i: (0, g * B + b, 0)),
                  pl.BlockSpec((None, n_seg, LANES), lambda b, g, i: (1, g * B + b, 0)),
                  pl.BlockSpec(ovt.shape, lambda b, g, i: (0, 0))],
        out_specs=[heads_spec,
                   pl.BlockSpec((None, tq, LANES), lambda b, g, i: (g, b * nq + i, 0)),
                   heads_spec],
        out_shape=[jax.ShapeDtypeStruct((NSA_HEADS, T, LANES), BF16),
                   jax.ShapeDtypeStruct((NSA_GROUPS, T, LANES), BF16),
                   jax.ShapeDtypeStruct((NSA_HEADS, T, LANES), BF16)],
        compiler_params=pltpu.CompilerParams(
            dimension_semantics=("parallel", "parallel", "parallel"),
            vmem_limit_bytes=VMEM_LIMIT),
        name="nsa_select",
    )(proj, cos, sin, kvc, kvc, ovt)


def _nsa_sel_kernel(qrot_ref, bias_ref, ks_ref, vs_ref, onehot_ref, o_ref,
                    ksaug_ref, p_scr, l_scr, *, tq):
    R, S, _ = qrot_ref.shape
    ksaug_ref[:, :HEAD_DIM] = ks_ref[...]
    ksaug_ref[:, HEAD_DIM:] = onehot_ref[...]

    n_pass = 0
    for c in range(S // tq):
        w = (c + 1) * tq
        q_rows = slice(c * tq, (c + 1) * tq)
        bias = bias_ref[q_rows, :]
        for r0 in range(0, R, HEADS_PER_PASS):
            heads = range(r0, r0 + HEADS_PER_PASS)
            p_c, l_c = p_scr.at[n_pass % 2], l_scr.at[n_pass % 2]
            n_pass += 1
            q_aug = jnp.concatenate(
                [jnp.concatenate([qrot_ref[r, q_rows, :], bias], axis=1) for r in heads], axis=0)
            s = _nt_dot(q_aug, ksaug_ref[:w, :])
            _softmax_static(s, p_c, l_c, n_rows=HEADS_PER_PASS * tq, tq=tq, width=w,
                            q0=c * tq, k0=0)
            o = jnp.dot(p_c[:, :w], vs_ref[:w, :], preferred_element_type=F32) / l_c[...]
            for k, r in enumerate(heads):
                o_ref[r, q_rows, :] = o[k * tq:(k + 1) * tq, :].astype(BF16)


def _nsa_sel(q_rot, bias, proj, onehot, *, B, S, tq):
    T = B * S
    R = NSA_REP
    heads_spec = pl.BlockSpec((R, S, LANES), lambda b, g: (g, b, 0))
    kv_spec = lambda blk: pl.BlockSpec((None, S, LANES), lambda b, g: (blk + g, b, 0))
    return pl.pallas_call(
        functools.partial(_nsa_sel_kernel, tq=tq),
        grid=(B, NSA_GROUPS),
        in_specs=[heads_spec,
                  pl.BlockSpec((None, S, LANES), lambda b, g: (g, b, 0)),
                  kv_spec(BLK_NKS), kv_spec(BLK_NVS),
                  pl.BlockSpec((S, LANES), lambda b, g: (0, 0))],
        out_specs=heads_spec,
        out_shape=jax.ShapeDtypeStruct((NSA_HEADS, T, LANES), BF16),
        scratch_shapes=[pltpu.VMEM((S, 2 * LANES), BF16),
                        pltpu.VMEM((2, HEADS_PER_PASS * tq, S), BF16),
                        pltpu.VMEM((2, HEADS_PER_PASS * tq, LANES), F32)],
        compiler_params=pltpu.CompilerParams(
            dimension_semantics=("parallel", "parallel"), vmem_limit_bytes=VMEM_LIMIT_BIG),
        name="nsa_sel_attn",
    )(q_rot, bias, proj, proj, onehot)


def _nsa_win_kernel(qrot_ref, kw_ref, vw_ref, oc_ref, os_ref, gate_ref, o_ref,
                    p_scr, l_scr, *, tq):
    R, S, _ = qrot_ref.shape
    n_pass = 0
    for c in range(S // tq):
        q_rows = slice(c * tq, (c + 1) * tq)
        k0 = max(0, c * tq - WINDOW)
        w = (c + 1) * tq - k0
        gt = _sigmoid(gate_ref[q_rows, :])
        for r0 in range(0, R, HEADS_PER_PASS):
            heads = range(r0, r0 + HEADS_PER_PASS)
            p_c, l_c = p_scr.at[n_pass % 2], l_scr.at[n_pass % 2]
            n_pass += 1
            q2 = jnp.concatenate([qrot_ref[r, q_rows, :] for r in heads], axis=0)
            s = _nt_dot(q2, kw_ref[k0:k0 + w, :])
            _softmax_static(s, p_c, l_c, n_rows=HEADS_PER_PASS * tq, tq=tq, width=w,
                            q0=c * tq, k0=k0, window=WINDOW)
            o_w = jnp.dot(p_c[:, :w], vw_ref[k0:k0 + w, :],
                          preferred_element_type=F32) / l_c[...]
            for k, r in enumerate(heads):
                o = (gt[:, 3 * r:3 * r + 1] * oc_ref[r, q_rows, :].astype(F32)
                     + gt[:, 3 * r + 1:3 * r + 2] * os_ref[r, q_rows, :].astype(F32)
                     + gt[:, 3 * r + 2:3 * r + 3] * o_w[k * tq:(k + 1) * tq, :])
                o_ref[r, q_rows, :] = o.astype(BF16)


def _nsa_win(q_rot, proj, o_c, o_s, gates, *, B, S, tq):
    T = B * S
    R = NSA_REP
    wmax = WINDOW + tq
    heads_spec = pl.BlockSpec((R, S, LANES), lambda b, g: (g, b, 0))
    kv_spec = lambda blk: pl.BlockSpec((None, S, LANES), lambda b, g: (blk + g, b, 0))
    return pl.pallas_call(
        functools.partial(_nsa_win_kernel, tq=tq),
        grid=(B, NSA_GROUPS),
        in_specs=[heads_spec, kv_spec(BLK_NKW), kv_spec(BLK_NVW), heads_spec, heads_spec,
                  pl.BlockSpec((None, S, LANES), lambda b, g: (g, b, 0))],
        out_specs=heads_spec,
        out_shape=jax.ShapeDtypeStruct((NSA_HEADS, T, LANES), BF16),
        scratch_shapes=[pltpu.VMEM((2, HEADS_PER_PASS * tq, wmax), BF16),
                        pltpu.VMEM((2, HEADS_PER_PASS * tq, LANES), F32)],
        compiler_params=pltpu.CompilerParams(
            dimension_semantics=("parallel", "parallel"), vmem_limit_bytes=VMEM_LIMIT_BIG),
        name="nsa_win_attn",
    )(q_rot, proj, proj, o_c, o_s, gates)


def _out_proj_kernel(om_ref, on_ref, zm_ref, zn_ref, gm_ref, gn_ref, x_ref, w_ref, gf_ref,
                     out_ref, y_scr):
    def gated_norm(o_ref, z_ref, g_ref, col0):
        z = z_ref[...].astype(F32)
        a = o_ref[...].astype(F32) * (z * _sigmoid(z))
        ss = jnp.sum(jnp.sum(a * a, axis=0), axis=-1, keepdims=True)
        inv = lax.rsqrt(ss * (1.0 / (a.shape[0] * LANES)) + EPS)
        for h in range(a.shape[0]):
            y = a[h] * inv * g_ref[h]
            y_scr[:, col0 + h * LANES:col0 + (h + 1) * LANES] = y.astype(BF16)

    gated_norm(om_ref, zm_ref, gm_ref, 0)
    gated_norm(on_ref, zn_ref, gn_ref, MOBA_HEADS * LANES)
    r = x_ref[...] + jnp.dot(y_scr[...], w_ref[...], preferred_element_type=F32)
    ms = jnp.mean(r * r, axis=-1, keepdims=True)
    out_ref[...] = r * lax.rsqrt(ms + EPS) * gf_ref[...]


def _out_proj(o_moba, o_nsa, proj, g_moba, g_nsa, x2d, w_out, g_final, *, tm):
    T = x2d.shape[0]
    H = MOBA_HEADS
    return pl.pallas_call(
        _out_proj_kernel,
        grid=(T // tm,),
        in_specs=[pl.BlockSpec((H, tm, LANES), lambda i: (0, i, 0)),
                  pl.BlockSpec((H, tm, LANES), lambda i: (0, i, 0)),
                  pl.BlockSpec((H, tm, LANES), lambda i: (BLK_MZ // H, i, 0)),
                  pl.BlockSpec((H, tm, LANES), lambda i: (BLK_NZ // H, i, 0)),
                  pl.BlockSpec((H, 1, LANES), lambda i: (0, 0, 0)),
                  pl.BlockSpec((H, 1, LANES), lambda i: (0, 0, 0)),
                  pl.BlockSpec((tm, D_MODEL), lambda i: (i, 0)),
                  pl.BlockSpec((D_MODEL, D_MODEL), lambda i: (0, 0)),
                  pl.BlockSpec((1, D_MODEL), lambda i: (0, 0))],
        out_specs=pl.BlockSpec((tm, D_MODEL), lambda i: (i, 0)),
        out_shape=jax.ShapeDtypeStruct((T, D_MODEL), F32),
        scratch_shapes=[pltpu.VMEM((tm, D_MODEL), BF16)],
        compiler_params=pltpu.CompilerParams(
            dimension_semantics=("parallel",), vmem_limit_bytes=VMEM_LIMIT),
        name="out_proj",
    )(o_moba, o_nsa, proj, proj, g_moba, g_nsa, x2d, w_out, g_final)


def _permute_w_in(w_in):
    mw, nw, kw = MOBA_HEADS * HEAD_DIM, NSA_HEADS * HEAD_DIM, NSA_GROUPS * HEAD_DIM
    sizes = [mw] * 4 + [nw] + [kw] * 6 + [3 * NSA_HEADS, nw]
    offs = np.concatenate([[0], np.cumsum(sizes)])
    names = ["mq", "mk", "mv", "mz", "nq", "nkc", "nvc", "nks", "nvs", "nkw", "nvw", "ng", "nz"]
    part = {n: w_in[:, int(offs[k]):int(offs[k + 1])] for k, n in enumerate(names)}
    order = ["mq", "mk", "mv", "mz", "nq", "nz", "nks", "nkw", "nkc", "nvc", "nvs", "nvw"]
    w_perm = jnp.concatenate([part[n] for n in order], axis=1).astype(BF16)
    per_group = 3 * NSA_REP
    wg = part["ng"].reshape(D_MODEL, NSA_GROUPS, per_group)
    wg = jnp.pad(wg, ((0, 0), (0, 0), (0, LANES - per_group)))
    return w_perm, wg.reshape(D_MODEL, NSA_GROUPS * LANES).astype(BF16)


def _block_onehot(S, block):
    ids = np.arange(S)[:, None] // block
    return jnp.asarray((ids == np.arange(LANES)[None, :]).astype(np.float32), dtype=BF16)


def _overlap_t(n_seg, n_cmp, n_sel_blk):
    cs = np.arange(n_seg)[None, :] * CMP_STRIDE
    ss = np.arange(n_sel_blk)[:, None] * SEL_BLOCK
    ov = (cs < ss + SEL_BLOCK) & (cs + CMP_BLOCK > ss) & (np.arange(n_seg)[None, :] < n_cmp)
    return jnp.asarray(ov.astype(np.float32), dtype=BF16)


def _layer(x, cos, sin, w_in, g_norm, pe_ck, pe_cv, w_ck1, w_ck2, w_cv1, w_cv2,
           g_out_moba, g_out_nsa, w_out, g_final, *, nsa_tq, tm_in, tm_out, blocks_per_tile):
    B, S, _ = x.shape
    T = B * S
    x2d = x.reshape(T, D_MODEL)
    w_perm, w_gate = _permute_w_in(w_in)
    proj, gates = _in_proj(x2d, g_norm.reshape(1, D_MODEL), w_perm, w_gate, cos, sin,
                           tm=tm_in, blocks_per_tile=blocks_per_tile)

    n_seg = S // CMP_STRIDE
    seg = proj[BLK_NKC:BLK_NKC + 4].reshape(2, NSA_GROUPS * B * n_seg, CMP_STRIDE * HEAD_DIM)
    pe = jnp.stack([pe_ck.reshape(1, -1), pe_cv.reshape(1, -1)])
    w1 = jnp.stack([w_ck1, w_cv1]).astype(BF16)
    w2 = jnp.stack([w_ck2, w_cv2]).astype(BF16)
    kvc = _compress(seg, pe, w1, w2)

    o_moba = _moba(proj, _block_onehot(S, MOBA_BLOCK), B=B, S=S)
    n_cmp = n_seg - CMP_BLOCK // CMP_STRIDE + 1
    q_rot, bias, o_c = _nsa_select(proj, cos, sin, kvc, _overlap_t(n_seg, n_cmp, S // SEL_BLOCK),
                                   B=B, S=S, tq=nsa_tq)
    o_s = _nsa_sel(q_rot, bias, proj, _block_onehot(S, SEL_BLOCK), B=B, S=S, tq=nsa_tq)
    o_nsa = _nsa_win(q_rot, proj, o_c, o_s, gates, B=B, S=S, tq=nsa_tq)
    out = _out_proj(o_moba, o_nsa, proj,
                    g_out_moba.reshape(MOBA_HEADS, 1, LANES), g_out_nsa.reshape(NSA_HEADS, 1, LANES),
                    x2d, w_out.astype(BF16), g_final.reshape(1, D_MODEL), tm=tm_out)
    return out.reshape(B, S, D_MODEL)


def kernel(x, positions, w_in, g_norm, pe_ck, pe_cv, w_ck1, w_ck2, w_cv1, w_cv2,
           g_out_moba, g_out_nsa, w_out, g_final):
    assert w_in.shape[0] == 1, "single-layer problem"
    cos, sin = _rope_tables(positions)
    return _layer(x, cos, sin, w_in[0], g_norm[0], pe_ck[0], pe_cv[0], w_ck1[0], w_ck2[0],
                  w_cv1[0], w_cv2[0], g_out_moba[0], g_out_nsa[0], w_out[0], g_final,
                  nsa_tq=256, tm_in=512, tm_out=256, blocks_per_tile=4)
```

```python
import functools

import numpy as np
import jax
import jax.numpy as jnp
from jax import lax
from jax.experimental import pallas as pl
from jax.experimental.pallas import tpu as pltpu

F32 = jnp.float32
BF16 = jnp.bfloat16

D_MODEL = 2048
HEAD_DIM = 128
MOBA_HEADS = 8
NSA_HEADS = 8
NSA_GROUPS = 2
NSA_REP = 4
MOBA_BLOCK = 256
MOBA_TOPK = 3
CMP_BLOCK = 32
CMP_STRIDE = 16
CMP_HIDDEN = 256
SEL_BLOCK = 64
SEL_TOPK = 8
WINDOW = 512
ROPE_THETA = 10000.0
EPS = 1e-6
SCALE = HEAD_DIM ** -0.5
NEG_BIG = -(2.0 ** 100)

LANES = 128
VMEM_LIMIT = 48 * 1024 * 1024
VMEM_LIMIT_BIG = 58 * 1024 * 1024

BLK_MQ, BLK_MK, BLK_MV, BLK_MZ, BLK_NQ, BLK_NZ = 0, 8, 16, 24, 32, 40
BLK_NKS, BLK_NKW, BLK_NKC, BLK_NVC, BLK_NVS, BLK_NVW = 48, 50, 52, 54, 56, 58
N_BLOCKS = 60
ROPE_BLOCKS = tuple(range(0, 16)) + (48, 49, 50, 51)


def _nt_dot(a, b):
    return lax.dot_general(a, b, (((1,), (1,)), ((), ())), preferred_element_type=F32)


def _sigmoid(x):
    return 1.0 / (1.0 + jnp.exp(-x))


def _rope(a, cos, sin_signed):
    return a * cos + pltpu.roll(a, HEAD_DIM // 2, axis=a.ndim - 1) * sin_signed


def _rope_table_kernel(pos_ref, invf_ref, sign_ref, cos_ref, sin_ref):
    ang = pos_ref[...].astype(F32) * invf_ref[...]
    cos_ref[...] = jnp.cos(ang)
    sin_ref[...] = jnp.sin(ang) * sign_ref[...]


def _rope_tables(positions):
    T = positions.size
    tile = min(T, 2048)
    half = HEAD_DIM // 2
    inv_freq = 1.0 / (ROPE_THETA ** (jnp.arange(0, HEAD_DIM, 2, dtype=F32) / HEAD_DIM))
    invf = jnp.concatenate([inv_freq, inv_freq]).reshape(1, HEAD_DIM)
    sign = jnp.concatenate([-jnp.ones((half,), F32), jnp.ones((half,), F32)]).reshape(1, HEAD_DIM)
    return pl.pallas_call(
        _rope_table_kernel,
        grid=(T // tile,),
        in_specs=[pl.BlockSpec((tile, 1), lambda i: (i, 0)),
                  pl.BlockSpec((1, HEAD_DIM), lambda i: (0, 0)),
                  pl.BlockSpec((1, HEAD_DIM), lambda i: (0, 0))],
        out_specs=[pl.BlockSpec((tile, HEAD_DIM), lambda i: (i, 0)),
                   pl.BlockSpec((tile, HEAD_DIM), lambda i: (i, 0))],
        out_shape=[jax.ShapeDtypeStruct((T, HEAD_DIM), F32)] * 2,
        name="rope_tables",
    )(positions.reshape(T, 1), invf, sign)


def _in_proj_kernel(x_ref, g_ref, w_ref, wg_ref, cos_ref, sin_ref, out_ref, gate_ref, h_scr,
                    *, rope_tiles, blocks_per_tile):
    j = pl.program_id(1)

    @pl.when(j == 0)
    def _():
        def norm_rows(r, carry):
            rows = pl.ds(pl.multiple_of(r * NORM_ROWS, NORM_ROWS), NORM_ROWS)
            x = x_ref[rows, :]
            ms = jnp.mean(x * x, axis=-1, keepdims=True)
            h_scr[rows, :] = (x * lax.rsqrt(ms + EPS) * g_ref[...]).astype(BF16)
            return carry

        lax.fori_loop(0, x_ref.shape[0] // NORM_ROWS, norm_rows, 0, unroll=8)
        gates = jnp.dot(h_scr[...], wg_ref[...], preferred_element_type=F32)
        for g in range(NSA_GROUPS):
            gate_ref[g] = gates[:, g * LANES:(g + 1) * LANES]

    is_rope = functools.reduce(jnp.logical_or, [j == t for t in rope_tiles])

    @pl.when(is_rope)
    def _():
        acc = jnp.dot(h_scr[...], w_ref[...], preferred_element_type=F32)
        cos = cos_ref[...]
        sin = sin_ref[...]
        for c in range(blocks_per_tile):
            out_ref[c] = _rope(acc[:, c * LANES:(c + 1) * LANES], cos, sin).astype(BF16)

    @pl.when(jnp.logical_not(is_rope))
    def _():
        acc = jnp.dot(h_scr[...], w_ref[...], preferred_element_type=F32)
        for c in range(blocks_per_tile):
            out_ref[c] = acc[:, c * LANES:(c + 1) * LANES].astype(BF16)


def _in_proj(x2d, g_norm, w_perm, w_gate, cos, sin, *, tm, blocks_per_tile):
    T = x2d.shape[0]
    tn = blocks_per_tile * LANES
    n_tiles = N_BLOCKS // blocks_per_tile
    w_tiles = w_perm.reshape(D_MODEL, n_tiles, tn).transpose(1, 0, 2)
    rope_tiles = tuple(sorted({b // blocks_per_tile for b in ROPE_BLOCKS}))
    assert all((t * blocks_per_tile + c) in ROPE_BLOCKS
               for t in rope_tiles for c in range(blocks_per_tile))
    kern = functools.partial(_in_proj_kernel, rope_tiles=rope_tiles,
                             blocks_per_tile=blocks_per_tile)
    return pl.pallas_call(
        kern,
        grid=(T // tm, n_tiles),
        in_specs=[pl.BlockSpec((tm, D_MODEL), lambda i, j: (i, 0)),
                  pl.BlockSpec((1, D_MODEL), lambda i, j: (0, 0)),
                  pl.BlockSpec((None, D_MODEL, tn), lambda i, j: (j, 0, 0)),
                  pl.BlockSpec((D_MODEL, NSA_GROUPS * LANES), lambda i, j: (0, 0)),
                  pl.BlockSpec((tm, HEAD_DIM), lambda i, j: (i, 0)),
                  pl.BlockSpec((tm, HEAD_DIM), lambda i, j: (i, 0))],
        out_specs=[pl.BlockSpec((blocks_per_tile, tm, LANES), lambda i, j: (j, i, 0)),
                   pl.BlockSpec((NSA_GROUPS, tm, LANES), lambda i, j: (0, i, 0))],
        out_shape=[jax.ShapeDtypeStruct((N_BLOCKS, T, LANES), BF16),
                   jax.ShapeDtypeStruct((NSA_GROUPS, T, LANES), F32)],
        scratch_shapes=[pltpu.VMEM((tm, D_MODEL), BF16)],
        compiler_params=pltpu.CompilerParams(
            dimension_semantics=("parallel", "arbitrary"), vmem_limit_bytes=VMEM_LIMIT),
        name="in_proj",
    )(x2d, g_norm, w_tiles, w_gate, cos, sin)


def _compress_kernel(seg_ref, pe_ref, w1_ref, w2_ref, out_ref):
    half = CMP_STRIDE * HEAD_DIM
    seg = seg_ref[0].astype(F32)
    pe = pe_ref[0]
    top = (seg + pe[:, :half]).astype(BF16)
    bot = (seg + pe[:, half:]).astype(BF16)
    a = jnp.dot(top, w1_ref[0, :half, :], preferred_element_type=F32)
    b = jnp.dot(bot, w1_ref[0, half:, :], preferred_element_type=F32)
    rows = a.shape[0]
    h = a + pltpu.roll(b, rows - 1, axis=0)
    hid = h * _sigmoid(h)
    out_ref[0] = jnp.dot(hid.astype(BF16), w2_ref[0], preferred_element_type=F32).astype(BF16)


def _compress(seg, pe, w1, w2):
    _, R, half = seg.shape
    return pl.pallas_call(
        _compress_kernel,
        grid=(2,),
        in_specs=[pl.BlockSpec((1, R, half), lambda c: (c, 0, 0)),
                  pl.BlockSpec((1, 1, 2 * half), lambda c: (c, 0, 0)),
                  pl.BlockSpec((1, 2 * half, CMP_HIDDEN), lambda c: (c, 0, 0)),
                  pl.BlockSpec((1, CMP_HIDDEN, HEAD_DIM), lambda c: (c, 0, 0))],
        out_specs=pl.BlockSpec((1, R, HEAD_DIM), lambda c: (c, 0, 0)),
        out_shape=jax.ShapeDtypeStruct((2, R, HEAD_DIM), BF16),
        compiler_params=pltpu.CompilerParams(
            dimension_semantics=("arbitrary",), vmem_limit_bytes=VMEM_LIMIT),
        name="compress",
    )(seg, pe, w1, w2)


def _select_bias_t(score_t, n_rows, n_keep):
    jrow = lax.broadcasted_iota(jnp.int32, score_t.shape, 0)
    cnt = jnp.zeros(score_t.shape, jnp.int32)
    for jp in range(n_rows):
        row = score_t[jp:jp + 1, :]
        beats = (row > score_t) | ((row == score_t) & (jrow > jp))
        cnt = cnt + beats.astype(jnp.int32)
    return (cnt < n_keep) & (score_t > -jnp.inf)


def _bias_t(keep_t):
    return jnp.where(keep_t, 0.0, NEG_BIG).astype(F32)


def _bias_columns(bias_t):
    rows, q = bias_t.shape
    if rows < LANES:
        bias_t = jnp.concatenate([bias_t, jnp.zeros((LANES - rows, q), F32)], axis=0)
    return bias_t.T.astype(BF16)


ROW_CHUNK = 64
HEADS_PER_PASS = 2
NORM_ROWS = 16


def _softmax_static(s, p_ref, l_ref, *, n_rows, tq, width, q0, k0, window=None):
    n_tiles = width // LANES
    for r0 in range(0, n_rows, ROW_CHUNK):
        rows = slice(r0, r0 + ROW_CHUNK)
        qlo = q0 + r0 % tq
        qhi = qlo + ROW_CHUNK - 1
        kinds = []
        for t in range(n_tiles):
            klo = k0 + t * LANES
            khi = klo + LANES - 1
            none = klo > qhi or (window is not None and khi <= qlo - window)
            full = khi <= qlo and (window is None or klo > qhi - window)
            kinds.append("none" if none else "full" if full else "part")
        mx = None
        masked = {}
        for t, kind in enumerate(kinds):
            if kind == "none":
                continue
            x = s[rows, t * LANES:(t + 1) * LANES]
            if kind == "part":
                qpos = qlo + lax.broadcasted_iota(jnp.int32, x.shape, 0)
                kpos = k0 + t * LANES + lax.broadcasted_iota(jnp.int32, x.shape, 1)
                ok = kpos <= qpos
                if window is not None:
                    ok = ok & (kpos > qpos - window)
                x = jnp.where(ok, x, NEG_BIG)
                masked[t] = x
            mx = x if mx is None else jnp.maximum(mx, x)
        m = jnp.broadcast_to(jnp.max(mx, axis=-1, keepdims=True), mx.shape)
        l = jnp.zeros(mx.shape, F32)
        for t, kind in enumerate(kinds):
            cols = slice(t * LANES, (t + 1) * LANES)
            if kind == "none":
                p_ref[rows, cols] = jnp.zeros((ROW_CHUNK, LANES), BF16)
                continue
            x = masked[t] if kind == "part" else s[rows, cols]
            p = jnp.exp((x - m) * SCALE)
            l = l + p
            p_ref[rows, cols] = p.astype(BF16)
        l_ref[rows, :] = jnp.broadcast_to(jnp.sum(l, axis=-1, keepdims=True), l.shape)


def _moba_kernel(q_ref, k_ref, v_ref, onehot_ref, o_ref, kaug_ref, qaug_ref, kmean_ref,
                 p_scr, l_scr, *, n_blk, k_top):
    S = k_ref.shape[0]
    tq = MOBA_BLOCK
    kaug_ref[:, :HEAD_DIM] = k_ref[...]
    kaug_ref[:, HEAD_DIM:] = onehot_ref[...]
    kmean_ref[...] = jnp.zeros(kmean_ref.shape, F32)
    for j in range(n_blk):
        kb = k_ref[j * MOBA_BLOCK:(j + 1) * MOBA_BLOCK, :].astype(F32)
        kmean_ref[j:j + 1, :] = jnp.sum(kb, axis=0, keepdims=True) * (1.0 / MOBA_BLOCK)

    q = q_ref[...]
    gate_t = _nt_dot(kmean_ref[...].astype(BF16), q)
    jrow = lax.broadcasted_iota(jnp.int32, gate_t.shape, 0)
    own = lax.broadcasted_iota(jnp.int32, gate_t.shape, 1) // MOBA_BLOCK
    gate_t = jnp.where((jrow < own) & jnp.isfinite(gate_t), gate_t, -jnp.inf)
    bias_t = _bias_t(_select_bias_t(gate_t, n_blk, k_top) | (jrow == own))
    qaug_ref[:, :HEAD_DIM] = q
    for c in range(S // tq):
        qaug_ref[c * tq:(c + 1) * tq, HEAD_DIM:] = _bias_columns(bias_t[:, c * tq:(c + 1) * tq])

    for c in range(S // tq):
        w = (c + 1) * tq
        p_c, l_c = p_scr.at[c % 2], l_scr.at[c % 2]
        s = _nt_dot(qaug_ref[c * tq:(c + 1) * tq, :], kaug_ref[:w, :])
        _softmax_static(s, p_c, l_c, n_rows=tq, tq=tq, width=w, q0=c * tq, k0=0)
        o = jnp.dot(p_c[:, :w], v_ref[:w, :], preferred_element_type=F32) / l_c[...]
        o_ref[c * tq:(c + 1) * tq, :] = o.astype(BF16)


def _moba(proj, onehot, *, B, S):
    T = B * S
    tq = MOBA_BLOCK
    n_blk = S // MOBA_BLOCK
    k_top = min(MOBA_TOPK, n_blk - 1)
    nb8 = -(-n_blk // 8) * 8
    kern = functools.partial(_moba_kernel, n_blk=n_blk, k_top=k_top)
    head_spec = lambda blk: pl.BlockSpec((None, S, LANES), lambda b, h: (blk + h, b, 0))
    return pl.pallas_call(
        kern,
        grid=(B, MOBA_HEADS),
        in_specs=[head_spec(BLK_MQ), head_spec(BLK_MK), head_spec(BLK_MV),
                  pl.BlockSpec((S, LANES), lambda b, h: (0, 0))],
        out_specs=head_spec(0),
        out_shape=jax.ShapeDtypeStruct((MOBA_HEADS, T, LANES), BF16),
        scratch_shapes=[pltpu.VMEM((S, 2 * LANES), BF16), pltpu.VMEM((S, 2 * LANES), BF16),
                        pltpu.VMEM((nb8, HEAD_DIM), F32),
                        pltpu.VMEM((2, tq, S), BF16), pltpu.VMEM((2, tq, LANES), F32)],
        compiler_params=pltpu.CompilerParams(
            dimension_semantics=("parallel", "parallel"), vmem_limit_bytes=VMEM_LIMIT),
        name="moba_attn",
    )(proj, proj, proj, onehot)


def _nsa_select_kernel(q_ref, cos_ref, sin_ref, kc_ref, vc_ref, ovt_ref,
                       qrot_ref, bias_ref, oc_ref, *, n_cmp, n_sel_blk, n_top):
    i = pl.program_id(2)
    R, tq, _ = q_ref.shape
    q_raw = q_ref[...]
    qrot_ref[...] = _rope(q_raw.astype(F32), cos_ref[...][None], sin_ref[...][None]).astype(BF16)

    n_seg = kc_ref.shape[0]
    s_c = (_nt_dot(q_raw.reshape(R * tq, HEAD_DIM), kc_ref[...]) * SCALE).reshape(R, tq, n_seg)
    n_idx = lax.broadcasted_iota(jnp.int32, s_c.shape, 2)
    pos3 = i * tq + lax.broadcasted_iota(jnp.int32, s_c.shape, 1)
    m_c = (n_idx * CMP_STRIDE + CMP_BLOCK - 1 <= pos3) & (n_idx < n_cmp)
    s_c = jnp.where(m_c, s_c, -jnp.inf)
    mx = jnp.max(s_c, axis=-1, keepdims=True)
    mx = jnp.where(jnp.isfinite(mx), mx, 0.0)
    e_c = jnp.where(m_c, jnp.exp(s_c - mx), 0.0)
    p_c = e_c / jnp.maximum(jnp.sum(e_c, axis=-1, keepdims=True), 1e-30)
    o_c = jnp.dot(p_c.reshape(R * tq, n_seg).astype(BF16), vc_ref[...],
                  preferred_element_type=F32).reshape(R, tq, HEAD_DIM)
    oc_ref[...] = o_c.astype(BF16)

    p_sum = jnp.sum(p_c, axis=0)
    p_hi = p_sum.astype(BF16)
    p_lo = (p_sum - p_hi.astype(F32)).astype(BF16)
    ovt = ovt_ref[...]
    imp_t = _nt_dot(ovt, p_hi) + _nt_dot(ovt, p_lo)
    jrow = lax.broadcasted_iota(jnp.int32, imp_t.shape, 0)
    posq = i * tq + lax.broadcasted_iota(jnp.int32, imp_t.shape, 1)
    own = posq // SEL_BLOCK
    forced = (jrow == 0) | (jrow == own) | (jrow == own - 1)
    future = jrow * SEL_BLOCK > posq
    score_t = jnp.where(future, -jnp.inf, jnp.where(forced, jnp.inf, imp_t))
    keep_t = _select_bias_t(score_t, n_sel_blk, n_top)
    bias_ref[...] = _bias_columns(_bias_t(keep_t))


def _nsa_select(proj, cos, sin, kvc, ovt, *, B, S, tq):
    T = B * S
    nq = S // tq
    n_seg = S // CMP_STRIDE
    n_cmp = n_seg - CMP_BLOCK // CMP_STRIDE + 1
    n_sel_blk = S // SEL_BLOCK
    R = NSA_REP
    kern = functools.partial(_nsa_select_kernel, n_cmp=n_cmp, n_sel_blk=n_sel_blk,
                             n_top=min(SEL_TOPK, n_sel_blk))
    heads_spec = pl.BlockSpec((R, tq, LANES), lambda b, g, i: (g, b * nq + i, 0))
    return pl.pallas_call(
        kern,
        grid=(B, NSA_GROUPS, nq),
        in_specs=[pl.BlockSpec((R, tq, LANES), lambda b, g, i: (BLK_NQ // R + g, b * nq + i, 0)),
                  pl.BlockSpec((tq, LANES), lambda b, g, i: (b * nq + i, 0)),
                  pl.BlockSpec((tq, LANES), lambda b, g, i: (b * nq + i, 0)),
                  pl.BlockSpec((None, n_seg, LANES), lambda b, g, i: (0, g * B + b, 0)),
                  pl.BlockSpec((None, n_seg, LANES), lambda b, g, i: (1, g * B + b, 0)),
                  pl.BlockSpec(ovt.shape, lambda b, g, i: (0, 0))],
        out_specs=[heads_spec,
                   pl.BlockSpec((None, tq, LANES), lambda b, g, i: (g, b * nq + i, 0)),
                   heads_spec],
        out_shape=[jax.ShapeDtypeStruct((NSA_HEADS, T, LANES), BF16),
                   jax.ShapeDtypeStruct((NSA_GROUPS, T, LANES), BF16),
                   jax.ShapeDtypeStruct((NSA_HEADS, T, LANES), BF16)],
        compiler_params=pltpu.CompilerParams(
            dimension_semantics=("parallel", "parallel", "parallel"),
            vmem_limit_bytes=VMEM_LIMIT),
        name="nsa_select",
    )(proj, cos, sin, kvc, kvc, ovt)


def _nsa_sel_kernel(qrot_ref, bias_ref, ks_ref, vs_ref, onehot_ref, o_ref,
                    ksaug_ref, p_scr, l_scr, *, tq):
    R, S, _ = qrot_ref.shape
    ksaug_ref[:, :HEAD_DIM] = ks_ref[...]
    ksaug_ref[:, HEAD_DIM:] = onehot_ref[...]

    n_pass = 0
    for c in range(S // tq):
        w = (c + 1) * tq
        q_rows = slice(c * tq, (c + 1) * tq)
        bias = bias_ref[q_rows, :]
        for r0 in range(0, R, HEADS_PER_PASS):
            heads = range(r0, r0 + HEADS_PER_PASS)
            p_c, l_c = p_scr.at[n_pass % 2], l_scr.at[n_pass % 2]
            n_pass += 1
            q_aug = jnp.concatenate(
                [jnp.concatenate([qrot_ref[r, q_rows, :], bias], axis=1) for r in heads], axis=0)
            s = _nt_dot(q_aug, ksaug_ref[:w, :])
            _softmax_static(s, p_c, l_c, n_rows=HEADS_PER_PASS * tq, tq=tq, width=w,
                            q0=c * tq, k0=0)
            o = jnp.dot(p_c[:, :w], vs_ref[:w, :], preferred_element_type=F32) / l_c[...]
            for k, r in enumerate(heads):
                o_ref[r, q_rows, :] = o[k * tq:(k + 1) * tq, :].astype(BF16)


def _nsa_sel(q_rot, bias, proj, onehot, *, B, S, tq):
    T = B * S
    R = NSA_REP
    heads_spec = pl.BlockSpec((R, S, LANES), lambda b, g: (g, b, 0))
    kv_spec = lambda blk: pl.BlockSpec((None, S, LANES), lambda b, g: (blk + g, b, 0))
    return pl.pallas_call(
        functools.partial(_nsa_sel_kernel, tq=tq),
        grid=(B, NSA_GROUPS),
        in_specs=[heads_spec,
                  pl.BlockSpec((None, S, LANES), lambda b, g: (g, b, 0)),
                  kv_spec(BLK_NKS), kv_spec(BLK_NVS),
                  pl.BlockSpec((S, LANES), lambda b, g: (0, 0))],
        out_specs=heads_spec,
        out_shape=jax.ShapeDtypeStruct((NSA_HEADS, T, LANES), BF16),
        scratch_shapes=[pltpu.VMEM((S, 2 * LANES), BF16),
                        pltpu.VMEM((2, HEADS_PER_PASS * tq, S), BF16),
                        pltpu.VMEM((2, HEADS_PER_PASS * tq, LANES), F32)],
        compiler_params=pltpu.CompilerParams(
            dimension_semantics=("parallel", "parallel"), vmem_limit_bytes=VMEM_LIMIT_BIG),
        name="nsa_sel_attn",
    )(q_rot, bias, proj, proj, onehot)


def _nsa_win_kernel(qrot_ref, kw_ref, vw_ref, oc_ref, os_ref, gate_ref, o_ref,
                    p_scr, l_scr, *, tq):
    R, S, _ = qrot_ref.shape
    n_pass = 0
    for c in range(S // tq):
        q_rows = slice(c * tq, (c + 1) * tq)
        k0 = max(0, c * tq - WINDOW)
        w = (c + 1) * tq - k0
        gt = _sigmoid(gate_ref[q_rows, :])
        for r0 in range(0, R, HEADS_PER_PASS):
            heads = range(r0, r0 + HEADS_PER_PASS)
            p_c, l_c = p_scr.at[n_pass % 2], l_scr.at[n_pass % 2]
            n_pass += 1
            q2 = jnp.concatenate([qrot_ref[r, q_rows, :] for r in heads], axis=0)
            s = _nt_dot(q2, kw_ref[k0:k0 + w, :])
            _softmax_static(s, p_c, l_c, n_rows=HEADS_PER_PASS * tq, tq=tq, width=w,
                            q0=c * tq, k0=k0, window=WINDOW)
            o_w = jnp.dot(p_c[:, :w], vw_ref[k0:k0 + w, :],
                          preferred_element_type=F32) / l_c[...]
            for k, r in enumerate(heads):
                o = (gt[:, 3 * r:3 * r + 1] * oc_ref[r, q_rows, :].astype(F32)
                     + gt[:, 3 * r + 1:3 * r + 2] * os_ref[r, q_rows, :].astype(F32)
                     + gt[:, 3 * r + 2:3 * r + 3] * o_w[k * tq:(k + 1) * tq, :])
                o_ref[r, q_rows, :] = o.astype(BF16)


def _nsa_win(q_rot, proj, o_c, o_s, gates, *, B, S, tq):
    T = B * S
    R = NSA_REP
    wmax = WINDOW + tq
    heads_spec = pl.BlockSpec((R, S, LANES), lambda b, g: (g, b, 0))
    kv_spec = lambda blk: pl.BlockSpec((None, S, LANES), lambda b, g: (blk + g, b, 0))
    return pl.pallas_call(
        functools.partial(_nsa_win_kernel, tq=tq),
        grid=(B, NSA_GROUPS),
        in_specs=[heads_spec, kv_spec(BLK_NKW), kv_spec(BLK_NVW), heads_spec, heads_spec,
                  pl.BlockSpec((None, S, LANES), lambda b, g: (g, b, 0))],
        out_specs=heads_spec,
        out_shape=jax.ShapeDtypeStruct((NSA_HEADS, T, LANES), BF16),
        scratch_shapes=[pltpu.VMEM((2, HEADS_PER_PASS * tq, wmax), BF16),
                        pltpu.VMEM((2, HEADS_PER_PASS * tq, LANES), F32)],
        compiler_params=pltpu.CompilerParams(
            dimension_semantics=("parallel", "parallel"), vmem_limit_bytes=VMEM_LIMIT_BIG),
        name="nsa_win_attn",
    )(q_rot, proj, proj, o_c, o_s, gates)


def _out_proj_kernel(om_ref, on_ref, zm_ref, zn_ref, gm_ref, gn_ref, x_ref, w_ref, gf_ref,
                     out_ref, y_scr):
    def gated_norm(o_ref, z_ref, g_ref, col0):
        z = z_ref[...].astype(F32)
        a = o_ref[...].astype(F32) * (z * _sigmoid(z))
        ss = jnp.sum(jnp.sum(a * a, axis=0), axis=-1, keepdims=True)
        inv = lax.rsqrt(ss * (1.0 / (a.shape[0] * LANES)) + EPS)
        for h in range(a.shape[0]):
            y = a[h] * inv * g_ref[h]
            y_scr[:, col0 + h * LANES:col0 + (h + 1) * LANES] = y.astype(BF16)

    gated_norm(om_ref, zm_ref, gm_ref, 0)
    gated_norm(on_ref, zn_ref, gn_ref, MOBA_HEADS * LANES)
    r = x_ref[...] + jnp.dot(y_scr[...], w_ref[...], preferred_element_type=F32)
    ms = jnp.mean(r * r, axis=-1, keepdims=True)
    out_ref[...] = r * lax.rsqrt(ms + EPS) * gf_ref[...]


def _out_proj(o_moba, o_nsa, proj, g_moba, g_nsa, x2d, w_out, g_final, *, tm):
    T = x2d.shape[0]
    H = MOBA_HEADS
    return pl.pallas_call(
        _out_proj_kernel,
        grid=(T // tm,),
        in_specs=[pl.BlockSpec((H, tm, LANES), lambda i: (0, i, 0)),
                  pl.BlockSpec((H, tm, LANES), lambda i: (0, i, 0)),
                  pl.BlockSpec((H, tm, LANES), lambda i: (BLK_MZ // H, i, 0)),
                  pl.BlockSpec((H, tm, LANES), lambda i: (BLK_NZ // H, i, 0)),
                  pl.BlockSpec((H, 1, LANES), lambda i: (0, 0, 0)),
                  pl.BlockSpec((H, 1, LANES), lambda i: (0, 0, 0)),
                  pl.BlockSpec((tm, D_MODEL), lambda i: (i, 0)),
                  pl.BlockSpec((D_MODEL, D_MODEL), lambda i: (0, 0)),
                  pl.BlockSpec((1, D_MODEL), lambda i: (0, 0))],
        out_specs=pl.BlockSpec((tm, D_MODEL), lambda i: (i, 0)),
        out_shape=jax.ShapeDtypeStruct((T, D_MODEL), F32),
        scratch_shapes=[pltpu.VMEM((tm, D_MODEL), BF16)],
        compiler_params=pltpu.CompilerParams(
            dimension_semantics=("parallel",), vmem_limit_bytes=VMEM_LIMIT),
        name="out_proj",
    )(o_moba, o_nsa, proj, proj, g_moba, g_nsa, x2d, w_out, g_final)


def _permute_w_in(w_in):
    mw, nw, kw = MOBA_HEADS * HEAD_DIM, NSA_HEADS * HEAD_DIM, NSA_GROUPS * HEAD_DIM
    sizes = [mw] * 4 + [nw] + [kw] * 6 + [3 * NSA_HEADS, nw]
    offs = np.concatenate([[0], np.cumsum(sizes)])
    names = ["mq", "mk", "mv", "mz", "nq", "nkc", "nvc", "nks", "nvs", "nkw", "nvw", "ng", "nz"]
    part = {n: w_in[:, int(offs[k]):int(offs[k + 1])] for k, n in enumerate(names)}
    order = ["mq", "mk", "mv", "mz", "nq", "nz", "nks", "nkw", "nkc", "nvc", "nvs", "nvw"]
    w_perm = jnp.concatenate([part[n] for n in order], axis=1).astype(BF16)
    per_group = 3 * NSA_REP
    wg = part["ng"].reshape(D_MODEL, NSA_GROUPS, per_group)
    wg = jnp.pad(wg, ((0, 0), (0, 0), (0, LANES - per_group)))
    return w_perm, wg.reshape(D_MODEL, NSA_GROUPS * LANES).astype(BF16)


def _block_onehot(S, block):
    ids = np.arange(S)[:, None] // block
    return jnp.asarray((ids == np.arange(LANES)[None, :]).astype(np.float32), dtype=BF16)


def _overlap_t(n_seg, n_cmp, n_sel_blk):
    cs = np.arange(n_seg)[None, :] * CMP_STRIDE
    ss = np.arange(n_sel_blk)[:, None] * SEL_BLOCK
    ov = (cs < ss + SEL_BLOCK) & (cs + CMP_BLOCK > ss) & (np.arange(n_seg)[None, :] < n_cmp)
    return jnp.asarray(ov.astype(np.float32), dtype=BF16)


def _layer(x, cos, sin, w_in, g_norm, pe_ck, pe_cv, w_ck1, w_ck2, w_cv1, w_cv2,
           g_out_moba, g_out_nsa, w_out, g_final, *, nsa_tq, tm_in, tm_out, blocks_per_tile):
    B, S, _ = x.shape
    T = B * S
    x2d = x.reshape(T, D_MODEL)
    w_perm, w_gate = _permute_w_in(w_in)
    proj, gates = _in_proj(x2d, g_norm.reshape(1, D_MODEL), w_perm, w_gate, cos, sin,
                           tm=tm_in, blocks_per_tile=blocks_per_tile)

    n_seg = S // CMP_STRIDE
    seg = proj[BLK_NKC:BLK_NKC + 4].reshape(2, NSA_GROUPS * B * n_seg, CMP_STRIDE * HEAD_DIM)
    pe = jnp.stack([pe_ck.reshape(1, -1), pe_cv.reshape(1, -1)])
    w1 = jnp.stack([w_ck1, w_cv1]).astype(BF16)
    w2 = jnp.stack([w_ck2, w_cv2]).astype(BF16)
    kvc = _compress(seg, pe, w1, w2)

    o_moba = _moba(proj, _block_onehot(S, MOBA_BLOCK), B=B, S=S)
    n_cmp = n_seg - CMP_BLOCK // CMP_STRIDE + 1
    q_rot, bias, o_c = _nsa_select(proj, cos, sin, kvc, _overlap_t(n_seg, n_cmp, S // SEL_BLOCK),
                                   B=B, S=S, tq=nsa_tq)
    o_s = _nsa_sel(q_rot, bias, proj, _block_onehot(S, SEL_BLOCK), B=B, S=S, tq=nsa_tq)
    o_nsa = _nsa_win(q_rot, proj, o_c, o_s, gates, B=B, S=S, tq=nsa_tq)
    out = _out_proj(o_moba, o_nsa, proj,
                    g_out_moba.reshape(MOBA_HEADS, 1, LANES), g_out_nsa.reshape(NSA_HEADS, 1, LANES),
                    x2d, w_out.astype(BF16), g_final.reshape(1, D_MODEL), tm=tm_out)
    return out.reshape(B, S, D_MODEL)


def kernel(x, positions, w_in, g_norm, pe_ck, pe_cv, w_ck1, w_ck2, w_cv1, w_cv2,
           g_out_moba, g_out_nsa, w_out, g_final):
    assert w_in.shape[0] == 1, "single-layer problem"
    cos, sin = _rope_tables(positions)
    return _layer(x, cos, sin, w_in[0], g_norm[0], pe_ck[0], pe_cv[0], w_ck1[0], w_ck2[0],
                  w_cv1[0], w_cv2[0], g_out_moba[0], g_out_nsa[0], w_out[0], g_final,
                  nsa_tq=256, tm_in=1024, tm_out=256, blocks_per_tile=4)
```

```python
import functools

import numpy as np
import jax
import jax.numpy as jnp
from jax import lax
from jax.experimental import pallas as pl
from jax.experimental.pallas import tpu as pltpu

F32 = jnp.float32
BF16 = jnp.bfloat16

D_MODEL = 2048
HEAD_DIM = 128
MOBA_HEADS = 8
NSA_HEADS = 8
NSA_GROUPS = 2
NSA_REP = 4
MOBA_BLOCK = 256
MOBA_TOPK = 3
CMP_BLOCK = 32
CMP_STRIDE = 16
CMP_HIDDEN = 256
SEL_BLOCK = 64
SEL_TOPK = 8
WINDOW = 512
ROPE_THETA = 10000.0
EPS = 1e-6
SCALE = HEAD_DIM ** -0.5
QK_PRESCALE = SCALE * float(np.log2(np.e))
NEG_BIG = -(2.0 ** 100)

LANES = 128
VMEM_LIMIT = 48 * 1024 * 1024
VMEM_LIMIT_BIG = 58 * 1024 * 1024

BLK_MQ, BLK_MK, BLK_MV, BLK_MZ, BLK_NQ, BLK_NZ = 0, 8, 16, 24, 32, 40
BLK_NKS, BLK_NKW, BLK_NKC, BLK_NVC, BLK_NVS, BLK_NVW = 48, 50, 52, 54, 56, 58
N_BLOCKS = 60
ROPE_BLOCKS = tuple(range(0, 16)) + (48, 49, 50, 51)


def _nt_dot(a, b):
    return lax.dot_general(a, b, (((1,), (1,)), ((), ())), preferred_element_type=F32)


def _sigmoid(x):
    return 1.0 / (1.0 + jnp.exp(-x))


def _rope(a, cos, sin_signed):
    return a * cos + pltpu.roll(a, HEAD_DIM // 2, axis=a.ndim - 1) * sin_signed


def _rope_table_kernel(pos_ref, invf_ref, sign_ref, cos_ref, sin_ref):
    ang = pos_ref[...].astype(F32) * invf_ref[...]
    cos_ref[...] = jnp.cos(ang)
    sin_ref[...] = jnp.sin(ang) * sign_ref[...]


def _rope_tables(positions):
    T = positions.size
    tile = min(T, 2048)
    half = HEAD_DIM // 2
    inv_freq = 1.0 / (ROPE_THETA ** (jnp.arange(0, HEAD_DIM, 2, dtype=F32) / HEAD_DIM))
    invf = jnp.concatenate([inv_freq, inv_freq]).reshape(1, HEAD_DIM)
    sign = jnp.concatenate([-jnp.ones((half,), F32), jnp.ones((half,), F32)]).reshape(1, HEAD_DIM)
    return pl.pallas_call(
        _rope_table_kernel,
        grid=(T // tile,),
        in_specs=[pl.BlockSpec((tile, 1), lambda i: (i, 0)),
                  pl.BlockSpec((1, HEAD_DIM), lambda i: (0, 0)),
                  pl.BlockSpec((1, HEAD_DIM), lambda i: (0, 0))],
        out_specs=[pl.BlockSpec((tile, HEAD_DIM), lambda i: (i, 0)),
                   pl.BlockSpec((tile, HEAD_DIM), lambda i: (i, 0))],
        out_shape=[jax.ShapeDtypeStruct((T, HEAD_DIM), F32)] * 2,
        name="rope_tables",
    )(positions.reshape(T, 1), invf, sign)


def _in_proj_kernel(x_ref, g_ref, w_ref, wg_ref, cos_ref, sin_ref, out_ref, gate_ref, h_scr,
                    *, rope_tiles, blocks_per_tile):
    j = pl.program_id(1)

    @pl.when(j == 0)
    def _():
        def norm_rows(r, carry):
            rows = pl.ds(pl.multiple_of(r * NORM_ROWS, NORM_ROWS), NORM_ROWS)
            x = x_ref[rows, :]
            ms = jnp.mean(x * x, axis=-1, keepdims=True)
            h_scr[rows, :] = (x * lax.rsqrt(ms + EPS) * g_ref[...]).astype(BF16)
            return carry

        lax.fori_loop(0, x_ref.shape[0] // NORM_ROWS, norm_rows, 0, unroll=8)
        gates = jnp.dot(h_scr[...], wg_ref[...], preferred_element_type=F32)
        for g in range(NSA_GROUPS):
            gate_ref[g] = gates[:, g * LANES:(g + 1) * LANES]

    is_rope = functools.reduce(jnp.logical_or, [j == t for t in rope_tiles])

    @pl.when(is_rope)
    def _():
        acc = jnp.dot(h_scr[...], w_ref[...], preferred_element_type=F32)
        f = jnp.where(j < BLK_MK // blocks_per_tile, QK_PRESCALE, 1.0).astype(F32)
        cos = cos_ref[...] * f
        sin = sin_ref[...] * f
        for c in range(blocks_per_tile):
            out_ref[c] = _rope(acc[:, c * LANES:(c + 1) * LANES], cos, sin).astype(BF16)

    @pl.when(jnp.logical_not(is_rope))
    def _():
        acc = jnp.dot(h_scr[...], w_ref[...], preferred_element_type=F32)
        for c in range(blocks_per_tile):
            out_ref[c] = acc[:, c * LANES:(c + 1) * LANES].astype(BF16)


def _in_proj(x2d, g_norm, w_perm, w_gate, cos, sin, *, tm, blocks_per_tile):
    T = x2d.shape[0]
    tn = blocks_per_tile * LANES
    n_tiles = N_BLOCKS // blocks_per_tile
    w_tiles = w_perm.reshape(D_MODEL, n_tiles, tn).transpose(1, 0, 2)
    rope_tiles = tuple(sorted({b // blocks_per_tile for b in ROPE_BLOCKS}))
    assert all((t * blocks_per_tile + c) in ROPE_BLOCKS
               for t in rope_tiles for c in range(blocks_per_tile))
    kern = functools.partial(_in_proj_kernel, rope_tiles=rope_tiles,
                             blocks_per_tile=blocks_per_tile)
    return pl.pallas_call(
        kern,
        grid=(T // tm, n_tiles),
        in_specs=[pl.BlockSpec((tm, D_MODEL), lambda i, j: (i, 0)),
                  pl.BlockSpec((1, D_MODEL), lambda i, j: (0, 0)),
                  pl.BlockSpec((None, D_MODEL, tn), lambda i, j: (j, 0, 0)),
                  pl.BlockSpec((D_MODEL, NSA_GROUPS * LANES), lambda i, j: (0, 0)),
                  pl.BlockSpec((tm, HEAD_DIM), lambda i, j: (i, 0)),
                  pl.BlockSpec((tm, HEAD_DIM), lambda i, j: (i, 0))],
        out_specs=[pl.BlockSpec((blocks_per_tile, tm, LANES), lambda i, j: (j, i, 0)),
                   pl.BlockSpec((NSA_GROUPS, tm, LANES), lambda i, j: (0, i, 0))],
        out_shape=[jax.ShapeDtypeStruct((N_BLOCKS, T, LANES), BF16),
                   jax.ShapeDtypeStruct((NSA_GROUPS, T, LANES), F32)],
        scratch_shapes=[pltpu.VMEM((tm, D_MODEL), BF16)],
        compiler_params=pltpu.CompilerParams(
            dimension_semantics=("parallel", "arbitrary"), vmem_limit_bytes=VMEM_LIMIT),
        name="in_proj",
    )(x2d, g_norm, w_tiles, w_gate, cos, sin)


def _compress_kernel(seg_ref, pe_ref, w1_ref, w2_ref, out_ref):
    half = CMP_STRIDE * HEAD_DIM
    seg = seg_ref[0].astype(F32)
    pe = pe_ref[0]
    top = (seg + pe[:, :half]).astype(BF16)
    bot = (seg + pe[:, half:]).astype(BF16)
    a = jnp.dot(top, w1_ref[0, :half, :], preferred_element_type=F32)
    b = jnp.dot(bot, w1_ref[0, half:, :], preferred_element_type=F32)
    rows = a.shape[0]
    h = a + pltpu.roll(b, rows - 1, axis=0)
    hid = h * _sigmoid(h)
    out_ref[0] = jnp.dot(hid.astype(BF16), w2_ref[0], preferred_element_type=F32).astype(BF16)


def _compress(seg, pe, w1, w2):
    _, R, half = seg.shape
    return pl.pallas_call(
        _compress_kernel,
        grid=(2,),
        in_specs=[pl.BlockSpec((1, R, half), lambda c: (c, 0, 0)),
                  pl.BlockSpec((1, 1, 2 * half), lambda c: (c, 0, 0)),
                  pl.BlockSpec((1, 2 * half, CMP_HIDDEN), lambda c: (c, 0, 0)),
                  pl.BlockSpec((1, CMP_HIDDEN, HEAD_DIM), lambda c: (c, 0, 0))],
        out_specs=pl.BlockSpec((1, R, HEAD_DIM), lambda c: (c, 0, 0)),
        out_shape=jax.ShapeDtypeStruct((2, R, HEAD_DIM), BF16),
        compiler_params=pltpu.CompilerParams(
            dimension_semantics=("arbitrary",), vmem_limit_bytes=VMEM_LIMIT),
        name="compress",
    )(seg, pe, w1, w2)


def _select_bias_t(score_t, n_rows, n_keep):
    sub = 8
    n_groups = score_t.shape[0] // sub
    groups = [score_t[g * sub:(g + 1) * sub, :] for g in range(n_groups)]
    jrow = lax.broadcasted_iota(jnp.int32, groups[0].shape, 0)
    cnts = [jnp.zeros(groups[0].shape, jnp.int32) for _ in range(n_groups)]
    for jp in range(n_rows):
        row = score_t[jp:jp + 1, :]
        for g, grp in enumerate(groups):
            if g * sub > jp:
                beats = row >= grp
            elif g * sub + sub - 1 <= jp:
                beats = row > grp
            else:
                beats = (row > grp) | ((row == grp) & (jrow + g * sub > jp))
            cnts[g] = cnts[g] + beats.astype(jnp.int32)
    cnt = cnts[0] if n_groups == 1 else jnp.concatenate(cnts, axis=0)
    return (cnt < n_keep) & (score_t > -jnp.inf)


def _bias_t(keep_t):
    return jnp.where(keep_t, 0.0, NEG_BIG).astype(F32)


def _bias_columns(bias_t):
    rows, q = bias_t.shape
    if rows < LANES:
        bias_t = jnp.concatenate([bias_t, jnp.zeros((LANES - rows, q), F32)], axis=0)
    return bias_t.T.astype(BF16)


ROW_CHUNK = 64
HEADS_PER_PASS = 2
NORM_ROWS = 16

def _softmax_static(s, p_ref, l_ref, *, n_rows, tq, width, q0, k0, window=None):
    n_tiles = width // LANES
    for r0 in range(0, n_rows, ROW_CHUNK):
        rows = slice(r0, r0 + ROW_CHUNK)
        qlo = q0 + r0 % tq
        qhi = qlo + ROW_CHUNK - 1
        kinds = []
        for t in range(n_tiles):
            klo = k0 + t * LANES
            khi = klo + LANES - 1
            none = klo > qhi or (window is not None and khi <= qlo - window)
            full = khi <= qlo and (window is None or klo > qhi - window)
            kinds.append("none" if none else "full" if full else "part")
        mx = None
        masked = {}
        for t, kind in enumerate(kinds):
            if kind == "none":
                continue
            x = s[rows, t * LANES:(t + 1) * LANES]
            if kind == "part":
                qpos = qlo + lax.broadcasted_iota(jnp.int32, x.shape, 0)
                kpos = k0 + t * LANES + lax.broadcasted_iota(jnp.int32, x.shape, 1)
                ok = kpos <= qpos
                if window is not None:
                    ok = ok & (kpos > qpos - window)
                x = jnp.where(ok, x, NEG_BIG)
                masked[t] = x
            mx = x if mx is None else jnp.maximum(mx, x)
        m = jnp.broadcast_to(jnp.max(mx, axis=-1, keepdims=True), mx.shape)
        l = jnp.zeros(mx.shape, F32)
        for t, kind in enumerate(kinds):
            cols = slice(t * LANES, (t + 1) * LANES)
            if kind == "none":
                p_ref[rows, cols] = jnp.zeros((ROW_CHUNK, LANES), BF16)
                continue
            x = masked[t] if kind == "part" else s[rows, cols]
            p = jnp.exp2(x - m)
            l = l + p
            p_ref[rows, cols] = p.astype(BF16)
        l_ref[rows, :] = jnp.broadcast_to(jnp.sum(l, axis=-1, keepdims=True), l.shape)


def _moba_kernel(q_ref, k_ref, v_ref, onehot_ref, o_ref, kaug_ref, qaug_ref, kmean_ref,
                 p_scr, l_scr, *, n_blk, k_top):
    S = k_ref.shape[0]
    tq = MOBA_BLOCK
    kaug_ref[:, :HEAD_DIM] = k_ref[...]
    kaug_ref[:, HEAD_DIM:] = onehot_ref[...]
    kmean_ref[...] = jnp.zeros(kmean_ref.shape, F32)
    for j in range(n_blk):
        kb = k_ref[j * MOBA_BLOCK:(j + 1) * MOBA_BLOCK, :].astype(F32)
        kmean_ref[j:j + 1, :] = jnp.sum(kb, axis=0, keepdims=True) * (1.0 / MOBA_BLOCK)

    q = q_ref[...]
    gate_t = _nt_dot(kmean_ref[...].astype(BF16), q)
    jrow = lax.broadcasted_iota(jnp.int32, gate_t.shape, 0)
    own = lax.broadcasted_iota(jnp.int32, gate_t.shape, 1) // MOBA_BLOCK
    gate_t = jnp.where((jrow < own) & jnp.isfinite(gate_t), gate_t, -jnp.inf)
    bias_t = _bias_t(_select_bias_t(gate_t, n_blk, k_top) | (jrow == own))
    qaug_ref[:, :HEAD_DIM] = q
    for c in range(S // tq):
        qaug_ref[c * tq:(c + 1) * tq, HEAD_DIM:] = _bias_columns(bias_t[:, c * tq:(c + 1) * tq])

    for c in range(S // tq):
        w = (c + 1) * tq
        p_c, l_c = p_scr.at[c % 2], l_scr.at[c % 2]
        s = _nt_dot(qaug_ref[c * tq:(c + 1) * tq, :], kaug_ref[:w, :])
        _softmax_static(s, p_c, l_c, n_rows=tq, tq=tq, width=w, q0=c * tq, k0=0)
        o = jnp.dot(p_c[:, :w], v_ref[:w, :], preferred_element_type=F32) / l_c[...]
        o_ref[c * tq:(c + 1) * tq, :] = o.astype(BF16)


def _moba(proj, onehot, *, B, S):
    T = B * S
    tq = MOBA_BLOCK
    n_blk = S // MOBA_BLOCK
    k_top = min(MOBA_TOPK, n_blk - 1)
    nb8 = -(-n_blk // 8) * 8
    kern = functools.partial(_moba_kernel, n_blk=n_blk, k_top=k_top)
    head_spec = lambda blk: pl.BlockSpec((None, S, LANES), lambda b, h: (blk + h, b, 0))
    return pl.pallas_call(
        kern,
        grid=(B, MOBA_HEADS),
        in_specs=[head_spec(BLK_MQ), head_spec(BLK_MK), head_spec(BLK_MV),
                  pl.BlockSpec((S, LANES), lambda b, h: (0, 0))],
        out_specs=head_spec(0),
        out_shape=jax.ShapeDtypeStruct((MOBA_HEADS, T, LANES), BF16),
        scratch_shapes=[pltpu.VMEM((S, 2 * LANES), BF16), pltpu.VMEM((S, 2 * LANES), BF16),
                        pltpu.VMEM((nb8, HEAD_DIM), F32),
                        pltpu.VMEM((2, tq, S), BF16), pltpu.VMEM((2, tq, LANES), F32)],
        compiler_params=pltpu.CompilerParams(
            dimension_semantics=("parallel", "parallel"), vmem_limit_bytes=VMEM_LIMIT),
        name="moba_attn",
    )(proj, proj, proj, onehot)


def _nsa_select_kernel(q_ref, cos_ref, sin_ref, kc_ref, vc_ref, ovt_ref,
                       qrot_ref, bias_ref, oc_ref, *, n_cmp, n_sel_blk, n_top):
    i = pl.program_id(2)
    R, tq, _ = q_ref.shape
    q_raw = q_ref[...]
    cos = (cos_ref[...] * QK_PRESCALE)[None]
    sin = (sin_ref[...] * QK_PRESCALE)[None]
    qrot_ref[...] = _rope(q_raw.astype(F32), cos, sin).astype(BF16)

    n_seg = kc_ref.shape[0]
    s_c = (_nt_dot(q_raw.reshape(R * tq, HEAD_DIM), kc_ref[...]) * SCALE).reshape(R, tq, n_seg)
    n_idx = lax.broadcasted_iota(jnp.int32, s_c.shape, 2)
    pos3 = i * tq + lax.broadcasted_iota(jnp.int32, s_c.shape, 1)
    m_c = (n_idx * CMP_STRIDE + CMP_BLOCK - 1 <= pos3) & (n_idx < n_cmp)
    s_c = jnp.where(m_c, s_c, -jnp.inf)
    mx = jnp.max(s_c, axis=-1, keepdims=True)
    mx = jnp.where(jnp.isfinite(mx), mx, 0.0)
    e_c = jnp.where(m_c, jnp.exp(s_c - mx), 0.0)
    p_c = e_c / jnp.maximum(jnp.sum(e_c, axis=-1, keepdims=True), 1e-30)
    o_c = jnp.dot(p_c.reshape(R * tq, n_seg).astype(BF16), vc_ref[...],
                  preferred_element_type=F32).reshape(R, tq, HEAD_DIM)
    oc_ref[...] = o_c.astype(BF16)

    p_sum = jnp.sum(p_c, axis=0)
    p_hi = p_sum.astype(BF16)
    p_lo = (p_sum - p_hi.astype(F32)).astype(BF16)
    ovt = ovt_ref[...]
    imp_t = _nt_dot(ovt, p_hi) + _nt_dot(ovt, p_lo)
    jrow = lax.broadcasted_iota(jnp.int32, imp_t.shape, 0)
    posq = i * tq + lax.broadcasted_iota(jnp.int32, imp_t.shape, 1)
    own = posq // SEL_BLOCK
    forced = (jrow == 0) | (jrow == own) | (jrow == own - 1)
    future = jrow * SEL_BLOCK > posq
    score_t = jnp.where(future, -jnp.inf, jnp.where(forced, jnp.inf, imp_t))
    keep_t = _select_bias_t(score_t, n_sel_blk, n_top)
    bias_ref[...] = _bias_columns(_bias_t(keep_t))


def _nsa_select(proj, cos, sin, kvc, ovt, *, B, S, tq):
    T = B * S
    nq = S // tq
    n_seg = S // CMP_STRIDE
    n_cmp = n_seg - CMP_BLOCK // CMP_STRIDE + 1
    n_sel_blk = S // SEL_BLOCK
    R = NSA_REP
    kern = functools.partial(_nsa_select_kernel, n_cmp=n_cmp, n_sel_blk=n_sel_blk,
                             n_top=min(SEL_TOPK, n_sel_blk))
    heads_spec = pl.BlockSpec((R, tq, LANES), lambda b, g, i: (g, b * nq + i, 0))
    return pl.pallas_call(
        kern,
        grid=(B, NSA_GROUPS, nq),
        in_specs=[pl.BlockSpec((R, tq, LANES), lambda b, g, i: (BLK_NQ // R + g, b * nq + i, 0)),
                  pl.BlockSpec((tq, LANES), lambda b, g, i: (b * nq + i, 0)),
                  pl.BlockSpec((tq, LANES), lambda b, g, i: (b * nq + i, 0)),
                  pl.BlockSpec((None, n_seg, LANES), lambda b, g, i: (0, g * B + b, 0)),
                  pl.BlockSpec((None, n_seg, LANES), lambda b, g, i: (1, g * B + b, 0)),
                  pl.BlockSpec(ovt.shape, lambda b, g, i: (0, 0))],
        out_specs=[heads_spec,
                   pl.BlockSpec((None, tq, LANES), lambda b, g, i: (g, b * nq + i, 0)),
                   heads_spec],
        out_shape=[jax.ShapeDtypeStruct((NSA_HEADS, T, LANES), BF16),
                   jax.ShapeDtypeStruct((NSA_GROUPS, T, LANES), BF16),
                   jax.ShapeDtypeStruct((NSA_HEADS, T, LANES), BF16)],
        compiler_params=pltpu.CompilerParams(
            dimension_semantics=("parallel", "parallel", "parallel"),
            vmem_limit_bytes=VMEM_LIMIT),
        name="nsa_select",
    )(proj, cos, sin, kvc, kvc, ovt)


def _nsa_sel_kernel(qrot_ref, bias_ref, ks_ref, vs_ref, onehot_ref, o_ref,
                    ksaug_ref, p_scr, l_scr, *, tq):
    R, S, _ = qrot_ref.shape
    ksaug_ref[:, :HEAD_DIM] = ks_ref[...]
    ksaug_ref[:, HEAD_DIM:] = onehot_ref[...]

    n_pass = 0
    for c in range(S // tq):
        w = (c + 1) * tq
        q_rows = slice(c * tq, (c + 1) * tq)
        bias = bias_ref[q_rows, :]
        for r0 in range(0, R, HEADS_PER_PASS):
            heads = range(r0, r0 + HEADS_PER_PASS)
            p_c, l_c = p_scr.at[n_pass % 2], l_scr.at[n_pass % 2]
            n_pass += 1
            q_aug = jnp.concatenate(
                [jnp.concatenate([qrot_ref[r, q_rows, :], bias], axis=1) for r in heads], axis=0)
            s = _nt_dot(q_aug, ksaug_ref[:w, :])
            _softmax_static(s, p_c, l_c, n_rows=HEADS_PER_PASS * tq, tq=tq, width=w,
                            q0=c * tq, k0=0)
            o = jnp.dot(p_c[:, :w], vs_ref[:w, :], preferred_element_type=F32) / l_c[...]
            for k, r in enumerate(heads):
                o_ref[r, q_rows, :] = o[k * tq:(k + 1) * tq, :].astype(BF16)


def _nsa_sel(q_rot, bias, proj, onehot, *, B, S, tq):
    T = B * S
    R = NSA_REP
    heads_spec = pl.BlockSpec((R, S, LANES), lambda b, g: (g, b, 0))
    kv_spec = lambda blk: pl.BlockSpec((None, S, LANES), lambda b, g: (blk + g, b, 0))
    return pl.pallas_call(
        functools.partial(_nsa_sel_kernel, tq=tq),
        grid=(B, NSA_GROUPS),
        in_specs=[heads_spec,
                  pl.BlockSpec((None, S, LANES), lambda b, g: (g, b, 0)),
                  kv_spec(BLK_NKS), kv_spec(BLK_NVS),
                  pl.BlockSpec((S, LANES), lambda b, g: (0, 0))],
        out_specs=heads_spec,
        out_shape=jax.ShapeDtypeStruct((NSA_HEADS, T, LANES), BF16),
        scratch_shapes=[pltpu.VMEM((S, 2 * LANES), BF16),
                        pltpu.VMEM((2, HEADS_PER_PASS * tq, S), BF16),
                        pltpu.VMEM((2, HEADS_PER_PASS * tq, LANES), F32)],
        compiler_params=pltpu.CompilerParams(
            dimension_semantics=("parallel", "parallel"), vmem_limit_bytes=VMEM_LIMIT_BIG),
        name="nsa_sel_attn",
    )(q_rot, bias, proj, proj, onehot)


def _nsa_win_kernel(qrot_ref, kw_ref, vw_ref, oc_ref, os_ref, gate_ref, o_ref,
                    p_scr, l_scr, *, tq):
    R, S, _ = qrot_ref.shape
    n_pass = 0
    for c in range(S // tq):
        q_rows = slice(c * tq, (c + 1) * tq)
        k0 = max(0, c * tq - WINDOW)
        w = (c + 1) * tq - k0
        gt = _sigmoid(gate_ref[q_rows, :])
        for r0 in range(0, R, HEADS_PER_PASS):
            heads = range(r0, r0 + HEADS_PER_PASS)
            p_c, l_c = p_scr.at[n_pass % 2], l_scr.at[n_pass % 2]
            n_pass += 1
            q2 = jnp.concatenate([qrot_ref[r, q_rows, :] for r in heads], axis=0)
            s = _nt_dot(q2, kw_ref[k0:k0 + w, :])
            _softmax_static(s, p_c, l_c, n_rows=HEADS_PER_PASS * tq, tq=tq, width=w,
                            q0=c * tq, k0=k0, window=WINDOW)
            o_w = jnp.dot(p_c[:, :w], vw_ref[k0:k0 + w, :],
                          preferred_element_type=F32) / l_c[...]
            for k, r in enumerate(heads):
                o = (gt[:, 3 * r:3 * r + 1] * oc_ref[r, q_rows, :].astype(F32)
                     + gt[:, 3 * r + 1:3 * r + 2] * os_ref[r, q_rows, :].astype(F32)
                     + gt[:, 3 * r + 2:3 * r + 3] * o_w[k * tq:(k + 1) * tq, :])
                o_ref[r, q_rows, :] = o.astype(BF16)


def _nsa_win(q_rot, proj, o_c, o_s, gates, *, B, S, tq):
    T = B * S
    R = NSA_REP
    wmax = WINDOW + tq
    heads_spec = pl.BlockSpec((R, S, LANES), lambda b, g: (g, b, 0))
    kv_spec = lambda blk: pl.BlockSpec((None, S, LANES), lambda b, g: (blk + g, b, 0))
    return pl.pallas_call(
        functools.partial(_nsa_win_kernel, tq=tq),
        grid=(B, NSA_GROUPS),
        in_specs=[heads_spec, kv_spec(BLK_NKW), kv_spec(BLK_NVW), heads_spec, heads_spec,
                  pl.BlockSpec((None, S, LANES), lambda b, g: (g, b, 0))],
        out_specs=heads_spec,
        out_shape=jax.ShapeDtypeStruct((NSA_HEADS, T, LANES), BF16),
        scratch_shapes=[pltpu.VMEM((2, HEADS_PER_PASS * tq, wmax), BF16),
                        pltpu.VMEM((2, HEADS_PER_PASS * tq, LANES), F32)],
        compiler_params=pltpu.CompilerParams(
            dimension_semantics=("parallel", "parallel"), vmem_limit_bytes=VMEM_LIMIT_BIG),
        name="nsa_win_attn",
    )(q_rot, proj, proj, o_c, o_s, gates)


def _out_proj_kernel(om_ref, on_ref, zm_ref, zn_ref, gm_ref, gn_ref, x_ref, w_ref, gf_ref,
                     out_ref, y_scr):
    def gated_norm(o_ref, z_ref, g_ref, col0):
        z = z_ref[...].astype(F32)
        a = o_ref[...].astype(F32) * (z * _sigmoid(z))
        ss = jnp.sum(jnp.sum(a * a, axis=0), axis=-1, keepdims=True)
        inv = lax.rsqrt(ss * (1.0 / (a.shape[0] * LANES)) + EPS)
        for h in range(a.shape[0]):
            y = a[h] * inv * g_ref[h]
            y_scr[:, col0 + h * LANES:col0 + (h + 1) * LANES] = y.astype(BF16)

    gated_norm(om_ref, zm_ref, gm_ref, 0)
    gated_norm(on_ref, zn_ref, gn_ref, MOBA_HEADS * LANES)
    r = x_ref[...] + jnp.dot(y_scr[...], w_ref[...], preferred_element_type=F32)
    ms = jnp.mean(r * r, axis=-1, keepdims=True)
    out_ref[...] = r * lax.rsqrt(ms + EPS) * gf_ref[...]


def _out_proj(o_moba, o_nsa, proj, g_moba, g_nsa, x2d, w_out, g_final, *, tm):
    T = x2d.shape[0]
    H = MOBA_HEADS
    return pl.pallas_call(
        _out_proj_kernel,
        grid=(T // tm,),
        in_specs=[pl.BlockSpec((H, tm, LANES), lambda i: (0, i, 0)),
                  pl.BlockSpec((H, tm, LANES), lambda i: (0, i, 0)),
                  pl.BlockSpec((H, tm, LANES), lambda i: (BLK_MZ // H, i, 0)),
                  pl.BlockSpec((H, tm, LANES), lambda i: (BLK_NZ // H, i, 0)),
                  pl.BlockSpec((H, 1, LANES), lambda i: (0, 0, 0)),
                  pl.BlockSpec((H, 1, LANES), lambda i: (0, 0, 0)),
                  pl.BlockSpec((tm, D_MODEL), lambda i: (i, 0)),
                  pl.BlockSpec((D_MODEL, D_MODEL), lambda i: (0, 0)),
                  pl.BlockSpec((1, D_MODEL), lambda i: (0, 0))],
        out_specs=pl.BlockSpec((tm, D_MODEL), lambda i: (i, 0)),
        out_shape=jax.ShapeDtypeStruct((T, D_MODEL), F32),
        scratch_shapes=[pltpu.VMEM((tm, D_MODEL), BF16)],
        compiler_params=pltpu.CompilerParams(
            dimension_semantics=("parallel",), vmem_limit_bytes=VMEM_LIMIT),
        name="out_proj",
    )(o_moba, o_nsa, proj, proj, g_moba, g_nsa, x2d, w_out, g_final)


def _permute_w_in(w_in):
    mw, nw, kw = MOBA_HEADS * HEAD_DIM, NSA_HEADS * HEAD_DIM, NSA_GROUPS * HEAD_DIM
    sizes = [mw] * 4 + [nw] + [kw] * 6 + [3 * NSA_HEADS, nw]
    offs = np.concatenate([[0], np.cumsum(sizes)])
    names = ["mq", "mk", "mv", "mz", "nq", "nkc", "nvc", "nks", "nvs", "nkw", "nvw", "ng", "nz"]
    part = {n: w_in[:, int(offs[k]):int(offs[k + 1])] for k, n in enumerate(names)}
    order = ["mq", "mk", "mv", "mz", "nq", "nz", "nks", "nkw", "nkc", "nvc", "nvs", "nvw"]
    w_perm = jnp.concatenate([part[n] for n in order], axis=1).astype(BF16)
    per_group = 3 * NSA_REP
    wg = part["ng"].reshape(D_MODEL, NSA_GROUPS, per_group)
    wg = jnp.pad(wg, ((0, 0), (0, 0), (0, LANES - per_group)))
    return w_perm, wg.reshape(D_MODEL, NSA_GROUPS * LANES).astype(BF16)


def _block_onehot(S, block):
    ids = np.arange(S)[:, None] // block
    return jnp.asarray((ids == np.arange(LANES)[None, :]).astype(np.float32), dtype=BF16)


def _overlap_t(n_seg, n_cmp, n_sel_blk):
    cs = np.arange(n_seg)[None, :] * CMP_STRIDE
    ss = np.arange(n_sel_blk)[:, None] * SEL_BLOCK
    ov = (cs < ss + SEL_BLOCK) & (cs + CMP_BLOCK > ss) & (np.arange(n_seg)[None, :] < n_cmp)
    return jnp.asarray(ov.astype(np.float32), dtype=BF16)


def _layer(x, cos, sin, w_in, g_norm, pe_ck, pe_cv, w_ck1, w_ck2, w_cv1, w_cv2,
           g_out_moba, g_out_nsa, w_out, g_final, *, nsa_tq, tm_in, tm_out, blocks_per_tile):
    B, S, _ = x.shape
    T = B * S
    x2d = x.reshape(T, D_MODEL)
    w_perm, w_gate = _permute_w_in(w_in)
    proj, gates = _in_proj(x2d, g_norm.reshape(1, D_MODEL), w_perm, w_gate, cos, sin,
                           tm=tm_in, blocks_per_tile=blocks_per_tile)

    n_seg = S // CMP_STRIDE
    seg = proj[BLK_NKC:BLK_NKC + 4].reshape(2, NSA_GROUPS * B * n_seg, CMP_STRIDE * HEAD_DIM)
    pe = jnp.stack([pe_ck.reshape(1, -1), pe_cv.reshape(1, -1)])
    w1 = jnp.stack([w_ck1, w_cv1]).astype(BF16)
    w2 = jnp.stack([w_ck2, w_cv2]).astype(BF16)
    kvc = _compress(seg, pe, w1, w2)

    o_moba = _moba(proj, _block_onehot(S, MOBA_BLOCK), B=B, S=S)
    n_cmp = n_seg - CMP_BLOCK // CMP_STRIDE + 1
    q_rot, bias, o_c = _nsa_select(proj, cos, sin, kvc, _overlap_t(n_seg, n_cmp, S // SEL_BLOCK),
                                   B=B, S=S, tq=2 * nsa_tq)
    o_s = _nsa_sel(q_rot, bias, proj, _block_onehot(S, SEL_BLOCK), B=B, S=S, tq=nsa_tq)
    o_nsa = _nsa_win(q_rot, proj, o_c, o_s, gates, B=B, S=S, tq=nsa_tq)
    out = _out_proj(o_moba, o_nsa, proj,
                    g_out_moba.reshape(MOBA_HEADS, 1, LANES), g_out_nsa.reshape(NSA_HEADS, 1, LANES),
                    x2d, w_out.astype(BF16), g_final.reshape(1, D_MODEL), tm=tm_out)
    return out.reshape(B, S, D_MODEL)


def kernel(x, positions, w_in, g_norm, pe_ck, pe_cv, w_ck1, w_ck2, w_cv1, w_cv2,
           g_out_moba, g_out_nsa, w_out, g_final):
    assert w_in.shape[0] == 1, "single-layer problem"
    cos, sin = _rope_tables(positions)
    return _layer(x, cos, sin, w_in[0], g_norm[0], pe_ck[0], pe_cv[0], w_ck1[0], w_ck2[0],
                  w_cv1[0], w_cv2[0], g_out_moba[0], g_out_nsa[0], w_out[0], g_final,
                  nsa_tq=256, tm_in=1024, tm_out=256, blocks_per_tile=4)
```

```python
import functools

import numpy as np
import jax
import jax.numpy as jnp
from jax import lax
from jax.experimental import pallas as pl
from jax.experimental.pallas import tpu as pltpu

F32 = jnp.float32
BF16 = jnp.bfloat16

D_MODEL = 2048
HEAD_DIM = 128
MOBA_HEADS = 8
NSA_HEADS = 8
NSA_GROUPS = 2
NSA_REP = 4
MOBA_BLOCK = 256
MOBA_TOPK = 3
CMP_BLOCK = 32
CMP_STRIDE = 16
CMP_HIDDEN = 256
SEL_BLOCK = 64
SEL_TOPK = 8
WINDOW = 512
ROPE_THETA = 10000.0
EPS = 1e-6
SCALE = HEAD_DIM ** -0.5
QK_PRESCALE = SCALE * float(np.log2(np.e))
NEG_BIG = -(2.0 ** 100)

LANES = 128
VMEM_LIMIT = 48 * 1024 * 1024
VMEM_LIMIT_BIG = 58 * 1024 * 1024

BLK_MQ, BLK_MK, BLK_MV, BLK_MZ, BLK_NQ, BLK_NZ = 0, 8, 16, 24, 32, 40
BLK_NKS, BLK_NKW, BLK_NKC, BLK_NVC, BLK_NVS, BLK_NVW = 48, 50, 52, 54, 56, 58
N_BLOCKS = 60
ROPE_BLOCKS = tuple(range(0, 16)) + (48, 49, 50, 51)


def _nt_dot(a, b):
    return lax.dot_general(a, b, (((1,), (1,)), ((), ())), preferred_element_type=F32)


def _sigmoid(x):
    return 1.0 / (1.0 + jnp.exp(-x))


def _rope(a, cos, sin_signed):
    return a * cos + pltpu.roll(a, HEAD_DIM // 2, axis=a.ndim - 1) * sin_signed


def _rope_table_kernel(pos_ref, invf_ref, sign_ref, cos_ref, sin_ref):
    ang = pos_ref[...].astype(F32) * invf_ref[...]
    cos_ref[...] = jnp.cos(ang)
    sin_ref[...] = jnp.sin(ang) * sign_ref[...]


def _rope_tables(positions):
    T = positions.size
    tile = min(T, 2048)
    half = HEAD_DIM // 2
    inv_freq = 1.0 / (ROPE_THETA ** (jnp.arange(0, HEAD_DIM, 2, dtype=F32) / HEAD_DIM))
    invf = jnp.concatenate([inv_freq, inv_freq]).reshape(1, HEAD_DIM)
    sign = jnp.concatenate([-jnp.ones((half,), F32), jnp.ones((half,), F32)]).reshape(1, HEAD_DIM)
    return pl.pallas_call(
        _rope_table_kernel,
        grid=(T // tile,),
        in_specs=[pl.BlockSpec((tile, 1), lambda i: (i, 0)),
                  pl.BlockSpec((1, HEAD_DIM), lambda i: (0, 0)),
                  pl.BlockSpec((1, HEAD_DIM), lambda i: (0, 0))],
        out_specs=[pl.BlockSpec((tile, HEAD_DIM), lambda i: (i, 0)),
                   pl.BlockSpec((tile, HEAD_DIM), lambda i: (i, 0))],
        out_shape=[jax.ShapeDtypeStruct((T, HEAD_DIM), F32)] * 2,
        name="rope_tables",
    )(positions.reshape(T, 1), invf, sign)


def _in_proj_kernel(x_ref, g_ref, w_ref, wg_ref, cos_ref, sin_ref, out_ref, gate_ref, h_scr,
                    *, rope_tiles, blocks_per_tile):
    j = pl.program_id(1)

    @pl.when(j == 0)
    def _():
        def norm_rows(r, carry):
            rows = pl.ds(pl.multiple_of(r * NORM_ROWS, NORM_ROWS), NORM_ROWS)
            x = x_ref[rows, :]
            ms = jnp.mean(x * x, axis=-1, keepdims=True)
            h_scr[rows, :] = (x * lax.rsqrt(ms + EPS) * g_ref[...]).astype(BF16)
            return carry

        lax.fori_loop(0, x_ref.shape[0] // NORM_ROWS, norm_rows, 0, unroll=8)
        gates = jnp.dot(h_scr[...], wg_ref[...], preferred_element_type=F32)
        for g in range(NSA_GROUPS):
            gate_ref[g] = gates[:, g * LANES:(g + 1) * LANES]

    is_rope = functools.reduce(jnp.logical_or, [j == t for t in rope_tiles])

    @pl.when(is_rope)
    def _():
        acc = jnp.dot(h_scr[...], w_ref[...], preferred_element_type=F32)
        f = jnp.where(j < BLK_MK // blocks_per_tile, QK_PRESCALE, 1.0).astype(F32)
        cos = cos_ref[...] * f
        sin = sin_ref[...] * f
        for c in range(blocks_per_tile):
            out_ref[c] = _rope(acc[:, c * LANES:(c + 1) * LANES], cos, sin).astype(BF16)

    @pl.when(jnp.logical_not(is_rope))
    def _():
        acc = jnp.dot(h_scr[...], w_ref[...], preferred_element_type=F32)
        for c in range(blocks_per_tile):
            out_ref[c] = acc[:, c * LANES:(c + 1) * LANES].astype(BF16)


def _in_proj(x2d, g_norm, w_perm, w_gate, cos, sin, *, tm, blocks_per_tile):
    T = x2d.shape[0]
    tn = blocks_per_tile * LANES
    n_tiles = N_BLOCKS // blocks_per_tile
    w_tiles = w_perm.reshape(D_MODEL, n_tiles, tn).transpose(1, 0, 2)
    rope_tiles = tuple(sorted({b // blocks_per_tile for b in ROPE_BLOCKS}))
    assert all((t * blocks_per_tile + c) in ROPE_BLOCKS
               for t in rope_tiles for c in range(blocks_per_tile))
    kern = functools.partial(_in_proj_kernel, rope_tiles=rope_tiles,
                             blocks_per_tile=blocks_per_tile)
    return pl.pallas_call(
        kern,
        grid=(T // tm, n_tiles),
        in_specs=[pl.BlockSpec((tm, D_MODEL), lambda i, j: (i, 0)),
                  pl.BlockSpec((1, D_MODEL), lambda i, j: (0, 0)),
                  pl.BlockSpec((None, D_MODEL, tn), lambda i, j: (j, 0, 0)),
                  pl.BlockSpec((D_MODEL, NSA_GROUPS * LANES), lambda i, j: (0, 0)),
                  pl.BlockSpec((tm, HEAD_DIM), lambda i, j: (i, 0)),
                  pl.BlockSpec((tm, HEAD_DIM), lambda i, j: (i, 0))],
        out_specs=[pl.BlockSpec((blocks_per_tile, tm, LANES), lambda i, j: (j, i, 0)),
                   pl.BlockSpec((NSA_GROUPS, tm, LANES), lambda i, j: (0, i, 0))],
        out_shape=[jax.ShapeDtypeStruct((N_BLOCKS, T, LANES), BF16),
                   jax.ShapeDtypeStruct((NSA_GROUPS, T, LANES), F32)],
        scratch_shapes=[pltpu.VMEM((tm, D_MODEL), BF16)],
        compiler_params=pltpu.CompilerParams(
            dimension_semantics=("parallel", "arbitrary"), vmem_limit_bytes=VMEM_LIMIT),
        name="in_proj",
    )(x2d, g_norm, w_tiles, w_gate, cos, sin)


def _compress_kernel(seg_ref, pe_ref, w1_ref, w2_ref, out_ref):
    half = CMP_STRIDE * HEAD_DIM
    seg = seg_ref[0].astype(F32)
    pe = pe_ref[0]
    top = (seg + pe[:, :half]).astype(BF16)
    bot = (seg + pe[:, half:]).astype(BF16)
    a = jnp.dot(top, w1_ref[0, :half, :], preferred_element_type=F32)
    b = jnp.dot(bot, w1_ref[0, half:, :], preferred_element_type=F32)
    rows = a.shape[0]
    h = a + pltpu.roll(b, rows - 1, axis=0)
    hid = h * _sigmoid(h)
    out_ref[0] = jnp.dot(hid.astype(BF16), w2_ref[0], preferred_element_type=F32).astype(BF16)


def _compress(seg, pe, w1, w2):
    _, R, half = seg.shape
    return pl.pallas_call(
        _compress_kernel,
        grid=(2,),
        in_specs=[pl.BlockSpec((1, R, half), lambda c: (c, 0, 0)),
                  pl.BlockSpec((1, 1, 2 * half), lambda c: (c, 0, 0)),
                  pl.BlockSpec((1, 2 * half, CMP_HIDDEN), lambda c: (c, 0, 0)),
                  pl.BlockSpec((1, CMP_HIDDEN, HEAD_DIM), lambda c: (c, 0, 0))],
        out_specs=pl.BlockSpec((1, R, HEAD_DIM), lambda c: (c, 0, 0)),
        out_shape=jax.ShapeDtypeStruct((2, R, HEAD_DIM), BF16),
        compiler_params=pltpu.CompilerParams(
            dimension_semantics=("arbitrary",), vmem_limit_bytes=VMEM_LIMIT),
        name="compress",
    )(seg, pe, w1, w2)


def _select_bias_t(score_t, n_rows, n_keep):
    sub = 8
    n_groups = score_t.shape[0] // sub
    groups = [score_t[g * sub:(g + 1) * sub, :] for g in range(n_groups)]
    jrow = lax.broadcasted_iota(jnp.int32, groups[0].shape, 0)
    cnts = [jnp.zeros(groups[0].shape, jnp.int32) for _ in range(n_groups)]
    for jp in range(n_rows):
        row = score_t[jp:jp + 1, :]
        for g, grp in enumerate(groups):
            if g * sub > jp:
                beats = row >= grp
            elif g * sub + sub - 1 <= jp:
                beats = row > grp
            else:
                beats = (row > grp) | ((row == grp) & (jrow + g * sub > jp))
            cnts[g] = cnts[g] + beats.astype(jnp.int32)
    cnt = cnts[0] if n_groups == 1 else jnp.concatenate(cnts, axis=0)
    return (cnt < n_keep) & (score_t > -jnp.inf)


def _bias_t(keep_t):
    return jnp.where(keep_t, 0.0, NEG_BIG).astype(F32)


def _bias_columns(bias_t):
    rows, q = bias_t.shape
    if rows < LANES:
        bias_t = jnp.concatenate([bias_t, jnp.zeros((LANES - rows, q), F32)], axis=0)
    return bias_t.T.astype(BF16)


ROW_CHUNK = 64
HEADS_PER_PASS = 2
NORM_ROWS = 16
OUT_SUB_ROWS = 256

def _softmax_static(s, p_ref, *, n_rows, tq, width, q0, k0, window=None):
    n_tiles = width // LANES
    for r0 in range(0, n_rows, ROW_CHUNK):
        rows = slice(r0, r0 + ROW_CHUNK)
        qlo = q0 + r0 % tq
        qhi = qlo + ROW_CHUNK - 1
        kinds = []
        for t in range(n_tiles):
            klo = k0 + t * LANES
            khi = klo + LANES - 1
            none = klo > qhi or (window is not None and khi <= qlo - window)
            full = khi <= qlo and (window is None or klo > qhi - window)
            kinds.append("none" if none else "full" if full else "part")
        mx = None
        masked = {}
        for t, kind in enumerate(kinds):
            if kind == "none":
                continue
            x = s[rows, t * LANES:(t + 1) * LANES]
            if kind == "part":
                qpos = qlo + lax.broadcasted_iota(jnp.int32, x.shape, 0)
                kpos = k0 + t * LANES + lax.broadcasted_iota(jnp.int32, x.shape, 1)
                ok = kpos <= qpos
                if window is not None:
                    ok = ok & (kpos > qpos - window)
                x = jnp.where(ok, x, NEG_BIG)
                masked[t] = x
            mx = x if mx is None else jnp.maximum(mx, x)
        m = jnp.broadcast_to(jnp.max(mx, axis=-1, keepdims=True), mx.shape)
        for t, kind in enumerate(kinds):
            cols = slice(t * LANES, (t + 1) * LANES)
            if kind == "none":
                p_ref[rows, cols] = jnp.zeros((ROW_CHUNK, LANES), BF16)
                continue
            x = masked[t] if kind == "part" else s[rows, cols]
            p_ref[rows, cols] = jnp.exp2(x - m).astype(BF16)


def _pv_normalized(p, v_ones):
    o = jnp.dot(p, v_ones, preferred_element_type=F32)
    return o[:, :HEAD_DIM] / o[:, HEAD_DIM:]


def _moba_kernel(q_ref, k_ref, v_ref, onehot_ref, o_ref, kaug_ref, qaug_ref, kmean_ref,
                 vaug_ref, p_scr, *, n_blk, k_top):
    S = k_ref.shape[0]
    tq = MOBA_BLOCK
    kaug_ref[:, :HEAD_DIM] = k_ref[...]
    kaug_ref[:, HEAD_DIM:] = onehot_ref[...]
    vaug_ref[:, :HEAD_DIM] = v_ref[...]
    vaug_ref[:, HEAD_DIM:] = jnp.ones((S, HEAD_DIM), BF16)
    kmean_ref[...] = jnp.zeros(kmean_ref.shape, F32)
    for j in range(n_blk):
        kb = k_ref[j * MOBA_BLOCK:(j + 1) * MOBA_BLOCK, :].astype(F32)
        kmean_ref[j:j + 1, :] = jnp.sum(kb, axis=0, keepdims=True) * (1.0 / MOBA_BLOCK)

    q = q_ref[...]
    gate_t = _nt_dot(kmean_ref[...].astype(BF16), q)
    jrow = lax.broadcasted_iota(jnp.int32, gate_t.shape, 0)
    own = lax.broadcasted_iota(jnp.int32, gate_t.shape, 1) // MOBA_BLOCK
    gate_t = jnp.where((jrow < own) & jnp.isfinite(gate_t), gate_t, -jnp.inf)
    bias_t = _bias_t(_select_bias_t(gate_t, n_blk, k_top) | (jrow == own))
    qaug_ref[:, :HEAD_DIM] = q
    for c in range(S // tq):
        qaug_ref[c * tq:(c + 1) * tq, HEAD_DIM:] = _bias_columns(bias_t[:, c * tq:(c + 1) * tq])

    for c in range(S // tq):
        w = (c + 1) * tq
        p_c = p_scr.at[c % 2]
        s = _nt_dot(qaug_ref[c * tq:(c + 1) * tq, :], kaug_ref[:w, :])
        _softmax_static(s, p_c, n_rows=tq, tq=tq, width=w, q0=c * tq, k0=0)
        o = _pv_normalized(p_c[:, :w], vaug_ref[:w, :])
        o_ref[c * tq:(c + 1) * tq, :] = o.astype(BF16)


def _moba(proj, onehot, *, B, S):
    T = B * S
    tq = MOBA_BLOCK
    n_blk = S // MOBA_BLOCK
    k_top = min(MOBA_TOPK, n_blk - 1)
    nb8 = -(-n_blk // 8) * 8
    kern = functools.partial(_moba_kernel, n_blk=n_blk, k_top=k_top)
    head_spec = lambda blk: pl.BlockSpec((None, S, LANES), lambda b, h: (blk + h, b, 0))
    return pl.pallas_call(
        kern,
        grid=(B, MOBA_HEADS),
        in_specs=[head_spec(BLK_MQ), head_spec(BLK_MK), head_spec(BLK_MV),
                  pl.BlockSpec((S, LANES), lambda b, h: (0, 0))],
        out_specs=head_spec(0),
        out_shape=jax.ShapeDtypeStruct((MOBA_HEADS, T, LANES), BF16),
        scratch_shapes=[pltpu.VMEM((S, 2 * LANES), BF16), pltpu.VMEM((S, 2 * LANES), BF16),
                        pltpu.VMEM((nb8, HEAD_DIM), F32),
                        pltpu.VMEM((S, 2 * LANES), BF16), pltpu.VMEM((2, tq, S), BF16)],
        compiler_params=pltpu.CompilerParams(
            dimension_semantics=("parallel", "parallel"), vmem_limit_bytes=VMEM_LIMIT),
        name="moba_attn",
    )(proj, proj, proj, onehot)


def _nsa_select_kernel(q_ref, cos_ref, sin_ref, kc_ref, vc_ref, ovt_ref,
                       qrot_ref, bias_ref, oc_ref, *, n_cmp, n_sel_blk, n_top):
    i = pl.program_id(2)
    R, tq, _ = q_ref.shape
    q_raw = q_ref[...]
    cos = (cos_ref[...] * QK_PRESCALE)[None]
    sin = (sin_ref[...] * QK_PRESCALE)[None]
    qrot_ref[...] = _rope(q_raw.astype(F32), cos, sin).astype(BF16)

    n_seg = kc_ref.shape[0]
    s_c = (_nt_dot(q_raw.reshape(R * tq, HEAD_DIM), kc_ref[...]) * SCALE).reshape(R, tq, n_seg)
    n_idx = lax.broadcasted_iota(jnp.int32, s_c.shape, 2)
    pos3 = i * tq + lax.broadcasted_iota(jnp.int32, s_c.shape, 1)
    m_c = (n_idx * CMP_STRIDE + CMP_BLOCK - 1 <= pos3) & (n_idx < n_cmp)
    s_c = jnp.where(m_c, s_c, -jnp.inf)
    mx = jnp.max(s_c, axis=-1, keepdims=True)
    mx = jnp.where(jnp.isfinite(mx), mx, 0.0)
    e_c = jnp.where(m_c, jnp.exp(s_c - mx), 0.0)
    p_c = e_c / jnp.maximum(jnp.sum(e_c, axis=-1, keepdims=True), 1e-30)
    o_c = jnp.dot(p_c.reshape(R * tq, n_seg).astype(BF16), vc_ref[...],
                  preferred_element_type=F32).reshape(R, tq, HEAD_DIM)
    oc_ref[...] = o_c.astype(BF16)

    p_sum = jnp.sum(p_c, axis=0)
    p_hi = p_sum.astype(BF16)
    p_lo = (p_sum - p_hi.astype(F32)).astype(BF16)
    ovt = ovt_ref[...]
    imp_t = _nt_dot(ovt, p_hi) + _nt_dot(ovt, p_lo)
    jrow = lax.broadcasted_iota(jnp.int32, imp_t.shape, 0)
    posq = i * tq + lax.broadcasted_iota(jnp.int32, imp_t.shape, 1)
    own = posq // SEL_BLOCK
    forced = (jrow == 0) | (jrow == own) | (jrow == own - 1)
    future = jrow * SEL_BLOCK > posq
    score_t = jnp.where(future, -jnp.inf, jnp.where(forced, jnp.inf, imp_t))
    keep_t = _select_bias_t(score_t, n_sel_blk, n_top)
    bias_ref[...] = _bias_columns(_bias_t(keep_t))


def _nsa_select(proj, cos, sin, kvc, ovt, *, B, S, tq):
    T = B * S
    nq = S // tq
    n_seg = S // CMP_STRIDE
    n_cmp = n_seg - CMP_BLOCK // CMP_STRIDE + 1
    n_sel_blk = S // SEL_BLOCK
    R = NSA_REP
    kern = functools.partial(_nsa_select_kernel, n_cmp=n_cmp, n_sel_blk=n_sel_blk,
                             n_top=min(SEL_TOPK, n_sel_blk))
    heads_spec = pl.BlockSpec((R, tq, LANES), lambda b, g, i: (g, b * nq + i, 0))
    return pl.pallas_call(
        kern,
        grid=(B, NSA_GROUPS, nq),
        in_specs=[pl.BlockSpec((R, tq, LANES), lambda b, g, i: (BLK_NQ // R + g, b * nq + i, 0)),
                  pl.BlockSpec((tq, LANES), lambda b, g, i: (b * nq + i, 0)),
                  pl.BlockSpec((tq, LANES), lambda b, g, i: (b * nq + i, 0)),
                  pl.BlockSpec((None, n_seg, LANES), lambda b, g, i: (0, g * B + b, 0)),
                  pl.BlockSpec((None, n_seg, LANES), lambda b, g, i: (1, g * B + b, 0)),
                  pl.BlockSpec(ovt.shape, lambda b, g, i: (0, 0))],
        out_specs=[heads_spec,
                   pl.BlockSpec((None, tq, LANES), lambda b, g, i: (g, b * nq + i, 0)),
                   heads_spec],
        out_shape=[jax.ShapeDtypeStruct((NSA_HEADS, T, LANES), BF16),
                   jax.ShapeDtypeStruct((NSA_GROUPS, T, LANES), BF16),
                   jax.ShapeDtypeStruct((NSA_HEADS, T, LANES), BF16)],
        compiler_params=pltpu.CompilerParams(
            dimension_semantics=("parallel", "parallel", "parallel"),
            vmem_limit_bytes=VMEM_LIMIT),
        name="nsa_select",
    )(proj, cos, sin, kvc, kvc, ovt)


def _nsa_sel_kernel(qrot_ref, bias_ref, ks_ref, vs_ref, onehot_ref, o_ref,
                    ksaug_ref, vaug_ref, p_scr, *, tq):
    R, S, _ = qrot_ref.shape
    ksaug_ref[:, :HEAD_DIM] = ks_ref[...]
    ksaug_ref[:, HEAD_DIM:] = onehot_ref[...]
    vaug_ref[:, :HEAD_DIM] = vs_ref[...]
    vaug_ref[:, HEAD_DIM:] = jnp.ones((S, HEAD_DIM), BF16)

    n_pass = 0
    for c in range(S // tq):
        w = (c + 1) * tq
        q_rows = slice(c * tq, (c + 1) * tq)
        bias = bias_ref[q_rows, :]
        for r0 in range(0, R, HEADS_PER_PASS):
            heads = range(r0, r0 + HEADS_PER_PASS)
            p_c = p_scr.at[n_pass % 2]
            n_pass += 1
            q_aug = jnp.concatenate(
                [jnp.concatenate([qrot_ref[r, q_rows, :], bias], axis=1) for r in heads], axis=0)
            s = _nt_dot(q_aug, ksaug_ref[:w, :])
            _softmax_static(s, p_c, n_rows=HEADS_PER_PASS * tq, tq=tq, width=w,
                            q0=c * tq, k0=0)
            o = _pv_normalized(p_c[:, :w], vaug_ref[:w, :])
            for k, r in enumerate(heads):
                o_ref[r, q_rows, :] = o[k * tq:(k + 1) * tq, :].astype(BF16)


def _nsa_sel(q_rot, bias, proj, onehot, *, B, S, tq):
    T = B * S
    R = NSA_REP
    heads_spec = pl.BlockSpec((R, S, LANES), lambda b, g: (g, b, 0))
    kv_spec = lambda blk: pl.BlockSpec((None, S, LANES), lambda b, g: (blk + g, b, 0))
    return pl.pallas_call(
        functools.partial(_nsa_sel_kernel, tq=tq),
        grid=(B, NSA_GROUPS),
        in_specs=[heads_spec,
                  pl.BlockSpec((None, S, LANES), lambda b, g: (g, b, 0)),
                  kv_spec(BLK_NKS), kv_spec(BLK_NVS),
                  pl.BlockSpec((S, LANES), lambda b, g: (0, 0))],
        out_specs=heads_spec,
        out_shape=jax.ShapeDtypeStruct((NSA_HEADS, T, LANES), BF16),
        scratch_shapes=[pltpu.VMEM((S, 2 * LANES), BF16), pltpu.VMEM((S, 2 * LANES), BF16),
                        pltpu.VMEM((2, HEADS_PER_PASS * tq, S), BF16)],
        compiler_params=pltpu.CompilerParams(
            dimension_semantics=("parallel", "parallel"), vmem_limit_bytes=VMEM_LIMIT_BIG),
        name="nsa_sel_attn",
    )(q_rot, bias, proj, proj, onehot)


def _nsa_win_kernel(qrot_ref, kw_ref, vw_ref, oc_ref, os_ref, gate_ref, o_ref,
                    vaug_ref, p_scr, *, tq):
    R, S, _ = qrot_ref.shape
    vaug_ref[:, :HEAD_DIM] = vw_ref[...]
    vaug_ref[:, HEAD_DIM:] = jnp.ones((S, HEAD_DIM), BF16)
    n_pass = 0
    for c in range(S // tq):
        q_rows = slice(c * tq, (c + 1) * tq)
        k0 = max(0, c * tq - WINDOW)
        w = (c + 1) * tq - k0
        gt = _sigmoid(gate_ref[q_rows, :])
        for r0 in range(0, R, HEADS_PER_PASS):
            heads = range(r0, r0 + HEADS_PER_PASS)
            p_c = p_scr.at[n_pass % 2]
            n_pass += 1
            q2 = jnp.concatenate([qrot_ref[r, q_rows, :] for r in heads], axis=0)
            s = _nt_dot(q2, kw_ref[k0:k0 + w, :])
            _softmax_static(s, p_c, n_rows=HEADS_PER_PASS * tq, tq=tq, width=w,
                            q0=c * tq, k0=k0, window=WINDOW)
            o_w = _pv_normalized(p_c[:, :w], vaug_ref[k0:k0 + w, :])
            for k, r in enumerate(heads):
                o = (gt[:, 3 * r:3 * r + 1] * oc_ref[r, q_rows, :].astype(F32)
                     + gt[:, 3 * r + 1:3 * r + 2] * os_ref[r, q_rows, :].astype(F32)
                     + gt[:, 3 * r + 2:3 * r + 3] * o_w[k * tq:(k + 1) * tq, :])
                o_ref[r, q_rows, :] = o.astype(BF16)


def _nsa_win(q_rot, proj, o_c, o_s, gates, *, B, S, tq):
    T = B * S
    R = NSA_REP
    wmax = WINDOW + tq
    heads_spec = pl.BlockSpec((R, S, LANES), lambda b, g: (g, b, 0))
    kv_spec = lambda blk: pl.BlockSpec((None, S, LANES), lambda b, g: (blk + g, b, 0))
    return pl.pallas_call(
        functools.partial(_nsa_win_kernel, tq=tq),
        grid=(B, NSA_GROUPS),
        in_specs=[heads_spec, kv_spec(BLK_NKW), kv_spec(BLK_NVW), heads_spec, heads_spec,
                  pl.BlockSpec((None, S, LANES), lambda b, g: (g, b, 0))],
        out_specs=heads_spec,
        out_shape=jax.ShapeDtypeStruct((NSA_HEADS, T, LANES), BF16),
        scratch_shapes=[pltpu.VMEM((S, 2 * LANES), BF16),
                        pltpu.VMEM((2, HEADS_PER_PASS * tq, wmax), BF16)],
        compiler_params=pltpu.CompilerParams(
            dimension_semantics=("parallel", "parallel"), vmem_limit_bytes=VMEM_LIMIT_BIG),
        name="nsa_win_attn",
    )(q_rot, proj, proj, o_c, o_s, gates)


def _out_proj_kernel(om_ref, on_ref, zm_ref, zn_ref, gm_ref, gn_ref, x_ref, w_ref, gf_ref,
                     out_ref, y_scr):
    def gated_norm(o_ref, z_ref, g_ref, col0, rows):
        n_heads = o_ref.shape[0]
        acts = []
        ss = None
        for h in range(n_heads):
            z = z_ref[h, rows, :].astype(F32)
            a = o_ref[h, rows, :].astype(F32) * (z * _sigmoid(z))
            acts.append(a)
            ss = a * a if ss is None else ss + a * a
        ms = jnp.sum(ss, axis=-1, keepdims=True) * (1.0 / (n_heads * LANES))
        inv = lax.rsqrt(ms + EPS)
        for h in range(n_heads):
            y = acts[h] * inv * g_ref[h]
            y_scr[rows, col0 + h * LANES:col0 + (h + 1) * LANES] = y.astype(BF16)

    for r0 in range(0, x_ref.shape[0], OUT_SUB_ROWS):
        rows = slice(r0, r0 + OUT_SUB_ROWS)
        gated_norm(om_ref, zm_ref, gm_ref, 0, rows)
        gated_norm(on_ref, zn_ref, gn_ref, MOBA_HEADS * LANES, rows)
        r = x_ref[rows, :] + jnp.dot(y_scr[rows, :], w_ref[...], preferred_element_type=F32)
        ms = jnp.mean(r * r, axis=-1, keepdims=True)
        out_ref[rows, :] = r * lax.rsqrt(ms + EPS) * gf_ref[...]


def _out_proj(o_moba, o_nsa, proj, g_moba, g_nsa, x2d, w_out, g_final, *, tm):
    T = x2d.shape[0]
    H = MOBA_HEADS
    return pl.pallas_call(
        _out_proj_kernel,
        grid=(T // tm,),
        in_specs=[pl.BlockSpec((H, tm, LANES), lambda i: (0, i, 0)),
                  pl.BlockSpec((H, tm, LANES), lambda i: (0, i, 0)),
                  pl.BlockSpec((H, tm, LANES), lambda i: (BLK_MZ // H, i, 0)),
                  pl.BlockSpec((H, tm, LANES), lambda i: (BLK_NZ // H, i, 0)),
                  pl.BlockSpec((H, 1, LANES), lambda i: (0, 0, 0)),
                  pl.BlockSpec((H, 1, LANES), lambda i: (0, 0, 0)),
                  pl.BlockSpec((tm, D_MODEL), lambda i: (i, 0)),
                  pl.BlockSpec((D_MODEL, D_MODEL), lambda i: (0, 0), pipeline_mode=pl.Buffered(1)),
                  pl.BlockSpec((1, D_MODEL), lambda i: (0, 0))],
        out_specs=pl.BlockSpec((tm, D_MODEL), lambda i: (i, 0)),
        out_shape=jax.ShapeDtypeStruct((T, D_MODEL), F32),
        scratch_shapes=[pltpu.VMEM((tm, D_MODEL), BF16)],
        compiler_params=pltpu.CompilerParams(
            dimension_semantics=("parallel",), vmem_limit_bytes=VMEM_LIMIT_BIG),
        name="out_proj",
    )(o_moba, o_nsa, proj, proj, g_moba, g_nsa, x2d, w_out, g_final)


def _permute_w_in(w_in):
    mw, nw, kw = MOBA_HEADS * HEAD_DIM, NSA_HEADS * HEAD_DIM, NSA_GROUPS * HEAD_DIM
    sizes = [mw] * 4 + [nw] + [kw] * 6 + [3 * NSA_HEADS, nw]
    offs = np.concatenate([[0], np.cumsum(sizes)])
    names = ["mq", "mk", "mv", "mz", "nq", "nkc", "nvc", "nks", "nvs", "nkw", "nvw", "ng", "nz"]
    part = {n: w_in[:, int(offs[k]):int(offs[k + 1])] for k, n in enumerate(names)}
    order = ["mq", "mk", "mv", "mz", "nq", "nz", "nks", "nkw", "nkc", "nvc", "nvs", "nvw"]
    w_perm = jnp.concatenate([part[n] for n in order], axis=1).astype(BF16)
    per_group = 3 * NSA_REP
    wg = part["ng"].reshape(D_MODEL, NSA_GROUPS, per_group)
    wg = jnp.pad(wg, ((0, 0), (0, 0), (0, LANES - per_group)))
    return w_perm, wg.reshape(D_MODEL, NSA_GROUPS * LANES).astype(BF16)


def _block_onehot(S, block):
    ids = np.arange(S)[:, None] // block
    return jnp.asarray((ids == np.arange(LANES)[None, :]).astype(np.float32), dtype=BF16)


def _overlap_t(n_seg, n_cmp, n_sel_blk):
    cs = np.arange(n_seg)[None, :] * CMP_STRIDE
    ss = np.arange(n_sel_blk)[:, None] * SEL_BLOCK
    ov = (cs < ss + SEL_BLOCK) & (cs + CMP_BLOCK > ss) & (np.arange(n_seg)[None, :] < n_cmp)
    return jnp.asarray(ov.astype(np.float32), dtype=BF16)


def _layer(x, cos, sin, w_in, g_norm, pe_ck, pe_cv, w_ck1, w_ck2, w_cv1, w_cv2,
           g_out_moba, g_out_nsa, w_out, g_final, *, nsa_tq, tm_in, tm_out, blocks_per_tile):
    B, S, _ = x.shape
    T = B * S
    x2d = x.reshape(T, D_MODEL)
    w_perm, w_gate = _permute_w_in(w_in)
    proj, gates = _in_proj(x2d, g_norm.reshape(1, D_MODEL), w_perm, w_gate, cos, sin,
                           tm=tm_in, blocks_per_tile=blocks_per_tile)

    n_seg = S // CMP_STRIDE
    seg = proj[BLK_NKC:BLK_NKC + 4].reshape(2, NSA_GROUPS * B * n_seg, CMP_STRIDE * HEAD_DIM)
    pe = jnp.stack([pe_ck.reshape(1, -1), pe_cv.reshape(1, -1)])
    w1 = jnp.stack([w_ck1, w_cv1]).astype(BF16)
    w2 = jnp.stack([w_ck2, w_cv2]).astype(BF16)
    kvc = _compress(seg, pe, w1, w2)

    o_moba = _moba(proj, _block_onehot(S, MOBA_BLOCK), B=B, S=S)
    n_cmp = n_seg - CMP_BLOCK // CMP_STRIDE + 1
    q_rot, bias, o_c = _nsa_select(proj, cos, sin, kvc, _overlap_t(n_seg, n_cmp, S // SEL_BLOCK),
                                   B=B, S=S, tq=2 * nsa_tq)
    o_s = _nsa_sel(q_rot, bias, proj, _block_onehot(S, SEL_BLOCK), B=B, S=S, tq=nsa_tq)
    o_nsa = _nsa_win(q_rot, proj, o_c, o_s, gates, B=B, S=S, tq=nsa_tq)
    out = _out_proj(o_moba, o_nsa, proj,
                    g_out_moba.reshape(MOBA_HEADS, 1, LANES), g_out_nsa.reshape(NSA_HEADS, 1, LANES),
                    x2d, w_out.astype(BF16), g_final.reshape(1, D_MODEL), tm=tm_out)
    return out.reshape(B, S, D_MODEL)


def kernel(x, positions, w_in, g_norm, pe_ck, pe_cv, w_ck1, w_ck2, w_cv1, w_cv2,
           g_out_moba, g_out_nsa, w_out, g_final):
    assert w_in.shape[0] == 1, "single-layer problem"
    cos, sin = _rope_tables(positions)
    return _layer(x, cos, sin, w_in[0], g_norm[0], pe_ck[0], pe_cv[0], w_ck1[0], w_ck2[0],
                  w_cv1[0], w_cv2[0], g_out_moba[0], g_out_nsa[0], w_out[0], g_final,
                  nsa_tq=256, tm_in=1024, tm_out=512, blocks_per_tile=4)
```

```python
import functools

import numpy as np
import jax
import jax.numpy as jnp
from jax import lax
from jax.experimental import pallas as pl
from jax.experimental.pallas import tpu as pltpu

F32 = jnp.float32
BF16 = jnp.bfloat16

D_MODEL = 2048
HEAD_DIM = 128
MOBA_HEADS = 8
NSA_HEADS = 8
NSA_GROUPS = 2
NSA_REP = 4
MOBA_BLOCK = 256
MOBA_TOPK = 3
CMP_BLOCK = 32
CMP_STRIDE = 16
CMP_HIDDEN = 256
SEL_BLOCK = 64
SEL_TOPK = 8
WINDOW = 512
ROPE_THETA = 10000.0
EPS = 1e-6
SCALE = HEAD_DIM ** -0.5
QK_PRESCALE = SCALE * float(np.log2(np.e))
NEG_BIG = -(2.0 ** 100)

LANES = 128
VMEM_LIMIT = 48 * 1024 * 1024
VMEM_LIMIT_BIG = 58 * 1024 * 1024

COLUMN_ORDER = (("mq", 8), ("mk", 8), ("nks", 2), ("nkw", 2), ("nkc", 2), ("nvc", 2),
                ("mv", 8), ("mz", 8), ("nq", 8), ("nz", 8), ("nvs", 2), ("nvw", 2))
BLK_MQ, BLK_MK, BLK_NKS, BLK_NKW, BLK_NKC, BLK_NVC = 0, 8, 16, 18, 20, 22
BLK_MV, BLK_MZ, BLK_NQ, BLK_NZ, BLK_NVS, BLK_NVW = 24, 32, 40, 48, 56, 58
N_BLOCKS = 60
N_ROPE_BLOCKS = 20


def _nt_dot(a, b):
    return lax.dot_general(a, b, (((1,), (1,)), ((), ())), preferred_element_type=F32)


def _sigmoid(x):
    return 1.0 / (1.0 + jnp.exp(-x))


def _rope(a, cos, sin_signed):
    return a * cos + pltpu.roll(a, HEAD_DIM // 2, axis=a.ndim - 1) * sin_signed


def _rope_table_kernel(pos_ref, invf_ref, sign_ref, cos_ref, sin_ref):
    ang = pos_ref[...].astype(F32) * invf_ref[...]
    cos_ref[...] = jnp.cos(ang)
    sin_ref[...] = jnp.sin(ang) * sign_ref[...]


def _rope_tables(positions):
    T = positions.size
    tile = min(T, 2048)
    half = HEAD_DIM // 2
    inv_freq = 1.0 / (ROPE_THETA ** (jnp.arange(0, HEAD_DIM, 2, dtype=F32) / HEAD_DIM))
    invf = jnp.concatenate([inv_freq, inv_freq]).reshape(1, HEAD_DIM)
    sign = jnp.concatenate([-jnp.ones((half,), F32), jnp.ones((half,), F32)]).reshape(1, HEAD_DIM)
    return pl.pallas_call(
        _rope_table_kernel,
        grid=(T // tile,),
        in_specs=[pl.BlockSpec((tile, 1), lambda i: (i, 0)),
                  pl.BlockSpec((1, HEAD_DIM), lambda i: (0, 0)),
                  pl.BlockSpec((1, HEAD_DIM), lambda i: (0, 0))],
        out_specs=[pl.BlockSpec((tile, HEAD_DIM), lambda i: (i, 0)),
                   pl.BlockSpec((tile, HEAD_DIM), lambda i: (i, 0))],
        out_shape=[jax.ShapeDtypeStruct((T, HEAD_DIM), F32)] * 2,
        name="rope_tables",
    )(positions.reshape(T, 1), invf, sign)


def _block_kind(b):
    return "rope_q" if b < BLK_MK else "rope" if b < N_ROPE_BLOCKS else "plain"


def _in_proj_kernel(x_ref, g_ref, w_ref, wg_ref, cos_ref, sin_ref, out_ref, gate_ref, h_scr,
                    *, tile_patterns):
    j = pl.program_id(1)

    @pl.when(j == 0)
    def _():
        def norm_rows(r, carry):
            rows = pl.ds(pl.multiple_of(r * NORM_ROWS, NORM_ROWS), NORM_ROWS)
            x = x_ref[rows, :]
            ms = jnp.mean(x * x, axis=-1, keepdims=True)
            h_scr[rows, :] = (x * lax.rsqrt(ms + EPS) * g_ref[...]).astype(BF16)
            return carry

        lax.fori_loop(0, x_ref.shape[0] // NORM_ROWS, norm_rows, 0, unroll=8)
        gates = jnp.dot(h_scr[...], wg_ref[...], preferred_element_type=F32)
        for g in range(NSA_GROUPS):
            gate_ref[g] = gates[:, g * LANES:(g + 1) * LANES]

    for pattern, tiles in tile_patterns:
        @pl.when(functools.reduce(jnp.logical_or, [j == t for t in tiles]))
        def _(pattern=pattern):
            acc = jnp.dot(h_scr[...], w_ref[...], preferred_element_type=F32)
            if any(kind != "plain" for kind in pattern):
                cos = cos_ref[...]
                sin = sin_ref[...]
            if "rope_q" in pattern:
                cos_q = cos * QK_PRESCALE
                sin_q = sin * QK_PRESCALE
            for c, kind in enumerate(pattern):
                a = acc[:, c * LANES:(c + 1) * LANES]
                if kind == "rope_q":
                    a = _rope(a, cos_q, sin_q)
                elif kind == "rope":
                    a = _rope(a, cos, sin)
                out_ref[c] = a.astype(BF16)


def _in_proj(x2d, g_norm, w_perm, w_gate, cos, sin, *, tm, blocks_per_tile):
    T = x2d.shape[0]
    tn = blocks_per_tile * LANES
    n_tiles = N_BLOCKS // blocks_per_tile
    w_tiles = w_perm.reshape(D_MODEL, n_tiles, tn).transpose(1, 0, 2)
    by_pattern = {}
    for t in range(n_tiles):
        pattern = tuple(_block_kind(t * blocks_per_tile + c) for c in range(blocks_per_tile))
        by_pattern.setdefault(pattern, []).append(t)
    kern = functools.partial(_in_proj_kernel, tile_patterns=tuple(by_pattern.items()))
    return pl.pallas_call(
        kern,
        grid=(T // tm, n_tiles),
        in_specs=[pl.BlockSpec((tm, D_MODEL), lambda i, j: (i, 0)),
                  pl.BlockSpec((1, D_MODEL), lambda i, j: (0, 0)),
                  pl.BlockSpec((None, D_MODEL, tn), lambda i, j: (j, 0, 0)),
                  pl.BlockSpec((D_MODEL, NSA_GROUPS * LANES), lambda i, j: (0, 0)),
                  pl.BlockSpec((tm, HEAD_DIM), lambda i, j: (i, 0)),
                  pl.BlockSpec((tm, HEAD_DIM), lambda i, j: (i, 0))],
        out_specs=[pl.BlockSpec((blocks_per_tile, tm, LANES), lambda i, j: (j, i, 0)),
                   pl.BlockSpec((NSA_GROUPS, tm, LANES), lambda i, j: (0, i, 0))],
        out_shape=[jax.ShapeDtypeStruct((N_BLOCKS, T, LANES), BF16),
                   jax.ShapeDtypeStruct((NSA_GROUPS, T, LANES), F32)],
        scratch_shapes=[pltpu.VMEM((tm, D_MODEL), BF16)],
        compiler_params=pltpu.CompilerParams(
            dimension_semantics=("parallel", "arbitrary"), vmem_limit_bytes=VMEM_LIMIT_BIG),
        name="in_proj",
    )(x2d, g_norm, w_tiles, w_gate, cos, sin)


def _compress_kernel(seg_ref, pe_ref, w1_ref, w2_ref, out_ref):
    half = CMP_STRIDE * HEAD_DIM
    seg = seg_ref[0].astype(F32)
    pe = pe_ref[0]
    top = (seg + pe[:, :half]).astype(BF16)
    bot = (seg + pe[:, half:]).astype(BF16)
    a = jnp.dot(top, w1_ref[0, :half, :], preferred_element_type=F32)
    b = jnp.dot(bot, w1_ref[0, half:, :], preferred_element_type=F32)
    rows = a.shape[0]
    h = a + pltpu.roll(b, rows - 1, axis=0)
    hid = h * _sigmoid(h)
    out_ref[0] = jnp.dot(hid.astype(BF16), w2_ref[0], preferred_element_type=F32).astype(BF16)


def _compress(seg, pe, w1, w2):
    _, R, half = seg.shape
    return pl.pallas_call(
        _compress_kernel,
        grid=(2,),
        in_specs=[pl.BlockSpec((1, R, half), lambda c: (c, 0, 0)),
                  pl.BlockSpec((1, 1, 2 * half), lambda c: (c, 0, 0)),
                  pl.BlockSpec((1, 2 * half, CMP_HIDDEN), lambda c: (c, 0, 0)),
                  pl.BlockSpec((1, CMP_HIDDEN, HEAD_DIM), lambda c: (c, 0, 0))],
        out_specs=pl.BlockSpec((1, R, HEAD_DIM), lambda c: (c, 0, 0)),
        out_shape=jax.ShapeDtypeStruct((2, R, HEAD_DIM), BF16),
        compiler_params=pltpu.CompilerParams(
            dimension_semantics=("arbitrary",), vmem_limit_bytes=VMEM_LIMIT),
        name="compress",
    )(seg, pe, w1, w2)


def _select_bias_t(score_t, n_rows, n_keep):
    sub = 8
    n_groups = score_t.shape[0] // sub
    groups = [score_t[g * sub:(g + 1) * sub, :] for g in range(n_groups)]
    jrow = lax.broadcasted_iota(jnp.int32, groups[0].shape, 0)
    cnts = [jnp.zeros(groups[0].shape, jnp.int32) for _ in range(n_groups)]
    for jp in range(n_rows):
        row = score_t[jp:jp + 1, :]
        for g, grp in enumerate(groups):
            if g * sub > jp:
                beats = row >= grp
            elif g * sub + sub - 1 <= jp:
                beats = row > grp
            else:
                beats = (row > grp) | ((row == grp) & (jrow + g * sub > jp))
            cnts[g] = cnts[g] + beats.astype(jnp.int32)
    cnt = cnts[0] if n_groups == 1 else jnp.concatenate(cnts, axis=0)
    return (cnt < n_keep) & (score_t > -jnp.inf)


def _bias_t(keep_t):
    return jnp.where(keep_t, 0.0, NEG_BIG).astype(F32)


def _bias_columns(bias_t):
    rows, q = bias_t.shape
    if rows < LANES:
        bias_t = jnp.concatenate([bias_t, jnp.zeros((LANES - rows, q), F32)], axis=0)
    return bias_t.T.astype(BF16)


ROW_CHUNK = 64
HEADS_PER_PASS = 2
NORM_ROWS = 16
OUT_SUB_ROWS = 256

def _softmax_static(s, p_ref, *, n_rows, tq, width, q0, k0, window=None):
    n_tiles = width // LANES
    for r0 in range(0, n_rows, ROW_CHUNK):
        rows = slice(r0, r0 + ROW_CHUNK)
        qlo = q0 + r0 % tq
        qhi = qlo + ROW_CHUNK - 1
        kinds = []
        for t in range(n_tiles):
            klo = k0 + t * LANES
            khi = klo + LANES - 1
            none = klo > qhi or (window is not None and khi <= qlo - window)
            full = khi <= qlo and (window is None or klo > qhi - window)
            kinds.append("none" if none else "full" if full else "part")
        mx = None
        masked = {}
        for t, kind in enumerate(kinds):
            if kind == "none":
                continue
            x = s[rows, t * LANES:(t + 1) * LANES]
            if kind == "part":
                qpos = qlo + lax.broadcasted_iota(jnp.int32, x.shape, 0)
                kpos = k0 + t * LANES + lax.broadcasted_iota(jnp.int32, x.shape, 1)
                ok = kpos <= qpos
                if window is not None:
                    ok = ok & (kpos > qpos - window)
                x = jnp.where(ok, x, NEG_BIG)
                masked[t] = x
            mx = x if mx is None else jnp.maximum(mx, x)
        m = jnp.broadcast_to(jnp.max(mx, axis=-1, keepdims=True), mx.shape)
        for t, kind in enumerate(kinds):
            cols = slice(t * LANES, (t + 1) * LANES)
            if kind == "none":
                p_ref[rows, cols] = jnp.zeros((ROW_CHUNK, LANES), BF16)
                continue
            x = masked[t] if kind == "part" else s[rows, cols]
            p_ref[rows, cols] = jnp.exp2(x - m).astype(BF16)


def _pv_normalized(p, v_ones):
    o = jnp.dot(p, v_ones, preferred_element_type=F32)
    return o[:, :HEAD_DIM] / o[:, HEAD_DIM:]


def _moba_kernel(q_ref, k_ref, v_ref, onehot_ref, o_ref, kaug_ref, qaug_ref, kmean_ref,
                 vaug_ref, p_scr, *, n_blk, k_top):
    S = k_ref.shape[0]
    tq = MOBA_BLOCK
    kaug_ref[:, :HEAD_DIM] = k_ref[...]
    kaug_ref[:, HEAD_DIM:] = onehot_ref[...]
    vaug_ref[:, :HEAD_DIM] = v_ref[...]
    vaug_ref[:, HEAD_DIM:] = jnp.ones((S, HEAD_DIM), BF16)
    kmean_ref[...] = jnp.zeros(kmean_ref.shape, F32)
    for j in range(n_blk):
        kb = k_ref[j * MOBA_BLOCK:(j + 1) * MOBA_BLOCK, :].astype(F32)
        kmean_ref[j:j + 1, :] = jnp.sum(kb, axis=0, keepdims=True) * (1.0 / MOBA_BLOCK)

    q = q_ref[...]
    gate_t = _nt_dot(kmean_ref[...].astype(BF16), q)
    jrow = lax.broadcasted_iota(jnp.int32, gate_t.shape, 0)
    own = lax.broadcasted_iota(jnp.int32, gate_t.shape, 1) // MOBA_BLOCK
    gate_t = jnp.where((jrow < own) & jnp.isfinite(gate_t), gate_t, -jnp.inf)
    bias_t = _bias_t(_select_bias_t(gate_t, n_blk, k_top) | (jrow == own))
    qaug_ref[:, :HEAD_DIM] = q
    for c in range(S // tq):
        qaug_ref[c * tq:(c + 1) * tq, HEAD_DIM:] = _bias_columns(bias_t[:, c * tq:(c + 1) * tq])

    for c in range(S // tq):
        w = (c + 1) * tq
        p_c = p_scr.at[c % 2]
        s = _nt_dot(qaug_ref[c * tq:(c + 1) * tq, :], kaug_ref[:w, :])
        _softmax_static(s, p_c, n_rows=tq, tq=tq, width=w, q0=c * tq, k0=0)
        o = _pv_normalized(p_c[:, :w], vaug_ref[:w, :])
        o_ref[c * tq:(c + 1) * tq, :] = o.astype(BF16)


def _moba(proj, onehot, *, B, S):
    T = B * S
    tq = MOBA_BLOCK
    n_blk = S // MOBA_BLOCK
    k_top = min(MOBA_TOPK, n_blk - 1)
    nb8 = -(-n_blk // 8) * 8
    kern = functools.partial(_moba_kernel, n_blk=n_blk, k_top=k_top)
    head_spec = lambda blk: pl.BlockSpec((None, S, LANES), lambda b, h: (blk + h, b, 0))
    return pl.pallas_call(
        kern,
        grid=(B, MOBA_HEADS),
        in_specs=[head_spec(BLK_MQ), head_spec(BLK_MK), head_spec(BLK_MV),
                  pl.BlockSpec((S, LANES), lambda b, h: (0, 0))],
        out_specs=head_spec(0),
        out_shape=jax.ShapeDtypeStruct((MOBA_HEADS, T, LANES), BF16),
        scratch_shapes=[pltpu.VMEM((S, 2 * LANES), BF16), pltpu.VMEM((S, 2 * LANES), BF16),
                        pltpu.VMEM((nb8, HEAD_DIM), F32),
                        pltpu.VMEM((S, 2 * LANES), BF16), pltpu.VMEM((2, tq, S), BF16)],
        compiler_params=pltpu.CompilerParams(
            dimension_semantics=("parallel", "parallel"), vmem_limit_bytes=VMEM_LIMIT),
        name="moba_attn",
    )(proj, proj, proj, onehot)


def _nsa_select_kernel(q_ref, cos_ref, sin_ref, kc_ref, vc_ref, ovt_ref,
                       qrot_ref, bias_ref, oc_ref, *, n_cmp, n_sel_blk, n_top):
    i = pl.program_id(2)
    R, tq, _ = q_ref.shape
    q_raw = q_ref[...]
    cos = (cos_ref[...] * QK_PRESCALE)[None]
    sin = (sin_ref[...] * QK_PRESCALE)[None]
    qrot_ref[...] = _rope(q_raw.astype(F32), cos, sin).astype(BF16)

    n_seg = kc_ref.shape[0]
    s_c = (_nt_dot(q_raw.reshape(R * tq, HEAD_DIM), kc_ref[...]) * SCALE).reshape(R, tq, n_seg)
    n_idx = lax.broadcasted_iota(jnp.int32, s_c.shape, 2)
    pos3 = i * tq + lax.broadcasted_iota(jnp.int32, s_c.shape, 1)
    m_c = (n_idx * CMP_STRIDE + CMP_BLOCK - 1 <= pos3) & (n_idx < n_cmp)
    s_c = jnp.where(m_c, s_c, -jnp.inf)
    mx = jnp.max(s_c, axis=-1, keepdims=True)
    mx = jnp.where(jnp.isfinite(mx), mx, 0.0)
    e_c = jnp.where(m_c, jnp.exp(s_c - mx), 0.0)
    p_c = e_c / jnp.maximum(jnp.sum(e_c, axis=-1, keepdims=True), 1e-30)
    o_c = jnp.dot(p_c.reshape(R * tq, n_seg).astype(BF16), vc_ref[...],
                  preferred_element_type=F32).reshape(R, tq, HEAD_DIM)
    oc_ref[...] = o_c.astype(BF16)

    p_sum = jnp.sum(p_c, axis=0)
    p_hi = p_sum.astype(BF16)
    p_lo = (p_sum - p_hi.astype(F32)).astype(BF16)
    ovt = ovt_ref[...]
    imp_t = _nt_dot(ovt, p_hi) + _nt_dot(ovt, p_lo)
    jrow = lax.broadcasted_iota(jnp.int32, imp_t.shape, 0)
    posq = i * tq + lax.broadcasted_iota(jnp.int32, imp_t.shape, 1)
    own = posq // SEL_BLOCK
    forced = (jrow == 0) | (jrow == own) | (jrow == own - 1)
    future = jrow * SEL_BLOCK > posq
    score_t = jnp.where(future, -jnp.inf, jnp.where(forced, jnp.inf, imp_t))
    keep_t = _select_bias_t(score_t, n_sel_blk, n_top)
    bias_ref[...] = _bias_columns(_bias_t(keep_t))


def _nsa_select(proj, cos, sin, kvc, ovt, *, B, S, tq):
    T = B * S
    nq = S // tq
    n_seg = S // CMP_STRIDE
    n_cmp = n_seg - CMP_BLOCK // CMP_STRIDE + 1
    n_sel_blk = S // SEL_BLOCK
    R = NSA_REP
    kern = functools.partial(_nsa_select_kernel, n_cmp=n_cmp, n_sel_blk=n_sel_blk,
                             n_top=min(SEL_TOPK, n_sel_blk))
    heads_spec = pl.BlockSpec((R, tq, LANES), lambda b, g, i: (g, b * nq + i, 0))
    return pl.pallas_call(
        kern,
        grid=(B, NSA_GROUPS, nq),
        in_specs=[pl.BlockSpec((R, tq, LANES), lambda b, g, i: (BLK_NQ // R + g, b * nq + i, 0)),
                  pl.BlockSpec((tq, LANES), lambda b, g, i: (b * nq + i, 0)),
                  pl.BlockSpec((tq, LANES), lambda b, g, i: (b * nq + i, 0)),
                  pl.BlockSpec((None, n_seg, LANES), lambda b, g, i: (0, g * B + b, 0)),
                  pl.BlockSpec((None, n_seg, LANES), lambda b, g, i: (1, g * B + b, 0)),
                  pl.BlockSpec(ovt.shape, lambda b, g, i: (0, 0))],
        out_specs=[heads_spec,
                   pl.BlockSpec((None, tq, LANES), lambda b, g, i: (g, b * nq + i, 0)),
                   heads_spec],
        out_shape=[jax.ShapeDtypeStruct((NSA_HEADS, T, LANES), BF16),
                   jax.ShapeDtypeStruct((NSA_GROUPS, T, LANES), BF16),
                   jax.ShapeDtypeStruct((NSA_HEADS, T, LANES), BF16)],
        compiler_params=pltpu.CompilerParams(
            dimension_semantics=("parallel", "parallel", "parallel"),
            vmem_limit_bytes=VMEM_LIMIT),
        name="nsa_select",
    )(proj, cos, sin, kvc, kvc, ovt)


def _nsa_sel_kernel(qrot_ref, bias_ref, ks_ref, vs_ref, onehot_ref, o_ref,
                    ksaug_ref, vaug_ref, p_scr, *, tq):
    R, S, _ = qrot_ref.shape
    ksaug_ref[:, :HEAD_DIM] = ks_ref[...]
    ksaug_ref[:, HEAD_DIM:] = onehot_ref[...]
    vaug_ref[:, :HEAD_DIM] = vs_ref[...]
    vaug_ref[:, HEAD_DIM:] = jnp.ones((S, HEAD_DIM), BF16)

    n_pass = 0
    for c in range(S // tq):
        w = (c + 1) * tq
        q_rows = slice(c * tq, (c + 1) * tq)
        bias = bias_ref[q_rows, :]
        for r0 in range(0, R, HEADS_PER_PASS):
            heads = range(r0, r0 + HEADS_PER_PASS)
            p_c = p_scr.at[n_pass % 2]
            n_pass += 1
            q_aug = jnp.concatenate(
                [jnp.concatenate([qrot_ref[r, q_rows, :], bias], axis=1) for r in heads], axis=0)
            s = _nt_dot(q_aug, ksaug_ref[:w, :])
            _softmax_static(s, p_c, n_rows=HEADS_PER_PASS * tq, tq=tq, width=w,
                            q0=c * tq, k0=0)
            o = _pv_normalized(p_c[:, :w], vaug_ref[:w, :])
            for k, r in enumerate(heads):
                o_ref[r, q_rows, :] = o[k * tq:(k + 1) * tq, :].astype(BF16)


def _nsa_sel(q_rot, bias, proj, onehot, *, B, S, tq):
    T = B * S
    R = NSA_REP
    heads_spec = pl.BlockSpec((R, S, LANES), lambda b, g: (g, b, 0))
    kv_spec = lambda blk: pl.BlockSpec((None, S, LANES), lambda b, g: (blk + g, b, 0))
    return pl.pallas_call(
        functools.partial(_nsa_sel_kernel, tq=tq),
        grid=(B, NSA_GROUPS),
        in_specs=[heads_spec,
                  pl.BlockSpec((None, S, LANES), lambda b, g: (g, b, 0)),
                  kv_spec(BLK_NKS), kv_spec(BLK_NVS),
                  pl.BlockSpec((S, LANES), lambda b, g: (0, 0))],
        out_specs=heads_spec,
        out_shape=jax.ShapeDtypeStruct((NSA_HEADS, T, LANES), BF16),
        scratch_shapes=[pltpu.VMEM((S, 2 * LANES), BF16), pltpu.VMEM((S, 2 * LANES), BF16),
                        pltpu.VMEM((2, HEADS_PER_PASS * tq, S), BF16)],
        compiler_params=pltpu.CompilerParams(
            dimension_semantics=("parallel", "parallel"), vmem_limit_bytes=VMEM_LIMIT_BIG),
        name="nsa_sel_attn",
    )(q_rot, bias, proj, proj, onehot)


def _nsa_win_kernel(qrot_ref, kw_ref, vw_ref, oc_ref, os_ref, gate_ref, o_ref,
                    vaug_ref, p_scr, *, tq):
    R, S, _ = qrot_ref.shape
    vaug_ref[:, :HEAD_DIM] = vw_ref[...]
    vaug_ref[:, HEAD_DIM:] = jnp.ones((S, HEAD_DIM), BF16)
    n_pass = 0
    for c in range(S // tq):
        q_rows = slice(c * tq, (c + 1) * tq)
        k0 = max(0, c * tq - WINDOW)
        w = (c + 1) * tq - k0
        gt = _sigmoid(gate_ref[q_rows, :])
        for r0 in range(0, R, HEADS_PER_PASS):
            heads = range(r0, r0 + HEADS_PER_PASS)
            p_c = p_scr.at[n_pass % 2]
            n_pass += 1
            q2 = jnp.concatenate([qrot_ref[r, q_rows, :] for r in heads], axis=0)
            s = _nt_dot(q2, kw_ref[k0:k0 + w, :])
            _softmax_static(s, p_c, n_rows=HEADS_PER_PASS * tq, tq=tq, width=w,
                            q0=c * tq, k0=k0, window=WINDOW)
            o_w = _pv_normalized(p_c[:, :w], vaug_ref[k0:k0 + w, :])
            for k, r in enumerate(heads):
                o = (gt[:, 3 * r:3 * r + 1] * oc_ref[r, q_rows, :].astype(F32)
                     + gt[:, 3 * r + 1:3 * r + 2] * os_ref[r, q_rows, :].astype(F32)
                     + gt[:, 3 * r + 2:3 * r + 3] * o_w[k * tq:(k + 1) * tq, :])
                o_ref[r, q_rows, :] = o.astype(BF16)


def _nsa_win(q_rot, proj, o_c, o_s, gates, *, B, S, tq):
    T = B * S
    R = NSA_REP
    wmax = WINDOW + tq
    heads_spec = pl.BlockSpec((R, S, LANES), lambda b, g: (g, b, 0))
    kv_spec = lambda blk: pl.BlockSpec((None, S, LANES), lambda b, g: (blk + g, b, 0))
    return pl.pallas_call(
        functools.partial(_nsa_win_kernel, tq=tq),
        grid=(B, NSA_GROUPS),
        in_specs=[heads_spec, kv_spec(BLK_NKW), kv_spec(BLK_NVW), heads_spec, heads_spec,
                  pl.BlockSpec((None, S, LANES), lambda b, g: (g, b, 0))],
        out_specs=heads_spec,
        out_shape=jax.ShapeDtypeStruct((NSA_HEADS, T, LANES), BF16),
        scratch_shapes=[pltpu.VMEM((S, 2 * LANES), BF16),
                        pltpu.VMEM((2, HEADS_PER_PASS * tq, wmax), BF16)],
        compiler_params=pltpu.CompilerParams(
            dimension_semantics=("parallel", "parallel"), vmem_limit_bytes=VMEM_LIMIT_BIG),
        name="nsa_win_attn",
    )(q_rot, proj, proj, o_c, o_s, gates)


def _out_proj_kernel(om_ref, on_ref, zm_ref, zn_ref, gm_ref, gn_ref, x_ref, w_ref, gf_ref,
                     out_ref, y_scr):
    def gated_norm(o_ref, z_ref, g_ref, col0, rows):
        n_heads = o_ref.shape[0]
        acts = []
        ss = None
        for h in range(n_heads):
            z = z_ref[h, rows, :].astype(F32)
            a = o_ref[h, rows, :].astype(F32) * (z * _sigmoid(z))
            acts.append(a)
            ss = a * a if ss is None else ss + a * a
        ms = jnp.sum(ss, axis=-1, keepdims=True) * (1.0 / (n_heads * LANES))
        inv = lax.rsqrt(ms + EPS)
        for h in range(n_heads):
            y = acts[h] * inv * g_ref[h]
            y_scr[rows, col0 + h * LANES:col0 + (h + 1) * LANES] = y.astype(BF16)

    for r0 in range(0, x_ref.shape[0], OUT_SUB_ROWS):
        rows = slice(r0, r0 + OUT_SUB_ROWS)
        gated_norm(om_ref, zm_ref, gm_ref, 0, rows)
        gated_norm(on_ref, zn_ref, gn_ref, MOBA_HEADS * LANES, rows)
        r = x_ref[rows, :] + jnp.dot(y_scr[rows, :], w_ref[...], preferred_element_type=F32)
        ms = jnp.mean(r * r, axis=-1, keepdims=True)
        out_ref[rows, :] = r * lax.rsqrt(ms + EPS) * gf_ref[...]


def _out_proj(o_moba, o_nsa, proj, g_moba, g_nsa, x2d, w_out, g_final, *, tm):
    T = x2d.shape[0]
    H = MOBA_HEADS
    return pl.pallas_call(
        _out_proj_kernel,
        grid=(T // tm,),
        in_specs=[pl.BlockSpec((H, tm, LANES), lambda i: (0, i, 0)),
                  pl.BlockSpec((H, tm, LANES), lambda i: (0, i, 0)),
                  pl.BlockSpec((H, tm, LANES), lambda i: (BLK_MZ // H, i, 0)),
                  pl.BlockSpec((H, tm, LANES), lambda i: (BLK_NZ // H, i, 0)),
                  pl.BlockSpec((H, 1, LANES), lambda i: (0, 0, 0)),
                  pl.BlockSpec((H, 1, LANES), lambda i: (0, 0, 0)),
                  pl.BlockSpec((tm, D_MODEL), lambda i: (i, 0)),
                  pl.BlockSpec((D_MODEL, D_MODEL), lambda i: (0, 0), pipeline_mode=pl.Buffered(1)),
                  pl.BlockSpec((1, D_MODEL), lambda i: (0, 0))],
        out_specs=pl.BlockSpec((tm, D_MODEL), lambda i: (i, 0)),
        out_shape=jax.ShapeDtypeStruct((T, D_MODEL), F32),
        scratch_shapes=[pltpu.VMEM((tm, D_MODEL), BF16)],
        compiler_params=pltpu.CompilerParams(
            dimension_semantics=("parallel",), vmem_limit_bytes=VMEM_LIMIT_BIG),
        name="out_proj",
    )(o_moba, o_nsa, proj, proj, g_moba, g_nsa, x2d, w_out, g_final)


def _permute_w_in(w_in):
    mw, nw, kw = MOBA_HEADS * HEAD_DIM, NSA_HEADS * HEAD_DIM, NSA_GROUPS * HEAD_DIM
    sizes = [mw] * 4 + [nw] + [kw] * 6 + [3 * NSA_HEADS, nw]
    offs = np.concatenate([[0], np.cumsum(sizes)])
    names = ["mq", "mk", "mv", "mz", "nq", "nkc", "nvc", "nks", "nvs", "nkw", "nvw", "ng", "nz"]
    part = {n: w_in[:, int(offs[k]):int(offs[k + 1])] for k, n in enumerate(names)}
    assert all(part[n].shape[1] == nb * LANES for n, nb in COLUMN_ORDER)
    w_perm = jnp.concatenate([part[n] for n, _ in COLUMN_ORDER], axis=1).astype(BF16)
    per_group = 3 * NSA_REP
    wg = part["ng"].reshape(D_MODEL, NSA_GROUPS, per_group)
    wg = jnp.pad(wg, ((0, 0), (0, 0), (0, LANES - per_group)))
    return w_perm, wg.reshape(D_MODEL, NSA_GROUPS * LANES).astype(BF16)


def _block_onehot(S, block):
    ids = np.arange(S)[:, None] // block
    return jnp.asarray((ids == np.arange(LANES)[None, :]).astype(np.float32), dtype=BF16)


def _overlap_t(n_seg, n_cmp, n_sel_blk):
    cs = np.arange(n_seg)[None, :] * CMP_STRIDE
    ss = np.arange(n_sel_blk)[:, None] * SEL_BLOCK
    ov = (cs < ss + SEL_BLOCK) & (cs + CMP_BLOCK > ss) & (np.arange(n_seg)[None, :] < n_cmp)
    return jnp.asarray(ov.astype(np.float32), dtype=BF16)


def _layer(x, cos, sin, w_in, g_norm, pe_ck, pe_cv, w_ck1, w_ck2, w_cv1, w_cv2,
           g_out_moba, g_out_nsa, w_out, g_final, *, nsa_tq, tm_in, tm_out, blocks_per_tile):
    B, S, _ = x.shape
    T = B * S
    x2d = x.reshape(T, D_MODEL)
    w_perm, w_gate = _permute_w_in(w_in)
    proj, gates = _in_proj(x2d, g_norm.reshape(1, D_MODEL), w_perm, w_gate, cos, sin,
                           tm=tm_in, blocks_per_tile=blocks_per_tile)

    n_seg = S // CMP_STRIDE
    seg = proj[BLK_NKC:BLK_NKC + 4].reshape(2, NSA_GROUPS * B * n_seg, CMP_STRIDE * HEAD_DIM)
    pe = jnp.stack([pe_ck.reshape(1, -1), pe_cv.reshape(1, -1)])
    w1 = jnp.stack([w_ck1, w_cv1]).astype(BF16)
    w2 = jnp.stack([w_ck2, w_cv2]).astype(BF16)
    kvc = _compress(seg, pe, w1, w2)

    o_moba = _moba(proj, _block_onehot(S, MOBA_BLOCK), B=B, S=S)
    n_cmp = n_seg - CMP_BLOCK // CMP_STRIDE + 1
    q_rot, bias, o_c = _nsa_select(proj, cos, sin, kvc, _overlap_t(n_seg, n_cmp, S // SEL_BLOCK),
                                   B=B, S=S, tq=2 * nsa_tq)
    o_s = _nsa_sel(q_rot, bias, proj, _block_onehot(S, SEL_BLOCK), B=B, S=S, tq=nsa_tq)
    o_nsa = _nsa_win(q_rot, proj, o_c, o_s, gates, B=B, S=S, tq=nsa_tq)
    out = _out_proj(o_moba, o_nsa, proj,
                    g_out_moba.reshape(MOBA_HEADS, 1, LANES), g_out_nsa.reshape(NSA_HEADS, 1, LANES),
                    x2d, w_out.astype(BF16), g_final.reshape(1, D_MODEL), tm=tm_out)
    return out.reshape(B, S, D_MODEL)


def kernel(x, positions, w_in, g_norm, pe_ck, pe_cv, w_ck1, w_ck2, w_cv1, w_cv2,
           g_out_moba, g_out_nsa, w_out, g_final):
    assert w_in.shape[0] == 1, "single-layer problem"
    cos, sin = _rope_tables(positions)
    return _layer(x, cos, sin, w_in[0], g_norm[0], pe_ck[0], pe_cv[0], w_ck1[0], w_ck2[0],
                  w_cv1[0], w_cv2[0], g_out_moba[0], g_out_nsa[0], w_out[0], g_final,
                  nsa_tq=256, tm_in=1024, tm_out=512, blocks_per_tile=10)
```

```python
import functools
from typing import Any, Callable, NamedTuple, Optional

import numpy as np
import jax
import jax.numpy as jnp
from jax import lax
from jax.experimental import pallas as pl
from jax.experimental.pallas import tpu as pltpu

F32 = jnp.float32
BF16 = jnp.bfloat16

D_MODEL = 2048
HEAD_DIM = 128
MOBA_HEADS = 8
NSA_HEADS = 8
NSA_GROUPS = 2
NSA_REP = 4
MOBA_BLOCK = 256
MOBA_TOPK = 3
CMP_BLOCK = 32
CMP_STRIDE = 16
CMP_HIDDEN = 256
SEL_BLOCK = 64
SEL_TOPK = 8
WINDOW = 512
ROPE_THETA = 10000.0
EPS = 1e-6
SCALE = HEAD_DIM ** -0.5
QK_PRESCALE = SCALE * float(np.log2(np.e))
NEG_BIG = -(2.0 ** 100)

LANES = 128
VMEM_LIMIT = 48 * 1024 * 1024
VMEM_LIMIT_BIG = 58 * 1024 * 1024

COLUMN_ORDER = (("mq", 8), ("mk", 8), ("nks", 2), ("nkw", 2), ("nkc", 2), ("nvc", 2),
                ("mv", 8), ("mz", 8), ("nq", 8), ("nz", 8), ("nvs", 2), ("nvw", 2))
BLK_MQ, BLK_MK, BLK_NKS, BLK_NKW, BLK_NKC, BLK_NVC = 0, 8, 16, 18, 20, 22
BLK_MV, BLK_MZ, BLK_NQ, BLK_NZ, BLK_NVS, BLK_NVW = 24, 32, 40, 48, 56, 58
N_BLOCKS = 60
N_ROPE_BLOCKS = 20


def _nt_dot(a, b):
    return lax.dot_general(a, b, (((1,), (1,)), ((), ())), preferred_element_type=F32)


def _sigmoid(x):
    return 1.0 / (1.0 + jnp.exp(-x))


def _rope(a, cos, sin_signed):
    return a * cos + pltpu.roll(a, HEAD_DIM // 2, axis=a.ndim - 1) * sin_signed


def _rope_table_kernel(pos_ref, invf_ref, sign_ref, cos_ref, sin_ref):
    ang = pos_ref[...].astype(F32) * invf_ref[...]
    cos_ref[...] = jnp.cos(ang)
    sin_ref[...] = jnp.sin(ang) * sign_ref[...]


def _rope_tables(positions):
    T = positions.size
    tile = min(T, 2048)
    half = HEAD_DIM // 2
    inv_freq = 1.0 / (ROPE_THETA ** (jnp.arange(0, HEAD_DIM, 2, dtype=F32) / HEAD_DIM))
    invf = jnp.concatenate([inv_freq, inv_freq]).reshape(1, HEAD_DIM)
    sign = jnp.concatenate([-jnp.ones((half,), F32), jnp.ones((half,), F32)]).reshape(1, HEAD_DIM)
    return pl.pallas_call(
        _rope_table_kernel,
        grid=(T // tile,),
        in_specs=[pl.BlockSpec((tile, 1), lambda i: (i, 0)),
                  pl.BlockSpec((1, HEAD_DIM), lambda i: (0, 0)),
                  pl.BlockSpec((1, HEAD_DIM), lambda i: (0, 0))],
        out_specs=[pl.BlockSpec((tile, HEAD_DIM), lambda i: (i, 0)),
                   pl.BlockSpec((tile, HEAD_DIM), lambda i: (i, 0))],
        out_shape=[jax.ShapeDtypeStruct((T, HEAD_DIM), F32)] * 2,
        name="rope_tables",
    )(positions.reshape(T, 1), invf, sign)


def _block_kind(b):
    return "rope_q" if b < BLK_MK else "rope" if b < N_ROPE_BLOCKS else "plain"


def _in_proj_kernel(x_ref, g_ref, w_ref, wg_ref, cos_ref, sin_ref, out_ref, gate_ref, h_scr,
                    *, tile_patterns):
    j = pl.program_id(1)

    @pl.when(j == 0)
    def _():
        def norm_rows(r, carry):
            rows = pl.ds(pl.multiple_of(r * NORM_ROWS, NORM_ROWS), NORM_ROWS)
            x = x_ref[rows, :]
            ms = jnp.mean(x * x, axis=-1, keepdims=True)
            h_scr[rows, :] = (x * lax.rsqrt(ms + EPS) * g_ref[...]).astype(BF16)
            return carry

        lax.fori_loop(0, x_ref.shape[0] // NORM_ROWS, norm_rows, 0, unroll=8)
        gates = jnp.dot(h_scr[...], wg_ref[...], preferred_element_type=F32)
        for g in range(NSA_GROUPS):
            gate_ref[g] = gates[:, g * LANES:(g + 1) * LANES]

    for pattern, tiles in tile_patterns:
        @pl.when(functools.reduce(jnp.logical_or, [j == t for t in tiles]))
        def _(pattern=pattern):
            acc = jnp.dot(h_scr[...], w_ref[...], preferred_element_type=F32)
            if any(kind != "plain" for kind in pattern):
                cos = cos_ref[...]
                sin = sin_ref[...]
            if "rope_q" in pattern:
                cos_q = cos * QK_PRESCALE
                sin_q = sin * QK_PRESCALE
            for c, kind in enumerate(pattern):
                a = acc[:, c * LANES:(c + 1) * LANES]
                if kind == "rope_q":
                    a = _rope(a, cos_q, sin_q)
                elif kind == "rope":
                    a = _rope(a, cos, sin)
                out_ref[c] = a.astype(BF16)


def _in_proj(x2d, g_norm, w_perm, w_gate, cos, sin, *, tm, blocks_per_tile):
    T = x2d.shape[0]
    tn = blocks_per_tile * LANES
    n_tiles = N_BLOCKS // blocks_per_tile
    w_tiles = w_perm.reshape(D_MODEL, n_tiles, tn).transpose(1, 0, 2)
    by_pattern = {}
    for t in range(n_tiles):
        pattern = tuple(_block_kind(t * blocks_per_tile + c) for c in range(blocks_per_tile))
        by_pattern.setdefault(pattern, []).append(t)
    kern = functools.partial(_in_proj_kernel, tile_patterns=tuple(by_pattern.items()))
    return pl.pallas_call(
        kern,
        grid=(T // tm, n_tiles),
        in_specs=[pl.BlockSpec((tm, D_MODEL), lambda i, j: (i, 0)),
                  pl.BlockSpec((1, D_MODEL), lambda i, j: (0, 0)),
                  pl.BlockSpec((None, D_MODEL, tn), lambda i, j: (j, 0, 0)),
                  pl.BlockSpec((D_MODEL, NSA_GROUPS * LANES), lambda i, j: (0, 0)),
                  pl.BlockSpec((tm, HEAD_DIM), lambda i, j: (i, 0)),
                  pl.BlockSpec((tm, HEAD_DIM), lambda i, j: (i, 0))],
        out_specs=[pl.BlockSpec((blocks_per_tile, tm, LANES), lambda i, j: (j, i, 0)),
                   pl.BlockSpec((NSA_GROUPS, tm, LANES), lambda i, j: (0, i, 0))],
        out_shape=[jax.ShapeDtypeStruct((N_BLOCKS, T, LANES), BF16),
                   jax.ShapeDtypeStruct((NSA_GROUPS, T, LANES), F32)],
        scratch_shapes=[pltpu.VMEM((tm, D_MODEL), BF16)],
        compiler_params=pltpu.CompilerParams(
            dimension_semantics=("parallel", "arbitrary"), vmem_limit_bytes=VMEM_LIMIT_BIG),
        name="in_proj",
    )(x2d, g_norm, w_tiles, w_gate, cos, sin)


def _compress_kernel(seg_ref, pe_ref, w1_ref, w2_ref, out_ref):
    half = CMP_STRIDE * HEAD_DIM
    seg = seg_ref[0].astype(F32)
    pe = pe_ref[0]
    top = (seg + pe[:, :half]).astype(BF16)
    bot = (seg + pe[:, half:]).astype(BF16)
    a = jnp.dot(top, w1_ref[0, :half, :], preferred_element_type=F32)
    b = jnp.dot(bot, w1_ref[0, half:, :], preferred_element_type=F32)
    rows = a.shape[0]
    h = a + pltpu.roll(b, rows - 1, axis=0)
    hid = h * _sigmoid(h)
    out_ref[0] = jnp.dot(hid.astype(BF16), w2_ref[0], preferred_element_type=F32).astype(BF16)


def _compress(seg, pe, w1, w2):
    _, R, half = seg.shape
    return pl.pallas_call(
        _compress_kernel,
        grid=(2,),
        in_specs=[pl.BlockSpec((1, R, half), lambda c: (c, 0, 0)),
                  pl.BlockSpec((1, 1, 2 * half), lambda c: (c, 0, 0)),
                  pl.BlockSpec((1, 2 * half, CMP_HIDDEN), lambda c: (c, 0, 0)),
                  pl.BlockSpec((1, CMP_HIDDEN, HEAD_DIM), lambda c: (c, 0, 0))],
        out_specs=pl.BlockSpec((1, R, HEAD_DIM), lambda c: (c, 0, 0)),
        out_shape=jax.ShapeDtypeStruct((2, R, HEAD_DIM), BF16),
        compiler_params=pltpu.CompilerParams(
            dimension_semantics=("arbitrary",), vmem_limit_bytes=VMEM_LIMIT),
        name="compress",
    )(seg, pe, w1, w2)


def _select_bias_t(score_t, n_rows, n_keep):
    sub = 8
    n_groups = score_t.shape[0] // sub
    groups = [score_t[g * sub:(g + 1) * sub, :] for g in range(n_groups)]
    jrow = lax.broadcasted_iota(jnp.int32, groups[0].shape, 0)
    cnts = [jnp.zeros(groups[0].shape, jnp.int32) for _ in range(n_groups)]
    for jp in range(n_rows):
        row = score_t[jp:jp + 1, :]
        for g, grp in enumerate(groups):
            if g * sub > jp:
                beats = row >= grp
            elif g * sub + sub - 1 <= jp:
                beats = row > grp
            else:
                beats = (row > grp) | ((row == grp) & (jrow + g * sub > jp))
            cnts[g] = cnts[g] + beats.astype(jnp.int32)
    cnt = cnts[0] if n_groups == 1 else jnp.concatenate(cnts, axis=0)
    return (cnt < n_keep) & (score_t > -jnp.inf)


def _bias_t(keep_t):
    return jnp.where(keep_t, 0.0, NEG_BIG).astype(F32)


def _bias_columns(bias_t):
    rows, q = bias_t.shape
    if rows < LANES:
        bias_t = jnp.concatenate([bias_t, jnp.zeros((LANES - rows, q), F32)], axis=0)
    return bias_t.T.astype(BF16)


ROW_CHUNK = 64
HEADS_PER_PASS = 2
NORM_ROWS = 16
OUT_SUB_ROWS = 256
MOBA_TQ = 512
KEY_BLOCK = 256


def _softmax_rows(s_blocks, r0, *, tq, q0, k0, window=None):
    per_block = KEY_BLOCK // LANES
    n_tiles = len(s_blocks) * per_block
    rows = slice(r0, r0 + ROW_CHUNK)
    qlo = q0 + r0 % tq
    qhi = qlo + ROW_CHUNK - 1

    def tile(t):
        lane0 = (t % per_block) * LANES
        return s_blocks[t // per_block][rows, lane0:lane0 + LANES]

    kinds = []
    for t in range(n_tiles):
        klo = k0 + t * LANES
        khi = klo + LANES - 1
        none = klo > qhi or (window is not None and khi <= qlo - window)
        full = khi <= qlo and (window is None or klo > qhi - window)
        kinds.append("none" if none else "full" if full else "part")
    mx = None
    masked = {}
    for t, kind in enumerate(kinds):
        if kind == "none":
            continue
        x = tile(t)
        if kind == "part":
            qpos = qlo + lax.broadcasted_iota(jnp.int32, x.shape, 0)
            kpos = k0 + t * LANES + lax.broadcasted_iota(jnp.int32, x.shape, 1)
            ok = kpos <= qpos
            if window is not None:
                ok = ok & (kpos > qpos - window)
            x = jnp.where(ok, x, NEG_BIG)
            masked[t] = x
        mx = x if mx is None else jnp.maximum(mx, x)
    m = jnp.broadcast_to(jnp.max(mx, axis=-1, keepdims=True), mx.shape)
    p_tiles = []
    for t, kind in enumerate(kinds):
        if kind == "none":
            p_tiles.append(jnp.zeros((ROW_CHUNK, LANES), BF16))
            continue
        x = masked[t] if kind == "part" else tile(t)
        p_tiles.append(jnp.exp2(x - m).astype(BF16))
    return jnp.concatenate(p_tiles, axis=1)


def _pv_normalized(p, v_ones):
    o = jnp.dot(p, v_ones, preferred_element_type=F32)
    return o[:, :HEAD_DIM] / o[:, HEAD_DIM:]


class _AttnPass(NamedTuple):
    q: Callable[[], jax.Array]
    k_ref: Any
    v_ref: Any
    k0: int
    width: int
    n_rows: int
    tq: int
    q0: int
    window: Optional[int]
    store: Callable[[jax.Array], None]


def _run_attention(passes):
    def score_thunks(ps):
        q = ps.q()
        return [lambda j=j: _nt_dot(
            q, ps.k_ref[ps.k0 + j * KEY_BLOCK:ps.k0 + (j + 1) * KEY_BLOCK, :])
                for j in range(ps.width // KEY_BLOCK)]

    s_blocks = [thunk() for thunk in score_thunks(passes[0])]
    pending_pv = None
    for t, ps in enumerate(passes):
        mxu_work = [] if pending_pv is None else [pending_pv]
        n_pv = len(mxu_work)
        if t + 1 < len(passes):
            mxu_work += score_thunks(passes[t + 1])
        row_starts = list(range(0, ps.n_rows, ROW_CHUNK))
        results, p_rows = [], []
        for k in range(max(len(mxu_work), len(row_starts))):
            if k < len(mxu_work):
                results.append(mxu_work[k]())
            if k < len(row_starts):
                p_rows.append(_softmax_rows(s_blocks, row_starts[k], tq=ps.tq, q0=ps.q0,
                                            k0=ps.k0, window=ps.window))
        s_blocks = results[n_pv:]

        def pending_pv(ps=ps, p_rows=p_rows):
            p = jnp.concatenate(p_rows, axis=0)
            ps.store(_pv_normalized(p, ps.v_ref[ps.k0:ps.k0 + ps.width, :]))
    pending_pv()


def _moba_kernel(q_ref, k_ref, v_ref, onehot_ref, o_ref, kaug_ref, qaug_ref, kmean_ref,
                 vaug_ref, *, n_blk, k_top):
    S = k_ref.shape[0]
    tq = MOBA_TQ
    kaug_ref[:, :HEAD_DIM] = k_ref[...]
    kaug_ref[:, HEAD_DIM:] = onehot_ref[...]
    vaug_ref[:, :HEAD_DIM] = v_ref[...]
    vaug_ref[:, HEAD_DIM:] = jnp.ones((S, HEAD_DIM), BF16)
    kmean_ref[...] = jnp.zeros(kmean_ref.shape, F32)
    for j in range(n_blk):
        kb = k_ref[j * MOBA_BLOCK:(j + 1) * MOBA_BLOCK, :].astype(F32)
        kmean_ref[j:j + 1, :] = jnp.sum(kb, axis=0, keepdims=True) * (1.0 / MOBA_BLOCK)

    q = q_ref[...]
    gate_t = _nt_dot(kmean_ref[...].astype(BF16), q)
    jrow = lax.broadcasted_iota(jnp.int32, gate_t.shape, 0)
    own = lax.broadcasted_iota(jnp.int32, gate_t.shape, 1) // MOBA_BLOCK
    gate_t = jnp.where((jrow < own) & jnp.isfinite(gate_t), gate_t, -jnp.inf)
    bias_t = _bias_t(_select_bias_t(gate_t, n_blk, k_top) | (jrow == own))
    qaug_ref[:, :HEAD_DIM] = q
    for c in range(S // tq):
        qaug_ref[c * tq:(c + 1) * tq, HEAD_DIM:] = _bias_columns(bias_t[:, c * tq:(c + 1) * tq])

    def tile_pass(c):
        q_rows = slice(c * tq, (c + 1) * tq)

        def store(o):
            o_ref[q_rows, :] = o.astype(BF16)

        return _AttnPass(q=lambda: qaug_ref[q_rows, :], k_ref=kaug_ref, v_ref=vaug_ref, k0=0,
                         width=(c + 1) * tq, n_rows=tq, tq=tq, q0=c * tq, window=None,
                         store=store)

    _run_attention([tile_pass(c) for c in range(S // tq)])


def _moba(proj, onehot, *, B, S):
    T = B * S
    tq = MOBA_BLOCK
    n_blk = S // MOBA_BLOCK
    k_top = min(MOBA_TOPK, n_blk - 1)
    nb8 = -(-n_blk // 8) * 8
    kern = functools.partial(_moba_kernel, n_blk=n_blk, k_top=k_top)
    head_spec = lambda blk: pl.BlockSpec((None, S, LANES), lambda b, h: (blk + h, b, 0))
    return pl.pallas_call(
        kern,
        grid=(B, MOBA_HEADS),
        in_specs=[head_spec(BLK_MQ), head_spec(BLK_MK), head_spec(BLK_MV),
                  pl.BlockSpec((S, LANES), lambda b, h: (0, 0))],
        out_specs=head_spec(0),
        out_shape=jax.ShapeDtypeStruct((MOBA_HEADS, T, LANES), BF16),
        scratch_shapes=[pltpu.VMEM((S, 2 * LANES), BF16), pltpu.VMEM((S, 2 * LANES), BF16),
                        pltpu.VMEM((nb8, HEAD_DIM), F32),
                        pltpu.VMEM((S, 2 * LANES), BF16)],
        compiler_params=pltpu.CompilerParams(
            dimension_semantics=("parallel", "parallel"), vmem_limit_bytes=VMEM_LIMIT),
        name="moba_attn",
    )(proj, proj, proj, onehot)


def _nsa_select_kernel(q_ref, cos_ref, sin_ref, kc_ref, vc_ref, ovt_ref,
                       qrot_ref, bias_ref, oc_ref, *, n_cmp, n_sel_blk, n_top):
    i = pl.program_id(2)
    R, tq, _ = q_ref.shape
    q_raw = q_ref[...]
    cos = (cos_ref[...] * QK_PRESCALE)[None]
    sin = (sin_ref[...] * QK_PRESCALE)[None]
    qrot_ref[...] = _rope(q_raw.astype(F32), cos, sin).astype(BF16)

    n_seg = kc_ref.shape[0]
    s_c = (_nt_dot(q_raw.reshape(R * tq, HEAD_DIM), kc_ref[...]) * SCALE).reshape(R, tq, n_seg)
    n_idx = lax.broadcasted_iota(jnp.int32, s_c.shape, 2)
    pos3 = i * tq + lax.broadcasted_iota(jnp.int32, s_c.shape, 1)
    m_c = (n_idx * CMP_STRIDE + CMP_BLOCK - 1 <= pos3) & (n_idx < n_cmp)
    s_c = jnp.where(m_c, s_c, -jnp.inf)
    mx = jnp.max(s_c, axis=-1, keepdims=True)
    mx = jnp.where(jnp.isfinite(mx), mx, 0.0)
    e_c = jnp.where(m_c, jnp.exp(s_c - mx), 0.0)
    p_c = e_c / jnp.maximum(jnp.sum(e_c, axis=-1, keepdims=True), 1e-30)
    o_c = jnp.dot(p_c.reshape(R * tq, n_seg).astype(BF16), vc_ref[...],
                  preferred_element_type=F32).reshape(R, tq, HEAD_DIM)
    oc_ref[...] = o_c.astype(BF16)

    p_sum = jnp.sum(p_c, axis=0)
    p_hi = p_sum.astype(BF16)
    p_lo = (p_sum - p_hi.astype(F32)).astype(BF16)
    ovt = ovt_ref[...]
    imp_t = _nt_dot(ovt, p_hi) + _nt_dot(ovt, p_lo)
    jrow = lax.broadcasted_iota(jnp.int32, imp_t.shape, 0)
    posq = i * tq + lax.broadcasted_iota(jnp.int32, imp_t.shape, 1)
    own = posq // SEL_BLOCK
    forced = (jrow == 0) | (jrow == own) | (jrow == own - 1)
    future = jrow * SEL_BLOCK > posq
    score_t = jnp.where(future, -jnp.inf, jnp.where(forced, jnp.inf, imp_t))
    keep_t = _select_bias_t(score_t, n_sel_blk, n_top)
    bias_ref[...] = _bias_columns(_bias_t(keep_t))


def _nsa_select(proj, cos, sin, kvc, ovt, *, B, S, tq):
    T = B * S
    nq = S // tq
    n_seg = S // CMP_STRIDE
    n_cmp = n_seg - CMP_BLOCK // CMP_STRIDE + 1
    n_sel_blk = S // SEL_BLOCK
    R = NSA_REP
    kern = functools.partial(_nsa_select_kernel, n_cmp=n_cmp, n_sel_blk=n_sel_blk,
                             n_top=min(SEL_TOPK, n_sel_blk))
    heads_spec = pl.BlockSpec((R, tq, LANES), lambda b, g, i: (g, b * nq + i, 0))
    return pl.pallas_call(
        kern,
        grid=(B, NSA_GROUPS, nq),
        in_specs=[pl.BlockSpec((R, tq, LANES), lambda b, g, i: (BLK_NQ // R + g, b * nq + i, 0)),
                  pl.BlockSpec((tq, LANES), lambda b, g, i: (b * nq + i, 0)),
                  pl.BlockSpec((tq, LANES), lambda b, g, i: (b * nq + i, 0)),
                  pl.BlockSpec((None, n_seg, LANES), lambda b, g, i: (0, g * B + b, 0)),
                  pl.BlockSpec((None, n_seg, LANES), lambda b, g, i: (1, g * B + b, 0)),
                  pl.BlockSpec(ovt.shape, lambda b, g, i: (0, 0))],
        out_specs=[heads_spec,
                   pl.BlockSpec((None, tq, LANES), lambda b, g, i: (g, b * nq + i, 0)),
                   heads_spec],
        out_shape=[jax.ShapeDtypeStruct((NSA_HEADS, T, LANES), BF16),
                   jax.ShapeDtypeStruct((NSA_GROUPS, T, LANES), BF16),
                   jax.ShapeDtypeStruct((NSA_HEADS, T, LANES), BF16)],
        compiler_params=pltpu.CompilerParams(
            dimension_semantics=("parallel", "parallel", "parallel"),
            vmem_limit_bytes=VMEM_LIMIT),
        name="nsa_select",
    )(proj, cos, sin, kvc, kvc, ovt)


def _nsa_sel_kernel(qrot_ref, bias_ref, ks_ref, vs_ref, onehot_ref, o_ref,
                    ksaug_ref, vaug_ref, *, tq):
    R, S, _ = qrot_ref.shape
    ksaug_ref[:, :HEAD_DIM] = ks_ref[...]
    ksaug_ref[:, HEAD_DIM:] = onehot_ref[...]
    vaug_ref[:, :HEAD_DIM] = vs_ref[...]
    vaug_ref[:, HEAD_DIM:] = jnp.ones((S, HEAD_DIM), BF16)

    def head_pass(c, heads):
        q_rows = slice(c * tq, (c + 1) * tq)

        def q_aug():
            bias = bias_ref[q_rows, :]
            return jnp.concatenate(
                [jnp.concatenate([qrot_ref[r, q_rows, :], bias], axis=1) for r in heads], axis=0)

        def store(o):
            for k, r in enumerate(heads):
                o_ref[r, q_rows, :] = o[k * tq:(k + 1) * tq, :].astype(BF16)

        return _AttnPass(q=q_aug, k_ref=ksaug_ref, v_ref=vaug_ref, k0=0, width=(c + 1) * tq,
                         n_rows=len(heads) * tq, tq=tq, q0=c * tq, window=None, store=store)

    _run_attention([head_pass(c, range(r0, r0 + HEADS_PER_PASS))
                    for c in range(S // tq) for r0 in range(0, R, HEADS_PER_PASS)])


def _nsa_sel(q_rot, bias, proj, onehot, *, B, S, tq):
    T = B * S
    R = NSA_REP
    heads_spec = pl.BlockSpec((R, S, LANES), lambda b, g: (g, b, 0))
    kv_spec = lambda blk: pl.BlockSpec((None, S, LANES), lambda b, g: (blk + g, b, 0))
    return pl.pallas_call(
        functools.partial(_nsa_sel_kernel, tq=tq),
        grid=(B, NSA_GROUPS),
        in_specs=[heads_spec,
                  pl.BlockSpec((None, S, LANES), lambda b, g: (g, b, 0)),
                  kv_spec(BLK_NKS), kv_spec(BLK_NVS),
                  pl.BlockSpec((S, LANES), lambda b, g: (0, 0))],
        out_specs=heads_spec,
        out_shape=jax.ShapeDtypeStruct((NSA_HEADS, T, LANES), BF16),
        scratch_shapes=[pltpu.VMEM((S, 2 * LANES), BF16), pltpu.VMEM((S, 2 * LANES), BF16)],
        compiler_params=pltpu.CompilerParams(
            dimension_semantics=("parallel", "parallel"), vmem_limit_bytes=VMEM_LIMIT_BIG),
        name="nsa_sel_attn",
    )(q_rot, bias, proj, proj, onehot)


def _nsa_win_kernel(qrot_ref, kw_ref, vw_ref, oc_ref, os_ref, gate_ref, o_ref,
                    vaug_ref, *, tq):
    R, S, _ = qrot_ref.shape
    vaug_ref[:, :HEAD_DIM] = vw_ref[...]
    vaug_ref[:, HEAD_DIM:] = jnp.ones((S, HEAD_DIM), BF16)
    def head_pass(c, heads):
        q_rows = slice(c * tq, (c + 1) * tq)
        k0 = max(0, c * tq - WINDOW)

        def store(o_w):
            gt = _sigmoid(gate_ref[q_rows, :])
            for k, r in enumerate(heads):
                o = (gt[:, 3 * r:3 * r + 1] * oc_ref[r, q_rows, :].astype(F32)
                     + gt[:, 3 * r + 1:3 * r + 2] * os_ref[r, q_rows, :].astype(F32)
                     + gt[:, 3 * r + 2:3 * r + 3] * o_w[k * tq:(k + 1) * tq, :])
                o_ref[r, q_rows, :] = o.astype(BF16)

        return _AttnPass(
            q=lambda: jnp.concatenate([qrot_ref[r, q_rows, :] for r in heads], axis=0),
            k_ref=kw_ref, v_ref=vaug_ref, k0=k0, width=(c + 1) * tq - k0,
            n_rows=len(heads) * tq, tq=tq, q0=c * tq, window=WINDOW, store=store)

    _run_attention([head_pass(c, range(r0, r0 + HEADS_PER_PASS))
                    for c in range(S // tq) for r0 in range(0, R, HEADS_PER_PASS)])


def _nsa_win(q_rot, proj, o_c, o_s, gates, *, B, S, tq):
    T = B * S
    R = NSA_REP
    wmax = WINDOW + tq
    heads_spec = pl.BlockSpec((R, S, LANES), lambda b, g: (g, b, 0))
    kv_spec = lambda blk: pl.BlockSpec((None, S, LANES), lambda b, g: (blk + g, b, 0))
    return pl.pallas_call(
        functools.partial(_nsa_win_kernel, tq=tq),
        grid=(B, NSA_GROUPS),
        in_specs=[heads_spec, kv_spec(BLK_NKW), kv_spec(BLK_NVW), heads_spec, heads_spec,
                  pl.BlockSpec((None, S, LANES), lambda b, g: (g, b, 0))],
        out_specs=heads_spec,
        out_shape=jax.ShapeDtypeStruct((NSA_HEADS, T, LANES), BF16),
        scratch_shapes=[pltpu.VMEM((S, 2 * LANES), BF16)],
        compiler_params=pltpu.CompilerParams(
            dimension_semantics=("parallel", "parallel"), vmem_limit_bytes=VMEM_LIMIT_BIG),
        name="nsa_win_attn",
    )(q_rot, proj, proj, o_c, o_s, gates)


def _out_proj_kernel(om_ref, on_ref, zm_ref, zn_ref, gm_ref, gn_ref, x_ref, w_ref, gf_ref,
                     out_ref):
    def gated_norm(o_ref, z_ref, g_ref, rows):
        n_heads = o_ref.shape[0]
        acts = []
        ss = None
        for h in range(n_heads):
            z = z_ref[h, rows, :].astype(F32)
            a = o_ref[h, rows, :].astype(F32) * (z * _sigmoid(z))
            acts.append(a)
            ss = a * a if ss is None else ss + a * a
        ms = jnp.sum(ss, axis=-1, keepdims=True) * (1.0 / (n_heads * LANES))
        inv = lax.rsqrt(ms + EPS)
        return [(acts[h] * inv * g_ref[h]).astype(BF16) for h in range(n_heads)]

    for r0 in range(0, x_ref.shape[0], OUT_SUB_ROWS):
        rows = slice(r0, r0 + OUT_SUB_ROWS)
        y = jnp.concatenate(gated_norm(om_ref, zm_ref, gm_ref, rows)
                            + gated_norm(on_ref, zn_ref, gn_ref, rows), axis=1)
        r = x_ref[rows, :] + jnp.dot(y, w_ref[...], preferred_element_type=F32)
        ms = jnp.mean(r * r, axis=-1, keepdims=True)
        out_ref[rows, :] = r * lax.rsqrt(ms + EPS) * gf_ref[...]


def _out_proj(o_moba, o_nsa, proj, g_moba, g_nsa, x2d, w_out, g_final, *, tm):
    T = x2d.shape[0]
    H = MOBA_HEADS
    return pl.pallas_call(
        _out_proj_kernel,
        grid=(T // tm,),
        in_specs=[pl.BlockSpec((H, tm, LANES), lambda i: (0, i, 0)),
                  pl.BlockSpec((H, tm, LANES), lambda i: (0, i, 0)),
                  pl.BlockSpec((H, tm, LANES), lambda i: (BLK_MZ // H, i, 0)),
                  pl.BlockSpec((H, tm, LANES), lambda i: (BLK_NZ // H, i, 0)),
                  pl.BlockSpec((H, 1, LANES), lambda i: (0, 0, 0)),
                  pl.BlockSpec((H, 1, LANES), lambda i: (0, 0, 0)),
                  pl.BlockSpec((tm, D_MODEL), lambda i: (i, 0)),
                  pl.BlockSpec((D_MODEL, D_MODEL), lambda i: (0, 0), pipeline_mode=pl.Buffered(1)),
                  pl.BlockSpec((1, D_MODEL), lambda i: (0, 0))],
        out_specs=pl.BlockSpec((tm, D_MODEL), lambda i: (i, 0)),
        out_shape=jax.ShapeDtypeStruct((T, D_MODEL), F32),
        compiler_params=pltpu.CompilerParams(
            dimension_semantics=("parallel",), vmem_limit_bytes=VMEM_LIMIT_BIG),
        name="out_proj",
    )(o_moba, o_nsa, proj, proj, g_moba, g_nsa, x2d, w_out, g_final)


def _permute_w_in(w_in):
    mw, nw, kw = MOBA_HEADS * HEAD_DIM, NSA_HEADS * HEAD_DIM, NSA_GROUPS * HEAD_DIM
    sizes = [mw] * 4 + [nw] + [kw] * 6 + [3 * NSA_HEADS, nw]
    offs = np.concatenate([[0], np.cumsum(sizes)])
    names = ["mq", "mk", "mv", "mz", "nq", "nkc", "nvc", "nks", "nvs", "nkw", "nvw", "ng", "nz"]
    part = {n: w_in[:, int(offs[k]):int(offs[k + 1])] for k, n in enumerate(names)}
    assert all(part[n].shape[1] == nb * LANES for n, nb in COLUMN_ORDER)
    w_perm = jnp.concatenate([part[n] for n, _ in COLUMN_ORDER], axis=1).astype(BF16)
    per_group = 3 * NSA_REP
    wg = part["ng"].reshape(D_MODEL, NSA_GROUPS, per_group)
    wg = jnp.pad(wg, ((0, 0), (0, 0), (0, LANES - per_group)))
    return w_perm, wg.reshape(D_MODEL, NSA_GROUPS * LANES).astype(BF16)


def _block_onehot(S, block):
    ids = np.arange(S)[:, None] // block
    return jnp.asarray((ids == np.arange(LANES)[None, :]).astype(np.float32), dtype=BF16)


def _overlap_t(n_seg, n_cmp, n_sel_blk):
    cs = np.arange(n_seg)[None, :] * CMP_STRIDE
    ss = np.arange(n_sel_blk)[:, None] * SEL_BLOCK
    ov = (cs < ss + SEL_BLOCK) & (cs + CMP_BLOCK > ss) & (np.arange(n_seg)[None, :] < n_cmp)
    return jnp.asarray(ov.astype(np.float32), dtype=BF16)


def _layer(x, cos, sin, w_in, g_norm, pe_ck, pe_cv, w_ck1, w_ck2, w_cv1, w_cv2,
           g_out_moba, g_out_nsa, w_out, g_final, *, nsa_tq, tm_in, tm_out, blocks_per_tile):
    B, S, _ = x.shape
    T = B * S
    x2d = x.reshape(T, D_MODEL)
    w_perm, w_gate = _permute_w_in(w_in)
    proj, gates = _in_proj(x2d, g_norm.reshape(1, D_MODEL), w_perm, w_gate, cos, sin,
                           tm=tm_in, blocks_per_tile=blocks_per_tile)

    n_seg = S // CMP_STRIDE
    seg = proj[BLK_NKC:BLK_NKC + 4].reshape(2, NSA_GROUPS * B * n_seg, CMP_STRIDE * HEAD_DIM)
    pe = jnp.stack([pe_ck.reshape(1, -1), pe_cv.reshape(1, -1)])
    w1 = jnp.stack([w_ck1, w_cv1]).astype(BF16)
    w2 = jnp.stack([w_ck2, w_cv2]).astype(BF16)
    kvc = _compress(seg, pe, w1, w2)

    o_moba = _moba(proj, _block_onehot(S, MOBA_BLOCK), B=B, S=S)
    n_cmp = n_seg - CMP_BLOCK // CMP_STRIDE + 1
    q_rot, bias, o_c = _nsa_select(proj, cos, sin, kvc, _overlap_t(n_seg, n_cmp, S // SEL_BLOCK),
                                   B=B, S=S, tq=2 * nsa_tq)
    o_s = _nsa_sel(q_rot, bias, proj, _block_onehot(S, SEL_BLOCK), B=B, S=S, tq=nsa_tq)
    o_nsa = _nsa_win(q_rot, proj, o_c, o_s, gates, B=B, S=S, tq=nsa_tq)
    out = _out_proj(o_moba, o_nsa, proj,
                    g_out_moba.reshape(MOBA_HEADS, 1, LANES), g_out_nsa.reshape(NSA_HEADS, 1, LANES),
                    x2d, w_out.astype(BF16), g_final.reshape(1, D_MODEL), tm=tm_out)
    return out.reshape(B, S, D_MODEL)


def kernel(x, positions, w_in, g_norm, pe_ck, pe_cv, w_ck1, w_ck2, w_cv1, w_cv2,
           g_out_moba, g_out_nsa, w_out, g_final):
    assert w_in.shape[0] == 1, "single-layer problem"
    cos, sin = _rope_tables(positions)
    return _layer(x, cos, sin, w_in[0], g_norm[0], pe_ck[0], pe_cv[0], w_ck1[0], w_ck2[0],
                  w_cv1[0], w_cv2[0], g_out_moba[0], g_out_nsa[0], w_out[0], g_final,
                  nsa_tq=256, tm_in=1024, tm_out=512, blocks_per_tile=10)
```

```python
import functools
from typing import Any, Callable, NamedTuple, Optional

import numpy as np
import jax
import jax.numpy as jnp
from jax import lax
from jax.experimental import pallas as pl
from jax.experimental.pallas import tpu as pltpu

F32 = jnp.float32
BF16 = jnp.bfloat16

D_MODEL = 2048
HEAD_DIM = 128
MOBA_HEADS = 8
NSA_HEADS = 8
NSA_GROUPS = 2
NSA_REP = 4
MOBA_BLOCK = 256
MOBA_TOPK = 3
CMP_BLOCK = 32
CMP_STRIDE = 16
CMP_HIDDEN = 256
SEL_BLOCK = 64
SEL_TOPK = 8
WINDOW = 512
ROPE_THETA = 10000.0
EPS = 1e-6
SCALE = HEAD_DIM ** -0.5
QK_PRESCALE = SCALE * float(np.log2(np.e))
NEG_BIG = -(2.0 ** 100)

LANES = 128
VMEM_LIMIT = 48 * 1024 * 1024
VMEM_LIMIT_BIG = 58 * 1024 * 1024

COLUMN_ORDER = (("mq", 8), ("mk", 8), ("nks", 2), ("nkw", 2), ("nkc", 2), ("nvc", 2),
                ("mv", 8), ("mz", 8), ("nq", 8), ("nz", 8), ("nvs", 2), ("nvw", 2))
BLK_MQ, BLK_MK, BLK_NKS, BLK_NKW, BLK_NKC, BLK_NVC = 0, 8, 16, 18, 20, 22
BLK_MV, BLK_MZ, BLK_NQ, BLK_NZ, BLK_NVS, BLK_NVW = 24, 32, 40, 48, 56, 58
N_BLOCKS = 60
N_ROPE_BLOCKS = 20

ROW_CHUNK = 64
HEADS_PER_PASS = 2
NORM_ROWS = 16
IN_SUB_ROWS = 256
MXU_COLS = 256
OUT_SUB_ROWS = 256
MOBA_TQ = 512
KEY_BLOCK = 256


def _nt_dot(a, b):
    return lax.dot_general(a, b, (((1,), (1,)), ((), ())), preferred_element_type=F32)


def _sigmoid(x):
    return 1.0 / (1.0 + jnp.exp(-x))


def _rope(a, cos, sin_signed):
    return a * cos + pltpu.roll(a, HEAD_DIM // 2, axis=a.ndim - 1) * sin_signed


def _interleave(*streams):
    n = max(len(s) for s in streams)
    for k in range(n):
        for s in streams:
            for thunk in s[k * len(s) // n:(k + 1) * len(s) // n]:
                thunk()


def _block_kind(b):
    return "rope_q" if b < BLK_MK else "rope" if b < N_ROPE_BLOCKS else "plain"


def _project_columns(h, w_ref, col0, kinds, cos, sin, store):
    n = len(kinds)
    acc = jnp.dot(h, w_ref[:, col0:col0 + n * LANES], preferred_element_type=F32)
    for c, kind in enumerate(kinds):
        a = acc[:, c * LANES:(c + 1) * LANES]
        if kind == "rope_q":
            a = _rope(a, cos * QK_PRESCALE, sin * QK_PRESCALE)
        elif kind == "rope":
            a = _rope(a, cos, sin)
        store(c, a.astype(BF16))


def _in_proj_kernel(x_ref, g_ref, w_ref, wg_ref, pos_ref, invf_ref, sign_ref,
                    out_ref, gate_ref, cos_ref, sin_ref, h_scr, *, tile_patterns):
    j = pl.program_id(1)
    tm = x_ref.shape[0]
    first_pattern = tile_patterns[0][0]
    assert tile_patterns[0][1] == [0]

    @pl.when(j == 0)
    def _():
        ang = pos_ref[...].astype(F32) * invf_ref[...]
        cos_ref[...] = jnp.cos(ang)
        sin_ref[...] = jnp.sin(ang) * sign_ref[...]

        def norm_piece(r0):
            rows = slice(r0, r0 + NORM_ROWS)
            x = x_ref[rows, :]
            ms = jnp.mean(x * x, axis=-1, keepdims=True)
            h_scr[rows, :] = (x * lax.rsqrt(ms + EPS) * g_ref[...]).astype(BF16)

        def norm_pieces(m):
            return [functools.partial(norm_piece, r0)
                    for r0 in range(m * IN_SUB_ROWS, (m + 1) * IN_SUB_ROWS, NORM_ROWS)]

        def matmul_pieces(m):
            rows = slice(m * IN_SUB_ROWS, (m + 1) * IN_SUB_ROWS)

            def gates():
                acc = jnp.dot(h_scr[rows, :], wg_ref[...], preferred_element_type=F32)
                for g in range(NSA_GROUPS):
                    gate_ref[g, rows, :] = acc[:, g * LANES:(g + 1) * LANES]

            def columns(c0):
                def store(c, val):
                    out_ref[c0 + c, rows, :] = val
                per = MXU_COLS // LANES
                _project_columns(h_scr[rows, :], w_ref, c0 * LANES, first_pattern[c0:c0 + per],
                                 cos_ref[rows, :], sin_ref[rows, :], store)

            return [gates] + [functools.partial(columns, c0)
                              for c0 in range(0, len(first_pattern), MXU_COLS // LANES)]

        n_sub = tm // IN_SUB_ROWS
        _interleave(norm_pieces(0))
        for m in range(n_sub):
            _interleave(matmul_pieces(m), norm_pieces(m + 1) if m + 1 < n_sub else [])

    for pattern, tiles in tile_patterns[1:]:
        @pl.when(functools.reduce(jnp.logical_or, [j == t for t in tiles]))
        def _(pattern=pattern):
            def store(c, val):
                out_ref[c] = val
            _project_columns(h_scr[...], w_ref, 0, pattern, cos_ref[...], sin_ref[...], store)


def _in_proj(x2d, g_norm, w_perm, w_gate, positions, *, tm, blocks_per_tile):
    T = x2d.shape[0]
    tn = blocks_per_tile * LANES
    n_tiles = N_BLOCKS // blocks_per_tile
    w_tiles = w_perm.reshape(D_MODEL, n_tiles, tn).transpose(1, 0, 2)
    by_pattern = {}
    for t in range(n_tiles):
        pattern = tuple(_block_kind(t * blocks_per_tile + c) for c in range(blocks_per_tile))
        by_pattern.setdefault(pattern, []).append(t)
    half = HEAD_DIM // 2
    inv_freq = 1.0 / (ROPE_THETA ** (jnp.arange(0, HEAD_DIM, 2, dtype=F32) / HEAD_DIM))
    invf = jnp.concatenate([inv_freq, inv_freq]).reshape(1, HEAD_DIM)
    sign = jnp.concatenate([-jnp.ones((half,), F32), jnp.ones((half,), F32)]).reshape(1, HEAD_DIM)
    kern = functools.partial(_in_proj_kernel, tile_patterns=tuple(by_pattern.items()))
    row_table = pl.BlockSpec((tm, HEAD_DIM), lambda i, j: (i, 0))
    return pl.pallas_call(
        kern,
        grid=(T // tm, n_tiles),
        in_specs=[pl.BlockSpec((tm, D_MODEL), lambda i, j: (i, 0)),
                  pl.BlockSpec((1, D_MODEL), lambda i, j: (0, 0)),
                  pl.BlockSpec((None, D_MODEL, tn), lambda i, j: (j, 0, 0)),
                  pl.BlockSpec((D_MODEL, NSA_GROUPS * LANES), lambda i, j: (0, 0)),
                  pl.BlockSpec((tm, 1), lambda i, j: (i, 0)),
                  pl.BlockSpec((1, HEAD_DIM), lambda i, j: (0, 0)),
                  pl.BlockSpec((1, HEAD_DIM), lambda i, j: (0, 0))],
        out_specs=[pl.BlockSpec((blocks_per_tile, tm, LANES), lambda i, j: (j, i, 0)),
                   pl.BlockSpec((NSA_GROUPS, tm, LANES), lambda i, j: (0, i, 0)),
                   row_table, row_table],
        out_shape=[jax.ShapeDtypeStruct((N_BLOCKS, T, LANES), BF16),
                   jax.ShapeDtypeStruct((NSA_GROUPS, T, LANES), F32),
                   jax.ShapeDtypeStruct((T, HEAD_DIM), F32),
                   jax.ShapeDtypeStruct((T, HEAD_DIM), F32)],
        scratch_shapes=[pltpu.VMEM((tm, D_MODEL), BF16)],
        compiler_params=pltpu.CompilerParams(
            dimension_semantics=("parallel", "arbitrary"), vmem_limit_bytes=VMEM_LIMIT_BIG),
        name="in_proj",
    )(x2d, g_norm, w_tiles, w_gate, positions.reshape(T, 1), invf, sign)


def _compress_kernel(seg_ref, pe_ref, w1_ref, w2_ref, out_ref):
    half = CMP_STRIDE * HEAD_DIM
    seg = seg_ref[0].astype(F32)
    pe = pe_ref[0]
    top = (seg + pe[:, :half]).astype(BF16)
    bot = (seg + pe[:, half:]).astype(BF16)
    a = jnp.dot(top, w1_ref[0, :half, :], preferred_element_type=F32)
    b = jnp.dot(bot, w1_ref[0, half:, :], preferred_element_type=F32)
    rows = a.shape[0]
    h = a + pltpu.roll(b, rows - 1, axis=0)
    hid = h * _sigmoid(h)
    out_ref[0] = jnp.dot(hid.astype(BF16), w2_ref[0], preferred_element_type=F32).astype(BF16)


def _compress(seg, pe, w1, w2):
    _, R, half = seg.shape
    return pl.pallas_call(
        _compress_kernel,
        grid=(2,),
        in_specs=[pl.BlockSpec((1, R, half), lambda c: (c, 0, 0)),
                  pl.BlockSpec((1, 1, 2 * half), lambda c: (c, 0, 0)),
                  pl.BlockSpec((1, 2 * half, CMP_HIDDEN), lambda c: (c, 0, 0)),
                  pl.BlockSpec((1, CMP_HIDDEN, HEAD_DIM), lambda c: (c, 0, 0))],
        out_specs=pl.BlockSpec((1, R, HEAD_DIM), lambda c: (c, 0, 0)),
        out_shape=jax.ShapeDtypeStruct((2, R, HEAD_DIM), BF16),
        compiler_params=pltpu.CompilerParams(
            dimension_semantics=("arbitrary",), vmem_limit_bytes=VMEM_LIMIT),
        name="compress",
    )(seg, pe, w1, w2)


def _select_bias_t(score_t, n_rows, n_keep):
    sub = 8
    n_groups = score_t.shape[0] // sub
    groups = [score_t[g * sub:(g + 1) * sub, :] for g in range(n_groups)]
    jrow = lax.broadcasted_iota(jnp.int32, groups[0].shape, 0)
    cnts = [jnp.zeros(groups[0].shape, jnp.int32) for _ in range(n_groups)]
    for jp in range(n_rows):
        row = score_t[jp:jp + 1, :]
        for g, grp in enumerate(groups):
            if g * sub > jp:
                beats = row >= grp
            elif g * sub + sub - 1 <= jp:
                beats = row > grp
            else:
                beats = (row > grp) | ((row == grp) & (jrow + g * sub > jp))
            cnts[g] = cnts[g] + beats.astype(jnp.int32)
    cnt = cnts[0] if n_groups == 1 else jnp.concatenate(cnts, axis=0)
    return (cnt < n_keep) & (score_t > -jnp.inf)


def _bias_t(keep_t):
    return jnp.where(keep_t, 0.0, NEG_BIG).astype(F32)


def _bias_columns(bias_t):
    rows, q = bias_t.shape
    if rows < LANES:
        bias_t = jnp.concatenate([bias_t, jnp.zeros((LANES - rows, q), F32)], axis=0)
    return bias_t.T.astype(BF16)


def _softmax_rows(s_blocks, r0, *, tq, q0, k0, window=None):
    per_block = KEY_BLOCK // LANES
    n_tiles = len(s_blocks) * per_block
    rows = slice(r0, r0 + ROW_CHUNK)
    qlo = q0 + r0 % tq
    qhi = qlo + ROW_CHUNK - 1

    def tile(t):
        lane0 = (t % per_block) * LANES
        return s_blocks[t // per_block][rows, lane0:lane0 + LANES]

    kinds = []
    for t in range(n_tiles):
        klo = k0 + t * LANES
        khi = klo + LANES - 1
        none = klo > qhi or (window is not None and khi <= qlo - window)
        full = khi <= qlo and (window is None or klo > qhi - window)
        kinds.append("none" if none else "full" if full else "part")
    mx = None
    masked = {}
    for t, kind in enumerate(kinds):
        if kind == "none":
            continue
        x = tile(t)
        if kind == "part":
            qpos = qlo + lax.broadcasted_iota(jnp.int32, x.shape, 0)
            kpos = k0 + t * LANES + lax.broadcasted_iota(jnp.int32, x.shape, 1)
            ok = kpos <= qpos
            if window is not None:
                ok = ok & (kpos > qpos - window)
            x = jnp.where(ok, x, NEG_BIG)
            masked[t] = x
        mx = x if mx is None else jnp.maximum(mx, x)
    m = jnp.broadcast_to(jnp.max(mx, axis=-1, keepdims=True), mx.shape)
    p_tiles = []
    for t, kind in enumerate(kinds):
        if kind == "none":
            p_tiles.append(jnp.zeros((ROW_CHUNK, LANES), BF16))
            continue
        x = masked[t] if kind == "part" else tile(t)
        p_tiles.append(jnp.exp2(x - m).astype(BF16))
    return jnp.concatenate(p_tiles, axis=1)


def _pv_normalized(p, v_ones):
    o = jnp.dot(p, v_ones, preferred_element_type=F32)
    return o[:, :HEAD_DIM] / o[:, HEAD_DIM:]


class _AttnPass(NamedTuple):
    q: Callable[[], jax.Array]
    k_ref: Any
    v_ref: Any
    k0: int
    width: int
    n_rows: int
    tq: int
    q0: int
    window: Optional[int]
    store: Callable[[jax.Array], None]


def _run_attention(passes):
    def score_thunks(ps):
        q = ps.q()
        return [lambda j=j: _nt_dot(
            q, ps.k_ref[ps.k0 + j * KEY_BLOCK:ps.k0 + (j + 1) * KEY_BLOCK, :])
                for j in range(ps.width // KEY_BLOCK)]

    s_blocks = [thunk() for thunk in score_thunks(passes[0])]
    pending_pv = None
    for t, ps in enumerate(passes):
        mxu_work = [] if pending_pv is None else [pending_pv]
        n_pv = len(mxu_work)
        if t + 1 < len(passes):
            mxu_work += score_thunks(passes[t + 1])
        row_starts = list(range(0, ps.n_rows, ROW_CHUNK))
        results, p_rows = [], []
        for k in range(max(len(mxu_work), len(row_starts))):
            if k < len(mxu_work):
                results.append(mxu_work[k]())
            if k < len(row_starts):
                p_rows.append(_softmax_rows(s_blocks, row_starts[k], tq=ps.tq, q0=ps.q0,
                                            k0=ps.k0, window=ps.window))
        s_blocks = results[n_pv:]

        def pending_pv(ps=ps, p_rows=p_rows):
            p = jnp.concatenate(p_rows, axis=0)
            ps.store(_pv_normalized(p, ps.v_ref[ps.k0:ps.k0 + ps.width, :]))
    pending_pv()


def _moba_kernel(q_ref, k_ref, v_ref, onehot_ref, o_ref, kaug_ref, qaug_ref, kmean_ref,
                 vaug_ref, *, n_blk, k_top):
    S = k_ref.shape[0]
    tq = MOBA_TQ
    kaug_ref[:, :HEAD_DIM] = k_ref[...]
    kaug_ref[:, HEAD_DIM:] = onehot_ref[...]
    vaug_ref[:, :HEAD_DIM] = v_ref[...]
    vaug_ref[:, HEAD_DIM:] = jnp.ones((S, HEAD_DIM), BF16)
    kmean_ref[...] = jnp.zeros(kmean_ref.shape, F32)
    for j in range(n_blk):
        kb = k_ref[j * MOBA_BLOCK:(j + 1) * MOBA_BLOCK, :].astype(F32)
        kmean_ref[j:j + 1, :] = jnp.sum(kb, axis=0, keepdims=True) * (1.0 / MOBA_BLOCK)

    q = q_ref[...]
    gate_t = _nt_dot(kmean_ref[...].astype(BF16), q)
    jrow = lax.broadcasted_iota(jnp.int32, gate_t.shape, 0)
    own = lax.broadcasted_iota(jnp.int32, gate_t.shape, 1) // MOBA_BLOCK
    gate_t = jnp.where((jrow < own) & jnp.isfinite(gate_t), gate_t, -jnp.inf)
    bias_t = _bias_t(_select_bias_t(gate_t, n_blk, k_top) | (jrow == own))
    qaug_ref[:, :HEAD_DIM] = q
    for c in range(S // tq):
        qaug_ref[c * tq:(c + 1) * tq, HEAD_DIM:] = _bias_columns(bias_t[:, c * tq:(c + 1) * tq])

    def tile_pass(c):
        q_rows = slice(c * tq, (c + 1) * tq)

        def store(o):
            o_ref[q_rows, :] = o.astype(BF16)

        return _AttnPass(q=lambda: qaug_ref[q_rows, :], k_ref=kaug_ref, v_ref=vaug_ref, k0=0,
                         width=(c + 1) * tq, n_rows=tq, tq=tq, q0=c * tq, window=None,
                         store=store)

    _run_attention([tile_pass(c) for c in range(S // tq)])


def _moba(proj, onehot, *, B, S):
    T = B * S
    n_blk = S // MOBA_BLOCK
    k_top = min(MOBA_TOPK, n_blk - 1)
    nb8 = -(-n_blk // 8) * 8
    kern = functools.partial(_moba_kernel, n_blk=n_blk, k_top=k_top)
    head_spec = lambda blk: pl.BlockSpec((None, S, LANES), lambda b, h: (blk + h, b, 0))
    return pl.pallas_call(
        kern,
        grid=(B, MOBA_HEADS),
        in_specs=[head_spec(BLK_MQ), head_spec(BLK_MK), head_spec(BLK_MV),
                  pl.BlockSpec((S, LANES), lambda b, h: (0, 0))],
        out_specs=head_spec(0),
        out_shape=jax.ShapeDtypeStruct((MOBA_HEADS, T, LANES), BF16),
        scratch_shapes=[pltpu.VMEM((S, 2 * LANES), BF16), pltpu.VMEM((S, 2 * LANES), BF16),
                        pltpu.VMEM((nb8, HEAD_DIM), F32),
                        pltpu.VMEM((S, 2 * LANES), BF16)],
        compiler_params=pltpu.CompilerParams(
            dimension_semantics=("parallel", "parallel"), vmem_limit_bytes=VMEM_LIMIT),
        name="moba_attn",
    )(proj, proj, proj, onehot)


def _nsa_select_kernel(q_ref, cos_ref, sin_ref, kc_ref, vc_ref, ovt_ref,
                       qrot_ref, bias_ref, oc_ref, *, n_cmp, n_sel_blk, n_top):
    i = pl.program_id(2)
    R, tq, _ = q_ref.shape
    q_raw = q_ref[...]
    cos = (cos_ref[...] * QK_PRESCALE)[None]
    sin = (sin_ref[...] * QK_PRESCALE)[None]
    qrot_ref[...] = _rope(q_raw.astype(F32), cos, sin).astype(BF16)

    n_seg = kc_ref.shape[0]
    s_c = (_nt_dot(q_raw.reshape(R * tq, HEAD_DIM), kc_ref[...]) * SCALE).reshape(R, tq, n_seg)
    n_idx = lax.broadcasted_iota(jnp.int32, s_c.shape, 2)
    pos3 = i * tq + lax.broadcasted_iota(jnp.int32, s_c.shape, 1)
    m_c = (n_idx * CMP_STRIDE + CMP_BLOCK - 1 <= pos3) & (n_idx < n_cmp)
    s_c = jnp.where(m_c, s_c, -jnp.inf)
    mx = jnp.max(s_c, axis=-1, keepdims=True)
    mx = jnp.where(jnp.isfinite(mx), mx, 0.0)
    e_c = jnp.where(m_c, jnp.exp(s_c - mx), 0.0)
    p_c = e_c / jnp.maximum(jnp.sum(e_c, axis=-1, keepdims=True), 1e-30)
    o_c = jnp.dot(p_c.reshape(R * tq, n_seg).astype(BF16), vc_ref[...],
                  preferred_element_type=F32).reshape(R, tq, HEAD_DIM)
    oc_ref[...] = o_c.astype(BF16)

    p_sum = jnp.sum(p_c, axis=0)
    p_hi = p_sum.astype(BF16)
    p_lo = (p_sum - p_hi.astype(F32)).astype(BF16)
    ovt = ovt_ref[...]
    imp_t = _nt_dot(ovt, p_hi) + _nt_dot(ovt, p_lo)
    jrow = lax.broadcasted_iota(jnp.int32, imp_t.shape, 0)
    posq = i * tq + lax.broadcasted_iota(jnp.int32, imp_t.shape, 1)
    own = posq // SEL_BLOCK
    forced = (jrow == 0) | (jrow == own) | (jrow == own - 1)
    future = jrow * SEL_BLOCK > posq
    score_t = jnp.where(future, -jnp.inf, jnp.where(forced, jnp.inf, imp_t))
    keep_t = _select_bias_t(score_t, n_sel_blk, n_top)
    bias_ref[...] = _bias_columns(_bias_t(keep_t))


def _nsa_select(proj, cos, sin, kvc, ovt, *, B, S, tq):
    T = B * S
    nq = S // tq
    n_seg = S // CMP_STRIDE
    n_cmp = n_seg - CMP_BLOCK // CMP_STRIDE + 1
    n_sel_blk = S // SEL_BLOCK
    R = NSA_REP
    kern = functools.partial(_nsa_select_kernel, n_cmp=n_cmp, n_sel_blk=n_sel_blk,
                             n_top=min(SEL_TOPK, n_sel_blk))
    heads_spec = pl.BlockSpec((R, tq, LANES), lambda b, g, i: (g, b * nq + i, 0))
    return pl.pallas_call(
        kern,
        grid=(B, NSA_GROUPS, nq),
        in_specs=[pl.BlockSpec((R, tq, LANES), lambda b, g, i: (BLK_NQ // R + g, b * nq + i, 0)),
                  pl.BlockSpec((tq, LANES), lambda b, g, i: (b * nq + i, 0)),
                  pl.BlockSpec((tq, LANES), lambda b, g, i: (b * nq + i, 0)),
                  pl.BlockSpec((None, n_seg, LANES), lambda b, g, i: (0, g * B + b, 0)),
                  pl.BlockSpec((None, n_seg, LANES), lambda b, g, i: (1, g * B + b, 0)),
                  pl.BlockSpec(ovt.shape, lambda b, g, i: (0, 0))],
        out_specs=[heads_spec,
                   pl.BlockSpec((None, tq, LANES), lambda b, g, i: (g, b * nq + i, 0)),
                   heads_spec],
        out_shape=[jax.ShapeDtypeStruct((NSA_HEADS, T, LANES), BF16),
                   jax.ShapeDtypeStruct((NSA_GROUPS, T, LANES), BF16),
                   jax.ShapeDtypeStruct((NSA_HEADS, T, LANES), BF16)],
        compiler_params=pltpu.CompilerParams(
            dimension_semantics=("parallel", "parallel", "parallel"),
            vmem_limit_bytes=VMEM_LIMIT),
        name="nsa_select",
    )(proj, cos, sin, kvc, kvc, ovt)


def _nsa_sel_kernel(qrot_ref, bias_ref, ks_ref, vs_ref, onehot_ref, o_ref,
                    ksaug_ref, vaug_ref, *, tq):
    R, S, _ = qrot_ref.shape
    ksaug_ref[:, :HEAD_DIM] = ks_ref[...]
    ksaug_ref[:, HEAD_DIM:] = onehot_ref[...]
    vaug_ref[:, :HEAD_DIM] = vs_ref[...]
    vaug_ref[:, HEAD_DIM:] = jnp.ones((S, HEAD_DIM), BF16)

    def head_pass(c, heads):
        q_rows = slice(c * tq, (c + 1) * tq)

        def q_aug():
            bias = bias_ref[q_rows, :]
            return jnp.concatenate(
                [jnp.concatenate([qrot_ref[r, q_rows, :], bias], axis=1) for r in heads], axis=0)

        def store(o):
            for k, r in enumerate(heads):
                o_ref[r, q_rows, :] = o[k * tq:(k + 1) * tq, :].astype(BF16)

        return _AttnPass(q=q_aug, k_ref=ksaug_ref, v_ref=vaug_ref, k0=0, width=(c + 1) * tq,
                         n_rows=len(heads) * tq, tq=tq, q0=c * tq, window=None, store=store)

    _run_attention([head_pass(c, range(r0, r0 + HEADS_PER_PASS))
                    for c in range(S // tq) for r0 in range(0, R, HEADS_PER_PASS)])


def _nsa_sel(q_rot, bias, proj, onehot, *, B, S, tq):
    T = B * S
    R = NSA_REP
    heads_spec = pl.BlockSpec((R, S, LANES), lambda b, g: (g, b, 0))
    kv_spec = lambda blk: pl.BlockSpec((None, S, LANES), lambda b, g: (blk + g, b, 0))
    return pl.pallas_call(
        functools.partial(_nsa_sel_kernel, tq=tq),
        grid=(B, NSA_GROUPS),
        in_specs=[heads_spec,
                  pl.BlockSpec((None, S, LANES), lambda b, g: (g, b, 0)),
                  kv_spec(BLK_NKS), kv_spec(BLK_NVS),
                  pl.BlockSpec((S, LANES), lambda b, g: (0, 0))],
        out_specs=heads_spec,
        out_shape=jax.ShapeDtypeStruct((NSA_HEADS, T, LANES), BF16),
        scratch_shapes=[pltpu.VMEM((S, 2 * LANES), BF16), pltpu.VMEM((S, 2 * LANES), BF16)],
        compiler_params=pltpu.CompilerParams(
            dimension_semantics=("parallel", "parallel"), vmem_limit_bytes=VMEM_LIMIT_BIG),
        name="nsa_sel_attn",
    )(q_rot, bias, proj, proj, onehot)


def _nsa_win_kernel(qrot_ref, kw_ref, vw_ref, oc_ref, os_ref, gate_ref, o_ref,
                    vaug_ref, *, tq):
    R, S, _ = qrot_ref.shape
    vaug_ref[:, :HEAD_DIM] = vw_ref[...]
    vaug_ref[:, HEAD_DIM:] = jnp.ones((S, HEAD_DIM), BF16)

    def head_pass(c, heads):
        q_rows = slice(c * tq, (c + 1) * tq)
        k0 = max(0, c * tq - WINDOW)

        def store(o_w):
            gt = _sigmoid(gate_ref[q_rows, :])
            for k, r in enumerate(heads):
                o = (gt[:, 3 * r:3 * r + 1] * oc_ref[r, q_rows, :].astype(F32)
                     + gt[:, 3 * r + 1:3 * r + 2] * os_ref[r, q_rows, :].astype(F32)
                     + gt[:, 3 * r + 2:3 * r + 3] * o_w[k * tq:(k + 1) * tq, :])
                o_ref[r, q_rows, :] = o.astype(BF16)

        return _AttnPass(
            q=lambda: jnp.concatenate([qrot_ref[r, q_rows, :] for r in heads], axis=0),
            k_ref=kw_ref, v_ref=vaug_ref, k0=k0, width=(c + 1) * tq - k0,
            n_rows=len(heads) * tq, tq=tq, q0=c * tq, window=WINDOW, store=store)

    _run_attention([head_pass(c, range(r0, r0 + HEADS_PER_PASS))
                    for c in range(S // tq) for r0 in range(0, R, HEADS_PER_PASS)])


def _nsa_win(q_rot, proj, o_c, o_s, gates, *, B, S, tq):
    T = B * S
    R = NSA_REP
    heads_spec = pl.BlockSpec((R, S, LANES), lambda b, g: (g, b, 0))
    kv_spec = lambda blk: pl.BlockSpec((None, S, LANES), lambda b, g: (blk + g, b, 0))
    return pl.pallas_call(
        functools.partial(_nsa_win_kernel, tq=tq),
        grid=(B, NSA_GROUPS),
        in_specs=[heads_spec, kv_spec(BLK_NKW), kv_spec(BLK_NVW), heads_spec, heads_spec,
                  pl.BlockSpec((None, S, LANES), lambda b, g: (g, b, 0))],
        out_specs=heads_spec,
        out_shape=jax.ShapeDtypeStruct((NSA_HEADS, T, LANES), BF16),
        scratch_shapes=[pltpu.VMEM((S, 2 * LANES), BF16)],
        compiler_params=pltpu.CompilerParams(
            dimension_semantics=("parallel", "parallel"), vmem_limit_bytes=VMEM_LIMIT_BIG),
        name="nsa_win_attn",
    )(q_rot, proj, proj, o_c, o_s, gates)


def _out_proj_kernel(om_ref, on_ref, zm_ref, zn_ref, gm_ref, gn_ref, x_ref, w_ref, gf_ref,
                     out_ref):
    def gated_norm(o_ref, z_ref, g_ref, rows):
        n_heads = o_ref.shape[0]
        acts = []
        ss = None
        for h in range(n_heads):
            z = z_ref[h, rows, :].astype(F32)
            a = o_ref[h, rows, :].astype(F32) * (z * _sigmoid(z))
            acts.append(a)
            ss = a * a if ss is None else ss + a * a
        ms = jnp.sum(ss, axis=-1, keepdims=True) * (1.0 / (n_heads * LANES))
        inv = lax.rsqrt(ms + EPS)
        return [(acts[h] * inv * g_ref[h]).astype(BF16) for h in range(n_heads)]

    for r0 in range(0, x_ref.shape[0], OUT_SUB_ROWS):
        rows = slice(r0, r0 + OUT_SUB_ROWS)
        y = jnp.concatenate(gated_norm(om_ref, zm_ref, gm_ref, rows)
                            + gated_norm(on_ref, zn_ref, gn_ref, rows), axis=1)
        r = x_ref[rows, :] + jnp.dot(y, w_ref[...], preferred_element_type=F32)
        ms = jnp.mean(r * r, axis=-1, keepdims=True)
        out_ref[rows, :] = r * lax.rsqrt(ms + EPS) * gf_ref[...]


def _out_proj(o_moba, o_nsa, proj, g_moba, g_nsa, x2d, w_out, g_final, *, tm):
    T = x2d.shape[0]
    H = MOBA_HEADS
    return pl.pallas_call(
        _out_proj_kernel,
        grid=(T // tm,),
        in_specs=[pl.BlockSpec((H, tm, LANES), lambda i: (0, i, 0)),
                  pl.BlockSpec((H, tm, LANES), lambda i: (0, i, 0)),
                  pl.BlockSpec((H, tm, LANES), lambda i: (BLK_MZ // H, i, 0)),
                  pl.BlockSpec((H, tm, LANES), lambda i: (BLK_NZ // H, i, 0)),
                  pl.BlockSpec((H, 1, LANES), lambda i: (0, 0, 0)),
                  pl.BlockSpec((H, 1, LANES), lambda i: (0, 0, 0)),
                  pl.BlockSpec((tm, D_MODEL), lambda i: (i, 0)),
                  pl.BlockSpec((D_MODEL, D_MODEL), lambda i: (0, 0), pipeline_mode=pl.Buffered(1)),
                  pl.BlockSpec((1, D_MODEL), lambda i: (0, 0))],
        out_specs=pl.BlockSpec((tm, D_MODEL), lambda i: (i, 0)),
        out_shape=jax.ShapeDtypeStruct((T, D_MODEL), F32),
        compiler_params=pltpu.CompilerParams(
            dimension_semantics=("parallel",), vmem_limit_bytes=VMEM_LIMIT_BIG),
        name="out_proj",
    )(o_moba, o_nsa, proj, proj, g_moba, g_nsa, x2d, w_out, g_final)


def _permute_w_in(w_in):
    mw, nw, kw = MOBA_HEADS * HEAD_DIM, NSA_HEADS * HEAD_DIM, NSA_GROUPS * HEAD_DIM
    sizes = [mw] * 4 + [nw] + [kw] * 6 + [3 * NSA_HEADS, nw]
    offs = np.concatenate([[0], np.cumsum(sizes)])
    names = ["mq", "mk", "mv", "mz", "nq", "nkc", "nvc", "nks", "nvs", "nkw", "nvw", "ng", "nz"]
    part = {n: w_in[:, int(offs[k]):int(offs[k + 1])] for k, n in enumerate(names)}
    assert all(part[n].shape[1] == nb * LANES for n, nb in COLUMN_ORDER)
    w_perm = jnp.concatenate([part[n] for n, _ in COLUMN_ORDER], axis=1).astype(BF16)
    per_group = 3 * NSA_REP
    wg = part["ng"].reshape(D_MODEL, NSA_GROUPS, per_group)
    wg = jnp.pad(wg, ((0, 0), (0, 0), (0, LANES - per_group)))
    return w_perm, wg.reshape(D_MODEL, NSA_GROUPS * LANES).astype(BF16)


def _block_onehot(S, block):
    ids = np.arange(S)[:, None] // block
    return jnp.asarray((ids == np.arange(LANES)[None, :]).astype(np.float32), dtype=BF16)


def _overlap_t(n_seg, n_cmp, n_sel_blk):
    cs = np.arange(n_seg)[None, :] * CMP_STRIDE
    ss = np.arange(n_sel_blk)[:, None] * SEL_BLOCK
    ov = (cs < ss + SEL_BLOCK) & (cs + CMP_BLOCK > ss) & (np.arange(n_seg)[None, :] < n_cmp)
    return jnp.asarray(ov.astype(np.float32), dtype=BF16)


def _layer(x, positions, w_in, g_norm, pe_ck, pe_cv, w_ck1, w_ck2, w_cv1, w_cv2,
           g_out_moba, g_out_nsa, w_out, g_final, *, nsa_tq, tm_in, tm_out, blocks_per_tile):
    B, S, _ = x.shape
    T = B * S
    x2d = x.reshape(T, D_MODEL)
    w_perm, w_gate = _permute_w_in(w_in)
    proj, gates, cos, sin = _in_proj(x2d, g_norm.reshape(1, D_MODEL), w_perm, w_gate, positions,
                                     tm=tm_in, blocks_per_tile=blocks_per_tile)

    n_seg = S // CMP_STRIDE
    seg = proj[BLK_NKC:BLK_NKC + 4].reshape(2, NSA_GROUPS * B * n_seg, CMP_STRIDE * HEAD_DIM)
    pe = jnp.stack([pe_ck.reshape(1, -1), pe_cv.reshape(1, -1)])
    w1 = jnp.stack([w_ck1, w_cv1]).astype(BF16)
    w2 = jnp.stack([w_ck2, w_cv2]).astype(BF16)
    kvc = _compress(seg, pe, w1, w2)

    o_moba = _moba(proj, _block_onehot(S, MOBA_BLOCK), B=B, S=S)
    n_cmp = n_seg - CMP_BLOCK // CMP_STRIDE + 1
    q_rot, bias, o_c = _nsa_select(proj, cos, sin, kvc, _overlap_t(n_seg, n_cmp, S // SEL_BLOCK),
                                   B=B, S=S, tq=2 * nsa_tq)
    o_s = _nsa_sel(q_rot, bias, proj, _block_onehot(S, SEL_BLOCK), B=B, S=S, tq=nsa_tq)
    o_nsa = _nsa_win(q_rot, proj, o_c, o_s, gates, B=B, S=S, tq=nsa_tq)
    out = _out_proj(o_moba, o_nsa, proj,
                    g_out_moba.reshape(MOBA_HEADS, 1, LANES), g_out_nsa.reshape(NSA_HEADS, 1, LANES),
                    x2d, w_out.astype(BF16), g_final.reshape(1, D_MODEL), tm=tm_out)
    return out.reshape(B, S, D_MODEL)


def kernel(x, positions, w_in, g_norm, pe_ck, pe_cv, w_ck1, w_ck2, w_cv1, w_cv2,
           g_out_moba, g_out_nsa, w_out, g_final):
    assert w_in.shape[0] == 1, "single-layer problem"
    return _layer(x, positions, w_in[0], g_norm[0], pe_ck[0], pe_cv[0], w_ck1[0], w_ck2[0],
                  w_cv1[0], w_cv2[0], g_out_moba[0], g_out_nsa[0], w_out[0], g_final,
                  nsa_tq=256, tm_in=1024, tm_out=512, blocks_per_tile=10)
```

```python
import functools
from typing import Any, Callable, NamedTuple, Optional

import numpy as np
import jax
import jax.numpy as jnp
from jax import lax
from jax.experimental import pallas as pl
from jax.experimental.pallas import tpu as pltpu

F32 = jnp.float32
BF16 = jnp.bfloat16

D_MODEL = 2048
HEAD_DIM = 128
MOBA_HEADS = 8
NSA_HEADS = 8
NSA_GROUPS = 2
NSA_REP = 4
MOBA_BLOCK = 256
MOBA_TOPK = 3
CMP_BLOCK = 32
CMP_STRIDE = 16
CMP_HIDDEN = 256
SEL_BLOCK = 64
SEL_TOPK = 8
WINDOW = 512
ROPE_THETA = 10000.0
EPS = 1e-6
SCALE = HEAD_DIM ** -0.5
QK_PRESCALE = SCALE * float(np.log2(np.e))
NEG_BIG = -(2.0 ** 100)

LANES = 128
VMEM_LIMIT = 48 * 1024 * 1024
VMEM_LIMIT_BIG = 58 * 1024 * 1024

COLUMN_ORDER = (("mq", 8), ("mk", 8), ("nks", 2), ("nkw", 2), ("nkc", 2), ("nvc", 2),
                ("mv", 8), ("mz", 8), ("nq", 8), ("nz", 8), ("nvs", 2), ("nvw", 2))
BLK_MQ, BLK_MK, BLK_NKS, BLK_NKW, BLK_NKC, BLK_NVC = 0, 8, 16, 18, 20, 22
BLK_MV, BLK_MZ, BLK_NQ, BLK_NZ, BLK_NVS, BLK_NVW = 24, 32, 40, 48, 56, 58
N_BLOCKS = 60
N_ROPE_BLOCKS = 20
N_SEG_BLOCKS = 4

ROW_CHUNK = 64
HEADS_PER_PASS = 2
NORM_ROWS = 16
IN_SUB_ROWS = 256
MXU_COLS = 256
OUT_SUB_ROWS = 256
MOBA_TQ = 512
KEY_BLOCK = 256


def _nt_dot(a, b):
    return lax.dot_general(a, b, (((1,), (1,)), ((), ())), preferred_element_type=F32)


def _sigmoid(x):
    return 1.0 / (1.0 + jnp.exp(-x))


def _rope(a, cos, sin_signed):
    return a * cos + pltpu.roll(a, HEAD_DIM // 2, axis=a.ndim - 1) * sin_signed


def _interleave(*streams):
    n = max(len(s) for s in streams)
    for k in range(n):
        for s in streams:
            for thunk in s[k * len(s) // n:(k + 1) * len(s) // n]:
                thunk()


def _block_kind(b):
    if BLK_NKC <= b < BLK_NKC + N_SEG_BLOCKS:
        return "seg"
    return "rope_q" if b < BLK_MK else "rope" if b < N_ROPE_BLOCKS else "plain"


def _project_columns(h, w_ref, col0, kinds, cos, sin, store, store_segments=None):
    n = len(kinds)
    acc = jnp.dot(h, w_ref[:, col0:col0 + n * LANES], preferred_element_type=F32)
    n_seg = 0
    for c, kind in enumerate(kinds):
        a = acc[:, c * LANES:(c + 1) * LANES]
        if kind == "rope_q":
            a = _rope(a, cos * QK_PRESCALE, sin * QK_PRESCALE)
        elif kind == "rope":
            a = _rope(a, cos, sin)
        elif kind == "seg":
            store_segments(n_seg, a)
            n_seg += 1
        store(c, a.astype(BF16))


def _in_proj_kernel(x_ref, g_ref, w_ref, wg_ref, pos_ref, invf_ref, sign_ref,
                    out_ref, gate_ref, cos_ref, sin_ref, seg_ref, h_scr, seg_scr,
                    *, tile_patterns):
    j = pl.program_id(1)
    tm = x_ref.shape[0]
    first_pattern = tile_patterns[0][0]
    assert tile_patterns[0][1] == [0]

    @pl.when(j == 0)
    def _():
        ang = pos_ref[...].astype(F32) * invf_ref[...]
        cos_ref[...] = jnp.cos(ang)
        sin_ref[...] = jnp.sin(ang) * sign_ref[...]

        def norm_piece(r0):
            rows = slice(r0, r0 + NORM_ROWS)
            x = x_ref[rows, :]
            ms = jnp.mean(x * x, axis=-1, keepdims=True)
            h_scr[rows, :] = (x * lax.rsqrt(ms + EPS) * g_ref[...]).astype(BF16)

        def norm_pieces(m):
            return [functools.partial(norm_piece, r0)
                    for r0 in range(m * IN_SUB_ROWS, (m + 1) * IN_SUB_ROWS, NORM_ROWS)]

        def matmul_pieces(m):
            rows = slice(m * IN_SUB_ROWS, (m + 1) * IN_SUB_ROWS)

            def gates():
                acc = jnp.dot(h_scr[rows, :], wg_ref[...], preferred_element_type=F32)
                for g in range(NSA_GROUPS):
                    gate_ref[g, rows, :] = acc[:, g * LANES:(g + 1) * LANES]

            def columns(c0):
                def store(c, val):
                    out_ref[c0 + c, rows, :] = val
                per = MXU_COLS // LANES
                _project_columns(h_scr[rows, :], w_ref, c0 * LANES, first_pattern[c0:c0 + per],
                                 cos_ref[rows, :], sin_ref[rows, :], store)

            return [gates] + [functools.partial(columns, c0)
                              for c0 in range(0, len(first_pattern), MXU_COLS // LANES)]

        n_sub = tm // IN_SUB_ROWS
        _interleave(norm_pieces(0))
        for m in range(n_sub):
            _interleave(matmul_pieces(m), norm_pieces(m + 1) if m + 1 < n_sub else [])

    for pattern, tiles in tile_patterns[1:]:
        @pl.when(functools.reduce(jnp.logical_or, [j == t for t in tiles]))
        def _(pattern=pattern):
            def store(c, val):
                out_ref[c] = val

            def store_segments(k, a):
                seg_scr[...] = a
                for t in range(CMP_STRIDE):
                    piece = seg_scr[pl.ds(t, tm // CMP_STRIDE, stride=CMP_STRIDE), :]
                    seg_ref[k, :, t * HEAD_DIM:(t + 1) * HEAD_DIM] = piece.astype(BF16)

            _project_columns(h_scr[...], w_ref, 0, pattern, cos_ref[...], sin_ref[...], store,
                             store_segments)


def _in_proj(x2d, g_norm, w_perm, w_gate, positions, *, tm, blocks_per_tile):
    T = x2d.shape[0]
    tn = blocks_per_tile * LANES
    n_tiles = N_BLOCKS // blocks_per_tile
    w_tiles = w_perm.reshape(D_MODEL, n_tiles, tn).transpose(1, 0, 2)
    by_pattern = {}
    for t in range(n_tiles):
        pattern = tuple(_block_kind(t * blocks_per_tile + c) for c in range(blocks_per_tile))
        by_pattern.setdefault(pattern, []).append(t)
    half = HEAD_DIM // 2
    inv_freq = 1.0 / (ROPE_THETA ** (jnp.arange(0, HEAD_DIM, 2, dtype=F32) / HEAD_DIM))
    invf = jnp.concatenate([inv_freq, inv_freq]).reshape(1, HEAD_DIM)
    sign = jnp.concatenate([-jnp.ones((half,), F32), jnp.ones((half,), F32)]).reshape(1, HEAD_DIM)
    kern = functools.partial(_in_proj_kernel, tile_patterns=tuple(by_pattern.items()))
    row_table = pl.BlockSpec((tm, HEAD_DIM), lambda i, j: (i, 0))
    return pl.pallas_call(
        kern,
        grid=(T // tm, n_tiles),
        in_specs=[pl.BlockSpec((tm, D_MODEL), lambda i, j: (i, 0)),
                  pl.BlockSpec((1, D_MODEL), lambda i, j: (0, 0)),
                  pl.BlockSpec((None, D_MODEL, tn), lambda i, j: (j, 0, 0)),
                  pl.BlockSpec((D_MODEL, NSA_GROUPS * LANES), lambda i, j: (0, 0)),
                  pl.BlockSpec((tm, 1), lambda i, j: (i, 0)),
                  pl.BlockSpec((1, HEAD_DIM), lambda i, j: (0, 0)),
                  pl.BlockSpec((1, HEAD_DIM), lambda i, j: (0, 0))],
        out_specs=[pl.BlockSpec((blocks_per_tile, tm, LANES), lambda i, j: (j, i, 0)),
                   pl.BlockSpec((NSA_GROUPS, tm, LANES), lambda i, j: (0, i, 0)),
                   row_table, row_table,
                   pl.BlockSpec((N_SEG_BLOCKS, tm // CMP_STRIDE, CMP_STRIDE * HEAD_DIM),
                                lambda i, j: (0, i, 0))],
        out_shape=[jax.ShapeDtypeStruct((N_BLOCKS, T, LANES), BF16),
                   jax.ShapeDtypeStruct((NSA_GROUPS, T, LANES), F32),
                   jax.ShapeDtypeStruct((T, HEAD_DIM), F32),
                   jax.ShapeDtypeStruct((T, HEAD_DIM), F32),
                   jax.ShapeDtypeStruct((N_SEG_BLOCKS, T // CMP_STRIDE, CMP_STRIDE * HEAD_DIM),
                                        BF16)],
        scratch_shapes=[pltpu.VMEM((tm, D_MODEL), BF16), pltpu.VMEM((tm, HEAD_DIM), F32)],
        compiler_params=pltpu.CompilerParams(
            dimension_semantics=("parallel", "arbitrary"), vmem_limit_bytes=VMEM_LIMIT_BIG),
        name="in_proj",
    )(x2d, g_norm, w_tiles, w_gate, positions.reshape(T, 1), invf, sign)


def _compress_kernel(seg_ref, pe_ref, w1_ref, w2_ref, out_ref):
    half = CMP_STRIDE * HEAD_DIM
    seg = seg_ref[0].astype(F32)
    pe = pe_ref[0]
    top = (seg + pe[:, :half]).astype(BF16)
    bot = (seg + pe[:, half:]).astype(BF16)
    a = jnp.dot(top, w1_ref[0, :half, :], preferred_element_type=F32)
    b = jnp.dot(bot, w1_ref[0, half:, :], preferred_element_type=F32)
    rows = a.shape[0]
    h = a + pltpu.roll(b, rows - 1, axis=0)
    hid = h * _sigmoid(h)
    out_ref[0] = jnp.dot(hid.astype(BF16), w2_ref[0], preferred_element_type=F32).astype(BF16)


def _compress(seg, pe, w1, w2):
    _, R, half = seg.shape
    return pl.pallas_call(
        _compress_kernel,
        grid=(2,),
        in_specs=[pl.BlockSpec((1, R, half), lambda c: (c, 0, 0)),
                  pl.BlockSpec((1, 1, 2 * half), lambda c: (c, 0, 0)),
                  pl.BlockSpec((1, 2 * half, CMP_HIDDEN), lambda c: (c, 0, 0)),
                  pl.BlockSpec((1, CMP_HIDDEN, HEAD_DIM), lambda c: (c, 0, 0))],
        out_specs=pl.BlockSpec((1, R, HEAD_DIM), lambda c: (c, 0, 0)),
        out_shape=jax.ShapeDtypeStruct((2, R, HEAD_DIM), BF16),
        compiler_params=pltpu.CompilerParams(
            dimension_semantics=("arbitrary",), vmem_limit_bytes=VMEM_LIMIT),
        name="compress",
    )(seg, pe, w1, w2)


def _select_bias_t(score_t, n_rows, n_keep):
    sub = 8
    n_groups = score_t.shape[0] // sub
    groups = [score_t[g * sub:(g + 1) * sub, :] for g in range(n_groups)]
    jrow = lax.broadcasted_iota(jnp.int32, groups[0].shape, 0)
    cnts = [jnp.zeros(groups[0].shape, jnp.int32) for _ in range(n_groups)]
    for jp in range(n_rows):
        row = score_t[jp:jp + 1, :]
        for g, grp in enumerate(groups):
            if g * sub > jp:
                beats = row >= grp
            elif g * sub + sub - 1 <= jp:
                beats = row > grp
            else:
                beats = (row > grp) | ((row == grp) & (jrow + g * sub > jp))
            cnts[g] = cnts[g] + beats.astype(jnp.int32)
    cnt = cnts[0] if n_groups == 1 else jnp.concatenate(cnts, axis=0)
    return (cnt < n_keep) & (score_t > -jnp.inf)


def _bias_t(keep_t):
    return jnp.where(keep_t, 0.0, NEG_BIG).astype(F32)


def _bias_columns(bias_t):
    rows, q = bias_t.shape
    if rows < LANES:
        bias_t = jnp.concatenate([bias_t, jnp.zeros((LANES - rows, q), F32)], axis=0)
    return bias_t.T.astype(BF16)


def _softmax_rows(s_blocks, r0, *, tq, q0, k0, window=None):
    per_block = KEY_BLOCK // LANES
    n_tiles = len(s_blocks) * per_block
    rows = slice(r0, r0 + ROW_CHUNK)
    qlo = q0 + r0 % tq
    qhi = qlo + ROW_CHUNK - 1

    def tile(t):
        lane0 = (t % per_block) * LANES
        return s_blocks[t // per_block][rows, lane0:lane0 + LANES]

    kinds = []
    for t in range(n_tiles):
        klo = k0 + t * LANES
        khi = klo + LANES - 1
        none = klo > qhi or (window is not None and khi <= qlo - window)
        full = khi <= qlo and (window is None or klo > qhi - window)
        kinds.append("none" if none else "full" if full else "part")
    mx = None
    masked = {}
    for t, kind in enumerate(kinds):
        if kind == "none":
            continue
        x = tile(t)
        if kind == "part":
            qpos = qlo + lax.broadcasted_iota(jnp.int32, x.shape, 0)
            kpos = k0 + t * LANES + lax.broadcasted_iota(jnp.int32, x.shape, 1)
            ok = kpos <= qpos
            if window is not None:
                ok = ok & (kpos > qpos - window)
            x = jnp.where(ok, x, NEG_BIG)
            masked[t] = x
        mx = x if mx is None else jnp.maximum(mx, x)
    m = jnp.broadcast_to(jnp.max(mx, axis=-1, keepdims=True), mx.shape)
    p_tiles = []
    for t, kind in enumerate(kinds):
        if kind == "none":
            p_tiles.append(jnp.zeros((ROW_CHUNK, LANES), BF16))
            continue
        x = masked[t] if kind == "part" else tile(t)
        p_tiles.append(jnp.exp2(x - m).astype(BF16))
    return jnp.concatenate(p_tiles, axis=1)


def _pv_normalized(p, v_ones):
    o = jnp.dot(p, v_ones, preferred_element_type=F32)
    return o[:, :HEAD_DIM] / o[:, HEAD_DIM:]


class _AttnPass(NamedTuple):
    q: Callable[[], jax.Array]
    k_ref: Any
    v_ref: Any
    k0: int
    width: int
    n_rows: int
    tq: int
    q0: int
    window: Optional[int]
    store: Callable[[jax.Array], None]


def _run_attention(passes):
    def score_thunks(ps):
        q = ps.q()
        return [lambda j=j: _nt_dot(
            q, ps.k_ref[ps.k0 + j * KEY_BLOCK:ps.k0 + (j + 1) * KEY_BLOCK, :])
                for j in range(ps.width // KEY_BLOCK)]

    s_blocks = [thunk() for thunk in score_thunks(passes[0])]
    pending_pv = None
    for t, ps in enumerate(passes):
        mxu_work = [] if pending_pv is None else [pending_pv]
        n_pv = len(mxu_work)
        if t + 1 < len(passes):
            mxu_work += score_thunks(passes[t + 1])
        row_starts = list(range(0, ps.n_rows, ROW_CHUNK))
        results, p_rows = [], []
        for k in range(max(len(mxu_work), len(row_starts))):
            if k < len(mxu_work):
                results.append(mxu_work[k]())
            if k < len(row_starts):
                p_rows.append(_softmax_rows(s_blocks, row_starts[k], tq=ps.tq, q0=ps.q0,
                                            k0=ps.k0, window=ps.window))
        s_blocks = results[n_pv:]

        def pending_pv(ps=ps, p_rows=p_rows):
            p = jnp.concatenate(p_rows, axis=0)
            ps.store(_pv_normalized(p, ps.v_ref[ps.k0:ps.k0 + ps.width, :]))
    pending_pv()


def _moba_kernel(q_ref, k_ref, v_ref, onehot_ref, o_ref, kaug_ref, qaug_ref, kmean_ref,
                 vaug_ref, *, n_blk, k_top):
    S = k_ref.shape[0]
    tq = MOBA_TQ
    kaug_ref[:, :HEAD_DIM] = k_ref[...]
    kaug_ref[:, HEAD_DIM:] = onehot_ref[...]
    vaug_ref[:, :HEAD_DIM] = v_ref[...]
    vaug_ref[:, HEAD_DIM:] = jnp.ones((S, HEAD_DIM), BF16)
    kmean_ref[...] = jnp.zeros(kmean_ref.shape, F32)
    for j in range(n_blk):
        kb = k_ref[j * MOBA_BLOCK:(j + 1) * MOBA_BLOCK, :].astype(F32)
        kmean_ref[j:j + 1, :] = jnp.sum(kb, axis=0, keepdims=True) * (1.0 / MOBA_BLOCK)

    q = q_ref[...]
    gate_t = _nt_dot(kmean_ref[...].astype(BF16), q)
    jrow = lax.broadcasted_iota(jnp.int32, gate_t.shape, 0)
    own = lax.broadcasted_iota(jnp.int32, gate_t.shape, 1) // MOBA_BLOCK
    gate_t = jnp.where((jrow < own) & jnp.isfinite(gate_t), gate_t, -jnp.inf)
    bias_t = _bias_t(_select_bias_t(gate_t, n_blk, k_top) | (jrow == own))
    qaug_ref[:, :HEAD_DIM] = q
    for c in range(S // tq):
        qaug_ref[c * tq:(c + 1) * tq, HEAD_DIM:] = _bias_columns(bias_t[:, c * tq:(c + 1) * tq])

    def tile_pass(c):
        q_rows = slice(c * tq, (c + 1) * tq)

        def store(o):
            o_ref[q_rows, :] = o.astype(BF16)

        return _AttnPass(q=lambda: qaug_ref[q_rows, :], k_ref=kaug_ref, v_ref=vaug_ref, k0=0,
                         width=(c + 1) * tq, n_rows=tq, tq=tq, q0=c * tq, window=None,
                         store=store)

    _run_attention([tile_pass(c) for c in range(S // tq)])


def _moba(proj, onehot, *, B, S):
    T = B * S
    n_blk = S // MOBA_BLOCK
    k_top = min(MOBA_TOPK, n_blk - 1)
    nb8 = -(-n_blk // 8) * 8
    kern = functools.partial(_moba_kernel, n_blk=n_blk, k_top=k_top)
    head_spec = lambda blk: pl.BlockSpec((None, S, LANES), lambda b, h: (blk + h, b, 0))
    return pl.pallas_call(
        kern,
        grid=(B, MOBA_HEADS),
        in_specs=[head_spec(BLK_MQ), head_spec(BLK_MK), head_spec(BLK_MV),
                  pl.BlockSpec((S, LANES), lambda b, h: (0, 0))],
        out_specs=head_spec(0),
        out_shape=jax.ShapeDtypeStruct((MOBA_HEADS, T, LANES), BF16),
        scratch_shapes=[pltpu.VMEM((S, 2 * LANES), BF16), pltpu.VMEM((S, 2 * LANES), BF16),
                        pltpu.VMEM((nb8, HEAD_DIM), F32),
                        pltpu.VMEM((S, 2 * LANES), BF16)],
        compiler_params=pltpu.CompilerParams(
            dimension_semantics=("parallel", "parallel"), vmem_limit_bytes=VMEM_LIMIT),
        name="moba_attn",
    )(proj, proj, proj, onehot)


def _nsa_select_kernel(q_ref, cos_ref, sin_ref, kc_ref, vc_ref, ovt_ref,
                       qrot_ref, bias_ref, oc_ref, *, n_cmp, n_sel_blk, n_top):
    i = pl.program_id(2)
    R, tq, _ = q_ref.shape
    q_raw = q_ref[...]
    cos = (cos_ref[...] * QK_PRESCALE)[None]
    sin = (sin_ref[...] * QK_PRESCALE)[None]
    qrot_ref[...] = _rope(q_raw.astype(F32), cos, sin).astype(BF16)

    n_seg = kc_ref.shape[0]
    s_c = (_nt_dot(q_raw.reshape(R * tq, HEAD_DIM), kc_ref[...]) * SCALE).reshape(R, tq, n_seg)
    n_idx = lax.broadcasted_iota(jnp.int32, s_c.shape, 2)
    pos3 = i * tq + lax.broadcasted_iota(jnp.int32, s_c.shape, 1)
    m_c = (n_idx * CMP_STRIDE + CMP_BLOCK - 1 <= pos3) & (n_idx < n_cmp)
    s_c = jnp.where(m_c, s_c, -jnp.inf)
    mx = jnp.max(s_c, axis=-1, keepdims=True)
    mx = jnp.where(jnp.isfinite(mx), mx, 0.0)
    e_c = jnp.where(m_c, jnp.exp(s_c - mx), 0.0)
    p_c = e_c / jnp.maximum(jnp.sum(e_c, axis=-1, keepdims=True), 1e-30)
    o_c = jnp.dot(p_c.reshape(R * tq, n_seg).astype(BF16), vc_ref[...],
                  preferred_element_type=F32).reshape(R, tq, HEAD_DIM)
    oc_ref[...] = o_c.astype(BF16)

    p_sum = jnp.sum(p_c, axis=0)
    p_hi = p_sum.astype(BF16)
    p_lo = (p_sum - p_hi.astype(F32)).astype(BF16)
    ovt = ovt_ref[...]
    imp_t = _nt_dot(ovt, p_hi) + _nt_dot(ovt, p_lo)
    jrow = lax.broadcasted_iota(jnp.int32, imp_t.shape, 0)
    posq = i * tq + lax.broadcasted_iota(jnp.int32, imp_t.shape, 1)
    own = posq // SEL_BLOCK
    forced = (jrow == 0) | (jrow == own) | (jrow == own - 1)
    future = jrow * SEL_BLOCK > posq
    score_t = jnp.where(future, -jnp.inf, jnp.where(forced, jnp.inf, imp_t))
    keep_t = _select_bias_t(score_t, n_sel_blk, n_top)
    bias_ref[...] = _bias_columns(_bias_t(keep_t))


def _nsa_select(proj, cos, sin, kvc, ovt, *, B, S, tq):
    T = B * S
    nq = S // tq
    n_seg = S // CMP_STRIDE
    n_cmp = n_seg - CMP_BLOCK // CMP_STRIDE + 1
    n_sel_blk = S // SEL_BLOCK
    R = NSA_REP
    kern = functools.partial(_nsa_select_kernel, n_cmp=n_cmp, n_sel_blk=n_sel_blk,
                             n_top=min(SEL_TOPK, n_sel_blk))
    heads_spec = pl.BlockSpec((R, tq, LANES), lambda b, g, i: (g, b * nq + i, 0))
    return pl.pallas_call(
        kern,
        grid=(B, NSA_GROUPS, nq),
        in_specs=[pl.BlockSpec((R, tq, LANES), lambda b, g, i: (BLK_NQ // R + g, b * nq + i, 0)),
                  pl.BlockSpec((tq, LANES), lambda b, g, i: (b * nq + i, 0)),
                  pl.BlockSpec((tq, LANES), lambda b, g, i: (b * nq + i, 0)),
                  pl.BlockSpec((None, n_seg, LANES), lambda b, g, i: (0, g * B + b, 0)),
                  pl.BlockSpec((None, n_seg, LANES), lambda b, g, i: (1, g * B + b, 0)),
                  pl.BlockSpec(ovt.shape, lambda b, g, i: (0, 0))],
        out_specs=[heads_spec,
                   pl.BlockSpec((None, tq, LANES), lambda b, g, i: (g, b * nq + i, 0)),
                   heads_spec],
        out_shape=[jax.ShapeDtypeStruct((NSA_HEADS, T, LANES), BF16),
                   jax.ShapeDtypeStruct((NSA_GROUPS, T, LANES), BF16),
                   jax.ShapeDtypeStruct((NSA_HEADS, T, LANES), BF16)],
        compiler_params=pltpu.CompilerParams(
            dimension_semantics=("parallel", "parallel", "parallel"),
            vmem_limit_bytes=VMEM_LIMIT),
        name="nsa_select",
    )(proj, cos, sin, kvc, kvc, ovt)


def _nsa_sel_kernel(qrot_ref, bias_ref, ks_ref, vs_ref, onehot_ref, o_ref,
                    ksaug_ref, vaug_ref, *, tq):
    R, S, _ = qrot_ref.shape
    ksaug_ref[:, :HEAD_DIM] = ks_ref[...]
    ksaug_ref[:, HEAD_DIM:] = onehot_ref[...]
    vaug_ref[:, :HEAD_DIM] = vs_ref[...]
    vaug_ref[:, HEAD_DIM:] = jnp.ones((S, HEAD_DIM), BF16)

    def head_pass(c, heads):
        q_rows = slice(c * tq, (c + 1) * tq)

        def q_aug():
            bias = bias_ref[q_rows, :]
            return jnp.concatenate(
                [jnp.concatenate([qrot_ref[r, q_rows, :], bias], axis=1) for r in heads], axis=0)

        def store(o):
            for k, r in enumerate(heads):
                o_ref[r, q_rows, :] = o[k * tq:(k + 1) * tq, :].astype(BF16)

        return _AttnPass(q=q_aug, k_ref=ksaug_ref, v_ref=vaug_ref, k0=0, width=(c + 1) * tq,
                         n_rows=len(heads) * tq, tq=tq, q0=c * tq, window=None, store=store)

    _run_attention([head_pass(c, range(r0, r0 + HEADS_PER_PASS))
                    for c in range(S // tq) for r0 in range(0, R, HEADS_PER_PASS)])


def _nsa_sel(q_rot, bias, proj, onehot, *, B, S, tq):
    T = B * S
    R = NSA_REP
    heads_spec = pl.BlockSpec((R, S, LANES), lambda b, g: (g, b, 0))
    kv_spec = lambda blk: pl.BlockSpec((None, S, LANES), lambda b, g: (blk + g, b, 0))
    return pl.pallas_call(
        functools.partial(_nsa_sel_kernel, tq=tq),
        grid=(B, NSA_GROUPS),
        in_specs=[heads_spec,
                  pl.BlockSpec((None, S, LANES), lambda b, g: (g, b, 0)),
                  kv_spec(BLK_NKS), kv_spec(BLK_NVS),
                  pl.BlockSpec((S, LANES), lambda b, g: (0, 0))],
        out_specs=heads_spec,
        out_shape=jax.ShapeDtypeStruct((NSA_HEADS, T, LANES), BF16),
        scratch_shapes=[pltpu.VMEM((S, 2 * LANES), BF16), pltpu.VMEM((S, 2 * LANES), BF16)],
        compiler_params=pltpu.CompilerParams(
            dimension_semantics=("parallel", "parallel"), vmem_limit_bytes=VMEM_LIMIT_BIG),
        name="nsa_sel_attn",
    )(q_rot, bias, proj, proj, onehot)


def _nsa_win_kernel(qrot_ref, kw_ref, vw_ref, oc_ref, os_ref, gate_ref, o_ref,
                    vaug_ref, *, tq):
    R, S, _ = qrot_ref.shape
    vaug_ref[:, :HEAD_DIM] = vw_ref[...]
    vaug_ref[:, HEAD_DIM:] = jnp.ones((S, HEAD_DIM), BF16)

    def head_pass(c, heads):
        q_rows = slice(c * tq, (c + 1) * tq)
        k0 = max(0, c * tq - WINDOW)

        def store(o_w):
            gt = _sigmoid(gate_ref[q_rows, :])
            for k, r in enumerate(heads):
                o = (gt[:, 3 * r:3 * r + 1] * oc_ref[r, q_rows, :].astype(F32)
                     + gt[:, 3 * r + 1:3 * r + 2] * os_ref[r, q_rows, :].astype(F32)
                     + gt[:, 3 * r + 2:3 * r + 3] * o_w[k * tq:(k + 1) * tq, :])
                o_ref[r, q_rows, :] = o.astype(BF16)

        return _AttnPass(
            q=lambda: jnp.concatenate([qrot_ref[r, q_rows, :] for r in heads], axis=0),
            k_ref=kw_ref, v_ref=vaug_ref, k0=k0, width=(c + 1) * tq - k0,
            n_rows=len(heads) * tq, tq=tq, q0=c * tq, window=WINDOW, store=store)

    _run_attention([head_pass(c, range(r0, r0 + HEADS_PER_PASS))
                    for c in range(S // tq) for r0 in range(0, R, HEADS_PER_PASS)])


def _nsa_win(q_rot, proj, o_c, o_s, gates, *, B, S, tq):
    T = B * S
    R = NSA_REP
    heads_spec = pl.BlockSpec((R, S, LANES), lambda b, g: (g, b, 0))
    kv_spec = lambda blk: pl.BlockSpec((None, S, LANES), lambda b, g: (blk + g, b, 0))
    return pl.pallas_call(
        functools.partial(_nsa_win_kernel, tq=tq),
        grid=(B, NSA_GROUPS),
        in_specs=[heads_spec, kv_spec(BLK_NKW), kv_spec(BLK_NVW), heads_spec, heads_spec,
                  pl.BlockSpec((None, S, LANES), lambda b, g: (g, b, 0))],
        out_specs=heads_spec,
        out_shape=jax.ShapeDtypeStruct((NSA_HEADS, T, LANES), BF16),
        scratch_shapes=[pltpu.VMEM((S, 2 * LANES), BF16)],
        compiler_params=pltpu.CompilerParams(
            dimension_semantics=("parallel", "parallel"), vmem_limit_bytes=VMEM_LIMIT_BIG),
        name="nsa_win_attn",
    )(q_rot, proj, proj, o_c, o_s, gates)


def _out_proj_kernel(om_ref, on_ref, zm_ref, zn_ref, gm_ref, gn_ref, x_ref, w_ref, gf_ref,
                     out_ref):
    def gated_norm(o_ref, z_ref, g_ref, rows):
        n_heads = o_ref.shape[0]
        acts = []
        ss = None
        for h in range(n_heads):
            z = z_ref[h, rows, :].astype(F32)
            a = o_ref[h, rows, :].astype(F32) * (z * _sigmoid(z))
            acts.append(a)
            ss = a * a if ss is None else ss + a * a
        ms = jnp.sum(ss, axis=-1, keepdims=True) * (1.0 / (n_heads * LANES))
        inv = lax.rsqrt(ms + EPS)
        return [(acts[h] * inv * g_ref[h]).astype(BF16) for h in range(n_heads)]

    for r0 in range(0, x_ref.shape[0], OUT_SUB_ROWS):
        rows = slice(r0, r0 + OUT_SUB_ROWS)
        y = jnp.concatenate(gated_norm(om_ref, zm_ref, gm_ref, rows)
                            + gated_norm(on_ref, zn_ref, gn_ref, rows), axis=1)
        r = x_ref[rows, :] + jnp.dot(y, w_ref[...], preferred_element_type=F32)
        ms = jnp.mean(r * r, axis=-1, keepdims=True)
        out_ref[rows, :] = r * lax.rsqrt(ms + EPS) * gf_ref[...]


def _out_proj(o_moba, o_nsa, proj, g_moba, g_nsa, x2d, w_out, g_final, *, tm):
    T = x2d.shape[0]
    H = MOBA_HEADS
    return pl.pallas_call(
        _out_proj_kernel,
        grid=(T // tm,),
        in_specs=[pl.BlockSpec((H, tm, LANES), lambda i: (0, i, 0)),
                  pl.BlockSpec((H, tm, LANES), lambda i: (0, i, 0)),
                  pl.BlockSpec((H, tm, LANES), lambda i: (BLK_MZ // H, i, 0)),
                  pl.BlockSpec((H, tm, LANES), lambda i: (BLK_NZ // H, i, 0)),
                  pl.BlockSpec((H, 1, LANES), lambda i: (0, 0, 0)),
                  pl.BlockSpec((H, 1, LANES), lambda i: (0, 0, 0)),
                  pl.BlockSpec((tm, D_MODEL), lambda i: (i, 0)),
                  pl.BlockSpec((D_MODEL, D_MODEL), lambda i: (0, 0), pipeline_mode=pl.Buffered(1)),
                  pl.BlockSpec((1, D_MODEL), lambda i: (0, 0))],
        out_specs=pl.BlockSpec((tm, D_MODEL), lambda i: (i, 0)),
        out_shape=jax.ShapeDtypeStruct((T, D_MODEL), F32),
        compiler_params=pltpu.CompilerParams(
            dimension_semantics=("parallel",), vmem_limit_bytes=VMEM_LIMIT_BIG),
        name="out_proj",
    )(o_moba, o_nsa, proj, proj, g_moba, g_nsa, x2d, w_out, g_final)


def _permute_w_in(w_in):
    mw, nw, kw = MOBA_HEADS * HEAD_DIM, NSA_HEADS * HEAD_DIM, NSA_GROUPS * HEAD_DIM
    sizes = [mw] * 4 + [nw] + [kw] * 6 + [3 * NSA_HEADS, nw]
    offs = np.concatenate([[0], np.cumsum(sizes)])
    names = ["mq", "mk", "mv", "mz", "nq", "nkc", "nvc", "nks", "nvs", "nkw", "nvw", "ng", "nz"]
    part = {n: w_in[:, int(offs[k]):int(offs[k + 1])] for k, n in enumerate(names)}
    assert all(part[n].shape[1] == nb * LANES for n, nb in COLUMN_ORDER)
    w_perm = jnp.concatenate([part[n] for n, _ in COLUMN_ORDER], axis=1).astype(BF16)
    per_group = 3 * NSA_REP
    wg = part["ng"].reshape(D_MODEL, NSA_GROUPS, per_group)
    wg = jnp.pad(wg, ((0, 0), (0, 0), (0, LANES - per_group)))
    return w_perm, wg.reshape(D_MODEL, NSA_GROUPS * LANES).astype(BF16)


def _block_onehot(S, block):
    ids = np.arange(S)[:, None] // block
    return jnp.asarray((ids == np.arange(LANES)[None, :]).astype(np.float32), dtype=BF16)


def _overlap_t(n_seg, n_cmp, n_sel_blk):
    cs = np.arange(n_seg)[None, :] * CMP_STRIDE
    ss = np.arange(n_sel_blk)[:, None] * SEL_BLOCK
    ov = (cs < ss + SEL_BLOCK) & (cs + CMP_BLOCK > ss) & (np.arange(n_seg)[None, :] < n_cmp)
    return jnp.asarray(ov.astype(np.float32), dtype=BF16)


def _layer(x, positions, w_in, g_norm, pe_ck, pe_cv, w_ck1, w_ck2, w_cv1, w_cv2,
           g_out_moba, g_out_nsa, w_out, g_final, *, nsa_tq, tm_in, tm_out, blocks_per_tile):
    B, S, _ = x.shape
    T = B * S
    x2d = x.reshape(T, D_MODEL)
    w_perm, w_gate = _permute_w_in(w_in)
    proj, gates, cos, sin, seg = _in_proj(x2d, g_norm.reshape(1, D_MODEL), w_perm, w_gate,
                                          positions, tm=tm_in, blocks_per_tile=blocks_per_tile)

    n_seg = S // CMP_STRIDE
    seg = seg.reshape(2, NSA_GROUPS * B * n_seg, CMP_STRIDE * HEAD_DIM)
    pe = jnp.stack([pe_ck.reshape(1, -1), pe_cv.reshape(1, -1)])
    w1 = jnp.stack([w_ck1, w_cv1]).astype(BF16)
    w2 = jnp.stack([w_ck2, w_cv2]).astype(BF16)
    kvc = _compress(seg, pe, w1, w2)

    o_moba = _moba(proj, _block_onehot(S, MOBA_BLOCK), B=B, S=S)
    n_cmp = n_seg - CMP_BLOCK // CMP_STRIDE + 1
    q_rot, bias, o_c = _nsa_select(proj, cos, sin, kvc, _overlap_t(n_seg, n_cmp, S // SEL_BLOCK),
                                   B=B, S=S, tq=2 * nsa_tq)
    o_s = _nsa_sel(q_rot, bias, proj, _block_onehot(S, SEL_BLOCK), B=B, S=S, tq=nsa_tq)
    o_nsa = _nsa_win(q_rot, proj, o_c, o_s, gates, B=B, S=S, tq=nsa_tq)
    out = _out_proj(o_moba, o_nsa, proj,
                    g_out_moba.reshape(MOBA_HEADS, 1, LANES), g_out_nsa.reshape(NSA_HEADS, 1, LANES),
                    x2d, w_out.astype(BF16), g_final.reshape(1, D_MODEL), tm=tm_out)
    return out.reshape(B, S, D_MODEL)


def kernel(x, positions, w_in, g_norm, pe_ck, pe_cv, w_ck1, w_ck2, w_cv1, w_cv2,
           g_out_moba, g_out_nsa, w_out, g_final):
    assert w_in.shape[0] == 1, "single-layer problem"
    return _layer(x, positions, w_in[0], g_norm[0], pe_ck[0], pe_cv[0], w_ck1[0], w_ck2[0],
                  w_cv1[0], w_cv2[0], g_out_moba[0], g_out_nsa[0], w_out[0], g_final,
                  nsa_tq=256, tm_in=1024, tm_out=512, blocks_per_tile=10)
```

```python
import functools
from typing import Any, Callable, NamedTuple, Optional

import numpy as np
import jax
import jax.numpy as jnp
from jax import lax
from jax.experimental import pallas as pl
from jax.experimental.pallas import tpu as pltpu

F32 = jnp.float32
BF16 = jnp.bfloat16

D_MODEL = 2048
HEAD_DIM = 128
MOBA_HEADS = 8
NSA_HEADS = 8
NSA_GROUPS = 2
NSA_REP = 4
MOBA_BLOCK = 256
MOBA_TOPK = 3
CMP_BLOCK = 32
CMP_STRIDE = 16
CMP_HIDDEN = 256
SEL_BLOCK = 64
SEL_TOPK = 8
WINDOW = 512
ROPE_THETA = 10000.0
EPS = 1e-6
SCALE = HEAD_DIM ** -0.5
QK_PRESCALE = SCALE * float(np.log2(np.e))
NEG_BIG = -(2.0 ** 100)

LANES = 128
VMEM_LIMIT = 48 * 1024 * 1024
VMEM_LIMIT_BIG = 58 * 1024 * 1024

COLUMN_ORDER = (("mq", 8), ("mk", 8), ("nks", 2), ("nkw", 2), ("nkc", 2), ("nvc", 2),
                ("mv", 8), ("mz", 8), ("nq", 8), ("nz", 8), ("nvs", 2), ("nvw", 2))
BLK_MQ, BLK_MK, BLK_NKS, BLK_NKW, BLK_NKC, BLK_NVC = 0, 8, 16, 18, 20, 22
BLK_MV, BLK_MZ, BLK_NQ, BLK_NZ, BLK_NVS, BLK_NVW = 24, 32, 40, 48, 56, 58
N_BLOCKS = 60
N_ROPE_BLOCKS = 20
N_SEG_BLOCKS = 4

ROW_CHUNK = 64
HEADS_PER_PASS = 2
NORM_ROWS = 16
IN_SUB_ROWS = 256
MXU_COLS = 256
OUT_SUB_ROWS = 256
MOBA_TQ = 512
KEY_BLOCK = 256


def _nt_dot(a, b):
    return lax.dot_general(a, b, (((1,), (1,)), ((), ())), preferred_element_type=F32)


def _sigmoid(x):
    return 1.0 / (1.0 + jnp.exp(-x))


def _rope(a, cos, sin_signed):
    return a * cos + pltpu.roll(a, HEAD_DIM // 2, axis=a.ndim - 1) * sin_signed


def _interleave(*streams):
    n = max(len(s) for s in streams)
    for k in range(n):
        for s in streams:
            for thunk in s[k * len(s) // n:(k + 1) * len(s) // n]:
                thunk()


def _block_kind(b):
    if BLK_NKC <= b < BLK_NKC + N_SEG_BLOCKS:
        return "seg"
    return "rope_q" if b < BLK_MK else "rope" if b < N_ROPE_BLOCKS else "plain"


def _project_columns(h, w_ref, col0, kinds, cos, sin, store, store_segments=None):
    n = len(kinds)
    acc = jnp.dot(h, w_ref[:, col0:col0 + n * LANES], preferred_element_type=F32)
    n_seg = 0
    for c, kind in enumerate(kinds):
        a = acc[:, c * LANES:(c + 1) * LANES]
        if kind == "rope_q":
            a = _rope(a, cos * QK_PRESCALE, sin * QK_PRESCALE)
        elif kind == "rope":
            a = _rope(a, cos, sin)
        elif kind == "seg":
            store_segments(n_seg, a)
            n_seg += 1
        store(c, a.astype(BF16))


def _in_proj_kernel(x_ref, g_ref, w_ref, wg_ref, pos_ref, invf_ref, sign_ref,
                    out_ref, gate_ref, cos_ref, sin_ref, seg_ref, h_scr, seg_scr,
                    *, tile_patterns):
    j = pl.program_id(1)
    tm = x_ref.shape[0]
    first_pattern = tile_patterns[0][0]
    assert tile_patterns[0][1] == [0]

    @pl.when(j == 0)
    def _():
        ang = pos_ref[...].astype(F32) * invf_ref[...]
        cos_ref[...] = jnp.cos(ang)
        sin_ref[...] = jnp.sin(ang) * sign_ref[...]

        def norm_piece(r0):
            rows = slice(r0, r0 + NORM_ROWS)
            x = x_ref[rows, :]
            ms = jnp.mean(x * x, axis=-1, keepdims=True)
            h_scr[rows, :] = (x * lax.rsqrt(ms + EPS) * g_ref[...]).astype(BF16)

        def norm_pieces(m):
            return [functools.partial(norm_piece, r0)
                    for r0 in range(m * IN_SUB_ROWS, (m + 1) * IN_SUB_ROWS, NORM_ROWS)]

        def matmul_pieces(m):
            rows = slice(m * IN_SUB_ROWS, (m + 1) * IN_SUB_ROWS)

            def gates():
                acc = jnp.dot(h_scr[rows, :], wg_ref[...], preferred_element_type=F32)
                for g in range(NSA_GROUPS):
                    gate_ref[g, rows, :] = acc[:, g * LANES:(g + 1) * LANES]

            def columns(c0):
                def store(c, val):
                    out_ref[c0 + c, rows, :] = val
                per = MXU_COLS // LANES
                _project_columns(h_scr[rows, :], w_ref, c0 * LANES, first_pattern[c0:c0 + per],
                                 cos_ref[rows, :], sin_ref[rows, :], store)

            return [gates] + [functools.partial(columns, c0)
                              for c0 in range(0, len(first_pattern), MXU_COLS // LANES)]

        n_sub = tm // IN_SUB_ROWS
        _interleave(norm_pieces(0))
        for m in range(n_sub):
            _interleave(matmul_pieces(m), norm_pieces(m + 1) if m + 1 < n_sub else [])

    for pattern, tiles in tile_patterns[1:]:
        @pl.when(functools.reduce(jnp.logical_or, [j == t for t in tiles]))
        def _(pattern=pattern):
            def store(c, val):
                out_ref[c] = val

            def store_segments(k, a):
                seg_scr[...] = a
                for t in range(CMP_STRIDE):
                    piece = seg_scr[pl.ds(t, tm // CMP_STRIDE, stride=CMP_STRIDE), :]
                    seg_ref[k, :, t * HEAD_DIM:(t + 1) * HEAD_DIM] = piece.astype(BF16)

            _project_columns(h_scr[...], w_ref, 0, pattern, cos_ref[...], sin_ref[...], store,
                             store_segments)


def _in_proj(x2d, g_norm, w_tiles, w_gate, positions, *, tm, blocks_per_tile):
    T = x2d.shape[0]
    tn = blocks_per_tile * LANES
    n_tiles = N_BLOCKS // blocks_per_tile
    by_pattern = {}
    for t in range(n_tiles):
        pattern = tuple(_block_kind(t * blocks_per_tile + c) for c in range(blocks_per_tile))
        by_pattern.setdefault(pattern, []).append(t)
    half = HEAD_DIM // 2
    inv_freq = 1.0 / (ROPE_THETA ** (jnp.arange(0, HEAD_DIM, 2, dtype=F32) / HEAD_DIM))
    invf = jnp.concatenate([inv_freq, inv_freq]).reshape(1, HEAD_DIM)
    sign = jnp.concatenate([-jnp.ones((half,), F32), jnp.ones((half,), F32)]).reshape(1, HEAD_DIM)
    kern = functools.partial(_in_proj_kernel, tile_patterns=tuple(by_pattern.items()))
    row_table = pl.BlockSpec((tm, HEAD_DIM), lambda i, j: (i, 0))
    return pl.pallas_call(
        kern,
        grid=(T // tm, n_tiles),
        in_specs=[pl.BlockSpec((tm, D_MODEL), lambda i, j: (i, 0)),
                  pl.BlockSpec((1, D_MODEL), lambda i, j: (0, 0)),
                  pl.BlockSpec((None, D_MODEL, tn), lambda i, j: (j, 0, 0)),
                  pl.BlockSpec((D_MODEL, NSA_GROUPS * LANES), lambda i, j: (0, 0)),
                  pl.BlockSpec((tm, 1), lambda i, j: (i, 0)),
                  pl.BlockSpec((1, HEAD_DIM), lambda i, j: (0, 0)),
                  pl.BlockSpec((1, HEAD_DIM), lambda i, j: (0, 0))],
        out_specs=[pl.BlockSpec((blocks_per_tile, tm, LANES), lambda i, j: (j, i, 0)),
                   pl.BlockSpec((NSA_GROUPS, tm, LANES), lambda i, j: (0, i, 0)),
                   row_table, row_table,
                   pl.BlockSpec((N_SEG_BLOCKS, tm // CMP_STRIDE, CMP_STRIDE * HEAD_DIM),
                                lambda i, j: (0, i, 0))],
        out_shape=[jax.ShapeDtypeStruct((N_BLOCKS, T, LANES), BF16),
                   jax.ShapeDtypeStruct((NSA_GROUPS, T, LANES), F32),
                   jax.ShapeDtypeStruct((T, HEAD_DIM), F32),
                   jax.ShapeDtypeStruct((T, HEAD_DIM), F32),
                   jax.ShapeDtypeStruct((N_SEG_BLOCKS, T // CMP_STRIDE, CMP_STRIDE * HEAD_DIM),
                                        BF16)],
        scratch_shapes=[pltpu.VMEM((tm, D_MODEL), BF16), pltpu.VMEM((tm, HEAD_DIM), F32)],
        compiler_params=pltpu.CompilerParams(
            dimension_semantics=("parallel", "arbitrary"), vmem_limit_bytes=VMEM_LIMIT_BIG),
        name="in_proj",
    )(x2d, g_norm, w_tiles, w_gate, positions.reshape(T, 1), invf, sign)


def _compress_kernel(seg_ref, pe_ref, w1_ref, w2_ref, out_ref):
    half = CMP_STRIDE * HEAD_DIM
    seg = seg_ref[0].astype(F32)
    pe = pe_ref[0]
    top = (seg + pe[:, :half]).astype(BF16)
    bot = (seg + pe[:, half:]).astype(BF16)
    a = jnp.dot(top, w1_ref[0, :half, :], preferred_element_type=F32)
    b = jnp.dot(bot, w1_ref[0, half:, :], preferred_element_type=F32)
    rows = a.shape[0]
    h = a + pltpu.roll(b, rows - 1, axis=0)
    hid = h * _sigmoid(h)
    out_ref[0] = jnp.dot(hid.astype(BF16), w2_ref[0], preferred_element_type=F32).astype(BF16)


def _compress(seg, pe, w1, w2):
    _, R, half = seg.shape
    return pl.pallas_call(
        _compress_kernel,
        grid=(2,),
        in_specs=[pl.BlockSpec((1, R, half), lambda c: (c, 0, 0)),
                  pl.BlockSpec((1, 1, 2 * half), lambda c: (c, 0, 0)),
                  pl.BlockSpec((1, 2 * half, CMP_HIDDEN), lambda c: (c, 0, 0)),
                  pl.BlockSpec((1, CMP_HIDDEN, HEAD_DIM), lambda c: (c, 0, 0))],
        out_specs=pl.BlockSpec((1, R, HEAD_DIM), lambda c: (c, 0, 0)),
        out_shape=jax.ShapeDtypeStruct((2, R, HEAD_DIM), BF16),
        compiler_params=pltpu.CompilerParams(
            dimension_semantics=("arbitrary",), vmem_limit_bytes=VMEM_LIMIT),
        name="compress",
    )(seg, pe, w1, w2)


def _select_bias_t(score_t, n_rows, n_keep):
    sub = 8
    n_groups = score_t.shape[0] // sub
    groups = [score_t[g * sub:(g + 1) * sub, :] for g in range(n_groups)]
    jrow = lax.broadcasted_iota(jnp.int32, groups[0].shape, 0)
    cnts = [jnp.zeros(groups[0].shape, jnp.int32) for _ in range(n_groups)]
    for jp in range(n_rows):
        row = score_t[jp:jp + 1, :]
        for g, grp in enumerate(groups):
            if g * sub > jp:
                beats = row >= grp
            elif g * sub + sub - 1 <= jp:
                beats = row > grp
            else:
                beats = (row > grp) | ((row == grp) & (jrow + g * sub > jp))
            cnts[g] = cnts[g] + beats.astype(jnp.int32)
    cnt = cnts[0] if n_groups == 1 else jnp.concatenate(cnts, axis=0)
    return (cnt < n_keep) & (score_t > -jnp.inf)


def _bias_t(keep_t):
    return jnp.where(keep_t, 0.0, NEG_BIG).astype(F32)


def _bias_columns(bias_t):
    rows, q = bias_t.shape
    if rows < LANES:
        bias_t = jnp.concatenate([bias_t, jnp.zeros((LANES - rows, q), F32)], axis=0)
    return bias_t.T.astype(BF16)


def _softmax_rows(s_blocks, r0, *, tq, q0, k0, window=None):
    per_block = KEY_BLOCK // LANES
    n_tiles = len(s_blocks) * per_block
    rows = slice(r0, r0 + ROW_CHUNK)
    qlo = q0 + r0 % tq
    qhi = qlo + ROW_CHUNK - 1

    def tile(t):
        lane0 = (t % per_block) * LANES
        return s_blocks[t // per_block][rows, lane0:lane0 + LANES]

    kinds = []
    for t in range(n_tiles):
        klo = k0 + t * LANES
        khi = klo + LANES - 1
        none = klo > qhi or (window is not None and khi <= qlo - window)
        full = khi <= qlo and (window is None or klo > qhi - window)
        kinds.append("none" if none else "full" if full else "part")
    mx = None
    masked = {}
    for t, kind in enumerate(kinds):
        if kind == "none":
            continue
        x = tile(t)
        if kind == "part":
            qpos = qlo + lax.broadcasted_iota(jnp.int32, x.shape, 0)
            kpos = k0 + t * LANES + lax.broadcasted_iota(jnp.int32, x.shape, 1)
            ok = kpos <= qpos
            if window is not None:
                ok = ok & (kpos > qpos - window)
            x = jnp.where(ok, x, NEG_BIG)
            masked[t] = x
        mx = x if mx is None else jnp.maximum(mx, x)
    m = jnp.broadcast_to(jnp.max(mx, axis=-1, keepdims=True), mx.shape)
    p_tiles = []
    for t, kind in enumerate(kinds):
        if kind == "none":
            p_tiles.append(jnp.zeros((ROW_CHUNK, LANES), BF16))
            continue
        x = masked[t] if kind == "part" else tile(t)
        p_tiles.append(jnp.exp2(x - m).astype(BF16))
    return jnp.concatenate(p_tiles, axis=1)


def _pv_normalized(p, v_ones):
    o = jnp.dot(p, v_ones, preferred_element_type=F32)
    return o[:, :HEAD_DIM] / o[:, HEAD_DIM:]


class _AttnPass(NamedTuple):
    q: Callable[[], jax.Array]
    k_ref: Any
    v_ref: Any
    k0: int
    width: int
    n_rows: int
    tq: int
    q0: int
    window: Optional[int]
    store: Callable[[jax.Array], None]


def _run_attention(passes):
    def score_thunks(ps):
        q = ps.q()
        return [lambda j=j: _nt_dot(
            q, ps.k_ref[ps.k0 + j * KEY_BLOCK:ps.k0 + (j + 1) * KEY_BLOCK, :])
                for j in range(ps.width // KEY_BLOCK)]

    s_blocks = [thunk() for thunk in score_thunks(passes[0])]
    pending_pv = None
    for t, ps in enumerate(passes):
        mxu_work = [] if pending_pv is None else [pending_pv]
        n_pv = len(mxu_work)
        if t + 1 < len(passes):
            mxu_work += score_thunks(passes[t + 1])
        row_starts = list(range(0, ps.n_rows, ROW_CHUNK))
        results, p_rows = [], []
        for k in range(max(len(mxu_work), len(row_starts))):
            if k < len(mxu_work):
                results.append(mxu_work[k]())
            if k < len(row_starts):
                p_rows.append(_softmax_rows(s_blocks, row_starts[k], tq=ps.tq, q0=ps.q0,
                                            k0=ps.k0, window=ps.window))
        s_blocks = results[n_pv:]

        def pending_pv(ps=ps, p_rows=p_rows):
            p = jnp.concatenate(p_rows, axis=0)
            ps.store(_pv_normalized(p, ps.v_ref[ps.k0:ps.k0 + ps.width, :]))
    pending_pv()


def _moba_kernel(q_ref, k_ref, v_ref, onehot_ref, o_ref, kaug_ref, qaug_ref, kmean_ref,
                 vaug_ref, *, n_blk, k_top):
    S = k_ref.shape[0]
    tq = MOBA_TQ
    kaug_ref[:, :HEAD_DIM] = k_ref[...]
    kaug_ref[:, HEAD_DIM:] = onehot_ref[...]
    vaug_ref[:, :HEAD_DIM] = v_ref[...]
    vaug_ref[:, HEAD_DIM:] = jnp.ones((S, HEAD_DIM), BF16)
    kmean_ref[...] = jnp.zeros(kmean_ref.shape, F32)
    for j in range(n_blk):
        kb = k_ref[j * MOBA_BLOCK:(j + 1) * MOBA_BLOCK, :].astype(F32)
        kmean_ref[j:j + 1, :] = jnp.sum(kb, axis=0, keepdims=True) * (1.0 / MOBA_BLOCK)

    q = q_ref[...]
    gate_t = _nt_dot(kmean_ref[...].astype(BF16), q)
    jrow = lax.broadcasted_iota(jnp.int32, gate_t.shape, 0)
    own = lax.broadcasted_iota(jnp.int32, gate_t.shape, 1) // MOBA_BLOCK
    gate_t = jnp.where((jrow < own) & jnp.isfinite(gate_t), gate_t, -jnp.inf)
    bias_t = _bias_t(_select_bias_t(gate_t, n_blk, k_top) | (jrow == own))
    qaug_ref[:, :HEAD_DIM] = q
    for c in range(S // tq):
        qaug_ref[c * tq:(c + 1) * tq, HEAD_DIM:] = _bias_columns(bias_t[:, c * tq:(c + 1) * tq])

    def tile_pass(c):
        q_rows = slice(c * tq, (c + 1) * tq)

        def store(o):
            o_ref[q_rows, :] = o.astype(BF16)

        return _AttnPass(q=lambda: qaug_ref[q_rows, :], k_ref=kaug_ref, v_ref=vaug_ref, k0=0,
                         width=(c + 1) * tq, n_rows=tq, tq=tq, q0=c * tq, window=None,
                         store=store)

    _run_attention([tile_pass(c) for c in range(S // tq)])


def _moba(proj, onehot, *, B, S):
    T = B * S
    n_blk = S // MOBA_BLOCK
    k_top = min(MOBA_TOPK, n_blk - 1)
    nb8 = -(-n_blk // 8) * 8
    kern = functools.partial(_moba_kernel, n_blk=n_blk, k_top=k_top)
    head_spec = lambda blk: pl.BlockSpec((None, S, LANES), lambda b, h: (blk + h, b, 0))
    return pl.pallas_call(
        kern,
        grid=(B, MOBA_HEADS),
        in_specs=[head_spec(BLK_MQ), head_spec(BLK_MK), head_spec(BLK_MV),
                  pl.BlockSpec((S, LANES), lambda b, h: (0, 0))],
        out_specs=head_spec(0),
        out_shape=jax.ShapeDtypeStruct((MOBA_HEADS, T, LANES), BF16),
        scratch_shapes=[pltpu.VMEM((S, 2 * LANES), BF16), pltpu.VMEM((S, 2 * LANES), BF16),
                        pltpu.VMEM((nb8, HEAD_DIM), F32),
                        pltpu.VMEM((S, 2 * LANES), BF16)],
        compiler_params=pltpu.CompilerParams(
            dimension_semantics=("parallel", "parallel"), vmem_limit_bytes=VMEM_LIMIT),
        name="moba_attn",
    )(proj, proj, proj, onehot)


def _nsa_select_kernel(q_ref, cos_ref, sin_ref, kc_ref, vc_ref, ovt_ref,
                       qrot_ref, bias_ref, oc_ref, *, n_cmp, n_sel_blk, n_top):
    i = pl.program_id(2)
    R, tq, _ = q_ref.shape
    q_raw = q_ref[...]
    cos = (cos_ref[...] * QK_PRESCALE)[None]
    sin = (sin_ref[...] * QK_PRESCALE)[None]
    qrot_ref[...] = _rope(q_raw.astype(F32), cos, sin).astype(BF16)

    n_seg = kc_ref.shape[0]
    s_c = (_nt_dot(q_raw.reshape(R * tq, HEAD_DIM), kc_ref[...]) * SCALE).reshape(R, tq, n_seg)
    n_idx = lax.broadcasted_iota(jnp.int32, s_c.shape, 2)
    pos3 = i * tq + lax.broadcasted_iota(jnp.int32, s_c.shape, 1)
    m_c = (n_idx * CMP_STRIDE + CMP_BLOCK - 1 <= pos3) & (n_idx < n_cmp)
    s_c = jnp.where(m_c, s_c, -jnp.inf)
    mx = jnp.max(s_c, axis=-1, keepdims=True)
    mx = jnp.where(jnp.isfinite(mx), mx, 0.0)
    e_c = jnp.where(m_c, jnp.exp(s_c - mx), 0.0)
    p_c = e_c / jnp.maximum(jnp.sum(e_c, axis=-1, keepdims=True), 1e-30)
    o_c = jnp.dot(p_c.reshape(R * tq, n_seg).astype(BF16), vc_ref[...],
                  preferred_element_type=F32).reshape(R, tq, HEAD_DIM)
    oc_ref[...] = o_c.astype(BF16)

    p_sum = jnp.sum(p_c, axis=0)
    p_hi = p_sum.astype(BF16)
    p_lo = (p_sum - p_hi.astype(F32)).astype(BF16)
    ovt = ovt_ref[...]
    imp_t = _nt_dot(ovt, p_hi) + _nt_dot(ovt, p_lo)
    jrow = lax.broadcasted_iota(jnp.int32, imp_t.shape, 0)
    posq = i * tq + lax.broadcasted_iota(jnp.int32, imp_t.shape, 1)
    own = posq // SEL_BLOCK
    forced = (jrow == 0) | (jrow == own) | (jrow == own - 1)
    future = jrow * SEL_BLOCK > posq
    score_t = jnp.where(future, -jnp.inf, jnp.where(forced, jnp.inf, imp_t))
    keep_t = _select_bias_t(score_t, n_sel_blk, n_top)
    bias_ref[...] = _bias_columns(_bias_t(keep_t))


def _nsa_select(proj, cos, sin, kvc, ovt, *, B, S, tq):
    T = B * S
    nq = S // tq
    n_seg = S // CMP_STRIDE
    n_cmp = n_seg - CMP_BLOCK // CMP_STRIDE + 1
    n_sel_blk = S // SEL_BLOCK
    R = NSA_REP
    kern = functools.partial(_nsa_select_kernel, n_cmp=n_cmp, n_sel_blk=n_sel_blk,
                             n_top=min(SEL_TOPK, n_sel_blk))
    heads_spec = pl.BlockSpec((R, tq, LANES), lambda b, g, i: (g, b * nq + i, 0))
    return pl.pallas_call(
        kern,
        grid=(B, NSA_GROUPS, nq),
        in_specs=[pl.BlockSpec((R, tq, LANES), lambda b, g, i: (BLK_NQ // R + g, b * nq + i, 0)),
                  pl.BlockSpec((tq, LANES), lambda b, g, i: (b * nq + i, 0)),
                  pl.BlockSpec((tq, LANES), lambda b, g, i: (b * nq + i, 0)),
                  pl.BlockSpec((None, n_seg, LANES), lambda b, g, i: (0, g * B + b, 0)),
                  pl.BlockSpec((None, n_seg, LANES), lambda b, g, i: (1, g * B + b, 0)),
                  pl.BlockSpec(ovt.shape, lambda b, g, i: (0, 0))],
        out_specs=[heads_spec,
                   pl.BlockSpec((None, tq, LANES), lambda b, g, i: (g, b * nq + i, 0)),
                   heads_spec],
        out_shape=[jax.ShapeDtypeStruct((NSA_HEADS, T, LANES), BF16),
                   jax.ShapeDtypeStruct((NSA_GROUPS, T, LANES), BF16),
                   jax.ShapeDtypeStruct((NSA_HEADS, T, LANES), BF16)],
        compiler_params=pltpu.CompilerParams(
            dimension_semantics=("parallel", "parallel", "parallel"),
            vmem_limit_bytes=VMEM_LIMIT),
        name="nsa_select",
    )(proj, cos, sin, kvc, kvc, ovt)


def _nsa_sel_kernel(qrot_ref, bias_ref, ks_ref, vs_ref, onehot_ref, o_ref,
                    ksaug_ref, vaug_ref, *, tq):
    R, S, _ = qrot_ref.shape
    ksaug_ref[:, :HEAD_DIM] = ks_ref[...]
    ksaug_ref[:, HEAD_DIM:] = onehot_ref[...]
    vaug_ref[:, :HEAD_DIM] = vs_ref[...]
    vaug_ref[:, HEAD_DIM:] = jnp.ones((S, HEAD_DIM), BF16)

    def head_pass(c, heads):
        q_rows = slice(c * tq, (c + 1) * tq)

        def q_aug():
            bias = bias_ref[q_rows, :]
            return jnp.concatenate(
                [jnp.concatenate([qrot_ref[r, q_rows, :], bias], axis=1) for r in heads], axis=0)

        def store(o):
            for k, r in enumerate(heads):
                o_ref[r, q_rows, :] = o[k * tq:(k + 1) * tq, :].astype(BF16)

        return _AttnPass(q=q_aug, k_ref=ksaug_ref, v_ref=vaug_ref, k0=0, width=(c + 1) * tq,
                         n_rows=len(heads) * tq, tq=tq, q0=c * tq, window=None, store=store)

    _run_attention([head_pass(c, range(r0, r0 + HEADS_PER_PASS))
                    for c in range(S // tq) for r0 in range(0, R, HEADS_PER_PASS)])


def _nsa_sel(q_rot, bias, proj, onehot, *, B, S, tq):
    T = B * S
    R = NSA_REP
    heads_spec = pl.BlockSpec((R, S, LANES), lambda b, g: (g, b, 0))
    kv_spec = lambda blk: pl.BlockSpec((None, S, LANES), lambda b, g: (blk + g, b, 0))
    return pl.pallas_call(
        functools.partial(_nsa_sel_kernel, tq=tq),
        grid=(B, NSA_GROUPS),
        in_specs=[heads_spec,
                  pl.BlockSpec((None, S, LANES), lambda b, g: (g, b, 0)),
                  kv_spec(BLK_NKS), kv_spec(BLK_NVS),
                  pl.BlockSpec((S, LANES), lambda b, g: (0, 0))],
        out_specs=heads_spec,
        out_shape=jax.ShapeDtypeStruct((NSA_HEADS, T, LANES), BF16),
        scratch_shapes=[pltpu.VMEM((S, 2 * LANES), BF16), pltpu.VMEM((S, 2 * LANES), BF16)],
        compiler_params=pltpu.CompilerParams(
            dimension_semantics=("parallel", "parallel"), vmem_limit_bytes=VMEM_LIMIT_BIG),
        name="nsa_sel_attn",
    )(q_rot, bias, proj, proj, onehot)


def _nsa_win_kernel(qrot_ref, kw_ref, vw_ref, oc_ref, os_ref, gate_ref, o_ref,
                    vaug_ref, *, tq):
    R, S, _ = qrot_ref.shape
    vaug_ref[:, :HEAD_DIM] = vw_ref[...]
    vaug_ref[:, HEAD_DIM:] = jnp.ones((S, HEAD_DIM), BF16)

    def head_pass(c, heads):
        q_rows = slice(c * tq, (c + 1) * tq)
        k0 = max(0, c * tq - WINDOW)

        def store(o_w):
            gt = _sigmoid(gate_ref[q_rows, :])
            for k, r in enumerate(heads):
                o = (gt[:, 3 * r:3 * r + 1] * oc_ref[r, q_rows, :].astype(F32)
                     + gt[:, 3 * r + 1:3 * r + 2] * os_ref[r, q_rows, :].astype(F32)
                     + gt[:, 3 * r + 2:3 * r + 3] * o_w[k * tq:(k + 1) * tq, :])
                o_ref[r, q_rows, :] = o.astype(BF16)

        return _AttnPass(
            q=lambda: jnp.concatenate([qrot_ref[r, q_rows, :] for r in heads], axis=0),
            k_ref=kw_ref, v_ref=vaug_ref, k0=k0, width=(c + 1) * tq - k0,
            n_rows=len(heads) * tq, tq=tq, q0=c * tq, window=WINDOW, store=store)

    _run_attention([head_pass(c, range(r0, r0 + HEADS_PER_PASS))
                    for c in range(S // tq) for r0 in range(0, R, HEADS_PER_PASS)])


def _nsa_win(q_rot, proj, o_c, o_s, gates, *, B, S, tq):
    T = B * S
    R = NSA_REP
    heads_spec = pl.BlockSpec((R, S, LANES), lambda b, g: (g, b, 0))
    kv_spec = lambda blk: pl.BlockSpec((None, S, LANES), lambda b, g: (blk + g, b, 0))
    return pl.pallas_call(
        functools.partial(_nsa_win_kernel, tq=tq),
        grid=(B, NSA_GROUPS),
        in_specs=[heads_spec, kv_spec(BLK_NKW), kv_spec(BLK_NVW), heads_spec, heads_spec,
                  pl.BlockSpec((None, S, LANES), lambda b, g: (g, b, 0))],
        out_specs=heads_spec,
        out_shape=jax.ShapeDtypeStruct((NSA_HEADS, T, LANES), BF16),
        scratch_shapes=[pltpu.VMEM((S, 2 * LANES), BF16)],
        compiler_params=pltpu.CompilerParams(
            dimension_semantics=("parallel", "parallel"), vmem_limit_bytes=VMEM_LIMIT_BIG),
        name="nsa_win_attn",
    )(q_rot, proj, proj, o_c, o_s, gates)


def _out_proj_kernel(om_ref, on_ref, zm_ref, zn_ref, gm_ref, gn_ref, x_ref, w_ref, gf_ref,
                     out_ref):
    def gated_norm(o_ref, z_ref, g_ref, rows):
        n_heads = o_ref.shape[0]
        acts = []
        ss = None
        for h in range(n_heads):
            z = z_ref[h, rows, :].astype(F32)
            a = o_ref[h, rows, :].astype(F32) * (z * _sigmoid(z))
            acts.append(a)
            ss = a * a if ss is None else ss + a * a
        ms = jnp.sum(ss, axis=-1, keepdims=True) * (1.0 / (n_heads * LANES))
        inv = lax.rsqrt(ms + EPS)
        return [(acts[h] * inv * g_ref[h]).astype(BF16) for h in range(n_heads)]

    for r0 in range(0, x_ref.shape[0], OUT_SUB_ROWS):
        rows = slice(r0, r0 + OUT_SUB_ROWS)
        y = jnp.concatenate(gated_norm(om_ref, zm_ref, gm_ref, rows)
                            + gated_norm(on_ref, zn_ref, gn_ref, rows), axis=1)
        r = x_ref[rows, :] + jnp.dot(y, w_ref[...], preferred_element_type=F32)
        ms = jnp.mean(r * r, axis=-1, keepdims=True)
        out_ref[rows, :] = r * lax.rsqrt(ms + EPS) * gf_ref[...]


def _out_proj(o_moba, o_nsa, proj, g_moba, g_nsa, x2d, w_out, g_final, *, tm):
    T = x2d.shape[0]
    H = MOBA_HEADS
    return pl.pallas_call(
        _out_proj_kernel,
        grid=(T // tm,),
        in_specs=[pl.BlockSpec((H, tm, LANES), lambda i: (0, i, 0)),
                  pl.BlockSpec((H, tm, LANES), lambda i: (0, i, 0)),
                  pl.BlockSpec((H, tm, LANES), lambda i: (BLK_MZ // H, i, 0)),
                  pl.BlockSpec((H, tm, LANES), lambda i: (BLK_NZ // H, i, 0)),
                  pl.BlockSpec((H, 1, LANES), lambda i: (0, 0, 0)),
                  pl.BlockSpec((H, 1, LANES), lambda i: (0, 0, 0)),
                  pl.BlockSpec((tm, D_MODEL), lambda i: (i, 0)),
                  pl.BlockSpec((D_MODEL, D_MODEL), lambda i: (0, 0), pipeline_mode=pl.Buffered(1)),
                  pl.BlockSpec((1, D_MODEL), lambda i: (0, 0))],
        out_specs=pl.BlockSpec((tm, D_MODEL), lambda i: (i, 0)),
        out_shape=jax.ShapeDtypeStruct((T, D_MODEL), F32),
        compiler_params=pltpu.CompilerParams(
            dimension_semantics=("parallel",), vmem_limit_bytes=VMEM_LIMIT_BIG),
        name="out_proj",
    )(o_moba, o_nsa, proj, proj, g_moba, g_nsa, x2d, w_out, g_final)


def _w_in_offsets():
    mw, nw, kw = MOBA_HEADS * HEAD_DIM, NSA_HEADS * HEAD_DIM, NSA_GROUPS * HEAD_DIM
    sizes = [mw] * 4 + [nw] + [kw] * 6 + [3 * NSA_HEADS, nw]
    names = ["mq", "mk", "mv", "mz", "nq", "nkc", "nvc", "nks", "nvs", "nkw", "nvw", "ng", "nz"]
    offs = np.concatenate([[0], np.cumsum(sizes)])
    return {n: (int(offs[k]), int(sizes[k])) for k, n in enumerate(names)}


def _weight_tiles_kernel(w_ref, out_ref, *, src_cols, blocks_per_tile):
    for b, src in enumerate(src_cols):
        lane0 = (b % blocks_per_tile) * LANES
        out_ref[b // blocks_per_tile, :, lane0:lane0 + LANES] = (
            w_ref[:, src:src + LANES].astype(BF16))


def _weight_tiles(w_in, blocks_per_tile):
    offsets = _w_in_offsets()
    src_cols = []
    for name, n_blocks in COLUMN_ORDER:
        off, size = offsets[name]
        assert size == n_blocks * LANES
        src_cols += [off + k * LANES for k in range(n_blocks)]
    n_tiles = N_BLOCKS // blocks_per_tile
    rows = 256
    kern = functools.partial(_weight_tiles_kernel, src_cols=tuple(src_cols),
                             blocks_per_tile=blocks_per_tile)
    return pl.pallas_call(
        kern,
        grid=(D_MODEL // rows,),
        in_specs=[pl.BlockSpec((None, rows, w_in.shape[2]), lambda r: (0, r, 0))],
        out_specs=pl.BlockSpec((n_tiles, rows, blocks_per_tile * LANES), lambda r: (0, r, 0)),
        out_shape=jax.ShapeDtypeStruct((n_tiles, D_MODEL, blocks_per_tile * LANES), BF16),
        compiler_params=pltpu.CompilerParams(
            dimension_semantics=("parallel",), vmem_limit_bytes=VMEM_LIMIT),
        name="weight_tiles",
    )(w_in)


def _gate_weight(w_in):
    off, size = _w_in_offsets()["ng"]
    per_group = 3 * NSA_REP
    wg = w_in[0, :, off:off + size].reshape(D_MODEL, NSA_GROUPS, per_group)
    wg = jnp.pad(wg, ((0, 0), (0, 0), (0, LANES - per_group)))
    return wg.reshape(D_MODEL, NSA_GROUPS * LANES).astype(BF16)


def _block_onehot(S, block):
    ids = np.arange(S)[:, None] // block
    return jnp.asarray((ids == np.arange(LANES)[None, :]).astype(np.float32), dtype=BF16)


def _overlap_t(n_seg, n_cmp, n_sel_blk):
    cs = np.arange(n_seg)[None, :] * CMP_STRIDE
    ss = np.arange(n_sel_blk)[:, None] * SEL_BLOCK
    ov = (cs < ss + SEL_BLOCK) & (cs + CMP_BLOCK > ss) & (np.arange(n_seg)[None, :] < n_cmp)
    return jnp.asarray(ov.astype(np.float32), dtype=BF16)


def _layer(x, positions, w_in, g_norm, pe_ck, pe_cv, w_ck1, w_ck2, w_cv1, w_cv2,
           g_out_moba, g_out_nsa, w_out, g_final, *, nsa_tq, tm_in, tm_out, blocks_per_tile):
    B, S, _ = x.shape
    T = B * S
    x2d = x.reshape(T, D_MODEL)
    w_tiles = _weight_tiles(w_in.astype(BF16), blocks_per_tile)
    w_gate = _gate_weight(w_in)
    proj, gates, cos, sin, seg = _in_proj(x2d, g_norm.reshape(1, D_MODEL), w_tiles, w_gate,
                                          positions, tm=tm_in, blocks_per_tile=blocks_per_tile)

    n_seg = S // CMP_STRIDE
    seg = seg.reshape(2, NSA_GROUPS * B * n_seg, CMP_STRIDE * HEAD_DIM)
    pe = jnp.stack([pe_ck.reshape(1, -1), pe_cv.reshape(1, -1)])
    w1 = jnp.stack([w_ck1, w_cv1]).astype(BF16)
    w2 = jnp.stack([w_ck2, w_cv2]).astype(BF16)
    kvc = _compress(seg, pe, w1, w2)

    o_moba = _moba(proj, _block_onehot(S, MOBA_BLOCK), B=B, S=S)
    n_cmp = n_seg - CMP_BLOCK // CMP_STRIDE + 1
    q_rot, bias, o_c = _nsa_select(proj, cos, sin, kvc, _overlap_t(n_seg, n_cmp, S // SEL_BLOCK),
                                   B=B, S=S, tq=2 * nsa_tq)
    o_s = _nsa_sel(q_rot, bias, proj, _block_onehot(S, SEL_BLOCK), B=B, S=S, tq=nsa_tq)
    o_nsa = _nsa_win(q_rot, proj, o_c, o_s, gates, B=B, S=S, tq=nsa_tq)
    out = _out_proj(o_moba, o_nsa, proj,
                    g_out_moba.reshape(MOBA_HEADS, 1, LANES), g_out_nsa.reshape(NSA_HEADS, 1, LANES),
                    x2d, w_out.astype(BF16), g_final.reshape(1, D_MODEL), tm=tm_out)
    return out.reshape(B, S, D_MODEL)


def kernel(x, positions, w_in, g_norm, pe_ck, pe_cv, w_ck1, w_ck2, w_cv1, w_cv2,
           g_out_moba, g_out_nsa, w_out, g_final):
    assert w_in.shape[0] == 1, "single-layer problem"
    return _layer(x, positions, w_in, g_norm[0], pe_ck[0], pe_cv[0], w_ck1[0], w_ck2[0],
                  w_cv1[0], w_cv2[0], g_out_moba[0], g_out_nsa[0], w_out[0], g_final,
                  nsa_tq=256, tm_in=1024, tm_out=512, blocks_per_tile=10)
```

```python
import functools
from typing import Any, Callable, NamedTuple, Optional

import numpy as np
import jax
import jax.numpy as jnp
from jax import lax
from jax.experimental import pallas as pl
from jax.experimental.pallas import tpu as pltpu

F32 = jnp.float32
BF16 = jnp.bfloat16

D_MODEL = 2048
HEAD_DIM = 128
MOBA_HEADS = 8
NSA_HEADS = 8
NSA_GROUPS = 2
NSA_REP = 4
MOBA_BLOCK = 256
MOBA_TOPK = 3
CMP_BLOCK = 32
CMP_STRIDE = 16
CMP_HIDDEN = 256
SEL_BLOCK = 64
SEL_TOPK = 8
WINDOW = 512
ROPE_THETA = 10000.0
EPS = 1e-6
SCALE = HEAD_DIM ** -0.5
QK_PRESCALE = SCALE * float(np.log2(np.e))
NEG_BIG = -(2.0 ** 100)

LANES = 128
VMEM_LIMIT = 48 * 1024 * 1024
VMEM_LIMIT_BIG = 58 * 1024 * 1024
COLUMN_ORDER = (("mq", 8), ("mk", 8), ("nks", 2), ("nkw", 2), ("nkc", 2), ("nvc", 2),
                ("mv", 8), ("mz", 8), ("nq", 8), ("nz", 8), ("nvs", 2), ("nvw", 2))
BLK_MQ, BLK_MK, BLK_NKS, BLK_NKW, BLK_NKC, BLK_NVC = 0, 8, 16, 18, 20, 22
BLK_MV, BLK_MZ, BLK_NQ, BLK_NZ, BLK_NVS, BLK_NVW = 24, 32, 40, 48, 56, 58
N_BLOCKS = 60
N_ROPE_BLOCKS = 20
N_SEG_BLOCKS = 4

ROW_CHUNK = 64
MOBA_PV_ROWS = 256
HEADS_PER_PASS = 2
NORM_ROWS = 16
IN_SUB_ROWS = 256
MXU_COLS = 256
OUT_SUB_ROWS = 256
MOBA_TQ = 256
KEY_BLOCK = 256


def _nt_dot(a, b):
    return lax.dot_general(a, b, (((1,), (1,)), ((), ())), preferred_element_type=F32)


def _sigmoid(x):
    return 1.0 / (1.0 + jnp.exp(-x))


def _rope(a, cos, sin_signed):
    return a * cos + pltpu.roll(a, HEAD_DIM // 2, axis=a.ndim - 1) * sin_signed


def _interleave(*streams):
    n = max(len(s) for s in streams)
    for k in range(n):
        for s in streams:
            for thunk in s[k * len(s) // n:(k + 1) * len(s) // n]:
                thunk()


def _block_kind(b):
    if BLK_NKC <= b < BLK_NKC + N_SEG_BLOCKS:
        return "seg"
    return "rope_q" if b < BLK_MK else "rope" if b < N_ROPE_BLOCKS else "plain"


def _project_columns(h, w_ref, col0, kinds, cos, sin, store, store_segments=None):
    n = len(kinds)
    acc = jnp.dot(h, w_ref[:, col0:col0 + n * LANES], preferred_element_type=F32)
    n_seg = 0
    for c, kind in enumerate(kinds):
        a = acc[:, c * LANES:(c + 1) * LANES]
        if kind == "rope_q":
            a = _rope(a, cos * QK_PRESCALE, sin * QK_PRESCALE)
        elif kind == "rope":
            a = _rope(a, cos, sin)
        elif kind == "seg":
            store_segments(n_seg, a)
            n_seg += 1
        store(c, a.astype(BF16))


def _in_proj_kernel(x_ref, g_ref, w_ref, wg_ref, pos_ref, invf_ref, sign_ref,
                    out_ref, gate_ref, cos_ref, sin_ref, seg_ref, h_scr, seg_scr,
                    *, tile_patterns):
    j = pl.program_id(1)
    tm = x_ref.shape[0]
    first_pattern = tile_patterns[0][0]
    assert tile_patterns[0][1] == [0]

    @pl.when(j == 0)
    def _():
        ang = pos_ref[...].astype(F32) * invf_ref[...]
        cos_ref[...] = jnp.cos(ang)
        sin_ref[...] = jnp.sin(ang) * sign_ref[...]

        def norm_piece(r0):
            rows = slice(r0, r0 + NORM_ROWS)
            x = x_ref[rows, :]
            ms = jnp.mean(x * x, axis=-1, keepdims=True)
            h_scr[rows, :] = (x * lax.rsqrt(ms + EPS) * g_ref[...]).astype(BF16)

        def norm_pieces(m):
            return [functools.partial(norm_piece, r0)
                    for r0 in range(m * IN_SUB_ROWS, (m + 1) * IN_SUB_ROWS, NORM_ROWS)]

        def matmul_pieces(m):
            rows = slice(m * IN_SUB_ROWS, (m + 1) * IN_SUB_ROWS)

            def gates():
                acc = jnp.dot(h_scr[rows, :], wg_ref[...], preferred_element_type=F32)
                for g in range(NSA_GROUPS):
                    gate_ref[g, rows, :] = acc[:, g * LANES:(g + 1) * LANES]

            def columns(c0):
                def store(c, val):
                    out_ref[c0 + c, rows, :] = val
                per = MXU_COLS // LANES
                _project_columns(h_scr[rows, :], w_ref, c0 * LANES, first_pattern[c0:c0 + per],
                                 cos_ref[rows, :], sin_ref[rows, :], store)

            return [gates] + [functools.partial(columns, c0)
                              for c0 in range(0, len(first_pattern), MXU_COLS // LANES)]

        n_sub = tm // IN_SUB_ROWS
        _interleave(norm_pieces(0))
        for m in range(n_sub):
            _interleave(matmul_pieces(m), norm_pieces(m + 1) if m + 1 < n_sub else [])

    for pattern, tiles in tile_patterns[1:]:
        @pl.when(functools.reduce(jnp.logical_or, [j == t for t in tiles]))
        def _(pattern=pattern):
            def store(c, val):
                out_ref[c] = val

            def store_segments(k, a):
                seg_scr[...] = a
                for t in range(CMP_STRIDE):
                    piece = seg_scr[pl.ds(t, tm // CMP_STRIDE, stride=CMP_STRIDE), :]
                    seg_ref[k, :, t * HEAD_DIM:(t + 1) * HEAD_DIM] = piece.astype(BF16)

            _project_columns(h_scr[...], w_ref, 0, pattern, cos_ref[...], sin_ref[...], store,
                             store_segments)


def _in_proj(x2d, g_norm, w_tiles, w_gate, positions, *, tm, blocks_per_tile):
    T = x2d.shape[0]
    tn = blocks_per_tile * LANES
    n_tiles = N_BLOCKS // blocks_per_tile
    by_pattern = {}
    for t in range(n_tiles):
        pattern = tuple(_block_kind(t * blocks_per_tile + c) for c in range(blocks_per_tile))
        by_pattern.setdefault(pattern, []).append(t)
    half = HEAD_DIM // 2
    inv_freq = 1.0 / (ROPE_THETA ** (jnp.arange(0, HEAD_DIM, 2, dtype=F32) / HEAD_DIM))
    invf = jnp.concatenate([inv_freq, inv_freq]).reshape(1, HEAD_DIM)
    sign = jnp.concatenate([-jnp.ones((half,), F32), jnp.ones((half,), F32)]).reshape(1, HEAD_DIM)
    kern = functools.partial(_in_proj_kernel, tile_patterns=tuple(by_pattern.items()))
    row_table = pl.BlockSpec((tm, HEAD_DIM), lambda i, j: (i, 0))
    return pl.pallas_call(
        kern,
        grid=(T // tm, n_tiles),
        in_specs=[pl.BlockSpec((tm, D_MODEL), lambda i, j: (i, 0)),
                  pl.BlockSpec((1, D_MODEL), lambda i, j: (0, 0)),
                  pl.BlockSpec((None, D_MODEL, tn), lambda i, j: (j, 0, 0)),
                  pl.BlockSpec((D_MODEL, NSA_GROUPS * LANES), lambda i, j: (0, 0)),
                  pl.BlockSpec((tm, 1), lambda i, j: (i, 0)),
                  pl.BlockSpec((1, HEAD_DIM), lambda i, j: (0, 0)),
                  pl.BlockSpec((1, HEAD_DIM), lambda i, j: (0, 0))],
        out_specs=[pl.BlockSpec((blocks_per_tile, tm, LANES), lambda i, j: (j, i, 0)),
                   pl.BlockSpec((NSA_GROUPS, tm, LANES), lambda i, j: (0, i, 0)),
                   row_table, row_table,
                   pl.BlockSpec((N_SEG_BLOCKS, tm // CMP_STRIDE, CMP_STRIDE * HEAD_DIM),
                                lambda i, j: (0, i, 0))],
        out_shape=[jax.ShapeDtypeStruct((N_BLOCKS, T, LANES), BF16),
                   jax.ShapeDtypeStruct((NSA_GROUPS, T, LANES), F32),
                   jax.ShapeDtypeStruct((T, HEAD_DIM), F32),
                   jax.ShapeDtypeStruct((T, HEAD_DIM), F32),
                   jax.ShapeDtypeStruct((N_SEG_BLOCKS, T // CMP_STRIDE, CMP_STRIDE * HEAD_DIM),
                                        BF16)],
        scratch_shapes=[pltpu.VMEM((tm, D_MODEL), BF16), pltpu.VMEM((tm, HEAD_DIM), F32)],
        compiler_params=pltpu.CompilerParams(
            dimension_semantics=("parallel", "arbitrary"), vmem_limit_bytes=VMEM_LIMIT_BIG),
        name="in_proj",
    )(x2d, g_norm, w_tiles, w_gate, positions.reshape(T, 1), invf, sign)


def _compress_kernel(seg_ref, pe_ref, w1_ref, w2_ref, out_ref):
    half = CMP_STRIDE * HEAD_DIM
    seg = seg_ref[0].astype(F32)
    pe = pe_ref[0]
    top = (seg + pe[:, :half]).astype(BF16)
    bot = (seg + pe[:, half:]).astype(BF16)
    a = jnp.dot(top, w1_ref[0, :half, :], preferred_element_type=F32)
    b = jnp.dot(bot, w1_ref[0, half:, :], preferred_element_type=F32)
    rows = a.shape[0]
    h = a + pltpu.roll(b, rows - 1, axis=0)
    hid = h * _sigmoid(h)
    out_ref[0] = jnp.dot(hid.astype(BF16), w2_ref[0], preferred_element_type=F32).astype(BF16)


def _compress(seg, pe, w1, w2):
    _, R, half = seg.shape
    return pl.pallas_call(
        _compress_kernel,
        grid=(2,),
        in_specs=[pl.BlockSpec((1, R, half), lambda c: (c, 0, 0)),
                  pl.BlockSpec((1, 1, 2 * half), lambda c: (c, 0, 0)),
                  pl.BlockSpec((1, 2 * half, CMP_HIDDEN), lambda c: (c, 0, 0)),
                  pl.BlockSpec((1, CMP_HIDDEN, HEAD_DIM), lambda c: (c, 0, 0))],
        out_specs=pl.BlockSpec((1, R, HEAD_DIM), lambda c: (c, 0, 0)),
        out_shape=jax.ShapeDtypeStruct((2, R, HEAD_DIM), BF16),
        compiler_params=pltpu.CompilerParams(
            dimension_semantics=("arbitrary",), vmem_limit_bytes=VMEM_LIMIT),
        name="compress",
    )(seg, pe, w1, w2)


def _select_bias_t(score_t, n_rows, n_keep):
    sub = 8
    n_groups = score_t.shape[0] // sub
    groups = [score_t[g * sub:(g + 1) * sub, :] for g in range(n_groups)]
    jrow = lax.broadcasted_iota(jnp.int32, groups[0].shape, 0)
    cnts = [jnp.zeros(groups[0].shape, jnp.int32) for _ in range(n_groups)]
    for jp in range(n_rows):
        row = score_t[jp:jp + 1, :]
        for g, grp in enumerate(groups):
            if g * sub > jp:
                beats = row >= grp
            elif g * sub + sub - 1 <= jp:
                beats = row > grp
            else:
                beats = (row > grp) | ((row == grp) & (jrow + g * sub > jp))
            cnts[g] = cnts[g] + beats.astype(jnp.int32)
    cnt = cnts[0] if n_groups == 1 else jnp.concatenate(cnts, axis=0)
    return (cnt < n_keep) & (score_t > -jnp.inf)


def _bias_t(keep_t):
    return jnp.where(keep_t, 0.0, NEG_BIG).astype(F32)


def _bias_columns(bias_t):
    rows, q = bias_t.shape
    if rows < LANES:
        bias_t = jnp.concatenate([bias_t, jnp.zeros((LANES - rows, q), F32)], axis=0)
    return bias_t.T.astype(BF16)


def _softmax_rows(s_blocks, r0, *, tq, q0, k0, window=None):
    per_block = KEY_BLOCK // LANES
    n_tiles = len(s_blocks) * per_block
    rows = slice(r0, r0 + ROW_CHUNK)
    qlo = q0 + r0 % tq
    qhi = qlo + ROW_CHUNK - 1

    def tile(t):
        lane0 = (t % per_block) * LANES
        return s_blocks[t // per_block][rows, lane0:lane0 + LANES]

    kinds = []
    for t in range(n_tiles):
        klo = k0 + t * LANES
        khi = klo + LANES - 1
        none = klo > qhi or (window is not None and khi <= qlo - window)
        full = khi <= qlo and (window is None or klo > qhi - window)
        kinds.append("none" if none else "full" if full else "part")
    mx = None
    masked = {}
    for t, kind in enumerate(kinds):
        if kind == "none":
            continue
        x = tile(t)
        if kind == "part":
            qpos = qlo + lax.broadcasted_iota(jnp.int32, x.shape, 0)
            kpos = k0 + t * LANES + lax.broadcasted_iota(jnp.int32, x.shape, 1)
            ok = kpos <= qpos
            if window is not None:
                ok = ok & (kpos > qpos - window)
            x = jnp.where(ok, x, NEG_BIG)
            masked[t] = x
        mx = x if mx is None else jnp.maximum(mx, x)
    m = jnp.broadcast_to(jnp.max(mx, axis=-1, keepdims=True), mx.shape)
    p_tiles = []
    for t, kind in enumerate(kinds):
        if kind == "none":
            p_tiles.append(jnp.zeros((ROW_CHUNK, LANES), BF16))
            continue
        x = masked[t] if kind == "part" else tile(t)
        p_tiles.append(jnp.exp2(x - m).astype(BF16))
    return jnp.concatenate(p_tiles, axis=1)


def _pv_normalized(p, v_ones):
    o = jnp.dot(p, v_ones, preferred_element_type=F32)
    return o[:, :HEAD_DIM] / o[:, HEAD_DIM:]


class _AttnPass(NamedTuple):
    q: Callable[[], jax.Array]
    k_ref: Any
    v_ref: Any
    k0: int
    width: int
    n_rows: int
    tq: int
    q0: int
    window: Optional[int]
    pv_rows: int
    store: Callable[[int, jax.Array], None]


def _run_attention(passes):
    def score_thunks(ps):
        q = ps.q()
        return [lambda j=j: _nt_dot(
            q, ps.k_ref[ps.k0 + j * KEY_BLOCK:ps.k0 + (j + 1) * KEY_BLOCK, :])
                for j in range(ps.width // KEY_BLOCK)]

    s_blocks = [thunk() for thunk in score_thunks(passes[0])]
    pending_pv = None
    for t, ps in enumerate(passes):
        mxu_work = [] if pending_pv is None else [pending_pv]
        n_pv = len(mxu_work)
        if t + 1 < len(passes):
            mxu_work += score_thunks(passes[t + 1])
        row_starts = list(range(0, ps.n_rows, ROW_CHUNK))
        results, p_rows = [], []
        for k in range(max(len(mxu_work), len(row_starts))):
            if k < len(mxu_work):
                results.append(mxu_work[k]())
            if k < len(row_starts):
                p_rows.append(_softmax_rows(s_blocks, row_starts[k], tq=ps.tq, q0=ps.q0,
                                            k0=ps.k0, window=ps.window))
        s_blocks = results[n_pv:]

        def pending_pv(ps=ps, p_rows=p_rows):
            per = ps.pv_rows // ROW_CHUNK
            for k in range(ps.n_rows // ps.pv_rows):
                p = jnp.concatenate(p_rows[k * per:(k + 1) * per], axis=0)
                ps.store(k * ps.pv_rows, _pv_normalized(p, ps.v_ref[ps.k0:ps.k0 + ps.width, :]))
    pending_pv()


def _moba_kernel(q_ref, k_ref, v_ref, onehot_ref, o_ref, kaug_ref, qaug_ref, kmean_ref,
                 vaug_ref, *, n_blk, k_top):
    S = k_ref.shape[0]
    tq = MOBA_TQ
    kaug_ref[:, :HEAD_DIM] = k_ref[...]
    kaug_ref[:, HEAD_DIM:] = onehot_ref[...]
    vaug_ref[:, :HEAD_DIM] = v_ref[...]
    vaug_ref[:, HEAD_DIM:] = jnp.ones((S, HEAD_DIM), BF16)
    kmean_ref[...] = jnp.zeros(kmean_ref.shape, F32)
    for j in range(n_blk):
        kb = k_ref[j * MOBA_BLOCK:(j + 1) * MOBA_BLOCK, :].astype(F32)
        kmean_ref[j:j + 1, :] = jnp.sum(kb, axis=0, keepdims=True) * (1.0 / MOBA_BLOCK)

    q = q_ref[...]
    gate_t = _nt_dot(kmean_ref[...].astype(BF16), q)
    jrow = lax.broadcasted_iota(jnp.int32, gate_t.shape, 0)
    own = lax.broadcasted_iota(jnp.int32, gate_t.shape, 1) // MOBA_BLOCK
    gate_t = jnp.where((jrow < own) & jnp.isfinite(gate_t), gate_t, -jnp.inf)
    bias_t = _bias_t(_select_bias_t(gate_t, n_blk, k_top) | (jrow == own))
    qaug_ref[:, :HEAD_DIM] = q
    for c in range(S // tq):
        qaug_ref[c * tq:(c + 1) * tq, HEAD_DIM:] = _bias_columns(bias_t[:, c * tq:(c + 1) * tq])

    def tile_pass(c):
        q_rows = slice(c * tq, (c + 1) * tq)

        def store(row0, o):
            o_ref[c * tq + row0:c * tq + row0 + o.shape[0], :] = o.astype(BF16)

        return _AttnPass(q=lambda: qaug_ref[q_rows, :], k_ref=kaug_ref, v_ref=vaug_ref, k0=0,
                         width=(c + 1) * tq, n_rows=tq, tq=tq, q0=c * tq, window=None,
                         pv_rows=MOBA_PV_ROWS, store=store)

    _run_attention([tile_pass(c) for c in range(S // tq)])


def _moba(proj, onehot, *, B, S):
    T = B * S
    n_blk = S // MOBA_BLOCK
    k_top = min(MOBA_TOPK, n_blk - 1)
    nb8 = -(-n_blk // 8) * 8
    kern = functools.partial(_moba_kernel, n_blk=n_blk, k_top=k_top)
    head_spec = lambda blk: pl.BlockSpec((None, S, LANES), lambda b, h: (blk + h, b, 0))
    return pl.pallas_call(
        kern,
        grid=(B, MOBA_HEADS),
        in_specs=[head_spec(BLK_MQ), head_spec(BLK_MK), head_spec(BLK_MV),
                  pl.BlockSpec((S, LANES), lambda b, h: (0, 0))],
        out_specs=head_spec(0),
        out_shape=jax.ShapeDtypeStruct((MOBA_HEADS, T, LANES), BF16),
        scratch_shapes=[pltpu.VMEM((S, 2 * LANES), BF16), pltpu.VMEM((S, 2 * LANES), BF16),
                        pltpu.VMEM((nb8, HEAD_DIM), F32),
                        pltpu.VMEM((S, 2 * LANES), BF16)],
        compiler_params=pltpu.CompilerParams(
            dimension_semantics=("parallel", "parallel"), vmem_limit_bytes=VMEM_LIMIT),
        name="moba_attn",
    )(proj, proj, proj, onehot)


def _nsa_select_kernel(q_ref, cos_ref, sin_ref, kc_ref, vc_ref, ovt_ref,
                       qrot_ref, bias_ref, oc_ref, *, n_cmp, n_sel_blk, n_top):
    i = pl.program_id(2)
    R, tq, _ = q_ref.shape
    q_raw = q_ref[...]
    cos = (cos_ref[...] * QK_PRESCALE)[None]
    sin = (sin_ref[...] * QK_PRESCALE)[None]
    qrot_ref[...] = _rope(q_raw.astype(F32), cos, sin).astype(BF16)

    n_seg = kc_ref.shape[0]
    s_c = (_nt_dot(q_raw.reshape(R * tq, HEAD_DIM), kc_ref[...]) * SCALE).reshape(R, tq, n_seg)
    n_idx = lax.broadcasted_iota(jnp.int32, s_c.shape, 2)
    pos3 = i * tq + lax.broadcasted_iota(jnp.int32, s_c.shape, 1)
    m_c = (n_idx * CMP_STRIDE + CMP_BLOCK - 1 <= pos3) & (n_idx < n_cmp)
    s_c = jnp.where(m_c, s_c, -jnp.inf)
    mx = jnp.max(s_c, axis=-1, keepdims=True)
    mx = jnp.where(jnp.isfinite(mx), mx, 0.0)
    e_c = jnp.where(m_c, jnp.exp(s_c - mx), 0.0)
    p_c = e_c / jnp.maximum(jnp.sum(e_c, axis=-1, keepdims=True), 1e-30)
    o_c = jnp.dot(p_c.reshape(R * tq, n_seg).astype(BF16), vc_ref[...],
                  preferred_element_type=F32).reshape(R, tq, HEAD_DIM)
    oc_ref[...] = o_c.astype(BF16)

    p_sum = jnp.sum(p_c, axis=0)
    p_hi = p_sum.astype(BF16)
    p_lo = (p_sum - p_hi.astype(F32)).astype(BF16)
    ovt = ovt_ref[...]
    imp_t = _nt_dot(ovt, p_hi) + _nt_dot(ovt, p_lo)
    jrow = lax.broadcasted_iota(jnp.int32, imp_t.shape, 0)
    posq = i * tq + lax.broadcasted_iota(jnp.int32, imp_t.shape, 1)
    own = posq // SEL_BLOCK
    forced = (jrow == 0) | (jrow == own) | (jrow == own - 1)
    future = jrow * SEL_BLOCK > posq
    score_t = jnp.where(future, -jnp.inf, jnp.where(forced, jnp.inf, imp_t))
    keep_t = _select_bias_t(score_t, n_sel_blk, n_top)
    bias_ref[...] = _bias_columns(_bias_t(keep_t))


def _nsa_select(proj, cos, sin, kvc, ovt, *, B, S, tq):
    T = B * S
    nq = S // tq
    n_seg = S // CMP_STRIDE
    n_cmp = n_seg - CMP_BLOCK // CMP_STRIDE + 1
    n_sel_blk = S // SEL_BLOCK
    R = NSA_REP
    kern = functools.partial(_nsa_select_kernel, n_cmp=n_cmp, n_sel_blk=n_sel_blk,
                             n_top=min(SEL_TOPK, n_sel_blk))
    heads_spec = pl.BlockSpec((R, tq, LANES), lambda b, g, i: (g, b * nq + i, 0))
    return pl.pallas_call(
        kern,
        grid=(B, NSA_GROUPS, nq),
        in_specs=[pl.BlockSpec((R, tq, LANES), lambda b, g, i: (BLK_NQ // R + g, b * nq + i, 0)),
                  pl.BlockSpec((tq, LANES), lambda b, g, i: (b * nq + i, 0)),
                  pl.BlockSpec((tq, LANES), lambda b, g, i: (b * nq + i, 0)),
                  pl.BlockSpec((None, n_seg, LANES), lambda b, g, i: (0, g * B + b, 0)),
                  pl.BlockSpec((None, n_seg, LANES), lambda b, g, i: (1, g * B + b, 0)),
                  pl.BlockSpec(ovt.shape, lambda b, g, i: (0, 0))],
        out_specs=[heads_spec,
                   pl.BlockSpec((None, tq, LANES), lambda b, g, i: (g, b * nq + i, 0)),
                   heads_spec],
        out_shape=[jax.ShapeDtypeStruct((NSA_HEADS, T, LANES), BF16),
                   jax.ShapeDtypeStruct((NSA_GROUPS, T, LANES), BF16),
                   jax.ShapeDtypeStruct((NSA_HEADS, T, LANES), BF16)],
        compiler_params=pltpu.CompilerParams(
            dimension_semantics=("parallel", "parallel", "parallel"),
            vmem_limit_bytes=VMEM_LIMIT),
        name="nsa_select",
    )(proj, cos, sin, kvc, kvc, ovt)


def _nsa_sel_kernel(qrot_ref, bias_ref, ks_ref, vs_ref, onehot_ref, o_ref,
                    ksaug_ref, vaug_ref, *, tq):
    R, S, _ = qrot_ref.shape
    ksaug_ref[:, :HEAD_DIM] = ks_ref[...]
    ksaug_ref[:, HEAD_DIM:] = onehot_ref[...]
    vaug_ref[:, :HEAD_DIM] = vs_ref[...]
    vaug_ref[:, HEAD_DIM:] = jnp.ones((S, HEAD_DIM), BF16)

    def head_pass(c, heads):
        q_rows = slice(c * tq, (c + 1) * tq)

        def q_aug():
            bias = bias_ref[q_rows, :]
            return jnp.concatenate(
                [jnp.concatenate([qrot_ref[r, q_rows, :], bias], axis=1) for r in heads], axis=0)

        def store(row0, o):
            for k, r in enumerate(heads):
                o_ref[r, q_rows, :] = o[k * tq:(k + 1) * tq, :].astype(BF16)

        return _AttnPass(q=q_aug, k_ref=ksaug_ref, v_ref=vaug_ref, k0=0, width=(c + 1) * tq,
                         n_rows=len(heads) * tq, tq=tq, q0=c * tq, window=None,
                         pv_rows=len(heads) * tq, store=store)

    _run_attention([head_pass(c, range(r0, r0 + HEADS_PER_PASS))
                    for c in range(S // tq) for r0 in range(0, R, HEADS_PER_PASS)])


def _nsa_sel(q_rot, bias, proj, onehot, *, B, S, tq):
    T = B * S
    R = NSA_REP
    heads_spec = pl.BlockSpec((R, S, LANES), lambda b, g: (g, b, 0))
    kv_spec = lambda blk: pl.BlockSpec((None, S, LANES), lambda b, g: (blk + g, b, 0))
    return pl.pallas_call(
        functools.partial(_nsa_sel_kernel, tq=tq),
        grid=(B, NSA_GROUPS),
        in_specs=[heads_spec,
                  pl.BlockSpec((None, S, LANES), lambda b, g: (g, b, 0)),
                  kv_spec(BLK_NKS), kv_spec(BLK_NVS),
                  pl.BlockSpec((S, LANES), lambda b, g: (0, 0))],
        out_specs=heads_spec,
        out_shape=jax.ShapeDtypeStruct((NSA_HEADS, T, LANES), BF16),
        scratch_shapes=[pltpu.VMEM((S, 2 * LANES), BF16), pltpu.VMEM((S, 2 * LANES), BF16)],
        compiler_params=pltpu.CompilerParams(
            dimension_semantics=("parallel", "parallel"), vmem_limit_bytes=VMEM_LIMIT_BIG),
        name="nsa_sel_attn",
    )(q_rot, bias, proj, proj, onehot)


def _nsa_win_kernel(qrot_ref, kw_ref, vw_ref, oc_ref, os_ref, gate_ref, o_ref,
                    vaug_ref, *, tq):
    R, S, _ = qrot_ref.shape
    vaug_ref[:, :HEAD_DIM] = vw_ref[...]
    vaug_ref[:, HEAD_DIM:] = jnp.ones((S, HEAD_DIM), BF16)

    def head_pass(c, heads):
        q_rows = slice(c * tq, (c + 1) * tq)
        k0 = max(0, c * tq - WINDOW)

        def store(row0, o_w):
            gt = _sigmoid(gate_ref[q_rows, :])
            for k, r in enumerate(heads):
                o = (gt[:, 3 * r:3 * r + 1] * oc_ref[r, q_rows, :].astype(F32)
                     + gt[:, 3 * r + 1:3 * r + 2] * os_ref[r, q_rows, :].astype(F32)
                     + gt[:, 3 * r + 2:3 * r + 3] * o_w[k * tq:(k + 1) * tq, :])
                o_ref[r, q_rows, :] = o.astype(BF16)

        return _AttnPass(
            q=lambda: jnp.concatenate([qrot_ref[r, q_rows, :] for r in heads], axis=0),
            k_ref=kw_ref, v_ref=vaug_ref, k0=k0, width=(c + 1) * tq - k0,
            n_rows=len(heads) * tq, tq=tq, q0=c * tq, window=WINDOW,
            pv_rows=len(heads) * tq, store=store)

    _run_attention([head_pass(c, range(r0, r0 + HEADS_PER_PASS))
                    for c in range(S // tq) for r0 in range(0, R, HEADS_PER_PASS)])


def _nsa_win(q_rot, proj, o_c, o_s, gates, *, B, S, tq):
    T = B * S
    R = NSA_REP
    heads_spec = pl.BlockSpec((R, S, LANES), lambda b, g: (g, b, 0))
    kv_spec = lambda blk: pl.BlockSpec((None, S, LANES), lambda b, g: (blk + g, b, 0))
    return pl.pallas_call(
        functools.partial(_nsa_win_kernel, tq=tq),
        grid=(B, NSA_GROUPS),
        in_specs=[heads_spec, kv_spec(BLK_NKW), kv_spec(BLK_NVW), heads_spec, heads_spec,
                  pl.BlockSpec((None, S, LANES), lambda b, g: (g, b, 0))],
        out_specs=heads_spec,
        out_shape=jax.ShapeDtypeStruct((NSA_HEADS, T, LANES), BF16),
        scratch_shapes=[pltpu.VMEM((S, 2 * LANES), BF16)],
        compiler_params=pltpu.CompilerParams(
            dimension_semantics=("parallel", "parallel"), vmem_limit_bytes=VMEM_LIMIT_BIG),
        name="nsa_win_attn",
    )(q_rot, proj, proj, o_c, o_s, gates)


def _out_proj_kernel(om_ref, on_ref, zm_ref, zn_ref, gm_ref, gn_ref, x_ref, w_ref, gf_ref,
                     out_ref):
    def gated_norm(o_ref, z_ref, g_ref, rows):
        n_heads = o_ref.shape[0]
        acts = []
        ss = None
        for h in range(n_heads):
            z = z_ref[h, rows, :].astype(F32)
            a = o_ref[h, rows, :].astype(F32) * (z * _sigmoid(z))
            acts.append(a)
            ss = a * a if ss is None else ss + a * a
        ms = jnp.sum(ss, axis=-1, keepdims=True) * (1.0 / (n_heads * LANES))
        inv = lax.rsqrt(ms + EPS)
        return [(acts[h] * inv * g_ref[h]).astype(BF16) for h in range(n_heads)]

    for r0 in range(0, x_ref.shape[0], OUT_SUB_ROWS):
        rows = slice(r0, r0 + OUT_SUB_ROWS)
        y = jnp.concatenate(gated_norm(om_ref, zm_ref, gm_ref, rows)
                            + gated_norm(on_ref, zn_ref, gn_ref, rows), axis=1)
        r = x_ref[rows, :] + jnp.dot(y, w_ref[...], preferred_element_type=F32)
        ms = jnp.mean(r * r, axis=-1, keepdims=True)
        out_ref[rows, :] = r * lax.rsqrt(ms + EPS) * gf_ref[...]


def _out_proj(o_moba, o_nsa, proj, g_moba, g_nsa, x2d, w_out, g_final, *, tm):
    T = x2d.shape[0]
    H = MOBA_HEADS
    return pl.pallas_call(
        _out_proj_kernel,
        grid=(T // tm,),
        in_specs=[pl.BlockSpec((H, tm, LANES), lambda i: (0, i, 0)),
                  pl.BlockSpec((H, tm, LANES), lambda i: (0, i, 0)),
                  pl.BlockSpec((H, tm, LANES), lambda i: (BLK_MZ // H, i, 0)),
                  pl.BlockSpec((H, tm, LANES), lambda i: (BLK_NZ // H, i, 0)),
                  pl.BlockSpec((H, 1, LANES), lambda i: (0, 0, 0)),
                  pl.BlockSpec((H, 1, LANES), lambda i: (0, 0, 0)),
                  pl.BlockSpec((tm, D_MODEL), lambda i: (i, 0)),
                  pl.BlockSpec((D_MODEL, D_MODEL), lambda i: (0, 0), pipeline_mode=pl.Buffered(1)),
                  pl.BlockSpec((1, D_MODEL), lambda i: (0, 0))],
        out_specs=pl.BlockSpec((tm, D_MODEL), lambda i: (i, 0)),
        out_shape=jax.ShapeDtypeStruct((T, D_MODEL), F32),
        compiler_params=pltpu.CompilerParams(
            dimension_semantics=("parallel",), vmem_limit_bytes=VMEM_LIMIT_BIG),
        name="out_proj",
    )(o_moba, o_nsa, proj, proj, g_moba, g_nsa, x2d, w_out, g_final)


def _w_in_offsets():
    mw, nw, kw = MOBA_HEADS * HEAD_DIM, NSA_HEADS * HEAD_DIM, NSA_GROUPS * HEAD_DIM
    sizes = [mw] * 4 + [nw] + [kw] * 6 + [3 * NSA_HEADS, nw]
    names = ["mq", "mk", "mv", "mz", "nq", "nkc", "nvc", "nks", "nvs", "nkw", "nvw", "ng", "nz"]
    offs = np.concatenate([[0], np.cumsum(sizes)])
    return {n: (int(offs[k]), int(sizes[k])) for k, n in enumerate(names)}


def _weight_tiles_kernel(w_ref, out_ref, *, src_cols, blocks_per_tile):
    for b, src in enumerate(src_cols):
        lane0 = (b % blocks_per_tile) * LANES
        out_ref[b // blocks_per_tile, :, lane0:lane0 + LANES] = (
            w_ref[:, src:src + LANES].astype(BF16))


def _weight_tiles(w_in, blocks_per_tile):
    offsets = _w_in_offsets()
    src_cols = []
    for name, n_blocks in COLUMN_ORDER:
        off, size = offsets[name]
        assert size == n_blocks * LANES
        src_cols += [off + k * LANES for k in range(n_blocks)]
    n_tiles = N_BLOCKS // blocks_per_tile
    rows = 256
    kern = functools.partial(_weight_tiles_kernel, src_cols=tuple(src_cols),
                             blocks_per_tile=blocks_per_tile)
    return pl.pallas_call(
        kern,
        grid=(D_MODEL // rows,),
        in_specs=[pl.BlockSpec((None, rows, w_in.shape[2]), lambda r: (0, r, 0))],
        out_specs=pl.BlockSpec((n_tiles, rows, blocks_per_tile * LANES), lambda r: (0, r, 0)),
        out_shape=jax.ShapeDtypeStruct((n_tiles, D_MODEL, blocks_per_tile * LANES), BF16),
        compiler_params=pltpu.CompilerParams(
            dimension_semantics=("parallel",), vmem_limit_bytes=VMEM_LIMIT),
        name="weight_tiles",
    )(w_in)


def _gate_weight(w_in):
    off, size = _w_in_offsets()["ng"]
    per_group = 3 * NSA_REP
    wg = w_in[0, :, off:off + size].reshape(D_MODEL, NSA_GROUPS, per_group)
    wg = jnp.pad(wg, ((0, 0), (0, 0), (0, LANES - per_group)))
    return wg.reshape(D_MODEL, NSA_GROUPS * LANES).astype(BF16)


def _block_onehot(S, block):
    ids = np.arange(S)[:, None] // block
    return jnp.asarray((ids == np.arange(LANES)[None, :]).astype(np.float32), dtype=BF16)


def _overlap_t(n_seg, n_cmp, n_sel_blk):
    cs = np.arange(n_seg)[None, :] * CMP_STRIDE
    ss = np.arange(n_sel_blk)[:, None] * SEL_BLOCK
    ov = (cs < ss + SEL_BLOCK) & (cs + CMP_BLOCK > ss) & (np.arange(n_seg)[None, :] < n_cmp)
    return jnp.asarray(ov.astype(np.float32), dtype=BF16)


def _layer(x, positions, w_in, g_norm, pe_ck, pe_cv, w_ck1, w_ck2, w_cv1, w_cv2,
           g_out_moba, g_out_nsa, w_out, g_final, *, nsa_tq, tm_in, tm_out, blocks_per_tile):
    B, S, _ = x.shape
    T = B * S
    x2d = x.reshape(T, D_MODEL)
    w_tiles = _weight_tiles(w_in.astype(BF16), blocks_per_tile)
    w_gate = _gate_weight(w_in)
    proj, gates, cos, sin, seg = _in_proj(x2d, g_norm.reshape(1, D_MODEL), w_tiles, w_gate,
                                          positions, tm=tm_in, blocks_per_tile=blocks_per_tile)

    n_seg = S // CMP_STRIDE
    seg = seg.reshape(2, NSA_GROUPS * B * n_seg, CMP_STRIDE * HEAD_DIM)
    pe = jnp.stack([pe_ck.reshape(1, -1), pe_cv.reshape(1, -1)])
    w1 = jnp.stack([w_ck1, w_cv1]).astype(BF16)
    w2 = jnp.stack([w_ck2, w_cv2]).astype(BF16)
    kvc = _compress(seg, pe, w1, w2)

    o_moba = _moba(proj, _block_onehot(S, MOBA_BLOCK), B=B, S=S)
    n_cmp = n_seg - CMP_BLOCK // CMP_STRIDE + 1
    q_rot, bias, o_c = _nsa_select(proj, cos, sin, kvc, _overlap_t(n_seg, n_cmp, S // SEL_BLOCK),
                                   B=B, S=S, tq=4 * nsa_tq)
    o_s = _nsa_sel(q_rot, bias, proj, _block_onehot(S, SEL_BLOCK), B=B, S=S, tq=nsa_tq)
    o_nsa = _nsa_win(q_rot, proj, o_c, o_s, gates, B=B, S=S, tq=nsa_tq)
    out = _out_proj(o_moba, o_nsa, proj,
                    g_out_moba.reshape(MOBA_HEADS, 1, LANES), g_out_nsa.reshape(NSA_HEADS, 1, LANES),
                    x2d, w_out.astype(BF16), g_final.reshape(1, D_MODEL), tm=tm_out)
    return out.reshape(B, S, D_MODEL)


def kernel(x, positions, w_in, g_norm, pe_ck, pe_cv, w_ck1, w_ck2, w_cv1, w_cv2,
           g_out_moba, g_out_nsa, w_out, g_final):
    assert w_in.shape[0] == 1, "single-layer problem"
    return _layer(x, positions, w_in, g_norm[0], pe_ck[0], pe_cv[0], w_ck1[0], w_ck2[0],
                  w_cv1[0], w_cv2[0], g_out_moba[0], g_out_nsa[0], w_out[0], g_final,
                  nsa_tq=256, tm_in=1024, tm_out=512, blocks_per_tile=10)
```

```python
import functools
from typing import Any, Callable, NamedTuple, Optional

import numpy as np
import jax
import jax.numpy as jnp
from jax import lax
from jax.experimental import pallas as pl
from jax.experimental.pallas import tpu as pltpu

F32 = jnp.float32
BF16 = jnp.bfloat16

D_MODEL = 2048
HEAD_DIM = 128
MOBA_HEADS = 8
NSA_HEADS = 8
NSA_GROUPS = 2
NSA_REP = 4
MOBA_BLOCK = 256
MOBA_TOPK = 3
CMP_BLOCK = 32
CMP_STRIDE = 16
CMP_HIDDEN = 256
SEL_BLOCK = 64
SEL_TOPK = 8
WINDOW = 512
ROPE_THETA = 10000.0
EPS = 1e-6
SCALE = HEAD_DIM ** -0.5
QK_PRESCALE = SCALE * float(np.log2(np.e))
NEG_BIG = -(2.0 ** 100)

LANES = 128
VMEM_LIMIT = 48 * 1024 * 1024
VMEM_LIMIT_BIG = 58 * 1024 * 1024
COLUMN_ORDER = (("mq", 8), ("mk", 8), ("nks", 2), ("nkw", 2), ("nkc", 2), ("nvc", 2),
                ("mv", 8), ("mz", 8), ("nq", 8), ("nz", 8), ("nvs", 2), ("nvw", 2))
BLK_MQ, BLK_MK, BLK_NKS, BLK_NKW, BLK_NKC, BLK_NVC = 0, 8, 16, 18, 20, 22
BLK_MV, BLK_MZ, BLK_NQ, BLK_NZ, BLK_NVS, BLK_NVW = 24, 32, 40, 48, 56, 58
N_BLOCKS = 60
N_ROPE_BLOCKS = 20
N_SEG_BLOCKS = 4

ROW_CHUNK = 64
MOBA_PV_ROWS = 256
HEADS_PER_PASS = 2
NORM_ROWS = 16
IN_SUB_ROWS = 256
MXU_COLS = 256
OUT_SUB_ROWS = 256
MOBA_TQ = 256
MOBA_HEADS_PER_STEP = 4
KEY_BLOCK = 256


def _nt_dot(a, b):
    return lax.dot_general(a, b, (((1,), (1,)), ((), ())), preferred_element_type=F32)


def _sigmoid(x):
    return 1.0 / (1.0 + jnp.exp(-x))


def _rope(a, cos, sin_signed):
    return a * cos + pltpu.roll(a, HEAD_DIM // 2, axis=a.ndim - 1) * sin_signed


def _interleave(*streams):
    n = max(len(s) for s in streams)
    for k in range(n):
        for s in streams:
            for thunk in s[k * len(s) // n:(k + 1) * len(s) // n]:
                thunk()


def _block_kind(b):
    if BLK_NKC <= b < BLK_NKC + N_SEG_BLOCKS:
        return "seg"
    return "rope_q" if b < BLK_MK else "rope" if b < N_ROPE_BLOCKS else "plain"


def _project_columns(h, w_ref, col0, kinds, cos, sin, store, store_segments=None):
    n = len(kinds)
    acc = jnp.dot(h, w_ref[:, col0:col0 + n * LANES], preferred_element_type=F32)
    n_seg = 0
    for c, kind in enumerate(kinds):
        a = acc[:, c * LANES:(c + 1) * LANES]
        if kind == "rope_q":
            a = _rope(a, cos * QK_PRESCALE, sin * QK_PRESCALE)
        elif kind == "rope":
            a = _rope(a, cos, sin)
        elif kind == "seg":
            store_segments(n_seg, a)
            n_seg += 1
        store(c, a.astype(BF16))


def _in_proj_kernel(x_ref, g_ref, w_ref, wg_ref, pos_ref, invf_ref, sign_ref,
                    out_ref, gate_ref, cos_ref, sin_ref, seg_ref, h_scr, seg_scr,
                    *, tile_patterns):
    j = pl.program_id(1)
    tm = x_ref.shape[0]
    first_pattern = tile_patterns[0][0]
    assert tile_patterns[0][1] == [0]

    @pl.when(j == 0)
    def _():
        ang = pos_ref[...].astype(F32) * invf_ref[...]
        cos_ref[...] = jnp.cos(ang)
        sin_ref[...] = jnp.sin(ang) * sign_ref[...]

        def norm_piece(r0):
            rows = slice(r0, r0 + NORM_ROWS)
            x = x_ref[rows, :]
            ms = jnp.mean(x * x, axis=-1, keepdims=True)
            h_scr[rows, :] = (x * lax.rsqrt(ms + EPS) * g_ref[...]).astype(BF16)

        def norm_pieces(m):
            return [functools.partial(norm_piece, r0)
                    for r0 in range(m * IN_SUB_ROWS, (m + 1) * IN_SUB_ROWS, NORM_ROWS)]

        def matmul_pieces(m):
            rows = slice(m * IN_SUB_ROWS, (m + 1) * IN_SUB_ROWS)

            def gates():
                acc = jnp.dot(h_scr[rows, :], wg_ref[...], preferred_element_type=F32)
                for g in range(NSA_GROUPS):
                    gate_ref[g, rows, :] = acc[:, g * LANES:(g + 1) * LANES]

            def columns(c0):
                def store(c, val):
                    out_ref[c0 + c, rows, :] = val
                per = MXU_COLS // LANES
                _project_columns(h_scr[rows, :], w_ref, c0 * LANES, first_pattern[c0:c0 + per],
                                 cos_ref[rows, :], sin_ref[rows, :], store)

            return [gates] + [functools.partial(columns, c0)
                              for c0 in range(0, len(first_pattern), MXU_COLS // LANES)]

        n_sub = tm // IN_SUB_ROWS
        _interleave(norm_pieces(0))
        for m in range(n_sub):
            _interleave(matmul_pieces(m), norm_pieces(m + 1) if m + 1 < n_sub else [])

    for pattern, tiles in tile_patterns[1:]:
        @pl.when(functools.reduce(jnp.logical_or, [j == t for t in tiles]))
        def _(pattern=pattern):
            def store(c, val):
                out_ref[c] = val

            def store_segments(k, a):
                seg_scr[...] = a
                for t in range(CMP_STRIDE):
                    piece = seg_scr[pl.ds(t, tm // CMP_STRIDE, stride=CMP_STRIDE), :]
                    seg_ref[k, :, t * HEAD_DIM:(t + 1) * HEAD_DIM] = piece.astype(BF16)

            _project_columns(h_scr[...], w_ref, 0, pattern, cos_ref[...], sin_ref[...], store,
                             store_segments)


def _in_proj(x2d, g_norm, w_tiles, w_gate, positions, *, tm, blocks_per_tile):
    T = x2d.shape[0]
    tn = blocks_per_tile * LANES
    n_tiles = N_BLOCKS // blocks_per_tile
    by_pattern = {}
    for t in range(n_tiles):
        pattern = tuple(_block_kind(t * blocks_per_tile + c) for c in range(blocks_per_tile))
        by_pattern.setdefault(pattern, []).append(t)
    half = HEAD_DIM // 2
    inv_freq = 1.0 / (ROPE_THETA ** (jnp.arange(0, HEAD_DIM, 2, dtype=F32) / HEAD_DIM))
    invf = jnp.concatenate([inv_freq, inv_freq]).reshape(1, HEAD_DIM)
    sign = jnp.concatenate([-jnp.ones((half,), F32), jnp.ones((half,), F32)]).reshape(1, HEAD_DIM)
    kern = functools.partial(_in_proj_kernel, tile_patterns=tuple(by_pattern.items()))
    row_table = pl.BlockSpec((tm, HEAD_DIM), lambda i, j: (i, 0))
    return pl.pallas_call(
        kern,
        grid=(T // tm, n_tiles),
        in_specs=[pl.BlockSpec((tm, D_MODEL), lambda i, j: (i, 0)),
                  pl.BlockSpec((1, D_MODEL), lambda i, j: (0, 0)),
                  pl.BlockSpec((None, D_MODEL, tn), lambda i, j: (j, 0, 0)),
                  pl.BlockSpec((D_MODEL, NSA_GROUPS * LANES), lambda i, j: (0, 0)),
                  pl.BlockSpec((tm, 1), lambda i, j: (i, 0)),
                  pl.BlockSpec((1, HEAD_DIM), lambda i, j: (0, 0)),
                  pl.BlockSpec((1, HEAD_DIM), lambda i, j: (0, 0))],
        out_specs=[pl.BlockSpec((blocks_per_tile, tm, LANES), lambda i, j: (j, i, 0)),
                   pl.BlockSpec((NSA_GROUPS, tm, LANES), lambda i, j: (0, i, 0)),
                   row_table, row_table,
                   pl.BlockSpec((N_SEG_BLOCKS, tm // CMP_STRIDE, CMP_STRIDE * HEAD_DIM),
                                lambda i, j: (0, i, 0))],
        out_shape=[jax.ShapeDtypeStruct((N_BLOCKS, T, LANES), BF16),
                   jax.ShapeDtypeStruct((NSA_GROUPS, T, LANES), F32),
                   jax.ShapeDtypeStruct((T, HEAD_DIM), F32),
                   jax.ShapeDtypeStruct((T, HEAD_DIM), F32),
                   jax.ShapeDtypeStruct((N_SEG_BLOCKS, T // CMP_STRIDE, CMP_STRIDE * HEAD_DIM),
                                        BF16)],
        scratch_shapes=[pltpu.VMEM((tm, D_MODEL), BF16), pltpu.VMEM((tm, HEAD_DIM), F32)],
        compiler_params=pltpu.CompilerParams(
            dimension_semantics=("parallel", "arbitrary"), vmem_limit_bytes=VMEM_LIMIT_BIG),
        name="in_proj",
    )(x2d, g_norm, w_tiles, w_gate, positions.reshape(T, 1), invf, sign)


def _compress_kernel(seg_ref, pe_ref, w1_ref, w2_ref, out_ref):
    half = CMP_STRIDE * HEAD_DIM
    seg = seg_ref[0].astype(F32)
    pe = pe_ref[0]
    top = (seg + pe[:, :half]).astype(BF16)
    bot = (seg + pe[:, half:]).astype(BF16)
    a = jnp.dot(top, w1_ref[0, :half, :], preferred_element_type=F32)
    b = jnp.dot(bot, w1_ref[0, half:, :], preferred_element_type=F32)
    rows = a.shape[0]
    h = a + pltpu.roll(b, rows - 1, axis=0)
    hid = h * _sigmoid(h)
    out_ref[0] = jnp.dot(hid.astype(BF16), w2_ref[0], preferred_element_type=F32).astype(BF16)


def _compress(seg, pe, w1, w2):
    _, R, half = seg.shape
    return pl.pallas_call(
        _compress_kernel,
        grid=(2,),
        in_specs=[pl.BlockSpec((1, R, half), lambda c: (c, 0, 0)),
                  pl.BlockSpec((1, 1, 2 * half), lambda c: (c, 0, 0)),
                  pl.BlockSpec((1, 2 * half, CMP_HIDDEN), lambda c: (c, 0, 0)),
                  pl.BlockSpec((1, CMP_HIDDEN, HEAD_DIM), lambda c: (c, 0, 0))],
        out_specs=pl.BlockSpec((1, R, HEAD_DIM), lambda c: (c, 0, 0)),
        out_shape=jax.ShapeDtypeStruct((2, R, HEAD_DIM), BF16),
        compiler_params=pltpu.CompilerParams(
            dimension_semantics=("arbitrary",), vmem_limit_bytes=VMEM_LIMIT),
        name="compress",
    )(seg, pe, w1, w2)


def _select_bias_t(score_t, n_rows, n_keep):
    sub = 8
    n_groups = score_t.shape[0] // sub
    groups = [score_t[g * sub:(g + 1) * sub, :] for g in range(n_groups)]
    jrow = lax.broadcasted_iota(jnp.int32, groups[0].shape, 0)
    cnts = [jnp.zeros(groups[0].shape, jnp.int32) for _ in range(n_groups)]
    for jp in range(n_rows):
        row = score_t[jp:jp + 1, :]
        for g, grp in enumerate(groups):
            if g * sub > jp:
                beats = row >= grp
            elif g * sub + sub - 1 <= jp:
                beats = row > grp
            else:
                beats = (row > grp) | ((row == grp) & (jrow + g * sub > jp))
            cnts[g] = cnts[g] + beats.astype(jnp.int32)
    cnt = cnts[0] if n_groups == 1 else jnp.concatenate(cnts, axis=0)
    return (cnt < n_keep) & (score_t > -jnp.inf)


def _bias_t(keep_t):
    return jnp.where(keep_t, 0.0, NEG_BIG).astype(F32)


def _bias_columns(bias_t):
    rows, q = bias_t.shape
    if rows < LANES:
        bias_t = jnp.concatenate([bias_t, jnp.zeros((LANES - rows, q), F32)], axis=0)
    return bias_t.T.astype(BF16)


def _softmax_rows(s_blocks, r0, *, tq, q0, k0, window=None):
    per_block = KEY_BLOCK // LANES
    n_tiles = len(s_blocks) * per_block
    rows = slice(r0, r0 + ROW_CHUNK)
    qlo = q0 + r0 % tq
    qhi = qlo + ROW_CHUNK - 1

    def tile(t):
        lane0 = (t % per_block) * LANES
        return s_blocks[t // per_block][rows, lane0:lane0 + LANES]

    kinds = []
    for t in range(n_tiles):
        klo = k0 + t * LANES
        khi = klo + LANES - 1
        none = klo > qhi or (window is not None and khi <= qlo - window)
        full = khi <= qlo and (window is None or klo > qhi - window)
        kinds.append("none" if none else "full" if full else "part")
    mx = None
    masked = {}
    for t, kind in enumerate(kinds):
        if kind == "none":
            continue
        x = tile(t)
        if kind == "part":
            qpos = qlo + lax.broadcasted_iota(jnp.int32, x.shape, 0)
            kpos = k0 + t * LANES + lax.broadcasted_iota(jnp.int32, x.shape, 1)
            ok = kpos <= qpos
            if window is not None:
                ok = ok & (kpos > qpos - window)
            x = jnp.where(ok, x, NEG_BIG)
            masked[t] = x
        mx = x if mx is None else jnp.maximum(mx, x)
    m = jnp.broadcast_to(jnp.max(mx, axis=-1, keepdims=True), mx.shape)
    p_tiles = []
    for t, kind in enumerate(kinds):
        if kind == "none":
            p_tiles.append(jnp.zeros((ROW_CHUNK, LANES), BF16))
            continue
        x = masked[t] if kind == "part" else tile(t)
        p_tiles.append(jnp.exp2(x - m).astype(BF16))
    return jnp.concatenate(p_tiles, axis=1)


def _pv_normalized(p, v_ones):
    o = jnp.dot(p, v_ones, preferred_element_type=F32)
    return o[:, :HEAD_DIM] / o[:, HEAD_DIM:]


class _AttnPass(NamedTuple):
    q: Callable[[], jax.Array]
    k_ref: Any
    v_ref: Any
    k0: int
    width: int
    n_rows: int
    tq: int
    q0: int
    window: Optional[int]
    pv_rows: int
    store: Callable[[int, jax.Array], None]


def _run_attention(passes):
    def score_thunks(ps):
        q = ps.q()
        return [lambda j=j: _nt_dot(
            q, ps.k_ref[ps.k0 + j * KEY_BLOCK:ps.k0 + (j + 1) * KEY_BLOCK, :])
                for j in range(ps.width // KEY_BLOCK)]

    s_blocks = [thunk() for thunk in score_thunks(passes[0])]
    pending_pv = None
    for t, ps in enumerate(passes):
        mxu_work = [] if pending_pv is None else [pending_pv]
        n_pv = len(mxu_work)
        if t + 1 < len(passes):
            mxu_work += score_thunks(passes[t + 1])
        row_starts = list(range(0, ps.n_rows, ROW_CHUNK))
        results, p_rows = [], []
        for k in range(max(len(mxu_work), len(row_starts))):
            if k < len(mxu_work):
                results.append(mxu_work[k]())
            if k < len(row_starts):
                p_rows.append(_softmax_rows(s_blocks, row_starts[k], tq=ps.tq, q0=ps.q0,
                                            k0=ps.k0, window=ps.window))
        s_blocks = results[n_pv:]

        def pending_pv(ps=ps, p_rows=p_rows):
            per = ps.pv_rows // ROW_CHUNK
            for k in range(ps.n_rows // ps.pv_rows):
                p = jnp.concatenate(p_rows[k * per:(k + 1) * per], axis=0)
                ps.store(k * ps.pv_rows, _pv_normalized(p, ps.v_ref[ps.k0:ps.k0 + ps.width, :]))
    pending_pv()


def _moba_kernel(q_ref, k_ref, v_ref, onehot_ref, o_ref, kaug_ref, qaug_ref, kmean_ref,
                 vaug_ref, *, n_blk, k_top):
    n_heads, S, _ = k_ref.shape
    tq = MOBA_TQ

    def prepare(h):
        kaug_ref[h, :, :HEAD_DIM] = k_ref[h]
        kaug_ref[h, :, HEAD_DIM:] = onehot_ref[...]
        vaug_ref[h, :, :HEAD_DIM] = v_ref[h]
        vaug_ref[h, :, HEAD_DIM:] = jnp.ones((S, HEAD_DIM), BF16)
        kmean_ref[h] = jnp.zeros(kmean_ref.shape[1:], F32)
        for j in range(n_blk):
            kb = k_ref[h, j * MOBA_BLOCK:(j + 1) * MOBA_BLOCK, :].astype(F32)
            kmean_ref[h, j:j + 1, :] = jnp.sum(kb, axis=0, keepdims=True) * (1.0 / MOBA_BLOCK)

        q = q_ref[h]
        gate_t = _nt_dot(kmean_ref[h].astype(BF16), q)
        jrow = lax.broadcasted_iota(jnp.int32, gate_t.shape, 0)
        own = lax.broadcasted_iota(jnp.int32, gate_t.shape, 1) // MOBA_BLOCK
        gate_t = jnp.where((jrow < own) & jnp.isfinite(gate_t), gate_t, -jnp.inf)
        bias_t = _bias_t(_select_bias_t(gate_t, n_blk, k_top) | (jrow == own))
        qaug_ref[h, :, :HEAD_DIM] = q
        for c in range(S // tq):
            qaug_ref[h, c * tq:(c + 1) * tq, HEAD_DIM:] = _bias_columns(
                bias_t[:, c * tq:(c + 1) * tq])

    def tile_pass(h, c):
        q_rows = slice(c * tq, (c + 1) * tq)

        def store(row0, o):
            o_ref[h, c * tq + row0:c * tq + row0 + o.shape[0], :] = o.astype(BF16)

        return _AttnPass(q=lambda: qaug_ref[h, q_rows, :], k_ref=kaug_ref.at[h],
                         v_ref=vaug_ref.at[h], k0=0, width=(c + 1) * tq, n_rows=tq, tq=tq,
                         q0=c * tq, window=None, pv_rows=MOBA_PV_ROWS, store=store)

    for h in range(n_heads):
        prepare(h)
    _run_attention([tile_pass(h, c) for c in range(S // tq) for h in range(n_heads)])


def _moba(proj, onehot, *, B, S):
    T = B * S
    n_blk = S // MOBA_BLOCK
    k_top = min(MOBA_TOPK, n_blk - 1)
    nb8 = -(-n_blk // 8) * 8
    hs = MOBA_HEADS_PER_STEP
    kern = functools.partial(_moba_kernel, n_blk=n_blk, k_top=k_top)
    head_spec = lambda blk: pl.BlockSpec((hs, S, LANES), lambda b, h: (blk // hs + h, b, 0))
    return pl.pallas_call(
        kern,
        grid=(B, MOBA_HEADS // hs),
        in_specs=[head_spec(BLK_MQ), head_spec(BLK_MK), head_spec(BLK_MV),
                  pl.BlockSpec((S, LANES), lambda b, h: (0, 0))],
        out_specs=head_spec(0),
        out_shape=jax.ShapeDtypeStruct((MOBA_HEADS, T, LANES), BF16),
        scratch_shapes=[pltpu.VMEM((hs, S, 2 * LANES), BF16), pltpu.VMEM((hs, S, 2 * LANES), BF16),
                        pltpu.VMEM((hs, nb8, HEAD_DIM), F32),
                        pltpu.VMEM((hs, S, 2 * LANES), BF16)],
        compiler_params=pltpu.CompilerParams(
            dimension_semantics=("parallel", "parallel"), vmem_limit_bytes=VMEM_LIMIT),
        name="moba_attn",
    )(proj, proj, proj, onehot)


def _nsa_select_kernel(q_ref, cos_ref, sin_ref, kc_ref, vc_ref, ovt_ref,
                       qrot_ref, bias_ref, oc_ref, *, n_cmp, n_sel_blk, n_top):
    i = pl.program_id(2)
    R, tq, _ = q_ref.shape
    q_raw = q_ref[...]
    cos = (cos_ref[...] * QK_PRESCALE)[None]
    sin = (sin_ref[...] * QK_PRESCALE)[None]
    qrot_ref[...] = _rope(q_raw.astype(F32), cos, sin).astype(BF16)

    n_seg = kc_ref.shape[0]
    s_c = (_nt_dot(q_raw.reshape(R * tq, HEAD_DIM), kc_ref[...]) * SCALE).reshape(R, tq, n_seg)
    n_idx = lax.broadcasted_iota(jnp.int32, s_c.shape, 2)
    pos3 = i * tq + lax.broadcasted_iota(jnp.int32, s_c.shape, 1)
    m_c = (n_idx * CMP_STRIDE + CMP_BLOCK - 1 <= pos3) & (n_idx < n_cmp)
    s_c = jnp.where(m_c, s_c, -jnp.inf)
    mx = jnp.max(s_c, axis=-1, keepdims=True)
    mx = jnp.where(jnp.isfinite(mx), mx, 0.0)
    e_c = jnp.where(m_c, jnp.exp(s_c - mx), 0.0)
    p_c = e_c / jnp.maximum(jnp.sum(e_c, axis=-1, keepdims=True), 1e-30)
    o_c = jnp.dot(p_c.reshape(R * tq, n_seg).astype(BF16), vc_ref[...],
                  preferred_element_type=F32).reshape(R, tq, HEAD_DIM)
    oc_ref[...] = o_c.astype(BF16)

    p_sum = jnp.sum(p_c, axis=0)
    p_hi = p_sum.astype(BF16)
    p_lo = (p_sum - p_hi.astype(F32)).astype(BF16)
    ovt = ovt_ref[...]
    imp_t = _nt_dot(ovt, p_hi) + _nt_dot(ovt, p_lo)
    jrow = lax.broadcasted_iota(jnp.int32, imp_t.shape, 0)
    posq = i * tq + lax.broadcasted_iota(jnp.int32, imp_t.shape, 1)
    own = posq // SEL_BLOCK
    forced = (jrow == 0) | (jrow == own) | (jrow == own - 1)
    future = jrow * SEL_BLOCK > posq
    score_t = jnp.where(future, -jnp.inf, jnp.where(forced, jnp.inf, imp_t))
    keep_t = _select_bias_t(score_t, n_sel_blk, n_top)
    bias_ref[...] = _bias_columns(_bias_t(keep_t))


def _nsa_select(proj, cos, sin, kvc, ovt, *, B, S, tq):
    T = B * S
    nq = S // tq
    n_seg = S // CMP_STRIDE
    n_cmp = n_seg - CMP_BLOCK // CMP_STRIDE + 1
    n_sel_blk = S // SEL_BLOCK
    R = NSA_REP
    kern = functools.partial(_nsa_select_kernel, n_cmp=n_cmp, n_sel_blk=n_sel_blk,
                             n_top=min(SEL_TOPK, n_sel_blk))
    heads_spec = pl.BlockSpec((R, tq, LANES), lambda b, g, i: (g, b * nq + i, 0))
    return pl.pallas_call(
        kern,
        grid=(B, NSA_GROUPS, nq),
        in_specs=[pl.BlockSpec((R, tq, LANES), lambda b, g, i: (BLK_NQ // R + g, b * nq + i, 0)),
                  pl.BlockSpec((tq, LANES), lambda b, g, i: (b * nq + i, 0)),
                  pl.BlockSpec((tq, LANES), lambda b, g, i: (b * nq + i, 0)),
                  pl.BlockSpec((None, n_seg, LANES), lambda b, g, i: (0, g * B + b, 0)),
                  pl.BlockSpec((None, n_seg, LANES), lambda b, g, i: (1, g * B + b, 0)),
                  pl.BlockSpec(ovt.shape, lambda b, g, i: (0, 0))],
        out_specs=[heads_spec,
                   pl.BlockSpec((None, tq, LANES), lambda b, g, i: (g, b * nq + i, 0)),
                   heads_spec],
        out_shape=[jax.ShapeDtypeStruct((NSA_HEADS, T, LANES), BF16),
                   jax.ShapeDtypeStruct((NSA_GROUPS, T, LANES), BF16),
                   jax.ShapeDtypeStruct((NSA_HEADS, T, LANES), BF16)],
        compiler_params=pltpu.CompilerParams(
            dimension_semantics=("parallel", "parallel", "parallel"),
            vmem_limit_bytes=VMEM_LIMIT),
        name="nsa_select",
    )(proj, cos, sin, kvc, kvc, ovt)


def _nsa_sel_kernel(qrot_ref, bias_ref, ks_ref, vs_ref, onehot_ref, o_ref,
                    ksaug_ref, vaug_ref, *, tq):
    R, S, _ = qrot_ref.shape
    ksaug_ref[:, :HEAD_DIM] = ks_ref[...]
    ksaug_ref[:, HEAD_DIM:] = onehot_ref[...]
    vaug_ref[:, :HEAD_DIM] = vs_ref[...]
    vaug_ref[:, HEAD_DIM:] = jnp.ones((S, HEAD_DIM), BF16)

    def head_pass(c, heads):
        q_rows = slice(c * tq, (c + 1) * tq)

        def q_aug():
            bias = bias_ref[q_rows, :]
            return jnp.concatenate(
                [jnp.concatenate([qrot_ref[r, q_rows, :], bias], axis=1) for r in heads], axis=0)

        def store(row0, o):
            for k, r in enumerate(heads):
                o_ref[r, q_rows, :] = o[k * tq:(k + 1) * tq, :].astype(BF16)

        return _AttnPass(q=q_aug, k_ref=ksaug_ref, v_ref=vaug_ref, k0=0, width=(c + 1) * tq,
                         n_rows=len(heads) * tq, tq=tq, q0=c * tq, window=None,
                         pv_rows=len(heads) * tq, store=store)

    _run_attention([head_pass(c, range(r0, r0 + HEADS_PER_PASS))
                    for c in range(S // tq) for r0 in range(0, R, HEADS_PER_PASS)])


def _nsa_sel(q_rot, bias, proj, onehot, *, B, S, tq):
    T = B * S
    R = NSA_REP
    heads_spec = pl.BlockSpec((R, S, LANES), lambda b, g: (g, b, 0))
    kv_spec = lambda blk: pl.BlockSpec((None, S, LANES), lambda b, g: (blk + g, b, 0))
    return pl.pallas_call(
        functools.partial(_nsa_sel_kernel, tq=tq),
        grid=(B, NSA_GROUPS),
        in_specs=[heads_spec,
                  pl.BlockSpec((None, S, LANES), lambda b, g: (g, b, 0)),
                  kv_spec(BLK_NKS), kv_spec(BLK_NVS),
                  pl.BlockSpec((S, LANES), lambda b, g: (0, 0))],
        out_specs=heads_spec,
        out_shape=jax.ShapeDtypeStruct((NSA_HEADS, T, LANES), BF16),
        scratch_shapes=[pltpu.VMEM((S, 2 * LANES), BF16), pltpu.VMEM((S, 2 * LANES), BF16)],
        compiler_params=pltpu.CompilerParams(
            dimension_semantics=("parallel", "parallel"), vmem_limit_bytes=VMEM_LIMIT_BIG),
        name="nsa_sel_attn",
    )(q_rot, bias, proj, proj, onehot)


def _nsa_win_kernel(qrot_ref, kw_ref, vw_ref, oc_ref, os_ref, gate_ref, o_ref,
                    vaug_ref, *, tq):
    R, S, _ = qrot_ref.shape
    vaug_ref[:, :HEAD_DIM] = vw_ref[...]
    vaug_ref[:, HEAD_DIM:] = jnp.ones((S, HEAD_DIM), BF16)

    def head_pass(c, heads):
        q_rows = slice(c * tq, (c + 1) * tq)
        k0 = max(0, c * tq - WINDOW)

        def store(row0, o_w):
            gt = _sigmoid(gate_ref[q_rows, :])
            for k, r in enumerate(heads):
                o = (gt[:, 3 * r:3 * r + 1] * oc_ref[r, q_rows, :].astype(F32)
                     + gt[:, 3 * r + 1:3 * r + 2] * os_ref[r, q_rows, :].astype(F32)
                     + gt[:, 3 * r + 2:3 * r + 3] * o_w[k * tq:(k + 1) * tq, :])
                o_ref[r, q_rows, :] = o.astype(BF16)

        return _AttnPass(
            q=lambda: jnp.concatenate([qrot_ref[r, q_rows, :] for r in heads], axis=0),
            k_ref=kw_ref, v_ref=vaug_ref, k0=k0, width=(c + 1) * tq - k0,
            n_rows=len(heads) * tq, tq=tq, q0=c * tq, window=WINDOW,
            pv_rows=len(heads) * tq, store=store)

    _run_attention([head_pass(c, range(r0, r0 + HEADS_PER_PASS))
                    for c in range(S // tq) for r0 in range(0, R, HEADS_PER_PASS)])


def _nsa_win(q_rot, proj, o_c, o_s, gates, *, B, S, tq):
    T = B * S
    R = NSA_REP
    heads_spec = pl.BlockSpec((R, S, LANES), lambda b, g: (g, b, 0))
    kv_spec = lambda blk: pl.BlockSpec((None, S, LANES), lambda b, g: (blk + g, b, 0))
    return pl.pallas_call(
        functools.partial(_nsa_win_kernel, tq=tq),
        grid=(B, NSA_GROUPS),
        in_specs=[heads_spec, kv_spec(BLK_NKW), kv_spec(BLK_NVW), heads_spec, heads_spec,
                  pl.BlockSpec((None, S, LANES), lambda b, g: (g, b, 0))],
        out_specs=heads_spec,
        out_shape=jax.ShapeDtypeStruct((NSA_HEADS, T, LANES), BF16),
        scratch_shapes=[pltpu.VMEM((S, 2 * LANES), BF16)],
        compiler_params=pltpu.CompilerParams(
            dimension_semantics=("parallel", "parallel"), vmem_limit_bytes=VMEM_LIMIT_BIG),
        name="nsa_win_attn",
    )(q_rot, proj, proj, o_c, o_s, gates)


def _out_proj_kernel(om_ref, on_ref, zm_ref, zn_ref, gm_ref, gn_ref, x_ref, w_ref, gf_ref,
                     out_ref):
    def gated_norm(o_ref, z_ref, g_ref, rows):
        n_heads = o_ref.shape[0]
        acts = []
        ss = None
        for h in range(n_heads):
            z = z_ref[h, rows, :].astype(F32)
            a = o_ref[h, rows, :].astype(F32) * (z * _sigmoid(z))
            acts.append(a)
            ss = a * a if ss is None else ss + a * a
        ms = jnp.sum(ss, axis=-1, keepdims=True) * (1.0 / (n_heads * LANES))
        inv = lax.rsqrt(ms + EPS)
        return [(acts[h] * inv * g_ref[h]).astype(BF16) for h in range(n_heads)]

    for r0 in range(0, x_ref.shape[0], OUT_SUB_ROWS):
        rows = slice(r0, r0 + OUT_SUB_ROWS)
        y = jnp.concatenate(gated_norm(om_ref, zm_ref, gm_ref, rows)
                            + gated_norm(on_ref, zn_ref, gn_ref, rows), axis=1)
        r = x_ref[rows, :] + jnp.dot(y, w_ref[...], preferred_element_type=F32)
        ms = jnp.mean(r * r, axis=-1, keepdims=True)
        out_ref[rows, :] = r * lax.rsqrt(ms + EPS) * gf_ref[...]


def _out_proj(o_moba, o_nsa, proj, g_moba, g_nsa, x2d, w_out, g_final, *, tm):
    T = x2d.shape[0]
    H = MOBA_HEADS
    return pl.pallas_call(
        _out_proj_kernel,
        grid=(T // tm,),
        in_specs=[pl.BlockSpec((H, tm, LANES), lambda i: (0, i, 0)),
                  pl.BlockSpec((H, tm, LANES), lambda i: (0, i, 0)),
                  pl.BlockSpec((H, tm, LANES), lambda i: (BLK_MZ // H, i, 0)),
                  pl.BlockSpec((H, tm, LANES), lambda i: (BLK_NZ // H, i, 0)),
                  pl.BlockSpec((H, 1, LANES), lambda i: (0, 0, 0)),
                  pl.BlockSpec((H, 1, LANES), lambda i: (0, 0, 0)),
                  pl.BlockSpec((tm, D_MODEL), lambda i: (i, 0)),
                  pl.BlockSpec((D_MODEL, D_MODEL), lambda i: (0, 0), pipeline_mode=pl.Buffered(1)),
                  pl.BlockSpec((1, D_MODEL), lambda i: (0, 0))],
        out_specs=pl.BlockSpec((tm, D_MODEL), lambda i: (i, 0)),
        out_shape=jax.ShapeDtypeStruct((T, D_MODEL), F32),
        compiler_params=pltpu.CompilerParams(
            dimension_semantics=("parallel",), vmem_limit_bytes=VMEM_LIMIT_BIG),
        name="out_proj",
    )(o_moba, o_nsa, proj, proj, g_moba, g_nsa, x2d, w_out, g_final)


def _w_in_offsets():
    mw, nw, kw = MOBA_HEADS * HEAD_DIM, NSA_HEADS * HEAD_DIM, NSA_GROUPS * HEAD_DIM
    sizes = [mw] * 4 + [nw] + [kw] * 6 + [3 * NSA_HEADS, nw]
    names = ["mq", "mk", "mv", "mz", "nq", "nkc", "nvc", "nks", "nvs", "nkw", "nvw", "ng", "nz"]
    offs = np.concatenate([[0], np.cumsum(sizes)])
    return {n: (int(offs[k]), int(sizes[k])) for k, n in enumerate(names)}


def _weight_tiles_kernel(w_ref, out_ref, *, src_cols, blocks_per_tile):
    for b, src in enumerate(src_cols):
        lane0 = (b % blocks_per_tile) * LANES
        out_ref[b // blocks_per_tile, :, lane0:lane0 + LANES] = (
            w_ref[:, src:src + LANES].astype(BF16))


def _weight_tiles(w_in, blocks_per_tile):
    offsets = _w_in_offsets()
    src_cols = []
    for name, n_blocks in COLUMN_ORDER:
        off, size = offsets[name]
        assert size == n_blocks * LANES
        src_cols += [off + k * LANES for k in range(n_blocks)]
    n_tiles = N_BLOCKS // blocks_per_tile
    rows = 256
    kern = functools.partial(_weight_tiles_kernel, src_cols=tuple(src_cols),
                             blocks_per_tile=blocks_per_tile)
    return pl.pallas_call(
        kern,
        grid=(D_MODEL // rows,),
        in_specs=[pl.BlockSpec((None, rows, w_in.shape[2]), lambda r: (0, r, 0))],
        out_specs=pl.BlockSpec((n_tiles, rows, blocks_per_tile * LANES), lambda r: (0, r, 0)),
        out_shape=jax.ShapeDtypeStruct((n_tiles, D_MODEL, blocks_per_tile * LANES), BF16),
        compiler_params=pltpu.CompilerParams(
            dimension_semantics=("parallel",), vmem_limit_bytes=VMEM_LIMIT),
        name="weight_tiles",
    )(w_in)


def _gate_weight(w_in):
    off, size = _w_in_offsets()["ng"]
    per_group = 3 * NSA_REP
    wg = w_in[0, :, off:off + size].reshape(D_MODEL, NSA_GROUPS, per_group)
    wg = jnp.pad(wg, ((0, 0), (0, 0), (0, LANES - per_group)))
    return wg.reshape(D_MODEL, NSA_GROUPS * LANES).astype(BF16)


def _block_onehot(S, block):
    ids = np.arange(S)[:, None] // block
    return jnp.asarray((ids == np.arange(LANES)[None, :]).astype(np.float32), dtype=BF16)


def _overlap_t(n_seg, n_cmp, n_sel_blk):
    cs = np.arange(n_seg)[None, :] * CMP_STRIDE
    ss = np.arange(n_sel_blk)[:, None] * SEL_BLOCK
    ov = (cs < ss + SEL_BLOCK) & (cs + CMP_BLOCK > ss) & (np.arange(n_seg)[None, :] < n_cmp)
    return jnp.asarray(ov.astype(np.float32), dtype=BF16)


def _layer(x, positions, w_in, g_norm, pe_ck, pe_cv, w_ck1, w_ck2, w_cv1, w_cv2,
           g_out_moba, g_out_nsa, w_out, g_final, *, nsa_tq, tm_in, tm_out, blocks_per_tile):
    B, S, _ = x.shape
    T = B * S
    x2d = x.reshape(T, D_MODEL)
    w_tiles = _weight_tiles(w_in.astype(BF16), blocks_per_tile)
    w_gate = _gate_weight(w_in)
    proj, gates, cos, sin, seg = _in_proj(x2d, g_norm.reshape(1, D_MODEL), w_tiles, w_gate,
                                          positions, tm=tm_in, blocks_per_tile=blocks_per_tile)

    n_seg = S // CMP_STRIDE
    seg = seg.reshape(2, NSA_GROUPS * B * n_seg, CMP_STRIDE * HEAD_DIM)
    pe = jnp.stack([pe_ck.reshape(1, -1), pe_cv.reshape(1, -1)])
    w1 = jnp.stack([w_ck1, w_cv1]).astype(BF16)
    w2 = jnp.stack([w_ck2, w_cv2]).astype(BF16)
    kvc = _compress(seg, pe, w1, w2)

    o_moba = _moba(proj, _block_onehot(S, MOBA_BLOCK), B=B, S=S)
    n_cmp = n_seg - CMP_BLOCK // CMP_STRIDE + 1
    q_rot, bias, o_c = _nsa_select(proj, cos, sin, kvc, _overlap_t(n_seg, n_cmp, S // SEL_BLOCK),
                                   B=B, S=S, tq=4 * nsa_tq)
    o_s = _nsa_sel(q_rot, bias, proj, _block_onehot(S, SEL_BLOCK), B=B, S=S, tq=nsa_tq)
    o_nsa = _nsa_win(q_rot, proj, o_c, o_s, gates, B=B, S=S, tq=nsa_tq)
    out = _out_proj(o_moba, o_nsa, proj,
                    g_out_moba.reshape(MOBA_HEADS, 1, LANES), g_out_nsa.reshape(NSA_HEADS, 1, LANES),
                    x2d, w_out.astype(BF16), g_final.reshape(1, D_MODEL), tm=tm_out)
    return out.reshape(B, S, D_MODEL)


def kernel(x, positions, w_in, g_norm, pe_ck, pe_cv, w_ck1, w_ck2, w_cv1, w_cv2,
           g_out_moba, g_out_nsa, w_out, g_final):
    assert w_in.shape[0] == 1, "single-layer problem"
    return _layer(x, positions, w_in, g_norm[0], pe_ck[0], pe_cv[0], w_ck1[0], w_ck2[0],
                  w_cv1[0], w_cv2[0], g_out_moba[0], g_out_nsa[0], w_out[0], g_final,
                  nsa_tq=256, tm_in=1024, tm_out=512, blocks_per_tile=10)
```

```python
import functools
from typing import Any, Callable, NamedTuple, Optional

import numpy as np
import jax
import jax.numpy as jnp
from jax import lax
from jax.experimental import pallas as pl
from jax.experimental.pallas import tpu as pltpu

F32 = jnp.float32
BF16 = jnp.bfloat16

D_MODEL = 2048
HEAD_DIM = 128
MOBA_HEADS = 8
NSA_HEADS = 8
NSA_GROUPS = 2
NSA_REP = 4
MOBA_BLOCK = 256
MOBA_TOPK = 3
CMP_BLOCK = 32
CMP_STRIDE = 16
CMP_HIDDEN = 256
SEL_BLOCK = 64
SEL_TOPK = 8
WINDOW = 512
ROPE_THETA = 10000.0
EPS = 1e-6
SCALE = HEAD_DIM ** -0.5
QK_PRESCALE = SCALE * float(np.log2(np.e))
NEG_BIG = -(2.0 ** 100)

LANES = 128
VMEM_LIMIT = 48 * 1024 * 1024
VMEM_LIMIT_BIG = 58 * 1024 * 1024
COLUMN_ORDER = (("mq", 8), ("mk", 8), ("nks", 2), ("nkw", 2), ("nkc", 2), ("nvc", 2),
                ("mv", 8), ("mz", 8), ("nq", 8), ("nz", 8), ("nvs", 2), ("nvw", 2))
BLK_MQ, BLK_MK, BLK_NKS, BLK_NKW, BLK_NKC, BLK_NVC = 0, 8, 16, 18, 20, 22
BLK_MV, BLK_MZ, BLK_NQ, BLK_NZ, BLK_NVS, BLK_NVW = 24, 32, 40, 48, 56, 58
N_BLOCKS = 60
N_ROPE_BLOCKS = 20
N_SEG_BLOCKS = 4

ROW_CHUNK = 64
MOBA_PV_ROWS = 256
HEADS_PER_PASS = 2
NORM_ROWS = 16
IN_SUB_ROWS = 256
MXU_COLS = 256
OUT_SUB_ROWS = 256
MOBA_TQ = 256
MOBA_HEADS_PER_STEP = 4
KEY_BLOCK = 256


def _nt_dot(a, b):
    return lax.dot_general(a, b, (((1,), (1,)), ((), ())), preferred_element_type=F32)


def _sigmoid(x):
    return 1.0 / (1.0 + jnp.exp(-x))


def _rope(a, cos, sin_signed):
    return a * cos + pltpu.roll(a, HEAD_DIM // 2, axis=a.ndim - 1) * sin_signed


def _interleave(*streams):
    n = max(len(s) for s in streams)
    for k in range(n):
        for s in streams:
            for thunk in s[k * len(s) // n:(k + 1) * len(s) // n]:
                thunk()


def _block_kind(b):
    if BLK_NKC <= b < BLK_NKC + N_SEG_BLOCKS:
        return "seg"
    return "rope_q" if b < BLK_MK else "rope" if b < N_ROPE_BLOCKS else "plain"


def _project_columns(h, w_ref, col0, kinds, cos, sin, store, store_segments=None):
    n = len(kinds)
    acc = jnp.dot(h, w_ref[:, col0:col0 + n * LANES], preferred_element_type=F32)
    n_seg = 0
    for c, kind in enumerate(kinds):
        a = acc[:, c * LANES:(c + 1) * LANES]
        if kind == "rope_q":
            a = _rope(a, cos * QK_PRESCALE, sin * QK_PRESCALE)
        elif kind == "rope":
            a = _rope(a, cos, sin)
        elif kind == "seg":
            store_segments(n_seg, a)
            n_seg += 1
        store(c, a.astype(BF16))


def _in_proj_kernel(x_ref, g_ref, w_ref, wg_ref, pos_ref, invf_ref, sign_ref,
                    out_ref, gate_ref, cos_ref, sin_ref, seg_ref, h_scr, seg_scr,
                    *, tile_patterns):
    j = pl.program_id(1)
    tm = x_ref.shape[0]
    first_pattern = tile_patterns[0][0]
    assert tile_patterns[0][1] == [0]

    @pl.when(j == 0)
    def _():
        ang = pos_ref[...].astype(F32) * invf_ref[...]
        cos_ref[...] = jnp.cos(ang)
        sin_ref[...] = jnp.sin(ang) * sign_ref[...]

        def norm_piece(r0):
            rows = slice(r0, r0 + NORM_ROWS)
            x = x_ref[rows, :]
            ms = jnp.mean(x * x, axis=-1, keepdims=True)
            h_scr[rows, :] = (x * lax.rsqrt(ms + EPS) * g_ref[...]).astype(BF16)

        def norm_pieces(m):
            return [functools.partial(norm_piece, r0)
                    for r0 in range(m * IN_SUB_ROWS, (m + 1) * IN_SUB_ROWS, NORM_ROWS)]

        def matmul_pieces(m):
            rows = slice(m * IN_SUB_ROWS, (m + 1) * IN_SUB_ROWS)

            def gates():
                acc = jnp.dot(h_scr[rows, :], wg_ref[...], preferred_element_type=F32)
                for g in range(NSA_GROUPS):
                    gate_ref[g, rows, :] = acc[:, g * LANES:(g + 1) * LANES]

            def columns(c0):
                def store(c, val):
                    out_ref[c0 + c, rows, :] = val
                per = MXU_COLS // LANES
                _project_columns(h_scr[rows, :], w_ref, c0 * LANES, first_pattern[c0:c0 + per],
                                 cos_ref[rows, :], sin_ref[rows, :], store)

            return [gates] + [functools.partial(columns, c0)
                              for c0 in range(0, len(first_pattern), MXU_COLS // LANES)]

        n_sub = tm // IN_SUB_ROWS
        _interleave(norm_pieces(0))
        for m in range(n_sub):
            _interleave(matmul_pieces(m), norm_pieces(m + 1) if m + 1 < n_sub else [])

    for pattern, tiles in tile_patterns[1:]:
        @pl.when(functools.reduce(jnp.logical_or, [j == t for t in tiles]))
        def _(pattern=pattern):
            def store(c, val):
                out_ref[c] = val

            def store_segments(k, a):
                seg_scr[...] = a
                for t in range(CMP_STRIDE):
                    piece = seg_scr[pl.ds(t, tm // CMP_STRIDE, stride=CMP_STRIDE), :]
                    seg_ref[k, :, t * HEAD_DIM:(t + 1) * HEAD_DIM] = piece.astype(BF16)

            _project_columns(h_scr[...], w_ref, 0, pattern, cos_ref[...], sin_ref[...], store,
                             store_segments)


def _in_proj(x2d, g_norm, w_tiles, w_gate, positions, *, tm, blocks_per_tile):
    T = x2d.shape[0]
    tn = blocks_per_tile * LANES
    n_tiles = N_BLOCKS // blocks_per_tile
    by_pattern = {}
    for t in range(n_tiles):
        pattern = tuple(_block_kind(t * blocks_per_tile + c) for c in range(blocks_per_tile))
        by_pattern.setdefault(pattern, []).append(t)
    half = HEAD_DIM // 2
    inv_freq = 1.0 / (ROPE_THETA ** (jnp.arange(0, HEAD_DIM, 2, dtype=F32) / HEAD_DIM))
    invf = jnp.concatenate([inv_freq, inv_freq]).reshape(1, HEAD_DIM)
    sign = jnp.concatenate([-jnp.ones((half,), F32), jnp.ones((half,), F32)]).reshape(1, HEAD_DIM)
    kern = functools.partial(_in_proj_kernel, tile_patterns=tuple(by_pattern.items()))
    row_table = pl.BlockSpec((tm, HEAD_DIM), lambda i, j: (i, 0))
    return pl.pallas_call(
        kern,
        grid=(T // tm, n_tiles),
        in_specs=[pl.BlockSpec((tm, D_MODEL), lambda i, j: (i, 0)),
                  pl.BlockSpec((1, D_MODEL), lambda i, j: (0, 0)),
                  pl.BlockSpec((None, D_MODEL, tn), lambda i, j: (j, 0, 0)),
                  pl.BlockSpec((D_MODEL, NSA_GROUPS * LANES), lambda i, j: (0, 0)),
                  pl.BlockSpec((tm, 1), lambda i, j: (i, 0)),
                  pl.BlockSpec((1, HEAD_DIM), lambda i, j: (0, 0)),
                  pl.BlockSpec((1, HEAD_DIM), lambda i, j: (0, 0))],
        out_specs=[pl.BlockSpec((blocks_per_tile, tm, LANES), lambda i, j: (j, i, 0)),
                   pl.BlockSpec((NSA_GROUPS, tm, LANES), lambda i, j: (0, i, 0)),
                   row_table, row_table,
                   pl.BlockSpec((N_SEG_BLOCKS, tm // CMP_STRIDE, CMP_STRIDE * HEAD_DIM),
                                lambda i, j: (0, i, 0))],
        out_shape=[jax.ShapeDtypeStruct((N_BLOCKS, T, LANES), BF16),
                   jax.ShapeDtypeStruct((NSA_GROUPS, T, LANES), F32),
                   jax.ShapeDtypeStruct((T, HEAD_DIM), F32),
                   jax.ShapeDtypeStruct((T, HEAD_DIM), F32),
                   jax.ShapeDtypeStruct((N_SEG_BLOCKS, T // CMP_STRIDE, CMP_STRIDE * HEAD_DIM),
                                        BF16)],
        scratch_shapes=[pltpu.VMEM((tm, D_MODEL), BF16), pltpu.VMEM((tm, HEAD_DIM), F32)],
        compiler_params=pltpu.CompilerParams(
            dimension_semantics=("parallel", "arbitrary"), vmem_limit_bytes=VMEM_LIMIT_BIG),
        name="in_proj",
    )(x2d, g_norm, w_tiles, w_gate, positions.reshape(T, 1), invf, sign)


def _compress_kernel(seg_ref, pe_ref, w1_ref, w2_ref, out_ref):
    half = CMP_STRIDE * HEAD_DIM
    seg = seg_ref[0].astype(F32)
    pe = pe_ref[0]
    top = (seg + pe[:, :half]).astype(BF16)
    bot = (seg + pe[:, half:]).astype(BF16)
    a = jnp.dot(top, w1_ref[0, :half, :], preferred_element_type=F32)
    b = jnp.dot(bot, w1_ref[0, half:, :], preferred_element_type=F32)
    rows = a.shape[0]
    h = a + pltpu.roll(b, rows - 1, axis=0)
    hid = h * _sigmoid(h)
    out_ref[0] = jnp.dot(hid.astype(BF16), w2_ref[0], preferred_element_type=F32).astype(BF16)


def _compress(seg, pe, w1, w2):
    _, R, half = seg.shape
    return pl.pallas_call(
        _compress_kernel,
        grid=(2,),
        in_specs=[pl.BlockSpec((1, R, half), lambda c: (c, 0, 0)),
                  pl.BlockSpec((1, 1, 2 * half), lambda c: (c, 0, 0)),
                  pl.BlockSpec((1, 2 * half, CMP_HIDDEN), lambda c: (c, 0, 0)),
                  pl.BlockSpec((1, CMP_HIDDEN, HEAD_DIM), lambda c: (c, 0, 0))],
        out_specs=pl.BlockSpec((1, R, HEAD_DIM), lambda c: (c, 0, 0)),
        out_shape=jax.ShapeDtypeStruct((2, R, HEAD_DIM), BF16),
        compiler_params=pltpu.CompilerParams(
            dimension_semantics=("arbitrary",), vmem_limit_bytes=VMEM_LIMIT),
        name="compress",
    )(seg, pe, w1, w2)


def _select_bias_t(score_t, n_rows, n_keep):
    sub = 8
    n_groups = score_t.shape[0] // sub
    groups = [score_t[g * sub:(g + 1) * sub, :] for g in range(n_groups)]
    jrow = lax.broadcasted_iota(jnp.int32, groups[0].shape, 0)
    cnts = [jnp.zeros(groups[0].shape, jnp.int32) for _ in range(n_groups)]
    for jp in range(n_rows):
        row = score_t[jp:jp + 1, :]
        for g, grp in enumerate(groups):
            if g * sub > jp:
                beats = row >= grp
            elif g * sub + sub - 1 <= jp:
                beats = row > grp
            else:
                beats = (row > grp) | ((row == grp) & (jrow + g * sub > jp))
            cnts[g] = cnts[g] + beats.astype(jnp.int32)
    cnt = cnts[0] if n_groups == 1 else jnp.concatenate(cnts, axis=0)
    return (cnt < n_keep) & (score_t > -jnp.inf)


def _bias_t(keep_t):
    return jnp.where(keep_t, 0.0, NEG_BIG).astype(F32)


def _bias_columns(bias_t):
    rows, q = bias_t.shape
    if rows < LANES:
        bias_t = jnp.concatenate([bias_t, jnp.zeros((LANES - rows, q), F32)], axis=0)
    return bias_t.T.astype(BF16)


def _softmax_rows(s_blocks, r0, *, tq, q0, k0, window=None):
    per_block = KEY_BLOCK // LANES
    n_tiles = len(s_blocks) * per_block
    rows = slice(r0, r0 + ROW_CHUNK)
    qlo = q0 + r0 % tq
    qhi = qlo + ROW_CHUNK - 1

    def tile(t):
        lane0 = (t % per_block) * LANES
        return s_blocks[t // per_block][rows, lane0:lane0 + LANES]

    kinds = []
    for t in range(n_tiles):
        klo = k0 + t * LANES
        khi = klo + LANES - 1
        none = klo > qhi or (window is not None and khi <= qlo - window)
        full = khi <= qlo and (window is None or klo > qhi - window)
        kinds.append("none" if none else "full" if full else "part")
    mx = None
    masked = {}
    for t, kind in enumerate(kinds):
        if kind == "none":
            continue
        x = tile(t)
        if kind == "part":
            qpos = qlo + lax.broadcasted_iota(jnp.int32, x.shape, 0)
            kpos = k0 + t * LANES + lax.broadcasted_iota(jnp.int32, x.shape, 1)
            ok = kpos <= qpos
            if window is not None:
                ok = ok & (kpos > qpos - window)
            x = jnp.where(ok, x, NEG_BIG)
            masked[t] = x
        mx = x if mx is None else jnp.maximum(mx, x)
    m = jnp.broadcast_to(jnp.max(mx, axis=-1, keepdims=True), mx.shape)
    p_tiles = []
    for t, kind in enumerate(kinds):
        if kind == "none":
            p_tiles.append(jnp.zeros((ROW_CHUNK, LANES), BF16))
            continue
        x = masked[t] if kind == "part" else tile(t)
        p_tiles.append(jnp.exp2(x - m).astype(BF16))
    return jnp.concatenate(p_tiles, axis=1)


def _pv_normalized(p, v_ones):
    o = jnp.dot(p, v_ones, preferred_element_type=F32)
    return o[:, :HEAD_DIM] / o[:, HEAD_DIM:]


class _AttnPass(NamedTuple):
    q: Callable[[], jax.Array]
    k_ref: Any
    v_ref: Any
    k0: int
    width: int
    n_rows: int
    tq: int
    q0: int
    window: Optional[int]
    pv_rows: int
    store: Callable[[int, jax.Array], None]


def _run_attention(passes):
    def score_thunks(ps):
        q = ps.q()
        return [lambda j=j: _nt_dot(
            q, ps.k_ref[ps.k0 + j * KEY_BLOCK:ps.k0 + (j + 1) * KEY_BLOCK, :])
                for j in range(ps.width // KEY_BLOCK)]

    s_blocks = [thunk() for thunk in score_thunks(passes[0])]
    pending_pv = None
    for t, ps in enumerate(passes):
        mxu_work = [] if pending_pv is None else [pending_pv]
        n_pv = len(mxu_work)
        if t + 1 < len(passes):
            mxu_work += score_thunks(passes[t + 1])
        row_starts = list(range(0, ps.n_rows, ROW_CHUNK))
        results, p_rows = [], []
        for k in range(max(len(mxu_work), len(row_starts))):
            if k < len(mxu_work):
                results.append(mxu_work[k]())
            if k < len(row_starts):
                p_rows.append(_softmax_rows(s_blocks, row_starts[k], tq=ps.tq, q0=ps.q0,
                                            k0=ps.k0, window=ps.window))
        s_blocks = results[n_pv:]

        def pending_pv(ps=ps, p_rows=p_rows):
            per = ps.pv_rows // ROW_CHUNK
            for k in range(ps.n_rows // ps.pv_rows):
                p = jnp.concatenate(p_rows[k * per:(k + 1) * per], axis=0)
                ps.store(k * ps.pv_rows, _pv_normalized(p, ps.v_ref[ps.k0:ps.k0 + ps.width, :]))
    pending_pv()


def _moba_kernel(q_ref, k_ref, v_ref, onehot_ref, o_ref, kaug_ref, qaug_ref, kmean_ref,
                 vaug_ref, *, n_blk, k_top):
    n_heads, S, _ = k_ref.shape
    tq = MOBA_TQ

    def prepare(h):
        kaug_ref[h, :, :HEAD_DIM] = k_ref[h]
        kaug_ref[h, :, HEAD_DIM:] = onehot_ref[...]
        vaug_ref[h, :, :HEAD_DIM] = v_ref[h]
        vaug_ref[h, :, HEAD_DIM:] = jnp.ones((S, HEAD_DIM), BF16)
        kmean_ref[h] = jnp.zeros(kmean_ref.shape[1:], F32)
        for j in range(n_blk):
            kb = k_ref[h, j * MOBA_BLOCK:(j + 1) * MOBA_BLOCK, :].astype(F32)
            kmean_ref[h, j:j + 1, :] = jnp.sum(kb, axis=0, keepdims=True) * (1.0 / MOBA_BLOCK)

        q = q_ref[h]
        gate_t = _nt_dot(kmean_ref[h].astype(BF16), q)
        jrow = lax.broadcasted_iota(jnp.int32, gate_t.shape, 0)
        own = lax.broadcasted_iota(jnp.int32, gate_t.shape, 1) // MOBA_BLOCK
        gate_t = jnp.where((jrow < own) & jnp.isfinite(gate_t), gate_t, -jnp.inf)
        bias_t = _bias_t(_select_bias_t(gate_t, n_blk, k_top) | (jrow == own))
        qaug_ref[h, :, :HEAD_DIM] = q
        for c in range(S // tq):
            qaug_ref[h, c * tq:(c + 1) * tq, HEAD_DIM:] = _bias_columns(
                bias_t[:, c * tq:(c + 1) * tq])

    def tile_pass(h, c):
        q_rows = slice(c * tq, (c + 1) * tq)

        def store(row0, o):
            o_ref[h, c * tq + row0:c * tq + row0 + o.shape[0], :] = o.astype(BF16)

        return _AttnPass(q=lambda: qaug_ref[h, q_rows, :], k_ref=kaug_ref.at[h],
                         v_ref=vaug_ref.at[h], k0=0, width=(c + 1) * tq, n_rows=tq, tq=tq,
                         q0=c * tq, window=None, pv_rows=MOBA_PV_ROWS, store=store)

    for h in range(n_heads):
        prepare(h)
    _run_attention([tile_pass(h, c) for c in range(S // tq) for h in range(n_heads)])


def _moba(proj, onehot, *, B, S):
    T = B * S
    n_blk = S // MOBA_BLOCK
    k_top = min(MOBA_TOPK, n_blk - 1)
    nb8 = -(-n_blk // 8) * 8
    hs = MOBA_HEADS_PER_STEP
    kern = functools.partial(_moba_kernel, n_blk=n_blk, k_top=k_top)
    head_spec = lambda blk: pl.BlockSpec((hs, S, LANES), lambda b, h: (blk // hs + h, b, 0))
    return pl.pallas_call(
        kern,
        grid=(B, MOBA_HEADS // hs),
        in_specs=[head_spec(BLK_MQ), head_spec(BLK_MK), head_spec(BLK_MV),
                  pl.BlockSpec((S, LANES), lambda b, h: (0, 0))],
        out_specs=head_spec(0),
        out_shape=jax.ShapeDtypeStruct((MOBA_HEADS, T, LANES), BF16),
        scratch_shapes=[pltpu.VMEM((hs, S, 2 * LANES), BF16), pltpu.VMEM((hs, S, 2 * LANES), BF16),
                        pltpu.VMEM((hs, nb8, HEAD_DIM), F32),
                        pltpu.VMEM((hs, S, 2 * LANES), BF16)],
        compiler_params=pltpu.CompilerParams(
            dimension_semantics=("parallel", "parallel"), vmem_limit_bytes=VMEM_LIMIT),
        name="moba_attn",
    )(proj, proj, proj, onehot)


def _nsa_select_kernel(q_ref, cos_ref, sin_ref, kc_ref, vc_ref, ovt_ref,
                       qrot_ref, bias_ref, oc_ref, *, n_cmp, n_sel_blk, n_top):
    i = pl.program_id(2)
    R, tq, _ = q_ref.shape
    q_raw = q_ref[...]
    cos = (cos_ref[...] * QK_PRESCALE)[None]
    sin = (sin_ref[...] * QK_PRESCALE)[None]
    qrot_ref[...] = _rope(q_raw.astype(F32), cos, sin).astype(BF16)

    n_seg = kc_ref.shape[0]
    s_c = (_nt_dot(q_raw.reshape(R * tq, HEAD_DIM), kc_ref[...]) * SCALE).reshape(R, tq, n_seg)
    n_idx = lax.broadcasted_iota(jnp.int32, s_c.shape, 2)
    pos3 = i * tq + lax.broadcasted_iota(jnp.int32, s_c.shape, 1)
    m_c = (n_idx * CMP_STRIDE + CMP_BLOCK - 1 <= pos3) & (n_idx < n_cmp)
    s_c = jnp.where(m_c, s_c, -jnp.inf)
    mx = jnp.max(s_c, axis=-1, keepdims=True)
    mx = jnp.where(jnp.isfinite(mx), mx, 0.0)
    e_c = jnp.where(m_c, jnp.exp(s_c - mx), 0.0)
    p_c = e_c / jnp.maximum(jnp.sum(e_c, axis=-1, keepdims=True), 1e-30)
    o_c = jnp.dot(p_c.reshape(R * tq, n_seg).astype(BF16), vc_ref[...],
                  preferred_element_type=F32).reshape(R, tq, HEAD_DIM)
    oc_ref[...] = o_c.astype(BF16)

    p_sum = jnp.sum(p_c, axis=0)
    p_hi = p_sum.astype(BF16)
    p_lo = (p_sum - p_hi.astype(F32)).astype(BF16)
    ovt = ovt_ref[...]
    imp_t = _nt_dot(ovt, p_hi) + _nt_dot(ovt, p_lo)
    jrow = lax.broadcasted_iota(jnp.int32, imp_t.shape, 0)
    posq = i * tq + lax.broadcasted_iota(jnp.int32, imp_t.shape, 1)
    own = posq // SEL_BLOCK
    forced = (jrow == 0) | (jrow == own) | (jrow == own - 1)
    future = jrow * SEL_BLOCK > posq
    score_t = jnp.where(future, -jnp.inf, jnp.where(forced, jnp.inf, imp_t))
    keep_t = _select_bias_t(score_t, n_sel_blk, n_top)
    bias_ref[...] = _bias_columns(_bias_t(keep_t))


def _nsa_select(proj, cos, sin, kvc, ovt, *, B, S, tq):
    T = B * S
    nq = S // tq
    n_seg = S // CMP_STRIDE
    n_cmp = n_seg - CMP_BLOCK // CMP_STRIDE + 1
    n_sel_blk = S // SEL_BLOCK
    R = NSA_REP
    kern = functools.partial(_nsa_select_kernel, n_cmp=n_cmp, n_sel_blk=n_sel_blk,
                             n_top=min(SEL_TOPK, n_sel_blk))
    heads_spec = pl.BlockSpec((R, tq, LANES), lambda b, g, i: (g, b * nq + i, 0))
    return pl.pallas_call(
        kern,
        grid=(B, NSA_GROUPS, nq),
        in_specs=[pl.BlockSpec((R, tq, LANES), lambda b, g, i: (BLK_NQ // R + g, b * nq + i, 0)),
                  pl.BlockSpec((tq, LANES), lambda b, g, i: (b * nq + i, 0)),
                  pl.BlockSpec((tq, LANES), lambda b, g, i: (b * nq + i, 0)),
                  pl.BlockSpec((None, n_seg, LANES), lambda b, g, i: (0, g * B + b, 0)),
                  pl.BlockSpec((None, n_seg, LANES), lambda b, g, i: (1, g * B + b, 0)),
                  pl.BlockSpec(ovt.shape, lambda b, g, i: (0, 0))],
        out_specs=[heads_spec,
                   pl.BlockSpec((None, tq, LANES), lambda b, g, i: (g, b * nq + i, 0)),
                   heads_spec],
        out_shape=[jax.ShapeDtypeStruct((NSA_HEADS, T, LANES), BF16),
                   jax.ShapeDtypeStruct((NSA_GROUPS, T, LANES), BF16),
                   jax.ShapeDtypeStruct((NSA_HEADS, T, LANES), BF16)],
        compiler_params=pltpu.CompilerParams(
            dimension_semantics=("parallel", "parallel", "parallel"),
            vmem_limit_bytes=VMEM_LIMIT),
        name="nsa_select",
    )(proj, cos, sin, kvc, kvc, ovt)


def _nsa_sel_kernel(qrot_ref, bias_ref, ks_ref, vs_ref, onehot_ref, o_ref,
                    ksaug_ref, vaug_ref, *, tq):
    R, S, _ = qrot_ref.shape
    ksaug_ref[:, :HEAD_DIM] = ks_ref[...]
    ksaug_ref[:, HEAD_DIM:] = onehot_ref[...]
    vaug_ref[:, :HEAD_DIM] = vs_ref[...]
    vaug_ref[:, HEAD_DIM:] = jnp.ones((S, HEAD_DIM), BF16)

    def head_pass(c, heads):
        q_rows = slice(c * tq, (c + 1) * tq)

        def q_aug():
            bias = bias_ref[q_rows, :]
            return jnp.concatenate(
                [jnp.concatenate([qrot_ref[r, q_rows, :], bias], axis=1) for r in heads], axis=0)

        def store(row0, o):
            for k, r in enumerate(heads):
                o_ref[r, q_rows, :] = o[k * tq:(k + 1) * tq, :].astype(BF16)

        return _AttnPass(q=q_aug, k_ref=ksaug_ref, v_ref=vaug_ref, k0=0, width=(c + 1) * tq,
                         n_rows=len(heads) * tq, tq=tq, q0=c * tq, window=None,
                         pv_rows=len(heads) * tq, store=store)

    _run_attention([head_pass(c, range(r0, r0 + HEADS_PER_PASS))
                    for c in range(S // tq) for r0 in range(0, R, HEADS_PER_PASS)])


def _nsa_sel(q_rot, bias, proj, onehot, *, B, S, tq):
    T = B * S
    R = NSA_REP
    heads_spec = pl.BlockSpec((R, S, LANES), lambda b, g: (g, b, 0))
    kv_spec = lambda blk: pl.BlockSpec((None, S, LANES), lambda b, g: (blk + g, b, 0))
    return pl.pallas_call(
        functools.partial(_nsa_sel_kernel, tq=tq),
        grid=(B, NSA_GROUPS),
        in_specs=[heads_spec,
                  pl.BlockSpec((None, S, LANES), lambda b, g: (g, b, 0)),
                  kv_spec(BLK_NKS), kv_spec(BLK_NVS),
                  pl.BlockSpec((S, LANES), lambda b, g: (0, 0))],
        out_specs=heads_spec,
        out_shape=jax.ShapeDtypeStruct((NSA_HEADS, T, LANES), BF16),
        scratch_shapes=[pltpu.VMEM((S, 2 * LANES), BF16), pltpu.VMEM((S, 2 * LANES), BF16)],
        compiler_params=pltpu.CompilerParams(
            dimension_semantics=("parallel", "parallel"), vmem_limit_bytes=VMEM_LIMIT_BIG),
        name="nsa_sel_attn",
    )(q_rot, bias, proj, proj, onehot)


def _nsa_win_kernel(qrot_ref, kw_ref, vw_ref, oc_ref, os_ref, gate_ref, o_ref,
                    vaug_ref, *, tq):
    R, S, _ = qrot_ref.shape
    vaug_ref[:, :HEAD_DIM] = vw_ref[...]
    vaug_ref[:, HEAD_DIM:] = jnp.ones((S, HEAD_DIM), BF16)

    def head_pass(c, heads):
        q_rows = slice(c * tq, (c + 1) * tq)
        k0 = max(0, c * tq - WINDOW)

        def store(row0, o_w):
            gt = _sigmoid(gate_ref[q_rows, :])
            for k, r in enumerate(heads):
                o = (gt[:, 3 * r:3 * r + 1] * oc_ref[r, q_rows, :].astype(F32)
                     + gt[:, 3 * r + 1:3 * r + 2] * os_ref[r, q_rows, :].astype(F32)
                     + gt[:, 3 * r + 2:3 * r + 3] * o_w[k * tq:(k + 1) * tq, :])
                o_ref[r, q_rows, :] = o.astype(BF16)

        return _AttnPass(
            q=lambda: jnp.concatenate([qrot_ref[r, q_rows, :] for r in heads], axis=0),
            k_ref=kw_ref, v_ref=vaug_ref, k0=k0, width=(c + 1) * tq - k0,
            n_rows=len(heads) * tq, tq=tq, q0=c * tq, window=WINDOW,
            pv_rows=len(heads) * tq, store=store)

    _run_attention([head_pass(c, range(r0, r0 + HEADS_PER_PASS))
                    for c in range(S // tq) for r0 in range(0, R, HEADS_PER_PASS)])


def _nsa_win(q_rot, proj, o_c, o_s, gates, *, B, S, tq):
    T = B * S
    R = NSA_REP
    heads_spec = pl.BlockSpec((R, S, LANES), lambda b, g: (g, b, 0))
    kv_spec = lambda blk: pl.BlockSpec((None, S, LANES), lambda b, g: (blk + g, b, 0))
    return pl.pallas_call(
        functools.partial(_nsa_win_kernel, tq=tq),
        grid=(B, NSA_GROUPS),
        in_specs=[heads_spec, kv_spec(BLK_NKW), kv_spec(BLK_NVW), heads_spec, heads_spec,
                  pl.BlockSpec((None, S, LANES), lambda b, g: (g, b, 0))],
        out_specs=heads_spec,
        out_shape=jax.ShapeDtypeStruct((NSA_HEADS, T, LANES), BF16),
        scratch_shapes=[pltpu.VMEM((S, 2 * LANES), BF16)],
        compiler_params=pltpu.CompilerParams(
            dimension_semantics=("parallel", "parallel"), vmem_limit_bytes=VMEM_LIMIT_BIG),
        name="nsa_win_attn",
    )(q_rot, proj, proj, o_c, o_s, gates)


def _out_proj_kernel(om_ref, on_ref, zm_ref, zn_ref, gm_ref, gn_ref, x_ref, w_ref, gf_ref,
                     out_ref):
    def gated_norm(o_ref, z_ref, g_ref, rows):
        n_heads = o_ref.shape[0]
        acts = []
        ss = None
        for h in range(n_heads):
            z = z_ref[h, rows, :].astype(F32)
            a = o_ref[h, rows, :].astype(F32) * (z * _sigmoid(z))
            acts.append(a)
            ss = a * a if ss is None else ss + a * a
        ms = jnp.sum(ss, axis=-1, keepdims=True) * (1.0 / (n_heads * LANES))
        inv = lax.rsqrt(ms + EPS)
        return [(acts[h] * inv * g_ref[h]).astype(BF16) for h in range(n_heads)]

    for r0 in range(0, x_ref.shape[0], OUT_SUB_ROWS):
        rows = slice(r0, r0 + OUT_SUB_ROWS)
        y = jnp.concatenate(gated_norm(om_ref, zm_ref, gm_ref, rows)
                            + gated_norm(on_ref, zn_ref, gn_ref, rows), axis=1)
        r = x_ref[rows, :] + jnp.dot(y, w_ref[...], preferred_element_type=F32)
        ms = jnp.mean(r * r, axis=-1, keepdims=True)
        out_ref[rows, :] = r * lax.rsqrt(ms + EPS) * gf_ref[...]


def _out_proj(o_moba, o_nsa, proj, g_moba, g_nsa, x2d, w_out, g_final, *, tm):
    T = x2d.shape[0]
    H = MOBA_HEADS
    return pl.pallas_call(
        _out_proj_kernel,
        grid=(T // tm,),
        in_specs=[pl.BlockSpec((H, tm, LANES), lambda i: (0, i, 0)),
                  pl.BlockSpec((H, tm, LANES), lambda i: (0, i, 0)),
                  pl.BlockSpec((H, tm, LANES), lambda i: (BLK_MZ // H, i, 0)),
                  pl.BlockSpec((H, tm, LANES), lambda i: (BLK_NZ // H, i, 0)),
                  pl.BlockSpec((H, 1, LANES), lambda i: (0, 0, 0)),
                  pl.BlockSpec((H, 1, LANES), lambda i: (0, 0, 0)),
                  pl.BlockSpec((tm, D_MODEL), lambda i: (i, 0)),
                  pl.BlockSpec((D_MODEL, D_MODEL), lambda i: (0, 0), pipeline_mode=pl.Buffered(1)),
                  pl.BlockSpec((1, D_MODEL), lambda i: (0, 0))],
        out_specs=pl.BlockSpec((tm, D_MODEL), lambda i: (i, 0)),
        out_shape=jax.ShapeDtypeStruct((T, D_MODEL), F32),
        compiler_params=pltpu.CompilerParams(
            dimension_semantics=("parallel",), vmem_limit_bytes=VMEM_LIMIT_BIG),
        name="out_proj",
    )(o_moba, o_nsa, proj, proj, g_moba, g_nsa, x2d, w_out, g_final)


def _w_in_offsets():
    mw, nw, kw = MOBA_HEADS * HEAD_DIM, NSA_HEADS * HEAD_DIM, NSA_GROUPS * HEAD_DIM
    sizes = [mw] * 4 + [nw] + [kw] * 6 + [3 * NSA_HEADS, nw]
    names = ["mq", "mk", "mv", "mz", "nq", "nkc", "nvc", "nks", "nvs", "nkw", "nvw", "ng", "nz"]
    offs = np.concatenate([[0], np.cumsum(sizes)])
    return {n: (int(offs[k]), int(sizes[k])) for k, n in enumerate(names)}


def _weight_tiles_kernel(w_ref, out_ref, *, src_cols, blocks_per_tile):
    for b, src in enumerate(src_cols):
        lane0 = (b % blocks_per_tile) * LANES
        out_ref[b // blocks_per_tile, :, lane0:lane0 + LANES] = (
            w_ref[:, src:src + LANES].astype(BF16))


def _weight_tiles(w_in, blocks_per_tile):
    offsets = _w_in_offsets()
    src_cols = []
    for name, n_blocks in COLUMN_ORDER:
        off, size = offsets[name]
        assert size == n_blocks * LANES
        src_cols += [off + k * LANES for k in range(n_blocks)]
    n_tiles = N_BLOCKS // blocks_per_tile
    rows = 256
    kern = functools.partial(_weight_tiles_kernel, src_cols=tuple(src_cols),
                             blocks_per_tile=blocks_per_tile)
    return pl.pallas_call(
        kern,
        grid=(D_MODEL // rows,),
        in_specs=[pl.BlockSpec((None, rows, w_in.shape[2]), lambda r: (0, r, 0))],
        out_specs=pl.BlockSpec((n_tiles, rows, blocks_per_tile * LANES), lambda r: (0, r, 0)),
        out_shape=jax.ShapeDtypeStruct((n_tiles, D_MODEL, blocks_per_tile * LANES), BF16),
        compiler_params=pltpu.CompilerParams(
            dimension_semantics=("parallel",), vmem_limit_bytes=VMEM_LIMIT),
        name="weight_tiles",
    )(w_in)


def _gate_weight(w_in):
    off, size = _w_in_offsets()["ng"]
    per_group = 3 * NSA_REP
    wg = w_in[0, :, off:off + size].reshape(D_MODEL, NSA_GROUPS, per_group)
    wg = jnp.pad(wg, ((0, 0), (0, 0), (0, LANES - per_group)))
    return wg.reshape(D_MODEL, NSA_GROUPS * LANES).astype(BF16)


def _block_onehot(S, block):
    ids = np.arange(S)[:, None] // block
    return jnp.asarray((ids == np.arange(LANES)[None, :]).astype(np.float32), dtype=BF16)


def _overlap_t(n_seg, n_cmp, n_sel_blk):
    cs = np.arange(n_seg)[None, :] * CMP_STRIDE
    ss = np.arange(n_sel_blk)[:, None] * SEL_BLOCK
    ov = (cs < ss + SEL_BLOCK) & (cs + CMP_BLOCK > ss) & (np.arange(n_seg)[None, :] < n_cmp)
    return jnp.asarray(ov.astype(np.float32), dtype=BF16)


def _layer(x, positions, w_in, g_norm, pe_ck, pe_cv, w_ck1, w_ck2, w_cv1, w_cv2,
           g_out_moba, g_out_nsa, w_out, g_final, *, nsa_tq, tm_in, tm_out, blocks_per_tile):
    B, S, _ = x.shape
    T = B * S
    x2d = x.reshape(T, D_MODEL)
    w_tiles = _weight_tiles(w_in.astype(BF16), blocks_per_tile)
    w_gate = _gate_weight(w_in)
    proj, gates, cos, sin, seg = _in_proj(x2d, g_norm.reshape(1, D_MODEL), w_tiles, w_gate,
                                          positions, tm=tm_in, blocks_per_tile=blocks_per_tile)

    n_seg = S // CMP_STRIDE
    seg = seg.reshape(2, NSA_GROUPS * B * n_seg, CMP_STRIDE * HEAD_DIM)
    pe = jnp.stack([pe_ck.reshape(1, -1), pe_cv.reshape(1, -1)])
    w1 = jnp.stack([w_ck1, w_cv1]).astype(BF16)
    w2 = jnp.stack([w_ck2, w_cv2]).astype(BF16)
    kvc = _compress(seg, pe, w1, w2)

    o_moba = _moba(proj, _block_onehot(S, MOBA_BLOCK), B=B, S=S)
    n_cmp = n_seg - CMP_BLOCK // CMP_STRIDE + 1
    q_rot, bias, o_c = _nsa_select(proj, cos, sin, kvc, _overlap_t(n_seg, n_cmp, S // SEL_BLOCK),
                                   B=B, S=S, tq=4 * nsa_tq)
    o_s = _nsa_sel(q_rot, bias, proj, _block_onehot(S, SEL_BLOCK), B=B, S=S, tq=nsa_tq)
    o_nsa = _nsa_win(q_rot, proj, o_c, o_s, gates, B=B, S=S, tq=nsa_tq)
    out = _out_proj(o_moba, o_nsa, proj,
                    g_out_moba.reshape(MOBA_HEADS, 1, LANES), g_out_nsa.reshape(NSA_HEADS, 1, LANES),
                    x2d, w_out.astype(BF16), g_final.reshape(1, D_MODEL), tm=tm_out)
    return out.reshape(B, S, D_MODEL)


def kernel(x, positions, w_in, g_norm, pe_ck, pe_cv, w_ck1, w_ck2, w_cv1, w_cv2,
           g_out_moba, g_out_nsa, w_out, g_final):
    assert w_in.shape[0] == 1, "single-layer problem"
    return _layer(x, positions, w_in, g_norm[0], pe_ck[0], pe_cv[0], w_ck1[0], w_ck2[0],
                  w_cv1[0], w_cv2[0], g_out_moba[0], g_out_nsa[0], w_out[0], g_final,
                  nsa_tq=256, tm_in=1024, tm_out=512, blocks_per_tile=12)
```

```python
import functools
from typing import Any, Callable, NamedTuple, Optional

import numpy as np
import jax
import jax.numpy as jnp
from jax import lax
from jax.experimental import pallas as pl
from jax.experimental.pallas import tpu as pltpu

F32 = jnp.float32
BF16 = jnp.bfloat16

D_MODEL = 2048
HEAD_DIM = 128
MOBA_HEADS = 8
NSA_HEADS = 8
NSA_GROUPS = 2
NSA_REP = 4
MOBA_BLOCK = 256
MOBA_TOPK = 3
CMP_BLOCK = 32
CMP_STRIDE = 16
CMP_HIDDEN = 256
SEL_BLOCK = 64
SEL_TOPK = 8
WINDOW = 512
ROPE_THETA = 10000.0
EPS = 1e-6
SCALE = HEAD_DIM ** -0.5
QK_PRESCALE = SCALE * float(np.log2(np.e))
NEG_BIG = -(2.0 ** 100)

LANES = 128
VMEM_LIMIT = 48 * 1024 * 1024
VMEM_LIMIT_BIG = 58 * 1024 * 1024
COLUMN_ORDER = (("mq", 8), ("mk", 8), ("nks", 2), ("nkw", 2), ("nkc", 2), ("nvc", 2),
                ("mv", 8), ("mz", 8), ("nq", 8), ("nz", 8), ("nvs", 2), ("nvw", 2))
BLK_MQ, BLK_MK, BLK_NKS, BLK_NKW, BLK_NKC, BLK_NVC = 0, 8, 16, 18, 20, 22
BLK_MV, BLK_MZ, BLK_NQ, BLK_NZ, BLK_NVS, BLK_NVW = 24, 32, 40, 48, 56, 58
N_BLOCKS = 60
N_ROPE_BLOCKS = 20
N_SEG_BLOCKS = 4

ROW_CHUNK = 64
MOBA_PV_ROWS = 256
HEADS_PER_PASS = 2
NORM_ROWS = 16
IN_SUB_ROWS = 256
MXU_COLS = 256
OUT_SUB_ROWS = 256
MOBA_TQ = 256
MOBA_HEADS_PER_STEP = 4
KEY_BLOCK = 256


def _nt_dot(a, b):
    return lax.dot_general(a, b, (((1,), (1,)), ((), ())), preferred_element_type=F32)


def _sigmoid(x):
    return 1.0 / (1.0 + jnp.exp(-x))


def _rope(a, cos, sin_signed):
    return a * cos + pltpu.roll(a, HEAD_DIM // 2, axis=a.ndim - 1) * sin_signed


def _interleave(*streams):
    n = max(len(s) for s in streams)
    for k in range(n):
        for s in streams:
            for thunk in s[k * len(s) // n:(k + 1) * len(s) // n]:
                thunk()


def _block_kind(b):
    if BLK_NKC <= b < BLK_NKC + N_SEG_BLOCKS:
        return "seg"
    return "rope_q" if b < BLK_MK else "rope" if b < N_ROPE_BLOCKS else "plain"


def _project_columns(h, w_ref, col0, kinds, cos, sin, store, store_segments=None):
    n = len(kinds)
    acc = jnp.dot(h, w_ref[:, col0:col0 + n * LANES], preferred_element_type=F32)
    n_seg = 0
    for c, kind in enumerate(kinds):
        a = acc[:, c * LANES:(c + 1) * LANES]
        if kind == "rope_q":
            a = _rope(a, cos * QK_PRESCALE, sin * QK_PRESCALE)
        elif kind == "rope":
            a = _rope(a, cos, sin)
        elif kind == "seg":
            store_segments(n_seg, a)
            n_seg += 1
        store(c, a.astype(BF16))


def _in_proj_kernel(x_ref, g_ref, w_ref, wg_ref, pos_ref, invf_ref, sign_ref,
                    out_ref, gate_ref, cos_ref, sin_ref, seg_ref, h_scr, seg_scr,
                    *, tile_patterns):
    j = pl.program_id(1)
    tm = x_ref.shape[0]
    first_pattern = tile_patterns[0][0]
    assert tile_patterns[0][1] == [0]

    @pl.when(j == 0)
    def _():
        ang = pos_ref[...].astype(F32) * invf_ref[...]
        cos_ref[...] = jnp.cos(ang)
        sin_ref[...] = jnp.sin(ang) * sign_ref[...]

        def norm_piece(r0):
            rows = slice(r0, r0 + NORM_ROWS)
            x = x_ref[rows, :]
            ms = jnp.mean(x * x, axis=-1, keepdims=True)
            h_scr[rows, :] = (x * lax.rsqrt(ms + EPS) * g_ref[...]).astype(BF16)

        def norm_pieces(m):
            return [functools.partial(norm_piece, r0)
                    for r0 in range(m * IN_SUB_ROWS, (m + 1) * IN_SUB_ROWS, NORM_ROWS)]

        def matmul_pieces(m):
            rows = slice(m * IN_SUB_ROWS, (m + 1) * IN_SUB_ROWS)

            def gates():
                acc = jnp.dot(h_scr[rows, :], wg_ref[...], preferred_element_type=F32)
                for g in range(NSA_GROUPS):
                    gate_ref[g, rows, :] = acc[:, g * LANES:(g + 1) * LANES]

            def columns(c0):
                def store(c, val):
                    out_ref[c0 + c, rows, :] = val
                per = MXU_COLS // LANES
                _project_columns(h_scr[rows, :], w_ref, c0 * LANES, first_pattern[c0:c0 + per],
                                 cos_ref[rows, :], sin_ref[rows, :], store)

            return [gates] + [functools.partial(columns, c0)
                              for c0 in range(0, len(first_pattern), MXU_COLS // LANES)]

        n_sub = tm // IN_SUB_ROWS
        _interleave(norm_pieces(0))
        for m in range(n_sub):
            _interleave(matmul_pieces(m), norm_pieces(m + 1) if m + 1 < n_sub else [])

    for pattern, tiles in tile_patterns[1:]:
        @pl.when(functools.reduce(jnp.logical_or, [j == t for t in tiles]))
        def _(pattern=pattern):
            def store(c, val):
                out_ref[c] = val

            def store_segments(k, a):
                seg_scr[...] = a
                for t in range(CMP_STRIDE):
                    piece = seg_scr[pl.ds(t, tm // CMP_STRIDE, stride=CMP_STRIDE), :]
                    seg_ref[k, :, t * HEAD_DIM:(t + 1) * HEAD_DIM] = piece.astype(BF16)

            _project_columns(h_scr[...], w_ref, 0, pattern, cos_ref[...], sin_ref[...], store,
                             store_segments)


def _in_proj(x2d, g_norm, w_tiles, w_gate, positions, *, tm, blocks_per_tile):
    T = x2d.shape[0]
    tn = blocks_per_tile * LANES
    n_tiles = N_BLOCKS // blocks_per_tile
    by_pattern = {}
    for t in range(n_tiles):
        pattern = tuple(_block_kind(t * blocks_per_tile + c) for c in range(blocks_per_tile))
        by_pattern.setdefault(pattern, []).append(t)
    half = HEAD_DIM // 2
    inv_freq = 1.0 / (ROPE_THETA ** (jnp.arange(0, HEAD_DIM, 2, dtype=F32) / HEAD_DIM))
    invf = jnp.concatenate([inv_freq, inv_freq]).reshape(1, HEAD_DIM)
    sign = jnp.concatenate([-jnp.ones((half,), F32), jnp.ones((half,), F32)]).reshape(1, HEAD_DIM)
    kern = functools.partial(_in_proj_kernel, tile_patterns=tuple(by_pattern.items()))
    row_table = pl.BlockSpec((tm, HEAD_DIM), lambda i, j: (i, 0))
    return pl.pallas_call(
        kern,
        grid=(T // tm, n_tiles),
        in_specs=[pl.BlockSpec((tm, D_MODEL), lambda i, j: (i, 0)),
                  pl.BlockSpec((1, D_MODEL), lambda i, j: (0, 0)),
                  pl.BlockSpec((None, D_MODEL, tn), lambda i, j: (j, 0, 0)),
                  pl.BlockSpec((D_MODEL, NSA_GROUPS * LANES), lambda i, j: (0, 0)),
                  pl.BlockSpec((tm, 1), lambda i, j: (i, 0)),
                  pl.BlockSpec((1, HEAD_DIM), lambda i, j: (0, 0)),
                  pl.BlockSpec((1, HEAD_DIM), lambda i, j: (0, 0))],
        out_specs=[pl.BlockSpec((blocks_per_tile, tm, LANES), lambda i, j: (j, i, 0)),
                   pl.BlockSpec((NSA_GROUPS, tm, LANES), lambda i, j: (0, i, 0)),
                   row_table, row_table,
                   pl.BlockSpec((N_SEG_BLOCKS, tm // CMP_STRIDE, CMP_STRIDE * HEAD_DIM),
                                lambda i, j: (0, i, 0))],
        out_shape=[jax.ShapeDtypeStruct((N_BLOCKS, T, LANES), BF16),
                   jax.ShapeDtypeStruct((NSA_GROUPS, T, LANES), F32),
                   jax.ShapeDtypeStruct((T, HEAD_DIM), F32),
                   jax.ShapeDtypeStruct((T, HEAD_DIM), F32),
                   jax.ShapeDtypeStruct((N_SEG_BLOCKS, T // CMP_STRIDE, CMP_STRIDE * HEAD_DIM),
                                        BF16)],
        scratch_shapes=[pltpu.VMEM((tm, D_MODEL), BF16), pltpu.VMEM((tm, HEAD_DIM), F32)],
        compiler_params=pltpu.CompilerParams(
            dimension_semantics=("parallel", "arbitrary"), vmem_limit_bytes=VMEM_LIMIT_BIG),
        name="in_proj",
    )(x2d, g_norm, w_tiles, w_gate, positions.reshape(T, 1), invf, sign)


def _compress_kernel(seg_ref, pe_ref, w1_ref, w2_ref, out_ref):
    half = CMP_STRIDE * HEAD_DIM
    seg = seg_ref[0].astype(F32)
    pe = pe_ref[0]
    top = (seg + pe[:, :half]).astype(BF16)
    bot = (seg + pe[:, half:]).astype(BF16)
    a = jnp.dot(top, w1_ref[0, :half, :], preferred_element_type=F32)
    b = jnp.dot(bot, w1_ref[0, half:, :], preferred_element_type=F32)
    rows = a.shape[0]
    h = a + pltpu.roll(b, rows - 1, axis=0)
    hid = h * _sigmoid(h)
    out_ref[0] = jnp.dot(hid.astype(BF16), w2_ref[0], preferred_element_type=F32).astype(BF16)


def _compress(seg, pe, w1, w2):
    _, R, half = seg.shape
    return pl.pallas_call(
        _compress_kernel,
        grid=(2,),
        in_specs=[pl.BlockSpec((1, R, half), lambda c: (c, 0, 0)),
                  pl.BlockSpec((1, 1, 2 * half), lambda c: (c, 0, 0)),
                  pl.BlockSpec((1, 2 * half, CMP_HIDDEN), lambda c: (c, 0, 0)),
                  pl.BlockSpec((1, CMP_HIDDEN, HEAD_DIM), lambda c: (c, 0, 0))],
        out_specs=pl.BlockSpec((1, R, HEAD_DIM), lambda c: (c, 0, 0)),
        out_shape=jax.ShapeDtypeStruct((2, R, HEAD_DIM), BF16),
        compiler_params=pltpu.CompilerParams(
            dimension_semantics=("arbitrary",), vmem_limit_bytes=VMEM_LIMIT),
        name="compress",
    )(seg, pe, w1, w2)


def _select_bias_t(score_t, n_rows, n_keep):
    sub = 8
    n_groups = score_t.shape[0] // sub
    groups = [score_t[g * sub:(g + 1) * sub, :] for g in range(n_groups)]
    jrow = lax.broadcasted_iota(jnp.int32, groups[0].shape, 0)
    cnts = [jnp.zeros(groups[0].shape, jnp.int32) for _ in range(n_groups)]
    for jp in range(n_rows):
        row = score_t[jp:jp + 1, :]
        for g, grp in enumerate(groups):
            if g * sub > jp:
                beats = row >= grp
            elif g * sub + sub - 1 <= jp:
                beats = row > grp
            else:
                beats = (row > grp) | ((row == grp) & (jrow + g * sub > jp))
            cnts[g] = cnts[g] + beats.astype(jnp.int32)
    cnt = cnts[0] if n_groups == 1 else jnp.concatenate(cnts, axis=0)
    return (cnt < n_keep) & (score_t > -jnp.inf)


def _bias_t(keep_t):
    return jnp.where(keep_t, 0.0, NEG_BIG).astype(F32)


def _bias_columns(bias_t):
    rows, q = bias_t.shape
    if rows < LANES:
        bias_t = jnp.concatenate([bias_t, jnp.zeros((LANES - rows, q), F32)], axis=0)
    return bias_t.T.astype(BF16)


def _softmax_rows(s_blocks, r0, *, tq, q0, k0, window=None):
    per_block = KEY_BLOCK // LANES
    n_tiles = len(s_blocks) * per_block
    rows = slice(r0, r0 + ROW_CHUNK)
    qlo = q0 + r0 % tq
    qhi = qlo + ROW_CHUNK - 1

    def tile(t):
        lane0 = (t % per_block) * LANES
        return s_blocks[t // per_block][rows, lane0:lane0 + LANES]

    kinds = []
    for t in range(n_tiles):
        klo = k0 + t * LANES
        khi = klo + LANES - 1
        none = klo > qhi or (window is not None and khi <= qlo - window)
        full = khi <= qlo and (window is None or klo > qhi - window)
        kinds.append("none" if none else "full" if full else "part")
    mx = None
    masked = {}
    for t, kind in enumerate(kinds):
        if kind == "none":
            continue
        x = tile(t)
        if kind == "part":
            qpos = qlo + lax.broadcasted_iota(jnp.int32, x.shape, 0)
            kpos = k0 + t * LANES + lax.broadcasted_iota(jnp.int32, x.shape, 1)
            ok = kpos <= qpos
            if window is not None:
                ok = ok & (kpos > qpos - window)
            x = jnp.where(ok, x, NEG_BIG)
            masked[t] = x
        mx = x if mx is None else jnp.maximum(mx, x)
    m = jnp.broadcast_to(jnp.max(mx, axis=-1, keepdims=True), mx.shape)
    p_tiles = []
    for t, kind in enumerate(kinds):
        if kind == "none":
            p_tiles.append(jnp.zeros((ROW_CHUNK, LANES), BF16))
            continue
        x = masked[t] if kind == "part" else tile(t)
        p_tiles.append(jnp.exp2(x - m).astype(BF16))
    return jnp.concatenate(p_tiles, axis=1)


def _pv_normalized(p, v_ones):
    o = jnp.dot(p, v_ones, preferred_element_type=F32)
    return o[:, :HEAD_DIM] / o[:, HEAD_DIM:]


class _AttnPass(NamedTuple):
    q: Callable[[], jax.Array]
    k_ref: Any
    v_ref: Any
    k0: int
    width: int
    n_rows: int
    tq: int
    q0: int
    window: Optional[int]
    pv_rows: int
    store: Callable[[int, jax.Array], None]


def _run_attention(passes):
    def score_thunks(ps):
        q = ps.q()
        return [lambda j=j: _nt_dot(
            q, ps.k_ref[ps.k0 + j * KEY_BLOCK:ps.k0 + (j + 1) * KEY_BLOCK, :])
                for j in range(ps.width // KEY_BLOCK)]

    s_blocks = [thunk() for thunk in score_thunks(passes[0])]
    pending_pv = None
    for t, ps in enumerate(passes):
        mxu_work = [] if pending_pv is None else [pending_pv]
        n_pv = len(mxu_work)
        if t + 1 < len(passes):
            mxu_work += score_thunks(passes[t + 1])
        row_starts = list(range(0, ps.n_rows, ROW_CHUNK))
        results, p_rows = [], []
        for k in range(max(len(mxu_work), len(row_starts))):
            if k < len(mxu_work):
                results.append(mxu_work[k]())
            if k < len(row_starts):
                p_rows.append(_softmax_rows(s_blocks, row_starts[k], tq=ps.tq, q0=ps.q0,
                                            k0=ps.k0, window=ps.window))
        s_blocks = results[n_pv:]

        def pending_pv(ps=ps, p_rows=p_rows):
            per = ps.pv_rows // ROW_CHUNK
            for k in range(ps.n_rows // ps.pv_rows):
                p = jnp.concatenate(p_rows[k * per:(k + 1) * per], axis=0)
                ps.store(k * ps.pv_rows, _pv_normalized(p, ps.v_ref[ps.k0:ps.k0 + ps.width, :]))
    pending_pv()


def _moba_kernel(q_ref, k_ref, v_ref, onehot_ref, o_ref, kaug_ref, qaug_ref, kmean_ref,
                 vaug_ref, *, n_blk, k_top):
    n_heads, S, _ = k_ref.shape
    tq = MOBA_TQ

    def prepare(h):
        kaug_ref[h, :, :HEAD_DIM] = k_ref[h]
        kaug_ref[h, :, HEAD_DIM:] = onehot_ref[...]
        vaug_ref[h, :, :HEAD_DIM] = v_ref[h]
        vaug_ref[h, :, HEAD_DIM:] = jnp.ones((S, HEAD_DIM), BF16)
        kmean_ref[h] = jnp.zeros(kmean_ref.shape[1:], F32)
        for j in range(n_blk):
            kb = k_ref[h, j * MOBA_BLOCK:(j + 1) * MOBA_BLOCK, :].astype(F32)
            kmean_ref[h, j:j + 1, :] = jnp.sum(kb, axis=0, keepdims=True) * (1.0 / MOBA_BLOCK)

        q = q_ref[h]
        gate_t = _nt_dot(kmean_ref[h].astype(BF16), q)
        jrow = lax.broadcasted_iota(jnp.int32, gate_t.shape, 0)
        own = lax.broadcasted_iota(jnp.int32, gate_t.shape, 1) // MOBA_BLOCK
        gate_t = jnp.where((jrow < own) & jnp.isfinite(gate_t), gate_t, -jnp.inf)
        bias_t = _bias_t(_select_bias_t(gate_t, n_blk, k_top) | (jrow == own))
        qaug_ref[h, :, :HEAD_DIM] = q
        for c in range(S // tq):
            qaug_ref[h, c * tq:(c + 1) * tq, HEAD_DIM:] = _bias_columns(
                bias_t[:, c * tq:(c + 1) * tq])

    def tile_pass(h, c):
        q_rows = slice(c * tq, (c + 1) * tq)

        def store(row0, o):
            o_ref[h, c * tq + row0:c * tq + row0 + o.shape[0], :] = o.astype(BF16)

        return _AttnPass(q=lambda: qaug_ref[h, q_rows, :], k_ref=kaug_ref.at[h],
                         v_ref=vaug_ref.at[h], k0=0, width=(c + 1) * tq, n_rows=tq, tq=tq,
                         q0=c * tq, window=None, pv_rows=MOBA_PV_ROWS, store=store)

    for h in range(n_heads):
        prepare(h)
    _run_attention([tile_pass(h, c) for c in range(S // tq) for h in range(n_heads)])


def _moba(proj, onehot, *, B, S):
    T = B * S
    n_blk = S // MOBA_BLOCK
    k_top = min(MOBA_TOPK, n_blk - 1)
    nb8 = -(-n_blk // 8) * 8
    hs = MOBA_HEADS_PER_STEP
    kern = functools.partial(_moba_kernel, n_blk=n_blk, k_top=k_top)
    head_spec = lambda blk: pl.BlockSpec((hs, S, LANES), lambda b, h: (blk // hs + h, b, 0))
    return pl.pallas_call(
        kern,
        grid=(B, MOBA_HEADS // hs),
        in_specs=[head_spec(BLK_MQ), head_spec(BLK_MK), head_spec(BLK_MV),
                  pl.BlockSpec((S, LANES), lambda b, h: (0, 0))],
        out_specs=head_spec(0),
        out_shape=jax.ShapeDtypeStruct((MOBA_HEADS, T, LANES), BF16),
        scratch_shapes=[pltpu.VMEM((hs, S, 2 * LANES), BF16), pltpu.VMEM((hs, S, 2 * LANES), BF16),
                        pltpu.VMEM((hs, nb8, HEAD_DIM), F32),
                        pltpu.VMEM((hs, S, 2 * LANES), BF16)],
        compiler_params=pltpu.CompilerParams(
            dimension_semantics=("parallel", "parallel"), vmem_limit_bytes=VMEM_LIMIT),
        name="moba_attn",
    )(proj, proj, proj, onehot)


def _nsa_select_kernel(q_ref, cos_ref, sin_ref, kc_ref, vc_ref, ovt_ref,
                       qrot_ref, bias_ref, oc_ref, *, n_cmp, n_sel_blk, n_top):
    i = pl.program_id(2)
    R, tq, _ = q_ref.shape
    q_raw = q_ref[...]
    cos = (cos_ref[...] * QK_PRESCALE)[None]
    sin = (sin_ref[...] * QK_PRESCALE)[None]
    qrot_ref[...] = _rope(q_raw.astype(F32), cos, sin).astype(BF16)

    n_seg = kc_ref.shape[0]
    s_c = (_nt_dot(q_raw.reshape(R * tq, HEAD_DIM), kc_ref[...]) * SCALE).reshape(R, tq, n_seg)
    n_idx = lax.broadcasted_iota(jnp.int32, s_c.shape, 2)
    pos3 = i * tq + lax.broadcasted_iota(jnp.int32, s_c.shape, 1)
    m_c = (n_idx * CMP_STRIDE + CMP_BLOCK - 1 <= pos3) & (n_idx < n_cmp)
    s_c = jnp.where(m_c, s_c, -jnp.inf)
    mx = jnp.max(s_c, axis=-1, keepdims=True)
    mx = jnp.where(jnp.isfinite(mx), mx, 0.0)
    e_c = jnp.where(m_c, jnp.exp(s_c - mx), 0.0)
    p_c = e_c * (1.0 / jnp.maximum(jnp.sum(e_c, axis=-1, keepdims=True), 1e-30))
    o_c = jnp.dot(p_c.reshape(R * tq, n_seg).astype(BF16), vc_ref[...],
                  preferred_element_type=F32).reshape(R, tq, HEAD_DIM)
    oc_ref[...] = o_c.astype(BF16)

    p_sum = jnp.sum(p_c, axis=0)
    p_hi = p_sum.astype(BF16)
    p_lo = (p_sum - p_hi.astype(F32)).astype(BF16)
    ovt = ovt_ref[...]
    imp_t = _nt_dot(ovt, p_hi) + _nt_dot(ovt, p_lo)
    jrow = lax.broadcasted_iota(jnp.int32, imp_t.shape, 0)
    posq = i * tq + lax.broadcasted_iota(jnp.int32, imp_t.shape, 1)
    own = posq // SEL_BLOCK
    forced = (jrow == 0) | (jrow == own) | (jrow == own - 1)
    future = jrow * SEL_BLOCK > posq
    score_t = jnp.where(future, -jnp.inf, jnp.where(forced, jnp.inf, imp_t))
    keep_t = _select_bias_t(score_t, n_sel_blk, n_top)
    bias_ref[...] = _bias_columns(_bias_t(keep_t))


def _nsa_select(proj, cos, sin, kvc, ovt, *, B, S, tq):
    T = B * S
    nq = S // tq
    n_seg = S // CMP_STRIDE
    n_cmp = n_seg - CMP_BLOCK // CMP_STRIDE + 1
    n_sel_blk = S // SEL_BLOCK
    R = NSA_REP
    kern = functools.partial(_nsa_select_kernel, n_cmp=n_cmp, n_sel_blk=n_sel_blk,
                             n_top=min(SEL_TOPK, n_sel_blk))
    heads_spec = pl.BlockSpec((R, tq, LANES), lambda b, g, i: (g, b * nq + i, 0))
    return pl.pallas_call(
        kern,
        grid=(B, NSA_GROUPS, nq),
        in_specs=[pl.BlockSpec((R, tq, LANES), lambda b, g, i: (BLK_NQ // R + g, b * nq + i, 0)),
                  pl.BlockSpec((tq, LANES), lambda b, g, i: (b * nq + i, 0)),
                  pl.BlockSpec((tq, LANES), lambda b, g, i: (b * nq + i, 0)),
                  pl.BlockSpec((None, n_seg, LANES), lambda b, g, i: (0, g * B + b, 0)),
                  pl.BlockSpec((None, n_seg, LANES), lambda b, g, i: (1, g * B + b, 0)),
                  pl.BlockSpec(ovt.shape, lambda b, g, i: (0, 0))],
        out_specs=[heads_spec,
                   pl.BlockSpec((None, tq, LANES), lambda b, g, i: (g, b * nq + i, 0)),
                   heads_spec],
        out_shape=[jax.ShapeDtypeStruct((NSA_HEADS, T, LANES), BF16),
                   jax.ShapeDtypeStruct((NSA_GROUPS, T, LANES), BF16),
                   jax.ShapeDtypeStruct((NSA_HEADS, T, LANES), BF16)],
        compiler_params=pltpu.CompilerParams(
            dimension_semantics=("parallel", "parallel", "parallel"),
            vmem_limit_bytes=VMEM_LIMIT),
        name="nsa_select",
    )(proj, cos, sin, kvc, kvc, ovt)


def _nsa_sel_kernel(qrot_ref, bias_ref, ks_ref, vs_ref, onehot_ref, oc_ref, gate_ref, o_ref,
                    ksaug_ref, vaug_ref, *, tq):
    R, S, _ = qrot_ref.shape
    ksaug_ref[:, :HEAD_DIM] = ks_ref[...]
    ksaug_ref[:, HEAD_DIM:] = onehot_ref[...]
    vaug_ref[:, :HEAD_DIM] = vs_ref[...]
    vaug_ref[:, HEAD_DIM:] = jnp.ones((S, HEAD_DIM), BF16)

    def head_pass(c, heads):
        q_rows = slice(c * tq, (c + 1) * tq)

        def q_aug():
            bias = bias_ref[q_rows, :]
            return jnp.concatenate(
                [jnp.concatenate([qrot_ref[r, q_rows, :], bias], axis=1) for r in heads], axis=0)

        def store(row0, o):
            gt = _sigmoid(gate_ref[q_rows, :])
            for k, r in enumerate(heads):
                part = (gt[:, 3 * r:3 * r + 1] * oc_ref[r, q_rows, :].astype(F32)
                        + gt[:, 3 * r + 1:3 * r + 2] * o[k * tq:(k + 1) * tq, :])
                o_ref[r, q_rows, :] = part.astype(BF16)

        return _AttnPass(q=q_aug, k_ref=ksaug_ref, v_ref=vaug_ref, k0=0, width=(c + 1) * tq,
                         n_rows=len(heads) * tq, tq=tq, q0=c * tq, window=None,
                         pv_rows=len(heads) * tq, store=store)

    _run_attention([head_pass(c, range(r0, r0 + HEADS_PER_PASS))
                    for c in range(S // tq) for r0 in range(0, R, HEADS_PER_PASS)])


def _nsa_sel(q_rot, bias, proj, onehot, o_c, gates, *, B, S, tq):
    T = B * S
    R = NSA_REP
    heads_spec = pl.BlockSpec((R, S, LANES), lambda b, g: (g, b, 0))
    group_spec = pl.BlockSpec((None, S, LANES), lambda b, g: (g, b, 0))
    kv_spec = lambda blk: pl.BlockSpec((None, S, LANES), lambda b, g: (blk + g, b, 0))
    return pl.pallas_call(
        functools.partial(_nsa_sel_kernel, tq=tq),
        grid=(B, NSA_GROUPS),
        in_specs=[heads_spec, group_spec,
                  kv_spec(BLK_NKS), kv_spec(BLK_NVS),
                  pl.BlockSpec((S, LANES), lambda b, g: (0, 0)),
                  heads_spec, group_spec],
        out_specs=heads_spec,
        out_shape=jax.ShapeDtypeStruct((NSA_HEADS, T, LANES), BF16),
        scratch_shapes=[pltpu.VMEM((S, 2 * LANES), BF16), pltpu.VMEM((S, 2 * LANES), BF16)],
        compiler_params=pltpu.CompilerParams(
            dimension_semantics=("parallel", "parallel"), vmem_limit_bytes=VMEM_LIMIT_BIG),
        name="nsa_sel_attn",
    )(q_rot, bias, proj, proj, onehot, o_c, gates)


def _nsa_win_kernel(qrot_ref, kw_ref, vw_ref, part_ref, gate_ref, o_ref, vaug_ref, *, tq):
    R, S, _ = qrot_ref.shape
    vaug_ref[:, :HEAD_DIM] = vw_ref[...]
    vaug_ref[:, HEAD_DIM:] = jnp.ones((S, HEAD_DIM), BF16)

    def head_pass(c, heads):
        q_rows = slice(c * tq, (c + 1) * tq)
        k0 = max(0, c * tq - WINDOW)

        def store(row0, o_w):
            gt = _sigmoid(gate_ref[q_rows, :])
            for k, r in enumerate(heads):
                o = (part_ref[r, q_rows, :].astype(F32)
                     + gt[:, 3 * r + 2:3 * r + 3] * o_w[k * tq:(k + 1) * tq, :])
                o_ref[r, q_rows, :] = o.astype(BF16)

        return _AttnPass(
            q=lambda: jnp.concatenate([qrot_ref[r, q_rows, :] for r in heads], axis=0),
            k_ref=kw_ref, v_ref=vaug_ref, k0=k0, width=(c + 1) * tq - k0,
            n_rows=len(heads) * tq, tq=tq, q0=c * tq, window=WINDOW,
            pv_rows=len(heads) * tq, store=store)

    _run_attention([head_pass(c, range(r0, r0 + HEADS_PER_PASS))
                    for c in range(S // tq) for r0 in range(0, R, HEADS_PER_PASS)])


def _nsa_win(q_rot, proj, part, gates, *, B, S, tq):
    T = B * S
    R = NSA_REP
    heads_spec = pl.BlockSpec((R, S, LANES), lambda b, g: (g, b, 0))
    kv_spec = lambda blk: pl.BlockSpec((None, S, LANES), lambda b, g: (blk + g, b, 0))
    return pl.pallas_call(
        functools.partial(_nsa_win_kernel, tq=tq),
        grid=(B, NSA_GROUPS),
        in_specs=[heads_spec, kv_spec(BLK_NKW), kv_spec(BLK_NVW), heads_spec,
                  pl.BlockSpec((None, S, LANES), lambda b, g: (g, b, 0))],
        out_specs=heads_spec,
        out_shape=jax.ShapeDtypeStruct((NSA_HEADS, T, LANES), BF16),
        scratch_shapes=[pltpu.VMEM((S, 2 * LANES), BF16)],
        compiler_params=pltpu.CompilerParams(
            dimension_semantics=("parallel", "parallel"), vmem_limit_bytes=VMEM_LIMIT_BIG),
        name="nsa_win_attn",
    )(q_rot, proj, proj, part, gates)


def _out_proj_kernel(om_ref, on_ref, zm_ref, zn_ref, gm_ref, gn_ref, x_ref, w_ref, gf_ref,
                     out_ref):
    def gated_norm(o_ref, z_ref, g_ref, rows):
        n_heads = o_ref.shape[0]
        acts = []
        ss = None
        for h in range(n_heads):
            z = z_ref[h, rows, :].astype(F32)
            a = o_ref[h, rows, :].astype(F32) * (z * _sigmoid(z))
            acts.append(a)
            ss = a * a if ss is None else ss + a * a
        ms = jnp.sum(ss, axis=-1, keepdims=True) * (1.0 / (n_heads * LANES))
        inv = lax.rsqrt(ms + EPS)
        return [(acts[h] * inv * g_ref[h]).astype(BF16) for h in range(n_heads)]

    for r0 in range(0, x_ref.shape[0], OUT_SUB_ROWS):
        rows = slice(r0, r0 + OUT_SUB_ROWS)
        y = jnp.concatenate(gated_norm(om_ref, zm_ref, gm_ref, rows)
                            + gated_norm(on_ref, zn_ref, gn_ref, rows), axis=1)
        r = x_ref[rows, :] + jnp.dot(y, w_ref[...], preferred_element_type=F32)
        ms = jnp.mean(r * r, axis=-1, keepdims=True)
        out_ref[rows, :] = r * lax.rsqrt(ms + EPS) * gf_ref[...]


def _out_proj(o_moba, o_nsa, proj, g_moba, g_nsa, x2d, w_out, g_final, *, tm):
    T = x2d.shape[0]
    H = MOBA_HEADS
    return pl.pallas_call(
        _out_proj_kernel,
        grid=(T // tm,),
        in_specs=[pl.BlockSpec((H, tm, LANES), lambda i: (0, i, 0)),
                  pl.BlockSpec((H, tm, LANES), lambda i: (0, i, 0)),
                  pl.BlockSpec((H, tm, LANES), lambda i: (BLK_MZ // H, i, 0)),
                  pl.BlockSpec((H, tm, LANES), lambda i: (BLK_NZ // H, i, 0)),
                  pl.BlockSpec((H, 1, LANES), lambda i: (0, 0, 0)),
                  pl.BlockSpec((H, 1, LANES), lambda i: (0, 0, 0)),
                  pl.BlockSpec((tm, D_MODEL), lambda i: (i, 0)),
                  pl.BlockSpec((D_MODEL, D_MODEL), lambda i: (0, 0), pipeline_mode=pl.Buffered(1)),
                  pl.BlockSpec((1, D_MODEL), lambda i: (0, 0))],
        out_specs=pl.BlockSpec((tm, D_MODEL), lambda i: (i, 0)),
        out_shape=jax.ShapeDtypeStruct((T, D_MODEL), F32),
        compiler_params=pltpu.CompilerParams(
            dimension_semantics=("parallel",), vmem_limit_bytes=VMEM_LIMIT_BIG),
        name="out_proj",
    )(o_moba, o_nsa, proj, proj, g_moba, g_nsa, x2d, w_out, g_final)


def _w_in_offsets():
    mw, nw, kw = MOBA_HEADS * HEAD_DIM, NSA_HEADS * HEAD_DIM, NSA_GROUPS * HEAD_DIM
    sizes = [mw] * 4 + [nw] + [kw] * 6 + [3 * NSA_HEADS, nw]
    names = ["mq", "mk", "mv", "mz", "nq", "nkc", "nvc", "nks", "nvs", "nkw", "nvw", "ng", "nz"]
    offs = np.concatenate([[0], np.cumsum(sizes)])
    return {n: (int(offs[k]), int(sizes[k])) for k, n in enumerate(names)}


def _weight_tiles_kernel(w_ref, out_ref, *, src_cols, blocks_per_tile):
    for b, src in enumerate(src_cols):
        lane0 = (b % blocks_per_tile) * LANES
        out_ref[b // blocks_per_tile, :, lane0:lane0 + LANES] = (
            w_ref[:, src:src + LANES].astype(BF16))


def _weight_tiles(w_in, blocks_per_tile):
    offsets = _w_in_offsets()
    src_cols = []
    for name, n_blocks in COLUMN_ORDER:
        off, size = offsets[name]
        assert size == n_blocks * LANES
        src_cols += [off + k * LANES for k in range(n_blocks)]
    n_tiles = N_BLOCKS // blocks_per_tile
    rows = 256
    kern = functools.partial(_weight_tiles_kernel, src_cols=tuple(src_cols),
                             blocks_per_tile=blocks_per_tile)
    return pl.pallas_call(
        kern,
        grid=(D_MODEL // rows,),
        in_specs=[pl.BlockSpec((None, rows, w_in.shape[2]), lambda r: (0, r, 0))],
        out_specs=pl.BlockSpec((n_tiles, rows, blocks_per_tile * LANES), lambda r: (0, r, 0)),
        out_shape=jax.ShapeDtypeStruct((n_tiles, D_MODEL, blocks_per_tile * LANES), BF16),
        compiler_params=pltpu.CompilerParams(
            dimension_semantics=("parallel",), vmem_limit_bytes=VMEM_LIMIT,
            allow_input_fusion=[True]),
        name="weight_tiles",
    )(w_in)


def _gate_weight(w_in):
    off, size = _w_in_offsets()["ng"]
    per_group = 3 * NSA_REP
    wg = w_in[0, :, off:off + size].reshape(D_MODEL, NSA_GROUPS, per_group)
    wg = jnp.pad(wg, ((0, 0), (0, 0), (0, LANES - per_group)))
    return wg.reshape(D_MODEL, NSA_GROUPS * LANES).astype(BF16)


def _block_onehot(S, block):
    ids = np.arange(S)[:, None] // block
    return jnp.asarray((ids == np.arange(LANES)[None, :]).astype(np.float32), dtype=BF16)


def _overlap_t(n_seg, n_cmp, n_sel_blk):
    cs = np.arange(n_seg)[None, :] * CMP_STRIDE
    ss = np.arange(n_sel_blk)[:, None] * SEL_BLOCK
    ov = (cs < ss + SEL_BLOCK) & (cs + CMP_BLOCK > ss) & (np.arange(n_seg)[None, :] < n_cmp)
    return jnp.asarray(ov.astype(np.float32), dtype=BF16)


def _layer(x, positions, w_in, g_norm, pe_ck, pe_cv, w_ck1, w_ck2, w_cv1, w_cv2,
           g_out_moba, g_out_nsa, w_out, g_final, *, nsa_tq, tm_in, tm_out, blocks_per_tile):
    B, S, _ = x.shape
    T = B * S
    x2d = x.reshape(T, D_MODEL)
    w_tiles = _weight_tiles(w_in.astype(BF16), blocks_per_tile)
    w_gate = _gate_weight(w_in)
    proj, gates, cos, sin, seg = _in_proj(x2d, g_norm.reshape(1, D_MODEL), w_tiles, w_gate,
                                          positions, tm=tm_in, blocks_per_tile=blocks_per_tile)

    n_seg = S // CMP_STRIDE
    seg = seg.reshape(2, NSA_GROUPS * B * n_seg, CMP_STRIDE * HEAD_DIM)
    pe = jnp.stack([pe_ck.reshape(1, -1), pe_cv.reshape(1, -1)])
    w1 = jnp.stack([w_ck1, w_cv1]).astype(BF16)
    w2 = jnp.stack([w_ck2, w_cv2]).astype(BF16)
    kvc = _compress(seg, pe, w1, w2)

    o_moba = _moba(proj, _block_onehot(S, MOBA_BLOCK), B=B, S=S)
    n_cmp = n_seg - CMP_BLOCK // CMP_STRIDE + 1
    q_rot, bias, o_c = _nsa_select(proj, cos, sin, kvc, _overlap_t(n_seg, n_cmp, S // SEL_BLOCK),
                                   B=B, S=S, tq=4 * nsa_tq)
    part = _nsa_sel(q_rot, bias, proj, _block_onehot(S, SEL_BLOCK), o_c, gates,
                    B=B, S=S, tq=nsa_tq)
    o_nsa = _nsa_win(q_rot, proj, part, gates, B=B, S=S, tq=nsa_tq)
    out = _out_proj(o_moba, o_nsa, proj,
                    g_out_moba.reshape(MOBA_HEADS, 1, LANES), g_out_nsa.reshape(NSA_HEADS, 1, LANES),
                    x2d, w_out.astype(BF16), g_final.reshape(1, D_MODEL), tm=tm_out)
    return out.reshape(B, S, D_MODEL)


def kernel(x, positions, w_in, g_norm, pe_ck, pe_cv, w_ck1, w_ck2, w_cv1, w_cv2,
           g_out_moba, g_out_nsa, w_out, g_final):
    assert w_in.shape[0] == 1, "single-layer problem"
    return _layer(x, positions, w_in, g_norm[0], pe_ck[0], pe_cv[0], w_ck1[0], w_ck2[0],
                  w_cv1[0], w_cv2[0], g_out_moba[0], g_out_nsa[0], w_out[0], g_final,
                  nsa_tq=256, tm_in=1024, tm_out=512, blocks_per_tile=10)
```

```python
import functools
from typing import Any, Callable, NamedTuple, Optional

import numpy as np
import jax
import jax.numpy as jnp
from jax import lax
from jax.experimental import pallas as pl
from jax.experimental.pallas import tpu as pltpu

F32 = jnp.float32
BF16 = jnp.bfloat16

D_MODEL = 2048
HEAD_DIM = 128
MOBA_HEADS = 8
NSA_HEADS = 8
NSA_GROUPS = 2
NSA_REP = 4
MOBA_BLOCK = 256
MOBA_TOPK = 3
CMP_BLOCK = 32
CMP_STRIDE = 16
CMP_HIDDEN = 256
SEL_BLOCK = 64
SEL_TOPK = 8
WINDOW = 512
ROPE_THETA = 10000.0
EPS = 1e-6
SCALE = HEAD_DIM ** -0.5
QK_PRESCALE = SCALE * float(np.log2(np.e))
NEG_BIG = -(2.0 ** 100)

LANES = 128
VMEM_LIMIT = 48 * 1024 * 1024
VMEM_LIMIT_BIG = 58 * 1024 * 1024
COLUMN_ORDER = (("mq", 8), ("mk", 8), ("nks", 2), ("nkw", 2), ("nkc", 2), ("nvc", 2),
                ("mv", 8), ("mz", 8), ("nq", 8), ("nz", 8), ("nvs", 2), ("nvw", 2))
BLK_MQ, BLK_MK, BLK_NKS, BLK_NKW, BLK_NKC, BLK_NVC = 0, 8, 16, 18, 20, 22
BLK_MV, BLK_MZ, BLK_NQ, BLK_NZ, BLK_NVS, BLK_NVW = 24, 32, 40, 48, 56, 58
N_BLOCKS = 60
N_ROPE_BLOCKS = 20
N_SEG_BLOCKS = 4

ROW_CHUNK = 64
MOBA_PV_ROWS = 256
HEADS_PER_PASS = 2
NORM_ROWS = 16
IN_SUB_ROWS = 256
MXU_COLS = 256
OUT_SUB_ROWS = 512
MOBA_TQ = 256
MOBA_HEADS_PER_STEP = 4
KEY_BLOCK = 256


def _nt_dot(a, b):
    return lax.dot_general(a, b, (((1,), (1,)), ((), ())), preferred_element_type=F32)


def _sigmoid(x):
    return 1.0 / (1.0 + jnp.exp(-x))


def _rope(a, cos, sin_signed):
    return a * cos + pltpu.roll(a, HEAD_DIM // 2, axis=a.ndim - 1) * sin_signed


def _interleave(*streams):
    n = max(len(s) for s in streams)
    for k in range(n):
        for s in streams:
            for thunk in s[k * len(s) // n:(k + 1) * len(s) // n]:
                thunk()


def _block_kind(b):
    if BLK_NKC <= b < BLK_NKC + N_SEG_BLOCKS:
        return "seg"
    return "rope_q" if b < BLK_MK else "rope" if b < N_ROPE_BLOCKS else "plain"


def _project_columns(h, w_ref, col0, kinds, cos, sin, store, store_segments=None):
    n = len(kinds)
    acc = jnp.dot(h, w_ref[:, col0:col0 + n * LANES], preferred_element_type=F32)
    n_seg = 0
    for c, kind in enumerate(kinds):
        a = acc[:, c * LANES:(c + 1) * LANES]
        if kind == "rope_q":
            a = _rope(a, cos * QK_PRESCALE, sin * QK_PRESCALE)
        elif kind == "rope":
            a = _rope(a, cos, sin)
        elif kind == "seg":
            store_segments(n_seg, a)
            n_seg += 1
        store(c, a.astype(BF16))


def _in_proj_kernel(x_ref, g_ref, w_ref, wg_ref, pos_ref, invf_ref, sign_ref,
                    out_ref, gate_ref, cos_ref, sin_ref, seg_ref, h_scr, seg_scr,
                    *, tile_patterns):
    j = pl.program_id(1)
    tm = x_ref.shape[0]
    first_pattern = tile_patterns[0][0]
    assert tile_patterns[0][1] == [0]

    @pl.when(j == 0)
    def _():
        ang = pos_ref[...].astype(F32) * invf_ref[...]
        cos_ref[...] = jnp.cos(ang)
        sin_ref[...] = jnp.sin(ang) * sign_ref[...]

        def norm_piece(r0):
            rows = slice(r0, r0 + NORM_ROWS)
            x = x_ref[rows, :]
            ms = jnp.mean(x * x, axis=-1, keepdims=True)
            h_scr[rows, :] = (x * lax.rsqrt(ms + EPS) * g_ref[...]).astype(BF16)

        def norm_pieces(m):
            return [functools.partial(norm_piece, r0)
                    for r0 in range(m * IN_SUB_ROWS, (m + 1) * IN_SUB_ROWS, NORM_ROWS)]

        def matmul_pieces(m):
            rows = slice(m * IN_SUB_ROWS, (m + 1) * IN_SUB_ROWS)

            def gates():
                acc = jnp.dot(h_scr[rows, :], wg_ref[...], preferred_element_type=F32)
                for g in range(NSA_GROUPS):
                    gate_ref[g, rows, :] = acc[:, g * LANES:(g + 1) * LANES]

            def columns(c0):
                def store(c, val):
                    out_ref[c0 + c, rows, :] = val
                per = MXU_COLS // LANES
                _project_columns(h_scr[rows, :], w_ref, c0 * LANES, first_pattern[c0:c0 + per],
                                 cos_ref[rows, :], sin_ref[rows, :], store)

            return [gates] + [functools.partial(columns, c0)
                              for c0 in range(0, len(first_pattern), MXU_COLS // LANES)]

        n_sub = tm // IN_SUB_ROWS
        _interleave(norm_pieces(0))
        for m in range(n_sub):
            _interleave(matmul_pieces(m), norm_pieces(m + 1) if m + 1 < n_sub else [])

    for pattern, tiles in tile_patterns[1:]:
        @pl.when(functools.reduce(jnp.logical_or, [j == t for t in tiles]))
        def _(pattern=pattern):
            def store(c, val):
                out_ref[c] = val

            def store_segments(k, a):
                seg_scr[...] = a
                for t in range(CMP_STRIDE):
                    piece = seg_scr[pl.ds(t, tm // CMP_STRIDE, stride=CMP_STRIDE), :]
                    seg_ref[k, :, t * HEAD_DIM:(t + 1) * HEAD_DIM] = piece.astype(BF16)

            _project_columns(h_scr[...], w_ref, 0, pattern, cos_ref[...], sin_ref[...], store,
                             store_segments)


def _in_proj(x2d, g_norm, w_tiles, w_gate, positions, *, tm, blocks_per_tile):
    T = x2d.shape[0]
    tn = blocks_per_tile * LANES
    n_tiles = N_BLOCKS // blocks_per_tile
    by_pattern = {}
    for t in range(n_tiles):
        pattern = tuple(_block_kind(t * blocks_per_tile + c) for c in range(blocks_per_tile))
        by_pattern.setdefault(pattern, []).append(t)
    half = HEAD_DIM // 2
    inv_freq = 1.0 / (ROPE_THETA ** (jnp.arange(0, HEAD_DIM, 2, dtype=F32) / HEAD_DIM))
    invf = jnp.concatenate([inv_freq, inv_freq]).reshape(1, HEAD_DIM)
    sign = jnp.concatenate([-jnp.ones((half,), F32), jnp.ones((half,), F32)]).reshape(1, HEAD_DIM)
    kern = functools.partial(_in_proj_kernel, tile_patterns=tuple(by_pattern.items()))
    row_table = pl.BlockSpec((tm, HEAD_DIM), lambda i, j: (i, 0))
    return pl.pallas_call(
        kern,
        grid=(T // tm, n_tiles),
        in_specs=[pl.BlockSpec((tm, D_MODEL), lambda i, j: (i, 0)),
                  pl.BlockSpec((1, D_MODEL), lambda i, j: (0, 0)),
                  pl.BlockSpec((None, D_MODEL, tn), lambda i, j: (j, 0, 0)),
                  pl.BlockSpec((D_MODEL, NSA_GROUPS * LANES), lambda i, j: (0, 0)),
                  pl.BlockSpec((tm, 1), lambda i, j: (i, 0)),
                  pl.BlockSpec((1, HEAD_DIM), lambda i, j: (0, 0)),
                  pl.BlockSpec((1, HEAD_DIM), lambda i, j: (0, 0))],
        out_specs=[pl.BlockSpec((blocks_per_tile, tm, LANES), lambda i, j: (j, i, 0)),
                   pl.BlockSpec((NSA_GROUPS, tm, LANES), lambda i, j: (0, i, 0)),
                   row_table, row_table,
                   pl.BlockSpec((N_SEG_BLOCKS, tm // CMP_STRIDE, CMP_STRIDE * HEAD_DIM),
                                lambda i, j: (0, i, 0))],
        out_shape=[jax.ShapeDtypeStruct((N_BLOCKS, T, LANES), BF16),
                   jax.ShapeDtypeStruct((NSA_GROUPS, T, LANES), F32),
                   jax.ShapeDtypeStruct((T, HEAD_DIM), F32),
                   jax.ShapeDtypeStruct((T, HEAD_DIM), F32),
                   jax.ShapeDtypeStruct((N_SEG_BLOCKS, T // CMP_STRIDE, CMP_STRIDE * HEAD_DIM),
                                        BF16)],
        scratch_shapes=[pltpu.VMEM((tm, D_MODEL), BF16), pltpu.VMEM((tm, HEAD_DIM), F32)],
        compiler_params=pltpu.CompilerParams(
            dimension_semantics=("parallel", "arbitrary"), vmem_limit_bytes=VMEM_LIMIT_BIG),
        name="in_proj",
    )(x2d, g_norm, w_tiles, w_gate, positions.reshape(T, 1), invf, sign)


def _compress_kernel(seg_ref, pe_ref, w1_ref, w2_ref, out_ref):
    half = CMP_STRIDE * HEAD_DIM
    seg = seg_ref[0].astype(F32)
    pe = pe_ref[0]
    top = (seg + pe[:, :half]).astype(BF16)
    bot = (seg + pe[:, half:]).astype(BF16)
    a = jnp.dot(top, w1_ref[0, :half, :], preferred_element_type=F32)
    b = jnp.dot(bot, w1_ref[0, half:, :], preferred_element_type=F32)
    rows = a.shape[0]
    h = a + pltpu.roll(b, rows - 1, axis=0)
    hid = h * _sigmoid(h)
    out_ref[0] = jnp.dot(hid.astype(BF16), w2_ref[0], preferred_element_type=F32).astype(BF16)


def _compress(seg, pe, w1, w2):
    _, R, half = seg.shape
    return pl.pallas_call(
        _compress_kernel,
        grid=(2,),
        in_specs=[pl.BlockSpec((1, R, half), lambda c: (c, 0, 0)),
                  pl.BlockSpec((1, 1, 2 * half), lambda c: (c, 0, 0)),
                  pl.BlockSpec((1, 2 * half, CMP_HIDDEN), lambda c: (c, 0, 0)),
                  pl.BlockSpec((1, CMP_HIDDEN, HEAD_DIM), lambda c: (c, 0, 0))],
        out_specs=pl.BlockSpec((1, R, HEAD_DIM), lambda c: (c, 0, 0)),
        out_shape=jax.ShapeDtypeStruct((2, R, HEAD_DIM), BF16),
        compiler_params=pltpu.CompilerParams(
            dimension_semantics=("arbitrary",), vmem_limit_bytes=VMEM_LIMIT),
        name="compress",
    )(seg, pe, w1, w2)


def _select_bias_t(score_t, n_rows, n_keep):
    sub = 8
    n_groups = score_t.shape[0] // sub
    groups = [score_t[g * sub:(g + 1) * sub, :] for g in range(n_groups)]
    jrow = lax.broadcasted_iota(jnp.int32, groups[0].shape, 0)
    cnts = [jnp.zeros(groups[0].shape, jnp.int32) for _ in range(n_groups)]
    for jp in range(n_rows):
        row = score_t[jp:jp + 1, :]
        for g, grp in enumerate(groups):
            if g * sub > jp:
                beats = row >= grp
            elif g * sub + sub - 1 <= jp:
                beats = row > grp
            else:
                beats = (row > grp) | ((row == grp) & (jrow + g * sub > jp))
            cnts[g] = cnts[g] + beats.astype(jnp.int32)
    cnt = cnts[0] if n_groups == 1 else jnp.concatenate(cnts, axis=0)
    return (cnt < n_keep) & (score_t > -jnp.inf)


def _bias_t(keep_t):
    return jnp.where(keep_t, 0.0, NEG_BIG).astype(F32)


def _bias_columns(bias_t):
    rows, q = bias_t.shape
    if rows < LANES:
        bias_t = jnp.concatenate([bias_t, jnp.zeros((LANES - rows, q), F32)], axis=0)
    return bias_t.T.astype(BF16)


def _softmax_rows(s_blocks, r0, *, tq, q0, k0, window=None):
    per_block = KEY_BLOCK // LANES
    n_tiles = len(s_blocks) * per_block
    rows = slice(r0, r0 + ROW_CHUNK)
    qlo = q0 + r0 % tq
    qhi = qlo + ROW_CHUNK - 1

    def tile(t):
        lane0 = (t % per_block) * LANES
        return s_blocks[t // per_block][rows, lane0:lane0 + LANES]

    kinds = []
    for t in range(n_tiles):
        klo = k0 + t * LANES
        khi = klo + LANES - 1
        none = klo > qhi or (window is not None and khi <= qlo - window)
        full = khi <= qlo and (window is None or klo > qhi - window)
        kinds.append("none" if none else "full" if full else "part")
    mx = None
    masked = {}
    for t, kind in enumerate(kinds):
        if kind == "none":
            continue
        x = tile(t)
        if kind == "part":
            qpos = qlo + lax.broadcasted_iota(jnp.int32, x.shape, 0)
            kpos = k0 + t * LANES + lax.broadcasted_iota(jnp.int32, x.shape, 1)
            ok = kpos <= qpos
            if window is not None:
                ok = ok & (kpos > qpos - window)
            x = jnp.where(ok, x, NEG_BIG)
            masked[t] = x
        mx = x if mx is None else jnp.maximum(mx, x)
    m = jnp.broadcast_to(jnp.max(mx, axis=-1, keepdims=True), mx.shape)
    p_tiles = []
    for t, kind in enumerate(kinds):
        if kind == "none":
            p_tiles.append(jnp.zeros((ROW_CHUNK, LANES), BF16))
            continue
        x = masked[t] if kind == "part" else tile(t)
        p_tiles.append(jnp.exp2(x - m).astype(BF16))
    return jnp.concatenate(p_tiles, axis=1)


def _pv_normalized(p, v_ones):
    o = jnp.dot(p, v_ones, preferred_element_type=F32)
    return o[:, :HEAD_DIM] / o[:, HEAD_DIM:]


class _AttnPass(NamedTuple):
    q: Callable[[], jax.Array]
    k_ref: Any
    v_ref: Any
    k0: int
    width: int
    n_rows: int
    tq: int
    q0: int
    window: Optional[int]
    pv_rows: int
    store: Callable[[int, jax.Array], None]


def _run_attention(passes):
    def score_thunks(ps):
        q = ps.q()
        return [lambda j=j: _nt_dot(
            q, ps.k_ref[ps.k0 + j * KEY_BLOCK:ps.k0 + (j + 1) * KEY_BLOCK, :])
                for j in range(ps.width // KEY_BLOCK)]

    s_blocks = [thunk() for thunk in score_thunks(passes[0])]
    pending_pv = None
    for t, ps in enumerate(passes):
        mxu_work = [] if pending_pv is None else [pending_pv]
        n_pv = len(mxu_work)
        if t + 1 < len(passes):
            mxu_work += score_thunks(passes[t + 1])
        row_starts = list(range(0, ps.n_rows, ROW_CHUNK))
        results, p_rows = [], []
        for k in range(max(len(mxu_work), len(row_starts))):
            if k < len(mxu_work):
                results.append(mxu_work[k]())
            if k < len(row_starts):
                p_rows.append(_softmax_rows(s_blocks, row_starts[k], tq=ps.tq, q0=ps.q0,
                                            k0=ps.k0, window=ps.window))
        s_blocks = results[n_pv:]

        def pending_pv(ps=ps, p_rows=p_rows):
            per = ps.pv_rows // ROW_CHUNK
            for k in range(ps.n_rows // ps.pv_rows):
                p = jnp.concatenate(p_rows[k * per:(k + 1) * per], axis=0)
                ps.store(k * ps.pv_rows, _pv_normalized(p, ps.v_ref[ps.k0:ps.k0 + ps.width, :]))
    pending_pv()


def _moba_kernel(q_ref, k_ref, v_ref, onehot_ref, o_ref, kaug_ref, qaug_ref, kmean_ref,
                 vaug_ref, *, n_blk, k_top):
    n_heads, S, _ = k_ref.shape
    tq = MOBA_TQ

    def prepare(h):
        kaug_ref[h, :, :HEAD_DIM] = k_ref[h]
        kaug_ref[h, :, HEAD_DIM:] = onehot_ref[...]
        vaug_ref[h, :, :HEAD_DIM] = v_ref[h]
        vaug_ref[h, :, HEAD_DIM:] = jnp.ones((S, HEAD_DIM), BF16)
        kmean_ref[h] = jnp.zeros(kmean_ref.shape[1:], F32)
        for j in range(n_blk):
            kb = k_ref[h, j * MOBA_BLOCK:(j + 1) * MOBA_BLOCK, :].astype(F32)
            kmean_ref[h, j:j + 1, :] = jnp.sum(kb, axis=0, keepdims=True) * (1.0 / MOBA_BLOCK)

        q = q_ref[h]
        gate_t = _nt_dot(kmean_ref[h].astype(BF16), q)
        jrow = lax.broadcasted_iota(jnp.int32, gate_t.shape, 0)
        own = lax.broadcasted_iota(jnp.int32, gate_t.shape, 1) // MOBA_BLOCK
        gate_t = jnp.where((jrow < own) & jnp.isfinite(gate_t), gate_t, -jnp.inf)
        bias_t = _bias_t(_select_bias_t(gate_t, n_blk, k_top) | (jrow == own))
        qaug_ref[h, :, :HEAD_DIM] = q
        for c in range(S // tq):
            qaug_ref[h, c * tq:(c + 1) * tq, HEAD_DIM:] = _bias_columns(
                bias_t[:, c * tq:(c + 1) * tq])

    def tile_pass(h, c):
        q_rows = slice(c * tq, (c + 1) * tq)

        def store(row0, o):
            o_ref[h, c * tq + row0:c * tq + row0 + o.shape[0], :] = o.astype(BF16)

        return _AttnPass(q=lambda: qaug_ref[h, q_rows, :], k_ref=kaug_ref.at[h],
                         v_ref=vaug_ref.at[h], k0=0, width=(c + 1) * tq, n_rows=tq, tq=tq,
                         q0=c * tq, window=None, pv_rows=MOBA_PV_ROWS, store=store)

    for h in range(n_heads):
        prepare(h)
    _run_attention([tile_pass(h, c) for c in range(S // tq) for h in range(n_heads)])


def _moba(proj, onehot, *, B, S):
    T = B * S
    n_blk = S // MOBA_BLOCK
    k_top = min(MOBA_TOPK, n_blk - 1)
    nb8 = -(-n_blk // 8) * 8
    hs = MOBA_HEADS_PER_STEP
    kern = functools.partial(_moba_kernel, n_blk=n_blk, k_top=k_top)
    head_spec = lambda blk: pl.BlockSpec((hs, S, LANES), lambda b, h: (blk // hs + h, b, 0))
    return pl.pallas_call(
        kern,
        grid=(B, MOBA_HEADS // hs),
        in_specs=[head_spec(BLK_MQ), head_spec(BLK_MK), head_spec(BLK_MV),
                  pl.BlockSpec((S, LANES), lambda b, h: (0, 0))],
        out_specs=head_spec(0),
        out_shape=jax.ShapeDtypeStruct((MOBA_HEADS, T, LANES), BF16),
        scratch_shapes=[pltpu.VMEM((hs, S, 2 * LANES), BF16), pltpu.VMEM((hs, S, 2 * LANES), BF16),
                        pltpu.VMEM((hs, nb8, HEAD_DIM), F32),
                        pltpu.VMEM((hs, S, 2 * LANES), BF16)],
        compiler_params=pltpu.CompilerParams(
            dimension_semantics=("parallel", "parallel"), vmem_limit_bytes=VMEM_LIMIT),
        name="moba_attn",
    )(proj, proj, proj, onehot)


def _nsa_select_kernel(q_ref, cos_ref, sin_ref, kc_ref, vc_ref, ovt_ref,
                       qrot_ref, bias_ref, oc_ref, *, n_cmp, n_sel_blk, n_top):
    i = pl.program_id(2)
    R, tq, _ = q_ref.shape
    q_raw = q_ref[...]
    cos = (cos_ref[...] * QK_PRESCALE)[None]
    sin = (sin_ref[...] * QK_PRESCALE)[None]
    qrot_ref[...] = _rope(q_raw.astype(F32), cos, sin).astype(BF16)

    n_seg = kc_ref.shape[0]
    s_c = (_nt_dot(q_raw.reshape(R * tq, HEAD_DIM), kc_ref[...]) * SCALE).reshape(R, tq, n_seg)
    n_idx = lax.broadcasted_iota(jnp.int32, s_c.shape, 2)
    pos3 = i * tq + lax.broadcasted_iota(jnp.int32, s_c.shape, 1)
    m_c = (n_idx * CMP_STRIDE + CMP_BLOCK - 1 <= pos3) & (n_idx < n_cmp)
    s_c = jnp.where(m_c, s_c, -jnp.inf)
    mx = jnp.max(s_c, axis=-1, keepdims=True)
    mx = jnp.where(jnp.isfinite(mx), mx, 0.0)
    e_c = jnp.where(m_c, jnp.exp(s_c - mx), 0.0)
    p_c = e_c * (1.0 / jnp.maximum(jnp.sum(e_c, axis=-1, keepdims=True), 1e-30))
    o_c = jnp.dot(p_c.reshape(R * tq, n_seg).astype(BF16), vc_ref[...],
                  preferred_element_type=F32).reshape(R, tq, HEAD_DIM)
    oc_ref[...] = o_c.astype(BF16)

    p_sum = jnp.sum(p_c, axis=0)
    p_hi = p_sum.astype(BF16)
    p_lo = (p_sum - p_hi.astype(F32)).astype(BF16)
    ovt = ovt_ref[...]
    imp_t = _nt_dot(ovt, p_hi) + _nt_dot(ovt, p_lo)
    jrow = lax.broadcasted_iota(jnp.int32, imp_t.shape, 0)
    posq = i * tq + lax.broadcasted_iota(jnp.int32, imp_t.shape, 1)
    own = posq // SEL_BLOCK
    forced = (jrow == 0) | (jrow == own) | (jrow == own - 1)
    future = jrow * SEL_BLOCK > posq
    score_t = jnp.where(future, -jnp.inf, jnp.where(forced, jnp.inf, imp_t))
    keep_t = _select_bias_t(score_t, n_sel_blk, n_top)
    bias_ref[...] = _bias_columns(_bias_t(keep_t))


def _nsa_select(proj, cos, sin, kvc, ovt, *, B, S, tq):
    T = B * S
    nq = S // tq
    n_seg = S // CMP_STRIDE
    n_cmp = n_seg - CMP_BLOCK // CMP_STRIDE + 1
    n_sel_blk = S // SEL_BLOCK
    R = NSA_REP
    kern = functools.partial(_nsa_select_kernel, n_cmp=n_cmp, n_sel_blk=n_sel_blk,
                             n_top=min(SEL_TOPK, n_sel_blk))
    heads_spec = pl.BlockSpec((R, tq, LANES), lambda b, g, i: (g, b * nq + i, 0))
    return pl.pallas_call(
        kern,
        grid=(B, NSA_GROUPS, nq),
        in_specs=[pl.BlockSpec((R, tq, LANES), lambda b, g, i: (BLK_NQ // R + g, b * nq + i, 0)),
                  pl.BlockSpec((tq, LANES), lambda b, g, i: (b * nq + i, 0)),
                  pl.BlockSpec((tq, LANES), lambda b, g, i: (b * nq + i, 0)),
                  pl.BlockSpec((None, n_seg, LANES), lambda b, g, i: (0, g * B + b, 0)),
                  pl.BlockSpec((None, n_seg, LANES), lambda b, g, i: (1, g * B + b, 0)),
                  pl.BlockSpec(ovt.shape, lambda b, g, i: (0, 0))],
        out_specs=[heads_spec,
                   pl.BlockSpec((None, tq, LANES), lambda b, g, i: (g, b * nq + i, 0)),
                   heads_spec],
        out_shape=[jax.ShapeDtypeStruct((NSA_HEADS, T, LANES), BF16),
                   jax.ShapeDtypeStruct((NSA_GROUPS, T, LANES), BF16),
                   jax.ShapeDtypeStruct((NSA_HEADS, T, LANES), BF16)],
        compiler_params=pltpu.CompilerParams(
            dimension_semantics=("parallel", "parallel", "parallel"),
            vmem_limit_bytes=VMEM_LIMIT),
        name="nsa_select",
    )(proj, cos, sin, kvc, kvc, ovt)


def _nsa_sel_kernel(qrot_ref, bias_ref, ks_ref, vs_ref, onehot_ref, oc_ref, gate_ref, o_ref,
                    ksaug_ref, vaug_ref, *, tq):
    R, S, _ = qrot_ref.shape
    ksaug_ref[:, :HEAD_DIM] = ks_ref[...]
    ksaug_ref[:, HEAD_DIM:] = onehot_ref[...]
    vaug_ref[:, :HEAD_DIM] = vs_ref[...]
    vaug_ref[:, HEAD_DIM:] = jnp.ones((S, HEAD_DIM), BF16)

    def head_pass(c, heads):
        q_rows = slice(c * tq, (c + 1) * tq)

        def q_aug():
            bias = bias_ref[q_rows, :]
            return jnp.concatenate(
                [jnp.concatenate([qrot_ref[r, q_rows, :], bias], axis=1) for r in heads], axis=0)

        def store(row0, o):
            gt = _sigmoid(gate_ref[q_rows, :])
            for k, r in enumerate(heads):
                part = (gt[:, 3 * r:3 * r + 1] * oc_ref[r, q_rows, :].astype(F32)
                        + gt[:, 3 * r + 1:3 * r + 2] * o[k * tq:(k + 1) * tq, :])
                o_ref[r, q_rows, :] = part.astype(BF16)

        return _AttnPass(q=q_aug, k_ref=ksaug_ref, v_ref=vaug_ref, k0=0, width=(c + 1) * tq,
                         n_rows=len(heads) * tq, tq=tq, q0=c * tq, window=None,
                         pv_rows=len(heads) * tq, store=store)

    _run_attention([head_pass(c, range(r0, r0 + HEADS_PER_PASS))
                    for c in range(S // tq) for r0 in range(0, R, HEADS_PER_PASS)])


def _nsa_sel(q_rot, bias, proj, onehot, o_c, gates, *, B, S, tq):
    T = B * S
    R = NSA_REP
    heads_spec = pl.BlockSpec((R, S, LANES), lambda b, g: (g, b, 0))
    group_spec = pl.BlockSpec((None, S, LANES), lambda b, g: (g, b, 0))
    kv_spec = lambda blk: pl.BlockSpec((None, S, LANES), lambda b, g: (blk + g, b, 0))
    return pl.pallas_call(
        functools.partial(_nsa_sel_kernel, tq=tq),
        grid=(B, NSA_GROUPS),
        in_specs=[heads_spec, group_spec,
                  kv_spec(BLK_NKS), kv_spec(BLK_NVS),
                  pl.BlockSpec((S, LANES), lambda b, g: (0, 0)),
                  heads_spec, group_spec],
        out_specs=heads_spec,
        out_shape=jax.ShapeDtypeStruct((NSA_HEADS, T, LANES), BF16),
        scratch_shapes=[pltpu.VMEM((S, 2 * LANES), BF16), pltpu.VMEM((S, 2 * LANES), BF16)],
        compiler_params=pltpu.CompilerParams(
            dimension_semantics=("parallel", "parallel"), vmem_limit_bytes=VMEM_LIMIT_BIG),
        name="nsa_sel_attn",
    )(q_rot, bias, proj, proj, onehot, o_c, gates)


def _nsa_win_kernel(qrot_ref, kw_ref, vw_ref, part_ref, gate_ref, o_ref, vaug_ref, *, tq):
    R, S, _ = qrot_ref.shape
    vaug_ref[:, :HEAD_DIM] = vw_ref[...]
    vaug_ref[:, HEAD_DIM:] = jnp.ones((S, HEAD_DIM), BF16)

    def head_pass(c, heads):
        q_rows = slice(c * tq, (c + 1) * tq)
        k0 = max(0, c * tq - WINDOW)

        def store(row0, o_w):
            gt = _sigmoid(gate_ref[q_rows, :])
            for k, r in enumerate(heads):
                o = (part_ref[r, q_rows, :].astype(F32)
                     + gt[:, 3 * r + 2:3 * r + 3] * o_w[k * tq:(k + 1) * tq, :])
                o_ref[r, q_rows, :] = o.astype(BF16)

        return _AttnPass(
            q=lambda: jnp.concatenate([qrot_ref[r, q_rows, :] for r in heads], axis=0),
            k_ref=kw_ref, v_ref=vaug_ref, k0=k0, width=(c + 1) * tq - k0,
            n_rows=len(heads) * tq, tq=tq, q0=c * tq, window=WINDOW,
            pv_rows=len(heads) * tq, store=store)

    _run_attention([head_pass(c, range(r0, r0 + HEADS_PER_PASS))
                    for c in range(S // tq) for r0 in range(0, R, HEADS_PER_PASS)])


def _nsa_win(q_rot, proj, part, gates, *, B, S, tq):
    T = B * S
    R = NSA_REP
    heads_spec = pl.BlockSpec((R, S, LANES), lambda b, g: (g, b, 0))
    kv_spec = lambda blk: pl.BlockSpec((None, S, LANES), lambda b, g: (blk + g, b, 0))
    return pl.pallas_call(
        functools.partial(_nsa_win_kernel, tq=tq),
        grid=(B, NSA_GROUPS),
        in_specs=[heads_spec, kv_spec(BLK_NKW), kv_spec(BLK_NVW), heads_spec,
                  pl.BlockSpec((None, S, LANES), lambda b, g: (g, b, 0))],
        out_specs=heads_spec,
        out_shape=jax.ShapeDtypeStruct((NSA_HEADS, T, LANES), BF16),
        scratch_shapes=[pltpu.VMEM((S, 2 * LANES), BF16)],
        compiler_params=pltpu.CompilerParams(
            dimension_semantics=("parallel", "parallel"), vmem_limit_bytes=VMEM_LIMIT_BIG),
        name="nsa_win_attn",
    )(q_rot, proj, proj, part, gates)


def _out_proj_kernel(om_ref, on_ref, zm_ref, zn_ref, gm_ref, gn_ref, x_ref, w_ref, gf_ref,
                     out_ref):
    def gated_norm(o_ref, z_ref, g_ref, rows):
        n_heads = o_ref.shape[0]
        acts = []
        ss = None
        for h in range(n_heads):
            z = z_ref[h, rows, :].astype(F32)
            a = o_ref[h, rows, :].astype(F32) * (z * _sigmoid(z))
            acts.append(a)
            ss = a * a if ss is None else ss + a * a
        ms = jnp.sum(ss, axis=-1, keepdims=True) * (1.0 / (n_heads * LANES))
        inv = lax.rsqrt(ms + EPS)
        return [(acts[h] * inv * g_ref[h]).astype(BF16) for h in range(n_heads)]

    for r0 in range(0, x_ref.shape[0], OUT_SUB_ROWS):
        rows = slice(r0, r0 + OUT_SUB_ROWS)
        y = jnp.concatenate(gated_norm(om_ref, zm_ref, gm_ref, rows)
                            + gated_norm(on_ref, zn_ref, gn_ref, rows), axis=1)
        r = x_ref[rows, :] + jnp.dot(y, w_ref[...], preferred_element_type=F32)
        ms = jnp.mean(r * r, axis=-1, keepdims=True)
        out_ref[rows, :] = r * lax.rsqrt(ms + EPS) * gf_ref[...]


def _out_proj(o_moba, o_nsa, proj, g_moba, g_nsa, x2d, w_out, g_final, *, tm):
    T = x2d.shape[0]
    H = MOBA_HEADS
    return pl.pallas_call(
        _out_proj_kernel,
        grid=(T // tm,),
        in_specs=[pl.BlockSpec((H, tm, LANES), lambda i: (0, i, 0)),
                  pl.BlockSpec((H, tm, LANES), lambda i: (0, i, 0)),
                  pl.BlockSpec((H, tm, LANES), lambda i: (BLK_MZ // H, i, 0)),
                  pl.BlockSpec((H, tm, LANES), lambda i: (BLK_NZ // H, i, 0)),
                  pl.BlockSpec((H, 1, LANES), lambda i: (0, 0, 0)),
                  pl.BlockSpec((H, 1, LANES), lambda i: (0, 0, 0)),
                  pl.BlockSpec((tm, D_MODEL), lambda i: (i, 0)),
                  pl.BlockSpec((D_MODEL, D_MODEL), lambda i: (0, 0), pipeline_mode=pl.Buffered(1)),
                  pl.BlockSpec((1, D_MODEL), lambda i: (0, 0))],
        out_specs=pl.BlockSpec((tm, D_MODEL), lambda i: (i, 0)),
        out_shape=jax.ShapeDtypeStruct((T, D_MODEL), F32),
        compiler_params=pltpu.CompilerParams(
            dimension_semantics=("parallel",), vmem_limit_bytes=VMEM_LIMIT_BIG),
        name="out_proj",
    )(o_moba, o_nsa, proj, proj, g_moba, g_nsa, x2d, w_out, g_final)


def _w_in_offsets():
    mw, nw, kw = MOBA_HEADS * HEAD_DIM, NSA_HEADS * HEAD_DIM, NSA_GROUPS * HEAD_DIM
    sizes = [mw] * 4 + [nw] + [kw] * 6 + [3 * NSA_HEADS, nw]
    names = ["mq", "mk", "mv", "mz", "nq", "nkc", "nvc", "nks", "nvs", "nkw", "nvw", "ng", "nz"]
    offs = np.concatenate([[0], np.cumsum(sizes)])
    return {n: (int(offs[k]), int(sizes[k])) for k, n in enumerate(names)}


def _weight_tiles_kernel(w_ref, out_ref, *, src_cols, blocks_per_tile):
    for b, src in enumerate(src_cols):
        lane0 = (b % blocks_per_tile) * LANES
        out_ref[b // blocks_per_tile, :, lane0:lane0 + LANES] = (
            w_ref[:, src:src + LANES].astype(BF16))


def _weight_tiles(w_in, blocks_per_tile):
    offsets = _w_in_offsets()
    src_cols = []
    for name, n_blocks in COLUMN_ORDER:
        off, size = offsets[name]
        assert size == n_blocks * LANES
        src_cols += [off + k * LANES for k in range(n_blocks)]
    n_tiles = N_BLOCKS // blocks_per_tile
    rows = 256
    kern = functools.partial(_weight_tiles_kernel, src_cols=tuple(src_cols),
                             blocks_per_tile=blocks_per_tile)
    return pl.pallas_call(
        kern,
        grid=(D_MODEL // rows,),
        in_specs=[pl.BlockSpec((None, rows, w_in.shape[2]), lambda r: (0, r, 0))],
        out_specs=pl.BlockSpec((n_tiles, rows, blocks_per_tile * LANES), lambda r: (0, r, 0)),
        out_shape=jax.ShapeDtypeStruct((n_tiles, D_MODEL, blocks_per_tile * LANES), BF16),
        compiler_params=pltpu.CompilerParams(
            dimension_semantics=("parallel",), vmem_limit_bytes=VMEM_LIMIT),
        name="weight_tiles",
    )(w_in)


def _gate_weight(w_in):
    off, size = _w_in_offsets()["ng"]
    per_group = 3 * NSA_REP
    wg = w_in[0, :, off:off + size].reshape(D_MODEL, NSA_GROUPS, per_group)
    wg = jnp.pad(wg, ((0, 0), (0, 0), (0, LANES - per_group)))
    return wg.reshape(D_MODEL, NSA_GROUPS * LANES).astype(BF16)


def _block_onehot(S, block):
    ids = np.arange(S)[:, None] // block
    return jnp.asarray((ids == np.arange(LANES)[None, :]).astype(np.float32), dtype=BF16)


def _overlap_t(n_seg, n_cmp, n_sel_blk):
    cs = np.arange(n_seg)[None, :] * CMP_STRIDE
    ss = np.arange(n_sel_blk)[:, None] * SEL_BLOCK
    ov = (cs < ss + SEL_BLOCK) & (cs + CMP_BLOCK > ss) & (np.arange(n_seg)[None, :] < n_cmp)
    return jnp.asarray(ov.astype(np.float32), dtype=BF16)


def _layer(x, positions, w_in, g_norm, pe_ck, pe_cv, w_ck1, w_ck2, w_cv1, w_cv2,
           g_out_moba, g_out_nsa, w_out, g_final, *, nsa_tq, tm_in, tm_out, blocks_per_tile):
    B, S, _ = x.shape
    T = B * S
    x2d = x.reshape(T, D_MODEL)
    w_tiles = _weight_tiles(w_in.astype(BF16), blocks_per_tile)
    w_gate = _gate_weight(w_in)
    proj, gates, cos, sin, seg = _in_proj(x2d, g_norm.reshape(1, D_MODEL), w_tiles, w_gate,
                                          positions, tm=tm_in, blocks_per_tile=blocks_per_tile)

    n_seg = S // CMP_STRIDE
    seg = seg.reshape(2, NSA_GROUPS * B * n_seg, CMP_STRIDE * HEAD_DIM)
    pe = jnp.stack([pe_ck.reshape(1, -1), pe_cv.reshape(1, -1)])
    w1 = jnp.stack([w_ck1, w_cv1]).astype(BF16)
    w2 = jnp.stack([w_ck2, w_cv2]).astype(BF16)
    kvc = _compress(seg, pe, w1, w2)

    o_moba = _moba(proj, _block_onehot(S, MOBA_BLOCK), B=B, S=S)
    n_cmp = n_seg - CMP_BLOCK // CMP_STRIDE + 1
    q_rot, bias, o_c = _nsa_select(proj, cos, sin, kvc, _overlap_t(n_seg, n_cmp, S // SEL_BLOCK),
                                   B=B, S=S, tq=4 * nsa_tq)
    part = _nsa_sel(q_rot, bias, proj, _block_onehot(S, SEL_BLOCK), o_c, gates,
                    B=B, S=S, tq=nsa_tq)
    o_nsa = _nsa_win(q_rot, proj, part, gates, B=B, S=S, tq=nsa_tq)
    out = _out_proj(o_moba, o_nsa, proj,
                    g_out_moba.reshape(MOBA_HEADS, 1, LANES), g_out_nsa.reshape(NSA_HEADS, 1, LANES),
                    x2d, w_out.astype(BF16), g_final.reshape(1, D_MODEL), tm=tm_out)
    return out.reshape(B, S, D_MODEL)


def kernel(x, positions, w_in, g_norm, pe_ck, pe_cv, w_ck1, w_ck2, w_cv1, w_cv2,
           g_out_moba, g_out_nsa, w_out, g_final):
    assert w_in.shape[0] == 1, "single-layer problem"
    return _layer(x, positions, w_in, g_norm[0], pe_ck[0], pe_cv[0], w_ck1[0], w_ck2[0],
                  w_cv1[0], w_cv2[0], g_out_moba[0], g_out_nsa[0], w_out[0], g_final,
                  nsa_tq=256, tm_in=1024, tm_out=512, blocks_per_tile=10)
```

```python
import functools
from typing import Any, Callable, NamedTuple, Optional

import numpy as np
import jax
import jax.numpy as jnp
from jax import lax
from jax.experimental import pallas as pl
from jax.experimental.pallas import tpu as pltpu

F32 = jnp.float32
BF16 = jnp.bfloat16

D_MODEL = 2048
HEAD_DIM = 128
MOBA_HEADS = 8
NSA_HEADS = 8
NSA_GROUPS = 2
NSA_REP = 4
MOBA_BLOCK = 256
MOBA_TOPK = 3
CMP_BLOCK = 32
CMP_STRIDE = 16
CMP_HIDDEN = 256
SEL_BLOCK = 64
SEL_TOPK = 8
WINDOW = 512
ROPE_THETA = 10000.0
EPS = 1e-6
SCALE = HEAD_DIM ** -0.5
QK_PRESCALE = SCALE * float(np.log2(np.e))
NEG_BIG = -(2.0 ** 100)

LANES = 128
VMEM_LIMIT = 48 * 1024 * 1024
VMEM_LIMIT_BIG = 58 * 1024 * 1024
COLUMN_ORDER = (("mq", 8), ("mk", 8), ("nks", 2), ("nkw", 2), ("nkc", 2), ("nvc", 2),
                ("mv", 8), ("mz", 8), ("nq", 8), ("nz", 8), ("nvs", 2), ("nvw", 2))
BLK_MQ, BLK_MK, BLK_NKS, BLK_NKW, BLK_NKC, BLK_NVC = 0, 8, 16, 18, 20, 22
BLK_MV, BLK_MZ, BLK_NQ, BLK_NZ, BLK_NVS, BLK_NVW = 24, 32, 40, 48, 56, 58
N_BLOCKS = 60
N_ROPE_BLOCKS = 20
N_SEG_BLOCKS = 4

ROW_CHUNK = 64
MOBA_PV_ROWS = 256
HEADS_PER_PASS = 2
NORM_ROWS = 16
IN_SUB_ROWS = 256
MXU_COLS = 256
OUT_SUB_ROWS = 512
MOBA_TQ = 256
MOBA_HEADS_PER_STEP = 4
KEY_BLOCK = 256


def _nt_dot(a, b):
    return lax.dot_general(a, b, (((1,), (1,)), ((), ())), preferred_element_type=F32)


def _sigmoid(x):
    return 1.0 / (1.0 + jnp.exp(-x))


def _rope(a, cos, sin_signed):
    return a * cos + pltpu.roll(a, HEAD_DIM // 2, axis=a.ndim - 1) * sin_signed


def _interleave(*streams):
    n = max(len(s) for s in streams)
    for k in range(n):
        for s in streams:
            for thunk in s[k * len(s) // n:(k + 1) * len(s) // n]:
                thunk()


def _block_kind(b):
    if BLK_NKC <= b < BLK_NKC + N_SEG_BLOCKS:
        return "seg"
    return "rope_q" if b < BLK_MK else "rope" if b < N_ROPE_BLOCKS else "plain"


def _project_columns(h, w_ref, col0, kinds, cos, sin, store, store_segments=None):
    n = len(kinds)
    acc = jnp.dot(h, w_ref[:, col0:col0 + n * LANES], preferred_element_type=F32)
    n_seg = 0
    for c, kind in enumerate(kinds):
        a = acc[:, c * LANES:(c + 1) * LANES]
        if kind == "rope_q":
            a = _rope(a, cos * QK_PRESCALE, sin * QK_PRESCALE)
        elif kind == "rope":
            a = _rope(a, cos, sin)
        elif kind == "seg":
            store_segments(n_seg, a)
            n_seg += 1
        store(c, a.astype(BF16))


def _in_proj_kernel(x_ref, g_ref, w_ref, wg_ref, pos_ref, invf_ref, sign_ref,
                    out_ref, gate_ref, cos_ref, sin_ref, seg_ref, h_scr, seg_scr,
                    *, tile_patterns):
    j = pl.program_id(1)
    tm = x_ref.shape[0]
    first_pattern = tile_patterns[0][0]
    assert tile_patterns[0][1] == [0]

    @pl.when(j == 0)
    def _():
        ang = pos_ref[...].astype(F32) * invf_ref[...]
        cos_ref[...] = jnp.cos(ang)
        sin_ref[...] = jnp.sin(ang) * sign_ref[...]

        def norm_piece(r0):
            rows = slice(r0, r0 + NORM_ROWS)
            x = x_ref[rows, :]
            ms = jnp.mean(x * x, axis=-1, keepdims=True)
            h_scr[rows, :] = (x * lax.rsqrt(ms + EPS) * g_ref[...]).astype(BF16)

        def norm_pieces(m):
            return [functools.partial(norm_piece, r0)
                    for r0 in range(m * IN_SUB_ROWS, (m + 1) * IN_SUB_ROWS, NORM_ROWS)]

        def matmul_pieces(m):
            rows = slice(m * IN_SUB_ROWS, (m + 1) * IN_SUB_ROWS)

            def gates():
                acc = jnp.dot(h_scr[rows, :], wg_ref[...], preferred_element_type=F32)
                for g in range(NSA_GROUPS):
                    gate_ref[g, rows, :] = acc[:, g * LANES:(g + 1) * LANES]

            def columns(c0):
                def store(c, val):
                    out_ref[c0 + c, rows, :] = val
                per = MXU_COLS // LANES
                _project_columns(h_scr[rows, :], w_ref, c0 * LANES, first_pattern[c0:c0 + per],
                                 cos_ref[rows, :], sin_ref[rows, :], store)

            return [gates] + [functools.partial(columns, c0)
                              for c0 in range(0, len(first_pattern), MXU_COLS // LANES)]

        n_sub = tm // IN_SUB_ROWS
        _interleave(norm_pieces(0))
        for m in range(n_sub):
            _interleave(matmul_pieces(m), norm_pieces(m + 1) if m + 1 < n_sub else [])

    for pattern, tiles in tile_patterns[1:]:
        @pl.when(functools.reduce(jnp.logical_or, [j == t for t in tiles]))
        def _(pattern=pattern):
            def store(c, val):
                out_ref[c] = val

            def store_segments(k, a):
                seg_scr[...] = a
                for t in range(CMP_STRIDE):
                    piece = seg_scr[pl.ds(t, tm // CMP_STRIDE, stride=CMP_STRIDE), :]
                    seg_ref[k, :, t * HEAD_DIM:(t + 1) * HEAD_DIM] = piece.astype(BF16)

            _project_columns(h_scr[...], w_ref, 0, pattern, cos_ref[...], sin_ref[...], store,
                             store_segments)


def _in_proj(x2d, g_norm, w_tiles, w_gate, positions, *, tm, blocks_per_tile):
    T = x2d.shape[0]
    tn = blocks_per_tile * LANES
    n_tiles = N_BLOCKS // blocks_per_tile
    by_pattern = {}
    for t in range(n_tiles):
        pattern = tuple(_block_kind(t * blocks_per_tile + c) for c in range(blocks_per_tile))
        by_pattern.setdefault(pattern, []).append(t)
    half = HEAD_DIM // 2
    inv_freq = 1.0 / (ROPE_THETA ** (jnp.arange(0, HEAD_DIM, 2, dtype=F32) / HEAD_DIM))
    invf = jnp.concatenate([inv_freq, inv_freq]).reshape(1, HEAD_DIM)
    sign = jnp.concatenate([-jnp.ones((half,), F32), jnp.ones((half,), F32)]).reshape(1, HEAD_DIM)
    kern = functools.partial(_in_proj_kernel, tile_patterns=tuple(by_pattern.items()))
    row_table = pl.BlockSpec((tm, HEAD_DIM), lambda i, j: (i, 0))
    return pl.pallas_call(
        kern,
        grid=(T // tm, n_tiles),
        in_specs=[pl.BlockSpec((tm, D_MODEL), lambda i, j: (i, 0)),
                  pl.BlockSpec((1, D_MODEL), lambda i, j: (0, 0)),
                  pl.BlockSpec((None, D_MODEL, tn), lambda i, j: (j, 0, 0)),
                  pl.BlockSpec((D_MODEL, NSA_GROUPS * LANES), lambda i, j: (0, 0)),
                  pl.BlockSpec((tm, 1), lambda i, j: (i, 0)),
                  pl.BlockSpec((1, HEAD_DIM), lambda i, j: (0, 0)),
                  pl.BlockSpec((1, HEAD_DIM), lambda i, j: (0, 0))],
        out_specs=[pl.BlockSpec((blocks_per_tile, tm, LANES), lambda i, j: (j, i, 0)),
                   pl.BlockSpec((NSA_GROUPS, tm, LANES), lambda i, j: (0, i, 0)),
                   row_table, row_table,
                   pl.BlockSpec((N_SEG_BLOCKS, tm // CMP_STRIDE, CMP_STRIDE * HEAD_DIM),
                                lambda i, j: (0, i, 0))],
        out_shape=[jax.ShapeDtypeStruct((N_BLOCKS, T, LANES), BF16),
                   jax.ShapeDtypeStruct((NSA_GROUPS, T, LANES), F32),
                   jax.ShapeDtypeStruct((T, HEAD_DIM), F32),
                   jax.ShapeDtypeStruct((T, HEAD_DIM), F32),
                   jax.ShapeDtypeStruct((N_SEG_BLOCKS, T // CMP_STRIDE, CMP_STRIDE * HEAD_DIM),
                                        BF16)],
        scratch_shapes=[pltpu.VMEM((tm, D_MODEL), BF16), pltpu.VMEM((tm, HEAD_DIM), F32)],
        compiler_params=pltpu.CompilerParams(
            dimension_semantics=("parallel", "arbitrary"), vmem_limit_bytes=VMEM_LIMIT_BIG),
        name="in_proj",
    )(x2d, g_norm, w_tiles, w_gate, positions.reshape(T, 1), invf, sign)


def _compress_kernel(seg_ref, pe_ref, w1_ref, w2_ref, out_ref):
    half = CMP_STRIDE * HEAD_DIM
    seg = seg_ref[0].astype(F32)
    pe = pe_ref[0]
    top = (seg + pe[:, :half]).astype(BF16)
    bot = (seg + pe[:, half:]).astype(BF16)
    a = jnp.dot(top, w1_ref[0, :half, :], preferred_element_type=F32)
    b = jnp.dot(bot, w1_ref[0, half:, :], preferred_element_type=F32)
    rows = a.shape[0]
    h = a + pltpu.roll(b, rows - 1, axis=0)
    hid = h * _sigmoid(h)
    out_ref[0] = jnp.dot(hid.astype(BF16), w2_ref[0], preferred_element_type=F32).astype(BF16)


def _compress(seg, pe, w1, w2):
    _, R, half = seg.shape
    return pl.pallas_call(
        _compress_kernel,
        grid=(2,),
        in_specs=[pl.BlockSpec((1, R, half), lambda c: (c, 0, 0)),
                  pl.BlockSpec((1, 1, 2 * half), lambda c: (c, 0, 0)),
                  pl.BlockSpec((1, 2 * half, CMP_HIDDEN), lambda c: (c, 0, 0)),
                  pl.BlockSpec((1, CMP_HIDDEN, HEAD_DIM), lambda c: (c, 0, 0))],
        out_specs=pl.BlockSpec((1, R, HEAD_DIM), lambda c: (c, 0, 0)),
        out_shape=jax.ShapeDtypeStruct((2, R, HEAD_DIM), BF16),
        compiler_params=pltpu.CompilerParams(
            dimension_semantics=("arbitrary",), vmem_limit_bytes=VMEM_LIMIT),
        name="compress",
    )(seg, pe, w1, w2)


def _select_bias_t(score_t, n_rows, n_keep):
    sub = 8
    n_groups = score_t.shape[0] // sub
    groups = [score_t[g * sub:(g + 1) * sub, :] for g in range(n_groups)]
    jrow = lax.broadcasted_iota(jnp.int32, groups[0].shape, 0)
    cnts = [jnp.zeros(groups[0].shape, jnp.int32) for _ in range(n_groups)]
    for jp in range(n_rows):
        row = score_t[jp:jp + 1, :]
        for g, grp in enumerate(groups):
            if g * sub > jp:
                beats = row >= grp
            elif g * sub + sub - 1 <= jp:
                beats = row > grp
            else:
                beats = (row > grp) | ((row == grp) & (jrow + g * sub > jp))
            cnts[g] = cnts[g] + beats.astype(jnp.int32)
    cnt = cnts[0] if n_groups == 1 else jnp.concatenate(cnts, axis=0)
    return (cnt < n_keep) & (score_t > -jnp.inf)


def _bias_t(keep_t):
    return jnp.where(keep_t, 0.0, NEG_BIG).astype(F32)


def _bias_columns(bias_t):
    rows, q = bias_t.shape
    if rows < LANES:
        bias_t = jnp.concatenate([bias_t, jnp.zeros((LANES - rows, q), F32)], axis=0)
    return bias_t.T.astype(BF16)


def _softmax_rows(s_blocks, r0, *, tq, q0, k0, window=None):
    per_block = KEY_BLOCK // LANES
    n_tiles = len(s_blocks) * per_block
    rows = slice(r0, r0 + ROW_CHUNK)
    qlo = q0 + r0 % tq
    qhi = qlo + ROW_CHUNK - 1

    def tile(t):
        lane0 = (t % per_block) * LANES
        return s_blocks[t // per_block][rows, lane0:lane0 + LANES]

    kinds = []
    for t in range(n_tiles):
        klo = k0 + t * LANES
        khi = klo + LANES - 1
        none = klo > qhi or (window is not None and khi <= qlo - window)
        full = khi <= qlo and (window is None or klo > qhi - window)
        kinds.append("none" if none else "full" if full else "part")
    mx = None
    masked = {}
    for t, kind in enumerate(kinds):
        if kind == "none":
            continue
        x = tile(t)
        if kind == "part":
            qpos = qlo + lax.broadcasted_iota(jnp.int32, x.shape, 0)
            kpos = k0 + t * LANES + lax.broadcasted_iota(jnp.int32, x.shape, 1)
            ok = kpos <= qpos
            if window is not None:
                ok = ok & (kpos > qpos - window)
            x = jnp.where(ok, x, NEG_BIG)
            masked[t] = x
        mx = x if mx is None else jnp.maximum(mx, x)
    m = jnp.broadcast_to(jnp.max(mx, axis=-1, keepdims=True), mx.shape)
    p_tiles = []
    for t, kind in enumerate(kinds):
        if kind == "none":
            p_tiles.append(jnp.zeros((ROW_CHUNK, LANES), BF16))
            continue
        x = masked[t] if kind == "part" else tile(t)
        p_tiles.append(jnp.exp2(x - m).astype(BF16))
    return jnp.concatenate(p_tiles, axis=1)


def _pv_normalized(p, v_ones):
    o = jnp.dot(p, v_ones, preferred_element_type=F32)
    return o[:, :HEAD_DIM] / o[:, HEAD_DIM:]


class _AttnPass(NamedTuple):
    q: Callable[[], jax.Array]
    k_ref: Any
    v_ref: Any
    k0: int
    width: int
    n_rows: int
    tq: int
    q0: int
    window: Optional[int]
    pv_rows: int
    store: Callable[[int, jax.Array], None]


def _run_attention(passes):
    def score_thunks(ps):
        q = ps.q()
        return [lambda j=j: _nt_dot(
            q, ps.k_ref[ps.k0 + j * KEY_BLOCK:ps.k0 + (j + 1) * KEY_BLOCK, :])
                for j in range(ps.width // KEY_BLOCK)]

    s_blocks = [thunk() for thunk in score_thunks(passes[0])]
    pending_pv = None
    for t, ps in enumerate(passes):
        mxu_work = [] if pending_pv is None else [pending_pv]
        n_pv = len(mxu_work)
        if t + 1 < len(passes):
            mxu_work += score_thunks(passes[t + 1])
        row_starts = list(range(0, ps.n_rows, ROW_CHUNK))
        results, p_rows = [], []
        for k in range(max(len(mxu_work), len(row_starts))):
            if k < len(mxu_work):
                results.append(mxu_work[k]())
            if k < len(row_starts):
                p_rows.append(_softmax_rows(s_blocks, row_starts[k], tq=ps.tq, q0=ps.q0,
                                            k0=ps.k0, window=ps.window))
        s_blocks = results[n_pv:]

        def pending_pv(ps=ps, p_rows=p_rows):
            per = ps.pv_rows // ROW_CHUNK
            for k in range(ps.n_rows // ps.pv_rows):
                p = jnp.concatenate(p_rows[k * per:(k + 1) * per], axis=0)
                ps.store(k * ps.pv_rows, _pv_normalized(p, ps.v_ref[ps.k0:ps.k0 + ps.width, :]))
    pending_pv()


def _moba_kernel(q_ref, k_ref, v_ref, onehot_ref, o_ref, kaug_ref, qaug_ref, kmean_ref,
                 vaug_ref, *, n_blk, k_top):
    n_heads, S, _ = k_ref.shape
    tq = MOBA_TQ

    def prepare(h):
        kaug_ref[h, :, :HEAD_DIM] = k_ref[h]
        kaug_ref[h, :, HEAD_DIM:] = onehot_ref[...]
        vaug_ref[h, :, :HEAD_DIM] = v_ref[h]
        vaug_ref[h, :, HEAD_DIM:] = jnp.ones((S, HEAD_DIM), BF16)
        kmean_ref[h] = jnp.zeros(kmean_ref.shape[1:], F32)
        for j in range(n_blk):
            kb = k_ref[h, j * MOBA_BLOCK:(j + 1) * MOBA_BLOCK, :].astype(F32)
            kmean_ref[h, j:j + 1, :] = jnp.sum(kb, axis=0, keepdims=True) * (1.0 / MOBA_BLOCK)

        q = q_ref[h]
        gate_t = _nt_dot(kmean_ref[h].astype(BF16), q)
        jrow = lax.broadcasted_iota(jnp.int32, gate_t.shape, 0)
        own = lax.broadcasted_iota(jnp.int32, gate_t.shape, 1) // MOBA_BLOCK
        gate_t = jnp.where((jrow < own) & jnp.isfinite(gate_t), gate_t, -jnp.inf)
        bias_t = _bias_t(_select_bias_t(gate_t, n_blk, k_top) | (jrow == own))
        qaug_ref[h, :, :HEAD_DIM] = q
        for c in range(S // tq):
            qaug_ref[h, c * tq:(c + 1) * tq, HEAD_DIM:] = _bias_columns(
                bias_t[:, c * tq:(c + 1) * tq])

    def tile_pass(h, c):
        q_rows = slice(c * tq, (c + 1) * tq)

        def store(row0, o):
            o_ref[h, c * tq + row0:c * tq + row0 + o.shape[0], :] = o.astype(BF16)

        return _AttnPass(q=lambda: qaug_ref[h, q_rows, :], k_ref=kaug_ref.at[h],
                         v_ref=vaug_ref.at[h], k0=0, width=(c + 1) * tq, n_rows=tq, tq=tq,
                         q0=c * tq, window=None, pv_rows=MOBA_PV_ROWS, store=store)

    for h in range(n_heads):
        prepare(h)
    _run_attention([tile_pass(h, c) for c in range(S // tq) for h in range(n_heads)])


def _moba(proj, onehot, *, B, S):
    T = B * S
    n_blk = S // MOBA_BLOCK
    k_top = min(MOBA_TOPK, n_blk - 1)
    nb8 = -(-n_blk // 8) * 8
    hs = MOBA_HEADS_PER_STEP
    kern = functools.partial(_moba_kernel, n_blk=n_blk, k_top=k_top)
    head_spec = lambda blk: pl.BlockSpec((hs, S, LANES), lambda b, h: (blk // hs + h, b, 0))
    return pl.pallas_call(
        kern,
        grid=(B, MOBA_HEADS // hs),
        in_specs=[head_spec(BLK_MQ), head_spec(BLK_MK), head_spec(BLK_MV),
                  pl.BlockSpec((S, LANES), lambda b, h: (0, 0))],
        out_specs=head_spec(0),
        out_shape=jax.ShapeDtypeStruct((MOBA_HEADS, T, LANES), BF16),
        scratch_shapes=[pltpu.VMEM((hs, S, 2 * LANES), BF16), pltpu.VMEM((hs, S, 2 * LANES), BF16),
                        pltpu.VMEM((hs, nb8, HEAD_DIM), F32),
                        pltpu.VMEM((hs, S, 2 * LANES), BF16)],
        compiler_params=pltpu.CompilerParams(
            dimension_semantics=("parallel", "parallel"), vmem_limit_bytes=VMEM_LIMIT),
        name="moba_attn",
    )(proj, proj, proj, onehot)


def _nsa_select_kernel(q_ref, cos_ref, sin_ref, kc_ref, vc_ref, ovt_ref,
                       qrot_ref, bias_ref, oc_ref, *, n_cmp, n_sel_blk, n_top):
    i = pl.program_id(2)
    R, tq, _ = q_ref.shape
    q_raw = q_ref[...]
    cos = (cos_ref[...] * QK_PRESCALE)[None]
    sin = (sin_ref[...] * QK_PRESCALE)[None]
    qrot_ref[...] = _rope(q_raw.astype(F32), cos, sin).astype(BF16)

    n_seg = kc_ref.shape[0]
    s_c = (_nt_dot(q_raw.reshape(R * tq, HEAD_DIM), kc_ref[...]) * SCALE).reshape(R, tq, n_seg)
    n_idx = lax.broadcasted_iota(jnp.int32, s_c.shape, 2)
    pos3 = i * tq + lax.broadcasted_iota(jnp.int32, s_c.shape, 1)
    m_c = (n_idx * CMP_STRIDE + CMP_BLOCK - 1 <= pos3) & (n_idx < n_cmp)
    s_c = jnp.where(m_c, s_c, -jnp.inf)
    mx = jnp.max(s_c, axis=-1, keepdims=True)
    mx = jnp.where(jnp.isfinite(mx), mx, 0.0)
    e_c = jnp.where(m_c, jnp.exp(s_c - mx), 0.0)
    p_c = e_c * (1.0 / jnp.maximum(jnp.sum(e_c, axis=-1, keepdims=True), 1e-30))
    o_c = jnp.dot(p_c.reshape(R * tq, n_seg).astype(BF16), vc_ref[...],
                  preferred_element_type=F32).reshape(R, tq, HEAD_DIM)
    oc_ref[...] = o_c.astype(BF16)

    p_sum = jnp.sum(p_c, axis=0)
    p_hi = p_sum.astype(BF16)
    p_lo = (p_sum - p_hi.astype(F32)).astype(BF16)
    ovt = ovt_ref[...]
    imp_t = _nt_dot(ovt, p_hi) + _nt_dot(ovt, p_lo)
    jrow = lax.broadcasted_iota(jnp.int32, imp_t.shape, 0)
    posq = i * tq + lax.broadcasted_iota(jnp.int32, imp_t.shape, 1)
    own = posq // SEL_BLOCK
    forced = (jrow == 0) | (jrow == own) | (jrow == own - 1)
    future = jrow * SEL_BLOCK > posq
    score_t = jnp.where(future, -jnp.inf, jnp.where(forced, jnp.inf, imp_t))
    keep_t = _select_bias_t(score_t, n_sel_blk, n_top)
    bias_ref[...] = _bias_columns(_bias_t(keep_t))


def _nsa_select(proj, cos, sin, kvc, ovt, *, B, S, tq):
    T = B * S
    nq = S // tq
    n_seg = S // CMP_STRIDE
    n_cmp = n_seg - CMP_BLOCK // CMP_STRIDE + 1
    n_sel_blk = S // SEL_BLOCK
    R = NSA_REP
    kern = functools.partial(_nsa_select_kernel, n_cmp=n_cmp, n_sel_blk=n_sel_blk,
                             n_top=min(SEL_TOPK, n_sel_blk))
    heads_spec = pl.BlockSpec((R, tq, LANES), lambda b, g, i: (g, b * nq + i, 0))
    return pl.pallas_call(
        kern,
        grid=(B, NSA_GROUPS, nq),
        in_specs=[pl.BlockSpec((R, tq, LANES), lambda b, g, i: (BLK_NQ // R + g, b * nq + i, 0)),
                  pl.BlockSpec((tq, LANES), lambda b, g, i: (b * nq + i, 0)),
                  pl.BlockSpec((tq, LANES), lambda b, g, i: (b * nq + i, 0)),
                  pl.BlockSpec((None, n_seg, LANES), lambda b, g, i: (0, g * B + b, 0)),
                  pl.BlockSpec((None, n_seg, LANES), lambda b, g, i: (1, g * B + b, 0)),
                  pl.BlockSpec(ovt.shape, lambda b, g, i: (0, 0))],
        out_specs=[heads_spec,
                   pl.BlockSpec((None, tq, LANES), lambda b, g, i: (g, b * nq + i, 0)),
                   heads_spec],
        out_shape=[jax.ShapeDtypeStruct((NSA_HEADS, T, LANES), BF16),
                   jax.ShapeDtypeStruct((NSA_GROUPS, T, LANES), BF16),
                   jax.ShapeDtypeStruct((NSA_HEADS, T, LANES), BF16)],
        compiler_params=pltpu.CompilerParams(
            dimension_semantics=("parallel", "parallel", "parallel"),
            vmem_limit_bytes=VMEM_LIMIT),
        name="nsa_select",
    )(proj, cos, sin, kvc, kvc, ovt)


def _nsa_sel_kernel(qrot_ref, bias_ref, ks_ref, vs_ref, onehot_ref, oc_ref, gate_ref, o_ref,
                    ksaug_ref, vaug_ref, *, tq):
    R, S, _ = qrot_ref.shape
    ksaug_ref[:, :HEAD_DIM] = ks_ref[...]
    ksaug_ref[:, HEAD_DIM:] = onehot_ref[...]
    vaug_ref[:, :HEAD_DIM] = vs_ref[...]
    vaug_ref[:, HEAD_DIM:] = jnp.ones((S, HEAD_DIM), BF16)

    def head_pass(c, heads):
        q_rows = slice(c * tq, (c + 1) * tq)

        def q_aug():
            bias = bias_ref[q_rows, :]
            return jnp.concatenate(
                [jnp.concatenate([qrot_ref[r, q_rows, :], bias], axis=1) for r in heads], axis=0)

        def store(row0, o):
            gt = _sigmoid(gate_ref[q_rows, :])
            for k, r in enumerate(heads):
                part = (gt[:, 3 * r:3 * r + 1] * oc_ref[r, q_rows, :].astype(F32)
                        + gt[:, 3 * r + 1:3 * r + 2] * o[k * tq:(k + 1) * tq, :])
                o_ref[r, q_rows, :] = part.astype(BF16)

        return _AttnPass(q=q_aug, k_ref=ksaug_ref, v_ref=vaug_ref, k0=0, width=(c + 1) * tq,
                         n_rows=len(heads) * tq, tq=tq, q0=c * tq, window=None,
                         pv_rows=len(heads) * tq, store=store)

    _run_attention([head_pass(c, range(r0, r0 + HEADS_PER_PASS))
                    for c in range(S // tq) for r0 in range(0, R, HEADS_PER_PASS)])


def _nsa_sel(q_rot, bias, proj, onehot, o_c, gates, *, B, S, tq):
    T = B * S
    R = NSA_REP
    heads_spec = pl.BlockSpec((R, S, LANES), lambda b, g: (g, b, 0))
    group_spec = pl.BlockSpec((None, S, LANES), lambda b, g: (g, b, 0))
    kv_spec = lambda blk: pl.BlockSpec((None, S, LANES), lambda b, g: (blk + g, b, 0))
    return pl.pallas_call(
        functools.partial(_nsa_sel_kernel, tq=tq),
        grid=(B, NSA_GROUPS),
        in_specs=[heads_spec, group_spec,
                  kv_spec(BLK_NKS), kv_spec(BLK_NVS),
                  pl.BlockSpec((S, LANES), lambda b, g: (0, 0)),
                  heads_spec, group_spec],
        out_specs=heads_spec,
        out_shape=jax.ShapeDtypeStruct((NSA_HEADS, T, LANES), BF16),
        scratch_shapes=[pltpu.VMEM((S, 2 * LANES), BF16), pltpu.VMEM((S, 2 * LANES), BF16)],
        compiler_params=pltpu.CompilerParams(
            dimension_semantics=("parallel", "parallel"), vmem_limit_bytes=VMEM_LIMIT_BIG),
        name="nsa_sel_attn",
    )(q_rot, bias, proj, proj, onehot, o_c, gates)


def _nsa_win_kernel(qrot_ref, kw_ref, vw_ref, part_ref, gate_ref, o_ref, vaug_ref, *, tq):
    R, S, _ = qrot_ref.shape
    vaug_ref[:, :HEAD_DIM] = vw_ref[...]
    vaug_ref[:, HEAD_DIM:] = jnp.ones((S, HEAD_DIM), BF16)

    def head_pass(c, heads):
        q_rows = slice(c * tq, (c + 1) * tq)
        k0 = max(0, c * tq - WINDOW)

        def store(row0, o_w):
            gt = _sigmoid(gate_ref[q_rows, :])
            for k, r in enumerate(heads):
                o = (part_ref[r, q_rows, :].astype(F32)
                     + gt[:, 3 * r + 2:3 * r + 3] * o_w[k * tq:(k + 1) * tq, :])
                o_ref[r, q_rows, :] = o.astype(BF16)

        return _AttnPass(
            q=lambda: jnp.concatenate([qrot_ref[r, q_rows, :] for r in heads], axis=0),
            k_ref=kw_ref, v_ref=vaug_ref, k0=k0, width=(c + 1) * tq - k0,
            n_rows=len(heads) * tq, tq=tq, q0=c * tq, window=WINDOW,
            pv_rows=len(heads) * tq, store=store)

    _run_attention([head_pass(c, range(r0, r0 + HEADS_PER_PASS))
                    for c in range(S // tq) for r0 in range(0, R, HEADS_PER_PASS)])


def _nsa_win(q_rot, proj, part, gates, *, B, S, tq):
    T = B * S
    R = NSA_REP
    heads_spec = pl.BlockSpec((R, S, LANES), lambda b, g: (g, b, 0))
    kv_spec = lambda blk: pl.BlockSpec((None, S, LANES), lambda b, g: (blk + g, b, 0))
    return pl.pallas_call(
        functools.partial(_nsa_win_kernel, tq=tq),
        grid=(B, NSA_GROUPS),
        in_specs=[heads_spec, kv_spec(BLK_NKW), kv_spec(BLK_NVW), heads_spec,
                  pl.BlockSpec((None, S, LANES), lambda b, g: (g, b, 0))],
        out_specs=heads_spec,
        out_shape=jax.ShapeDtypeStruct((NSA_HEADS, T, LANES), BF16),
        scratch_shapes=[pltpu.VMEM((S, 2 * LANES), BF16)],
        compiler_params=pltpu.CompilerParams(
            dimension_semantics=("parallel", "parallel"), vmem_limit_bytes=VMEM_LIMIT_BIG),
        name="nsa_win_attn",
    )(q_rot, proj, proj, part, gates)


def _out_proj_kernel(om_ref, on_ref, zm_ref, zn_ref, gm_ref, gn_ref, x_ref, w_ref, gf_ref,
                     out_ref):
    def gated_norm(o_ref, z_ref, g_ref, rows):
        n_heads = o_ref.shape[0]
        acts = []
        ss = None
        for h in range(n_heads):
            z = z_ref[h, rows, :].astype(F32)
            a = o_ref[h, rows, :].astype(F32) * (z * _sigmoid(z))
            acts.append(a)
            ss = a * a if ss is None else ss + a * a
        ms = jnp.sum(ss, axis=-1, keepdims=True) * (1.0 / (n_heads * LANES))
        inv = lax.rsqrt(ms + EPS)
        return [(acts[h] * inv * g_ref[h]).astype(BF16) for h in range(n_heads)]

    for r0 in range(0, x_ref.shape[0], OUT_SUB_ROWS):
        rows = slice(r0, r0 + OUT_SUB_ROWS)
        y = jnp.concatenate(gated_norm(om_ref, zm_ref, gm_ref, rows)
                            + gated_norm(on_ref, zn_ref, gn_ref, rows), axis=1)
        r = x_ref[rows, :] + jnp.dot(y, w_ref[...], preferred_element_type=F32)
        ms = jnp.mean(r * r, axis=-1, keepdims=True)
        out_ref[rows, :] = r * lax.rsqrt(ms + EPS) * gf_ref[...]


def _out_proj(o_moba, o_nsa, proj, g_moba, g_nsa, x2d, w_out, g_final, *, tm):
    T = x2d.shape[0]
    H = MOBA_HEADS
    return pl.pallas_call(
        _out_proj_kernel,
        grid=(T // tm,),
        in_specs=[pl.BlockSpec((H, tm, LANES), lambda i: (0, i, 0)),
                  pl.BlockSpec((H, tm, LANES), lambda i: (0, i, 0)),
                  pl.BlockSpec((H, tm, LANES), lambda i: (BLK_MZ // H, i, 0)),
                  pl.BlockSpec((H, tm, LANES), lambda i: (BLK_NZ // H, i, 0)),
                  pl.BlockSpec((H, 1, LANES), lambda i: (0, 0, 0)),
                  pl.BlockSpec((H, 1, LANES), lambda i: (0, 0, 0)),
                  pl.BlockSpec((tm, D_MODEL), lambda i: (i, 0)),
                  pl.BlockSpec((D_MODEL, D_MODEL), lambda i: (0, 0), pipeline_mode=pl.Buffered(1)),
                  pl.BlockSpec((1, D_MODEL), lambda i: (0, 0))],
        out_specs=pl.BlockSpec((tm, D_MODEL), lambda i: (i, 0)),
        out_shape=jax.ShapeDtypeStruct((T, D_MODEL), F32),
        compiler_params=pltpu.CompilerParams(
            dimension_semantics=("parallel",), vmem_limit_bytes=VMEM_LIMIT_BIG),
        name="out_proj",
    )(o_moba, o_nsa, proj, proj, g_moba, g_nsa, x2d, w_out, g_final)


def _w_in_offsets():
    mw, nw, kw = MOBA_HEADS * HEAD_DIM, NSA_HEADS * HEAD_DIM, NSA_GROUPS * HEAD_DIM
    sizes = [mw] * 4 + [nw] + [kw] * 6 + [3 * NSA_HEADS, nw]
    names = ["mq", "mk", "mv", "mz", "nq", "nkc", "nvc", "nks", "nvs", "nkw", "nvw", "ng", "nz"]
    offs = np.concatenate([[0], np.cumsum(sizes)])
    return {n: (int(offs[k]), int(sizes[k])) for k, n in enumerate(names)}


def _weight_tiles_kernel(w_ref, out_ref, *, src_cols, blocks_per_tile):
    for b, src in enumerate(src_cols):
        lane0 = (b % blocks_per_tile) * LANES
        out_ref[b // blocks_per_tile, :, lane0:lane0 + LANES] = (
            w_ref[:, src:src + LANES].astype(BF16))


def _weight_tiles(w_in, blocks_per_tile):
    offsets = _w_in_offsets()
    src_cols = []
    for name, n_blocks in COLUMN_ORDER:
        off, size = offsets[name]
        assert size == n_blocks * LANES
        src_cols += [off + k * LANES for k in range(n_blocks)]
    n_tiles = N_BLOCKS // blocks_per_tile
    rows = 256
    kern = functools.partial(_weight_tiles_kernel, src_cols=tuple(src_cols),
                             blocks_per_tile=blocks_per_tile)
    return pl.pallas_call(
        kern,
        grid=(D_MODEL // rows,),
        in_specs=[pl.BlockSpec((None, rows, w_in.shape[2]), lambda r: (0, r, 0))],
        out_specs=pl.BlockSpec((n_tiles, rows, blocks_per_tile * LANES), lambda r: (0, r, 0)),
        out_shape=jax.ShapeDtypeStruct((n_tiles, D_MODEL, blocks_per_tile * LANES), BF16),
        compiler_params=pltpu.CompilerParams(
            dimension_semantics=("parallel",), vmem_limit_bytes=VMEM_LIMIT),
        name="weight_tiles",
    )(w_in)


def _gate_weight(w_in):
    off, size = _w_in_offsets()["ng"]
    per_group = 3 * NSA_REP
    wg = w_in[0, :, off:off + size].reshape(D_MODEL, NSA_GROUPS, per_group)
    wg = jnp.pad(wg, ((0, 0), (0, 0), (0, LANES - per_group)))
    return wg.reshape(D_MODEL, NSA_GROUPS * LANES).astype(BF16)


def _block_onehot(S, block):
    ids = np.arange(S)[:, None] // block
    return jnp.asarray((ids == np.arange(LANES)[None, :]).astype(np.float32), dtype=BF16)


def _overlap_t(n_seg, n_cmp, n_sel_blk):
    cs = np.arange(n_seg)[None, :] * CMP_STRIDE
    ss = np.arange(n_sel_blk)[:, None] * SEL_BLOCK
    ov = (cs < ss + SEL_BLOCK) & (cs + CMP_BLOCK > ss) & (np.arange(n_seg)[None, :] < n_cmp)
    return jnp.asarray(ov.astype(np.float32), dtype=BF16)


def _layer(x, positions, w_in, g_norm, pe_ck, pe_cv, w_ck1, w_ck2, w_cv1, w_cv2,
           g_out_moba, g_out_nsa, w_out, g_final, *, nsa_tq, tm_in, tm_out, blocks_per_tile):
    B, S, _ = x.shape
    T = B * S
    x2d = x.reshape(T, D_MODEL)
    w_tiles = _weight_tiles(w_in.astype(BF16), blocks_per_tile)
    w_gate = _gate_weight(w_in)
    proj, gates, cos, sin, seg = _in_proj(x2d, g_norm.reshape(1, D_MODEL), w_tiles, w_gate,
                                          positions, tm=tm_in, blocks_per_tile=blocks_per_tile)

    n_seg = S // CMP_STRIDE
    seg = seg.reshape(2, NSA_GROUPS * B * n_seg, CMP_STRIDE * HEAD_DIM)
    pe = jnp.stack([pe_ck.reshape(1, -1), pe_cv.reshape(1, -1)])
    w1 = jnp.stack([w_ck1, w_cv1]).astype(BF16)
    w2 = jnp.stack([w_ck2, w_cv2]).astype(BF16)
    kvc = _compress(seg, pe, w1, w2)

    o_moba = _moba(proj, _block_onehot(S, MOBA_BLOCK), B=B, S=S)
    n_cmp = n_seg - CMP_BLOCK // CMP_STRIDE + 1
    q_rot, bias, o_c = _nsa_select(proj, cos, sin, kvc, _overlap_t(n_seg, n_cmp, S // SEL_BLOCK),
                                   B=B, S=S, tq=8 * nsa_tq)
    part = _nsa_sel(q_rot, bias, proj, _block_onehot(S, SEL_BLOCK), o_c, gates,
                    B=B, S=S, tq=nsa_tq)
    o_nsa = _nsa_win(q_rot, proj, part, gates, B=B, S=S, tq=nsa_tq)
    out = _out_proj(o_moba, o_nsa, proj,
                    g_out_moba.reshape(MOBA_HEADS, 1, LANES), g_out_nsa.reshape(NSA_HEADS, 1, LANES),
                    x2d, w_out.astype(BF16), g_final.reshape(1, D_MODEL), tm=tm_out)
    return out.reshape(B, S, D_MODEL)


def kernel(x, positions, w_in, g_norm, pe_ck, pe_cv, w_ck1, w_ck2, w_cv1, w_cv2,
           g_out_moba, g_out_nsa, w_out, g_final):
    assert w_in.shape[0] == 1, "single-layer problem"
    return _layer(x, positions, w_in, g_norm[0], pe_ck[0], pe_cv[0], w_ck1[0], w_ck2[0],
                  w_cv1[0], w_cv2[0], g_out_moba[0], g_out_nsa[0], w_out[0], g_final,
                  nsa_tq=256, tm_in=1024, tm_out=512, blocks_per_tile=10)
```

```python
import functools
from typing import Any, Callable, NamedTuple, Optional

import numpy as np
import jax
import jax.numpy as jnp
from jax import lax
from jax.experimental import pallas as pl
from jax.experimental.pallas import tpu as pltpu

F32 = jnp.float32
BF16 = jnp.bfloat16

D_MODEL = 2048
HEAD_DIM = 128
MOBA_HEADS = 8
NSA_HEADS = 8
NSA_GROUPS = 2
NSA_REP = 4
MOBA_BLOCK = 256
MOBA_TOPK = 3
CMP_BLOCK = 32
CMP_STRIDE = 16
CMP_HIDDEN = 256
SEL_BLOCK = 64
SEL_TOPK = 8
WINDOW = 512
ROPE_THETA = 10000.0
EPS = 1e-6
SCALE = HEAD_DIM ** -0.5
QK_PRESCALE = SCALE * float(np.log2(np.e))
NEG_BIG = -(2.0 ** 100)

LANES = 128
VMEM_LIMIT = 48 * 1024 * 1024
VMEM_LIMIT_BIG = 58 * 1024 * 1024
COLUMN_ORDER = (("mq", 8), ("mk", 8), ("nks", 2), ("nkw", 2), ("nkc", 2), ("nvc", 2),
                ("mv", 8), ("mz", 8), ("nq", 8), ("nz", 8), ("nvs", 2), ("nvw", 2))
BLK_MQ, BLK_MK, BLK_NKS, BLK_NKW, BLK_NKC, BLK_NVC = 0, 8, 16, 18, 20, 22
BLK_MV, BLK_MZ, BLK_NQ, BLK_NZ, BLK_NVS, BLK_NVW = 24, 32, 40, 48, 56, 58
N_BLOCKS = 60
N_ROPE_BLOCKS = 20
N_SEG_BLOCKS = 4

ROW_CHUNK = 64
MOBA_PV_ROWS = 256
HEADS_PER_PASS = 2
NORM_ROWS = 16
IN_SUB_ROWS = 256
MXU_COLS = 256
OUT_SUB_ROWS = 512
MOBA_TQ = 256
MOBA_HEADS_PER_STEP = 4
KEY_BLOCK = 256
WEIGHT_PREP_ROWS = 256


def _nt_dot(a, b):
    return lax.dot_general(a, b, (((1,), (1,)), ((), ())), preferred_element_type=F32)


def _sigmoid(x):
    return 1.0 / (1.0 + jnp.exp(-x))


def _rope(a, cos, sin_signed):
    return a * cos + pltpu.roll(a, HEAD_DIM // 2, axis=a.ndim - 1) * sin_signed


def _interleave(*streams):
    n = max(len(s) for s in streams)
    for k in range(n):
        for s in streams:
            for thunk in s[k * len(s) // n:(k + 1) * len(s) // n]:
                thunk()


def _block_kind(b):
    if BLK_NKC <= b < BLK_NKC + N_SEG_BLOCKS:
        return "seg"
    return "rope_q" if b < BLK_MK else "rope" if b < N_ROPE_BLOCKS else "plain"


def _project_columns(h, w_ref, col0, kinds, cos, sin, store, store_segments=None):
    n = len(kinds)
    acc = jnp.dot(h, w_ref[:, col0:col0 + n * LANES], preferred_element_type=F32)
    n_seg = 0
    for c, kind in enumerate(kinds):
        a = acc[:, c * LANES:(c + 1) * LANES]
        if kind == "rope_q":
            a = _rope(a, cos * QK_PRESCALE, sin * QK_PRESCALE)
        elif kind == "rope":
            a = _rope(a, cos, sin)
        elif kind == "seg":
            store_segments(n_seg, a)
            n_seg += 1
        store(c, a.astype(BF16))


def _in_proj_kernel(x_ref, g_ref, w_ref, wg_ref, pos_ref, invf_ref, sign_ref,
                    out_ref, gate_ref, cos_ref, sin_ref, seg_ref, h_scr, seg_scr,
                    *, tile_patterns):
    j = pl.program_id(1)
    tm = x_ref.shape[0]
    first_pattern = tile_patterns[0][0]
    assert tile_patterns[0][1] == [0]

    @pl.when(j == 0)
    def _():
        ang = pos_ref[...].astype(F32) * invf_ref[...]
        cos_ref[...] = jnp.cos(ang)
        sin_ref[...] = jnp.sin(ang) * sign_ref[...]

        def norm_piece(r0):
            rows = slice(r0, r0 + NORM_ROWS)
            x = x_ref[rows, :]
            ms = jnp.mean(x * x, axis=-1, keepdims=True)
            h_scr[rows, :] = (x * lax.rsqrt(ms + EPS) * g_ref[...]).astype(BF16)

        def norm_pieces(m):
            return [functools.partial(norm_piece, r0)
                    for r0 in range(m * IN_SUB_ROWS, (m + 1) * IN_SUB_ROWS, NORM_ROWS)]

        def matmul_pieces(m):
            rows = slice(m * IN_SUB_ROWS, (m + 1) * IN_SUB_ROWS)

            def gates():
                acc = jnp.dot(h_scr[rows, :], wg_ref[...], preferred_element_type=F32)
                for g in range(NSA_GROUPS):
                    gate_ref[g, rows, :] = acc[:, g * LANES:(g + 1) * LANES]

            def columns(c0):
                def store(c, val):
                    out_ref[c0 + c, rows, :] = val
                per = MXU_COLS // LANES
                _project_columns(h_scr[rows, :], w_ref, c0 * LANES, first_pattern[c0:c0 + per],
                                 cos_ref[rows, :], sin_ref[rows, :], store)

            return [gates] + [functools.partial(columns, c0)
                              for c0 in range(0, len(first_pattern), MXU_COLS // LANES)]

        n_sub = tm // IN_SUB_ROWS
        _interleave(norm_pieces(0))
        for m in range(n_sub):
            _interleave(matmul_pieces(m), norm_pieces(m + 1) if m + 1 < n_sub else [])

    for pattern, tiles in tile_patterns[1:]:
        @pl.when(functools.reduce(jnp.logical_or, [j == t for t in tiles]))
        def _(pattern=pattern):
            def store(c, val):
                out_ref[c] = val

            def store_segments(k, a):
                seg_scr[...] = a
                for t in range(CMP_STRIDE):
                    piece = seg_scr[pl.ds(t, tm // CMP_STRIDE, stride=CMP_STRIDE), :]
                    seg_ref[k, :, t * HEAD_DIM:(t + 1) * HEAD_DIM] = piece.astype(BF16)

            _project_columns(h_scr[...], w_ref, 0, pattern, cos_ref[...], sin_ref[...], store,
                             store_segments)


def _in_proj(x2d, g_norm, w_tiles, w_gate, positions, *, tm, blocks_per_tile):
    T = x2d.shape[0]
    assert T % tm == 0 and tm % IN_SUB_ROWS == 0 and N_BLOCKS % blocks_per_tile == 0
    tn = blocks_per_tile * LANES
    n_tiles = N_BLOCKS // blocks_per_tile
    by_pattern = {}
    for t in range(n_tiles):
        pattern = tuple(_block_kind(t * blocks_per_tile + c) for c in range(blocks_per_tile))
        by_pattern.setdefault(pattern, []).append(t)
    half = HEAD_DIM // 2
    inv_freq = 1.0 / (ROPE_THETA ** (jnp.arange(0, HEAD_DIM, 2, dtype=F32) / HEAD_DIM))
    invf = jnp.concatenate([inv_freq, inv_freq]).reshape(1, HEAD_DIM)
    sign = jnp.concatenate([-jnp.ones((half,), F32), jnp.ones((half,), F32)]).reshape(1, HEAD_DIM)
    kern = functools.partial(_in_proj_kernel, tile_patterns=tuple(by_pattern.items()))
    row_table = pl.BlockSpec((tm, HEAD_DIM), lambda i, j: (i, 0))
    return pl.pallas_call(
        kern,
        grid=(T // tm, n_tiles),
        in_specs=[pl.BlockSpec((tm, D_MODEL), lambda i, j: (i, 0)),
                  pl.BlockSpec((1, D_MODEL), lambda i, j: (0, 0)),
                  pl.BlockSpec((None, D_MODEL, tn), lambda i, j: (j, 0, 0)),
                  pl.BlockSpec((D_MODEL, NSA_GROUPS * LANES), lambda i, j: (0, 0)),
                  pl.BlockSpec((tm, 1), lambda i, j: (i, 0)),
                  pl.BlockSpec((1, HEAD_DIM), lambda i, j: (0, 0)),
                  pl.BlockSpec((1, HEAD_DIM), lambda i, j: (0, 0))],
        out_specs=[pl.BlockSpec((blocks_per_tile, tm, LANES), lambda i, j: (j, i, 0)),
                   pl.BlockSpec((NSA_GROUPS, tm, LANES), lambda i, j: (0, i, 0)),
                   row_table, row_table,
                   pl.BlockSpec((N_SEG_BLOCKS, tm // CMP_STRIDE, CMP_STRIDE * HEAD_DIM),
                                lambda i, j: (0, i, 0))],
        out_shape=[jax.ShapeDtypeStruct((N_BLOCKS, T, LANES), BF16),
                   jax.ShapeDtypeStruct((NSA_GROUPS, T, LANES), F32),
                   jax.ShapeDtypeStruct((T, HEAD_DIM), F32),
                   jax.ShapeDtypeStruct((T, HEAD_DIM), F32),
                   jax.ShapeDtypeStruct((N_SEG_BLOCKS, T // CMP_STRIDE, CMP_STRIDE * HEAD_DIM),
                                        BF16)],
        scratch_shapes=[pltpu.VMEM((tm, D_MODEL), BF16), pltpu.VMEM((tm, HEAD_DIM), F32)],
        compiler_params=pltpu.CompilerParams(
            dimension_semantics=("parallel", "arbitrary"), vmem_limit_bytes=VMEM_LIMIT_BIG),
        name="in_proj",
    )(x2d, g_norm, w_tiles, w_gate, positions.reshape(T, 1), invf, sign)


def _compress_kernel(seg_ref, pe_ref, w1_ref, w2_ref, out_ref):
    half = CMP_STRIDE * HEAD_DIM
    seg = seg_ref[0].astype(F32)
    pe = pe_ref[0]
    top = (seg + pe[:, :half]).astype(BF16)
    bot = (seg + pe[:, half:]).astype(BF16)
    a = jnp.dot(top, w1_ref[0, :half, :], preferred_element_type=F32)
    b = jnp.dot(bot, w1_ref[0, half:, :], preferred_element_type=F32)
    rows = a.shape[0]
    h = a + pltpu.roll(b, rows - 1, axis=0)
    hid = h * _sigmoid(h)
    out_ref[0] = jnp.dot(hid.astype(BF16), w2_ref[0], preferred_element_type=F32).astype(BF16)


def _compress(seg, pe, w1, w2):
    _, R, half = seg.shape
    return pl.pallas_call(
        _compress_kernel,
        grid=(2,),
        in_specs=[pl.BlockSpec((1, R, half), lambda c: (c, 0, 0)),
                  pl.BlockSpec((1, 1, 2 * half), lambda c: (c, 0, 0)),
                  pl.BlockSpec((1, 2 * half, CMP_HIDDEN), lambda c: (c, 0, 0)),
                  pl.BlockSpec((1, CMP_HIDDEN, HEAD_DIM), lambda c: (c, 0, 0))],
        out_specs=pl.BlockSpec((1, R, HEAD_DIM), lambda c: (c, 0, 0)),
        out_shape=jax.ShapeDtypeStruct((2, R, HEAD_DIM), BF16),
        compiler_params=pltpu.CompilerParams(
            dimension_semantics=("arbitrary",), vmem_limit_bytes=VMEM_LIMIT),
        name="compress",
    )(seg, pe, w1, w2)


def _select_bias_t(score_t, n_rows, n_keep):
    sub = 8
    n_groups = score_t.shape[0] // sub
    groups = [score_t[g * sub:(g + 1) * sub, :] for g in range(n_groups)]
    jrow = lax.broadcasted_iota(jnp.int32, groups[0].shape, 0)
    cnts = [jnp.zeros(groups[0].shape, jnp.int32) for _ in range(n_groups)]
    for jp in range(n_rows):
        row = score_t[jp:jp + 1, :]
        for g, grp in enumerate(groups):
            if g * sub > jp:
                beats = row >= grp
            elif g * sub + sub - 1 <= jp:
                beats = row > grp
            else:
                beats = (row > grp) | ((row == grp) & (jrow + g * sub > jp))
            cnts[g] = cnts[g] + beats.astype(jnp.int32)
    cnt = cnts[0] if n_groups == 1 else jnp.concatenate(cnts, axis=0)
    return (cnt < n_keep) & (score_t > -jnp.inf)


def _bias_t(keep_t):
    return jnp.where(keep_t, 0.0, NEG_BIG).astype(F32)


def _bias_columns(bias_t):
    rows, q = bias_t.shape
    if rows < LANES:
        bias_t = jnp.concatenate([bias_t, jnp.zeros((LANES - rows, q), F32)], axis=0)
    return bias_t.T.astype(BF16)


def _softmax_rows(s_blocks, r0, *, tq, q0, k0, window=None):
    per_block = KEY_BLOCK // LANES
    n_tiles = len(s_blocks) * per_block
    rows = slice(r0, r0 + ROW_CHUNK)
    qlo = q0 + r0 % tq
    qhi = qlo + ROW_CHUNK - 1

    def tile(t):
        lane0 = (t % per_block) * LANES
        return s_blocks[t // per_block][rows, lane0:lane0 + LANES]

    kinds = []
    for t in range(n_tiles):
        klo = k0 + t * LANES
        khi = klo + LANES - 1
        none = klo > qhi or (window is not None and khi <= qlo - window)
        full = khi <= qlo and (window is None or klo > qhi - window)
        kinds.append("none" if none else "full" if full else "part")
    mx = None
    masked = {}
    for t, kind in enumerate(kinds):
        if kind == "none":
            continue
        x = tile(t)
        if kind == "part":
            qpos = qlo + lax.broadcasted_iota(jnp.int32, x.shape, 0)
            kpos = k0 + t * LANES + lax.broadcasted_iota(jnp.int32, x.shape, 1)
            ok = kpos <= qpos
            if window is not None:
                ok = ok & (kpos > qpos - window)
            x = jnp.where(ok, x, NEG_BIG)
            masked[t] = x
        mx = x if mx is None else jnp.maximum(mx, x)
    m = jnp.broadcast_to(jnp.max(mx, axis=-1, keepdims=True), mx.shape)
    p_tiles = []
    for t, kind in enumerate(kinds):
        if kind == "none":
            p_tiles.append(jnp.zeros((ROW_CHUNK, LANES), BF16))
            continue
        x = masked[t] if kind == "part" else tile(t)
        p_tiles.append(jnp.exp2(x - m).astype(BF16))
    return jnp.concatenate(p_tiles, axis=1)


def _pv_normalized(p, v_ones):
    o = jnp.dot(p, v_ones, preferred_element_type=F32)
    return o[:, :HEAD_DIM] / o[:, HEAD_DIM:]


class _AttnPass(NamedTuple):
    q: Callable[[], jax.Array]
    k_ref: Any
    v_ref: Any
    k0: int
    width: int
    n_rows: int
    tq: int
    q0: int
    window: Optional[int]
    pv_rows: int
    store: Callable[[int, jax.Array], None]


def _run_attention(passes):
    def score_thunks(ps):
        q = ps.q()
        return [lambda j=j: _nt_dot(
            q, ps.k_ref[ps.k0 + j * KEY_BLOCK:ps.k0 + (j + 1) * KEY_BLOCK, :])
                for j in range(ps.width // KEY_BLOCK)]

    s_blocks = [thunk() for thunk in score_thunks(passes[0])]
    pending_pv = None
    for t, ps in enumerate(passes):
        mxu_work = [] if pending_pv is None else [pending_pv]
        n_pv = len(mxu_work)
        if t + 1 < len(passes):
            mxu_work += score_thunks(passes[t + 1])
        row_starts = list(range(0, ps.n_rows, ROW_CHUNK))
        results, p_rows = [], []
        for k in range(max(len(mxu_work), len(row_starts))):
            if k < len(mxu_work):
                results.append(mxu_work[k]())
            if k < len(row_starts):
                p_rows.append(_softmax_rows(s_blocks, row_starts[k], tq=ps.tq, q0=ps.q0,
                                            k0=ps.k0, window=ps.window))
        s_blocks = results[n_pv:]

        def pending_pv(ps=ps, p_rows=p_rows):
            per = ps.pv_rows // ROW_CHUNK
            for k in range(ps.n_rows // ps.pv_rows):
                p = jnp.concatenate(p_rows[k * per:(k + 1) * per], axis=0)
                ps.store(k * ps.pv_rows, _pv_normalized(p, ps.v_ref[ps.k0:ps.k0 + ps.width, :]))
    pending_pv()


def _moba_kernel(q_ref, k_ref, v_ref, onehot_ref, o_ref, kaug_ref, qaug_ref, kmean_ref,
                 vaug_ref, *, n_blk, k_top):
    n_heads, S, _ = k_ref.shape
    tq = MOBA_TQ

    def prepare(h):
        kaug_ref[h, :, :HEAD_DIM] = k_ref[h]
        kaug_ref[h, :, HEAD_DIM:] = onehot_ref[...]
        vaug_ref[h, :, :HEAD_DIM] = v_ref[h]
        vaug_ref[h, :, HEAD_DIM:] = jnp.ones((S, HEAD_DIM), BF16)
        kmean_ref[h] = jnp.zeros(kmean_ref.shape[1:], F32)
        for j in range(n_blk):
            kb = k_ref[h, j * MOBA_BLOCK:(j + 1) * MOBA_BLOCK, :].astype(F32)
            kmean_ref[h, j:j + 1, :] = jnp.sum(kb, axis=0, keepdims=True) * (1.0 / MOBA_BLOCK)

        q = q_ref[h]
        gate_t = _nt_dot(kmean_ref[h].astype(BF16), q)
        jrow = lax.broadcasted_iota(jnp.int32, gate_t.shape, 0)
        own = lax.broadcasted_iota(jnp.int32, gate_t.shape, 1) // MOBA_BLOCK
        gate_t = jnp.where((jrow < own) & jnp.isfinite(gate_t), gate_t, -jnp.inf)
        bias_t = _bias_t(_select_bias_t(gate_t, n_blk, k_top) | (jrow == own))
        qaug_ref[h, :, :HEAD_DIM] = q
        for c in range(S // tq):
            qaug_ref[h, c * tq:(c + 1) * tq, HEAD_DIM:] = _bias_columns(
                bias_t[:, c * tq:(c + 1) * tq])

    def tile_pass(h, c):
        q_rows = slice(c * tq, (c + 1) * tq)

        def store(row0, o):
            o_ref[h, c * tq + row0:c * tq + row0 + o.shape[0], :] = o.astype(BF16)

        return _AttnPass(q=lambda: qaug_ref[h, q_rows, :], k_ref=kaug_ref.at[h],
                         v_ref=vaug_ref.at[h], k0=0, width=(c + 1) * tq, n_rows=tq, tq=tq,
                         q0=c * tq, window=None, pv_rows=MOBA_PV_ROWS, store=store)

    for h in range(n_heads):
        prepare(h)
    _run_attention([tile_pass(h, c) for c in range(S // tq) for h in range(n_heads)])


def _moba(proj, onehot, *, B, S):
    T = B * S
    assert S % MOBA_TQ == 0 and S % KEY_BLOCK == 0 and MOBA_HEADS % MOBA_HEADS_PER_STEP == 0
    n_blk = S // MOBA_BLOCK
    k_top = min(MOBA_TOPK, n_blk - 1)
    nb8 = -(-n_blk // 8) * 8
    hs = MOBA_HEADS_PER_STEP
    kern = functools.partial(_moba_kernel, n_blk=n_blk, k_top=k_top)
    head_spec = lambda blk: pl.BlockSpec((hs, S, LANES), lambda b, h: (blk // hs + h, b, 0))
    return pl.pallas_call(
        kern,
        grid=(B, MOBA_HEADS // hs),
        in_specs=[head_spec(BLK_MQ), head_spec(BLK_MK), head_spec(BLK_MV),
                  pl.BlockSpec((S, LANES), lambda b, h: (0, 0))],
        out_specs=head_spec(0),
        out_shape=jax.ShapeDtypeStruct((MOBA_HEADS, T, LANES), BF16),
        scratch_shapes=[pltpu.VMEM((hs, S, 2 * LANES), BF16), pltpu.VMEM((hs, S, 2 * LANES), BF16),
                        pltpu.VMEM((hs, nb8, HEAD_DIM), F32),
                        pltpu.VMEM((hs, S, 2 * LANES), BF16)],
        compiler_params=pltpu.CompilerParams(
            dimension_semantics=("parallel", "parallel"), vmem_limit_bytes=VMEM_LIMIT),
        name="moba_attn",
    )(proj, proj, proj, onehot)


def _nsa_select_kernel(q_ref, cos_ref, sin_ref, kc_ref, vc_ref, ovt_ref,
                       qrot_ref, bias_ref, oc_ref, *, n_cmp, n_sel_blk, n_top):
    i = pl.program_id(2)
    R, tq, _ = q_ref.shape
    q_raw = q_ref[...]
    cos = (cos_ref[...] * QK_PRESCALE)[None]
    sin = (sin_ref[...] * QK_PRESCALE)[None]
    qrot_ref[...] = _rope(q_raw.astype(F32), cos, sin).astype(BF16)

    n_seg = kc_ref.shape[0]
    s_c = (_nt_dot(q_raw.reshape(R * tq, HEAD_DIM), kc_ref[...]) * SCALE).reshape(R, tq, n_seg)
    n_idx = lax.broadcasted_iota(jnp.int32, s_c.shape, 2)
    pos3 = i * tq + lax.broadcasted_iota(jnp.int32, s_c.shape, 1)
    m_c = (n_idx * CMP_STRIDE + CMP_BLOCK - 1 <= pos3) & (n_idx < n_cmp)
    s_c = jnp.where(m_c, s_c, -jnp.inf)
    mx = jnp.max(s_c, axis=-1, keepdims=True)
    mx = jnp.where(jnp.isfinite(mx), mx, 0.0)
    e_c = jnp.where(m_c, jnp.exp(s_c - mx), 0.0)
    p_c = e_c * (1.0 / jnp.maximum(jnp.sum(e_c, axis=-1, keepdims=True), 1e-30))
    o_c = jnp.dot(p_c.reshape(R * tq, n_seg).astype(BF16), vc_ref[...],
                  preferred_element_type=F32).reshape(R, tq, HEAD_DIM)
    oc_ref[...] = o_c.astype(BF16)

    p_sum = jnp.sum(p_c, axis=0)
    p_hi = p_sum.astype(BF16)
    p_lo = (p_sum - p_hi.astype(F32)).astype(BF16)
    ovt = ovt_ref[...]
    imp_t = _nt_dot(ovt, p_hi) + _nt_dot(ovt, p_lo)
    jrow = lax.broadcasted_iota(jnp.int32, imp_t.shape, 0)
    posq = i * tq + lax.broadcasted_iota(jnp.int32, imp_t.shape, 1)
    own = posq // SEL_BLOCK
    forced = (jrow == 0) | (jrow == own) | (jrow == own - 1)
    future = jrow * SEL_BLOCK > posq
    score_t = jnp.where(future, -jnp.inf, jnp.where(forced, jnp.inf, imp_t))
    keep_t = _select_bias_t(score_t, n_sel_blk, n_top)
    bias_ref[...] = _bias_columns(_bias_t(keep_t))


def _nsa_select(proj, cos, sin, kvc, ovt, *, B, S, tq):
    T = B * S
    assert S % tq == 0
    nq = S // tq
    n_seg = S // CMP_STRIDE
    n_cmp = n_seg - CMP_BLOCK // CMP_STRIDE + 1
    n_sel_blk = S // SEL_BLOCK
    R = NSA_REP
    kern = functools.partial(_nsa_select_kernel, n_cmp=n_cmp, n_sel_blk=n_sel_blk,
                             n_top=min(SEL_TOPK, n_sel_blk))
    heads_spec = pl.BlockSpec((R, tq, LANES), lambda b, g, i: (g, b * nq + i, 0))
    return pl.pallas_call(
        kern,
        grid=(B, NSA_GROUPS, nq),
        in_specs=[pl.BlockSpec((R, tq, LANES), lambda b, g, i: (BLK_NQ // R + g, b * nq + i, 0)),
                  pl.BlockSpec((tq, LANES), lambda b, g, i: (b * nq + i, 0)),
                  pl.BlockSpec((tq, LANES), lambda b, g, i: (b * nq + i, 0)),
                  pl.BlockSpec((None, n_seg, LANES), lambda b, g, i: (0, g * B + b, 0)),
                  pl.BlockSpec((None, n_seg, LANES), lambda b, g, i: (1, g * B + b, 0)),
                  pl.BlockSpec(ovt.shape, lambda b, g, i: (0, 0))],
        out_specs=[heads_spec,
                   pl.BlockSpec((None, tq, LANES), lambda b, g, i: (g, b * nq + i, 0)),
                   heads_spec],
        out_shape=[jax.ShapeDtypeStruct((NSA_HEADS, T, LANES), BF16),
                   jax.ShapeDtypeStruct((NSA_GROUPS, T, LANES), BF16),
                   jax.ShapeDtypeStruct((NSA_HEADS, T, LANES), BF16)],
        compiler_params=pltpu.CompilerParams(
            dimension_semantics=("parallel", "parallel", "parallel"),
            vmem_limit_bytes=VMEM_LIMIT),
        name="nsa_select",
    )(proj, cos, sin, kvc, kvc, ovt)


def _nsa_sel_kernel(qrot_ref, bias_ref, ks_ref, vs_ref, onehot_ref, oc_ref, gate_ref, o_ref,
                    ksaug_ref, vaug_ref, *, tq):
    R, S, _ = qrot_ref.shape
    ksaug_ref[:, :HEAD_DIM] = ks_ref[...]
    ksaug_ref[:, HEAD_DIM:] = onehot_ref[...]
    vaug_ref[:, :HEAD_DIM] = vs_ref[...]
    vaug_ref[:, HEAD_DIM:] = jnp.ones((S, HEAD_DIM), BF16)

    def head_pass(c, heads):
        q_rows = slice(c * tq, (c + 1) * tq)

        def q_aug():
            bias = bias_ref[q_rows, :]
            return jnp.concatenate(
                [jnp.concatenate([qrot_ref[r, q_rows, :], bias], axis=1) for r in heads], axis=0)

        def store(row0, o):
            gt = _sigmoid(gate_ref[q_rows, :])
            for k, r in enumerate(heads):
                part = (gt[:, 3 * r:3 * r + 1] * oc_ref[r, q_rows, :].astype(F32)
                        + gt[:, 3 * r + 1:3 * r + 2] * o[k * tq:(k + 1) * tq, :])
                o_ref[r, q_rows, :] = part.astype(BF16)

        return _AttnPass(q=q_aug, k_ref=ksaug_ref, v_ref=vaug_ref, k0=0, width=(c + 1) * tq,
                         n_rows=len(heads) * tq, tq=tq, q0=c * tq, window=None,
                         pv_rows=len(heads) * tq, store=store)

    _run_attention([head_pass(c, range(r0, r0 + HEADS_PER_PASS))
                    for c in range(S // tq) for r0 in range(0, R, HEADS_PER_PASS)])


def _nsa_sel(q_rot, bias, proj, onehot, o_c, gates, *, B, S, tq):
    T = B * S
    assert S % tq == 0 and tq % KEY_BLOCK == 0
    R = NSA_REP
    heads_spec = pl.BlockSpec((R, S, LANES), lambda b, g: (g, b, 0))
    group_spec = pl.BlockSpec((None, S, LANES), lambda b, g: (g, b, 0))
    kv_spec = lambda blk: pl.BlockSpec((None, S, LANES), lambda b, g: (blk + g, b, 0))
    return pl.pallas_call(
        functools.partial(_nsa_sel_kernel, tq=tq),
        grid=(B, NSA_GROUPS),
        in_specs=[heads_spec, group_spec,
                  kv_spec(BLK_NKS), kv_spec(BLK_NVS),
                  pl.BlockSpec((S, LANES), lambda b, g: (0, 0)),
                  heads_spec, group_spec],
        out_specs=heads_spec,
        out_shape=jax.ShapeDtypeStruct((NSA_HEADS, T, LANES), BF16),
        scratch_shapes=[pltpu.VMEM((S, 2 * LANES), BF16), pltpu.VMEM((S, 2 * LANES), BF16)],
        compiler_params=pltpu.CompilerParams(
            dimension_semantics=("parallel", "parallel"), vmem_limit_bytes=VMEM_LIMIT_BIG),
        name="nsa_sel_attn",
    )(q_rot, bias, proj, proj, onehot, o_c, gates)


def _nsa_win_kernel(qrot_ref, kw_ref, vw_ref, part_ref, gate_ref, o_ref, vaug_ref, *, tq):
    R, S, _ = qrot_ref.shape
    vaug_ref[:, :HEAD_DIM] = vw_ref[...]
    vaug_ref[:, HEAD_DIM:] = jnp.ones((S, HEAD_DIM), BF16)

    def head_pass(c, heads):
        q_rows = slice(c * tq, (c + 1) * tq)
        k0 = max(0, c * tq - WINDOW)

        def store(row0, o_w):
            gt = _sigmoid(gate_ref[q_rows, :])
            for k, r in enumerate(heads):
                o = (part_ref[r, q_rows, :].astype(F32)
                     + gt[:, 3 * r + 2:3 * r + 3] * o_w[k * tq:(k + 1) * tq, :])
                o_ref[r, q_rows, :] = o.astype(BF16)

        return _AttnPass(
            q=lambda: jnp.concatenate([qrot_ref[r, q_rows, :] for r in heads], axis=0),
            k_ref=kw_ref, v_ref=vaug_ref, k0=k0, width=(c + 1) * tq - k0,
            n_rows=len(heads) * tq, tq=tq, q0=c * tq, window=WINDOW,
            pv_rows=len(heads) * tq, store=store)

    _run_attention([head_pass(c, range(r0, r0 + HEADS_PER_PASS))
                    for c in range(S // tq) for r0 in range(0, R, HEADS_PER_PASS)])


def _nsa_win(q_rot, proj, part, gates, *, B, S, tq):
    T = B * S
    assert S % tq == 0 and tq % KEY_BLOCK == 0 and WINDOW % KEY_BLOCK == 0
    R = NSA_REP
    heads_spec = pl.BlockSpec((R, S, LANES), lambda b, g: (g, b, 0))
    kv_spec = lambda blk: pl.BlockSpec((None, S, LANES), lambda b, g: (blk + g, b, 0))
    return pl.pallas_call(
        functools.partial(_nsa_win_kernel, tq=tq),
        grid=(B, NSA_GROUPS),
        in_specs=[heads_spec, kv_spec(BLK_NKW), kv_spec(BLK_NVW), heads_spec,
                  pl.BlockSpec((None, S, LANES), lambda b, g: (g, b, 0))],
        out_specs=heads_spec,
        out_shape=jax.ShapeDtypeStruct((NSA_HEADS, T, LANES), BF16),
        scratch_shapes=[pltpu.VMEM((S, 2 * LANES), BF16)],
        compiler_params=pltpu.CompilerParams(
            dimension_semantics=("parallel", "parallel"), vmem_limit_bytes=VMEM_LIMIT_BIG),
        name="nsa_win_attn",
    )(q_rot, proj, proj, part, gates)


def _out_proj_kernel(om_ref, on_ref, zm_ref, zn_ref, gm_ref, gn_ref, x_ref, w_ref, gf_ref,
                     out_ref):
    def gated_norm(o_ref, z_ref, g_ref, rows):
        n_heads = o_ref.shape[0]
        acts = []
        ss = None
        for h in range(n_heads):
            z = z_ref[h, rows, :].astype(F32)
            a = o_ref[h, rows, :].astype(F32) * (z * _sigmoid(z))
            acts.append(a)
            ss = a * a if ss is None else ss + a * a
        ms = jnp.sum(ss, axis=-1, keepdims=True) * (1.0 / (n_heads * LANES))
        inv = lax.rsqrt(ms + EPS)
        return [(acts[h] * inv * g_ref[h]).astype(BF16) for h in range(n_heads)]

    for r0 in range(0, x_ref.shape[0], OUT_SUB_ROWS):
        rows = slice(r0, r0 + OUT_SUB_ROWS)
        y = jnp.concatenate(gated_norm(om_ref, zm_ref, gm_ref, rows)
                            + gated_norm(on_ref, zn_ref, gn_ref, rows), axis=1)
        r = x_ref[rows, :] + jnp.dot(y, w_ref[...], preferred_element_type=F32)
        ms = jnp.mean(r * r, axis=-1, keepdims=True)
        out_ref[rows, :] = r * lax.rsqrt(ms + EPS) * gf_ref[...]


def _out_proj(o_moba, o_nsa, proj, g_moba, g_nsa, x2d, w_out, g_final, *, tm):
    T = x2d.shape[0]
    assert T % tm == 0 and tm % OUT_SUB_ROWS == 0
    H = MOBA_HEADS
    return pl.pallas_call(
        _out_proj_kernel,
        grid=(T // tm,),
        in_specs=[pl.BlockSpec((H, tm, LANES), lambda i: (0, i, 0)),
                  pl.BlockSpec((H, tm, LANES), lambda i: (0, i, 0)),
                  pl.BlockSpec((H, tm, LANES), lambda i: (BLK_MZ // H, i, 0)),
                  pl.BlockSpec((H, tm, LANES), lambda i: (BLK_NZ // H, i, 0)),
                  pl.BlockSpec((H, 1, LANES), lambda i: (0, 0, 0)),
                  pl.BlockSpec((H, 1, LANES), lambda i: (0, 0, 0)),
                  pl.BlockSpec((tm, D_MODEL), lambda i: (i, 0)),
                  pl.BlockSpec((D_MODEL, D_MODEL), lambda i: (0, 0), pipeline_mode=pl.Buffered(1)),
                  pl.BlockSpec((1, D_MODEL), lambda i: (0, 0))],
        out_specs=pl.BlockSpec((tm, D_MODEL), lambda i: (i, 0)),
        out_shape=jax.ShapeDtypeStruct((T, D_MODEL), F32),
        compiler_params=pltpu.CompilerParams(
            dimension_semantics=("parallel",), vmem_limit_bytes=VMEM_LIMIT_BIG),
        name="out_proj",
    )(o_moba, o_nsa, proj, proj, g_moba, g_nsa, x2d, w_out, g_final)


def _w_in_offsets():
    mw, nw, kw = MOBA_HEADS * HEAD_DIM, NSA_HEADS * HEAD_DIM, NSA_GROUPS * HEAD_DIM
    sizes = [mw] * 4 + [nw] + [kw] * 6 + [3 * NSA_HEADS, nw]
    names = ["mq", "mk", "mv", "mz", "nq", "nkc", "nvc", "nks", "nvs", "nkw", "nvw", "ng", "nz"]
    offs = np.concatenate([[0], np.cumsum(sizes)])
    return {n: (int(offs[k]), int(sizes[k])) for k, n in enumerate(names)}


def _weight_tiles_kernel(w_ref, out_ref, *, src_cols, blocks_per_tile):
    for b, src in enumerate(src_cols):
        lane0 = (b % blocks_per_tile) * LANES
        out_ref[b // blocks_per_tile, :, lane0:lane0 + LANES] = (
            w_ref[:, src:src + LANES].astype(BF16))


def _weight_tiles(w_in, blocks_per_tile):
    offsets = _w_in_offsets()
    src_cols = []
    for name, n_blocks in COLUMN_ORDER:
        off, size = offsets[name]
        assert size == n_blocks * LANES
        src_cols += [off + k * LANES for k in range(n_blocks)]
    n_tiles = N_BLOCKS // blocks_per_tile
    rows = WEIGHT_PREP_ROWS
    kern = functools.partial(_weight_tiles_kernel, src_cols=tuple(src_cols),
                             blocks_per_tile=blocks_per_tile)
    return pl.pallas_call(
        kern,
        grid=(D_MODEL // rows,),
        in_specs=[pl.BlockSpec((None, rows, w_in.shape[2]), lambda r: (0, r, 0))],
        out_specs=pl.BlockSpec((n_tiles, rows, blocks_per_tile * LANES), lambda r: (0, r, 0)),
        out_shape=jax.ShapeDtypeStruct((n_tiles, D_MODEL, blocks_per_tile * LANES), BF16),
        compiler_params=pltpu.CompilerParams(
            dimension_semantics=("parallel",), vmem_limit_bytes=VMEM_LIMIT),
        name="weight_tiles",
    )(w_in)


def _gate_weight(w_in):
    off, size = _w_in_offsets()["ng"]
    per_group = 3 * NSA_REP
    wg = w_in[0, :, off:off + size].reshape(D_MODEL, NSA_GROUPS, per_group)
    wg = jnp.pad(wg, ((0, 0), (0, 0), (0, LANES - per_group)))
    return wg.reshape(D_MODEL, NSA_GROUPS * LANES).astype(BF16)


def _block_onehot(S, block):
    ids = np.arange(S)[:, None] // block
    return jnp.asarray((ids == np.arange(LANES)[None, :]).astype(np.float32), dtype=BF16)


def _overlap_t(n_seg, n_cmp, n_sel_blk):
    cs = np.arange(n_seg)[None, :] * CMP_STRIDE
    ss = np.arange(n_sel_blk)[:, None] * SEL_BLOCK
    ov = (cs < ss + SEL_BLOCK) & (cs + CMP_BLOCK > ss) & (np.arange(n_seg)[None, :] < n_cmp)
    return jnp.asarray(ov.astype(np.float32), dtype=BF16)


def _layer(x, positions, w_in, g_norm, pe_ck, pe_cv, w_ck1, w_ck2, w_cv1, w_cv2,
           g_out_moba, g_out_nsa, w_out, g_final, *, nsa_tq, tm_in, tm_out, blocks_per_tile):
    B, S, _ = x.shape
    T = B * S
    x2d = x.reshape(T, D_MODEL)
    w_tiles = _weight_tiles(w_in.astype(BF16), blocks_per_tile)
    w_gate = _gate_weight(w_in)
    proj, gates, cos, sin, seg = _in_proj(x2d, g_norm.reshape(1, D_MODEL), w_tiles, w_gate,
                                          positions, tm=tm_in, blocks_per_tile=blocks_per_tile)

    n_seg = S // CMP_STRIDE
    seg = seg.reshape(2, NSA_GROUPS * B * n_seg, CMP_STRIDE * HEAD_DIM)
    pe = jnp.stack([pe_ck.reshape(1, -1), pe_cv.reshape(1, -1)])
    w1 = jnp.stack([w_ck1, w_cv1]).astype(BF16)
    w2 = jnp.stack([w_ck2, w_cv2]).astype(BF16)
    kvc = _compress(seg, pe, w1, w2)

    o_moba = _moba(proj, _block_onehot(S, MOBA_BLOCK), B=B, S=S)
    n_cmp = n_seg - CMP_BLOCK // CMP_STRIDE + 1
    q_rot, bias, o_c = _nsa_select(proj, cos, sin, kvc, _overlap_t(n_seg, n_cmp, S // SEL_BLOCK),
                                   B=B, S=S, tq=4 * nsa_tq)
    part = _nsa_sel(q_rot, bias, proj, _block_onehot(S, SEL_BLOCK), o_c, gates,
                    B=B, S=S, tq=nsa_tq)
    o_nsa = _nsa_win(q_rot, proj, part, gates, B=B, S=S, tq=nsa_tq)
    out = _out_proj(o_moba, o_nsa, proj,
                    g_out_moba.reshape(MOBA_HEADS, 1, LANES), g_out_nsa.reshape(NSA_HEADS, 1, LANES),
                    x2d, w_out.astype(BF16), g_final.reshape(1, D_MODEL), tm=tm_out)
    return out.reshape(B, S, D_MODEL)


def kernel(x, positions, w_in, g_norm, pe_ck, pe_cv, w_ck1, w_ck2, w_cv1, w_cv2,
           g_out_moba, g_out_nsa, w_out, g_final):
    assert w_in.shape[0] == 1, "single-layer problem"
    return _layer(x, positions, w_in, g_norm[0], pe_ck[0], pe_cv[0], w_ck1[0], w_ck2[0],
                  w_cv1[0], w_cv2[0], g_out_moba[0], g_out_nsa[0], w_out[0], g_final,
                  nsa_tq=256, tm_in=1024, tm_out=512, blocks_per_tile=10)
```

```python
import functools
from typing import Any, Callable, NamedTuple, Optional

import numpy as np
import jax
import jax.numpy as jnp
from jax import lax
from jax.experimental import pallas as pl
from jax.experimental.pallas import tpu as pltpu

F32 = jnp.float32
BF16 = jnp.bfloat16

D_MODEL = 2048
HEAD_DIM = 128
MOBA_HEADS = 8
NSA_HEADS = 8
NSA_GROUPS = 2
NSA_REP = 4
MOBA_BLOCK = 256
MOBA_TOPK = 3
CMP_BLOCK = 32
CMP_STRIDE = 16
CMP_HIDDEN = 256
SEL_BLOCK = 64
SEL_TOPK = 8
WINDOW = 512
ROPE_THETA = 10000.0
EPS = 1e-6
SCALE = HEAD_DIM ** -0.5
QK_PRESCALE = SCALE * float(np.log2(np.e))
NEG_BIG = -(2.0 ** 100)

LANES = 128
VMEM_LIMIT = 48 * 1024 * 1024
VMEM_LIMIT_BIG = 58 * 1024 * 1024
COLUMN_ORDER = (("mq", 8), ("mk", 8), ("nks", 2), ("nkw", 2), ("nkc", 2), ("nvc", 2),
                ("mv", 8), ("mz", 8), ("nq", 8), ("nz", 8), ("nvs", 2), ("nvw", 2))
BLK_MQ, BLK_MK, BLK_NKS, BLK_NKW, BLK_NKC, BLK_NVC = 0, 8, 16, 18, 20, 22
BLK_MV, BLK_MZ, BLK_NQ, BLK_NZ, BLK_NVS, BLK_NVW = 24, 32, 40, 48, 56, 58
N_BLOCKS = 60
N_ROPE_BLOCKS = 20
N_SEG_BLOCKS = 4

ROW_CHUNK = 64
MOBA_PV_ROWS = 256
HEADS_PER_PASS = 2
NORM_ROWS = 16
IN_SUB_ROWS = 256
MXU_COLS = 256
OUT_SUB_ROWS = 512
MOBA_TQ = 256
MOBA_HEADS_PER_STEP = 4
KEY_BLOCK = 256
WEIGHT_PREP_ROWS = 256


def _nt_dot(a, b):
    return lax.dot_general(a, b, (((1,), (1,)), ((), ())), preferred_element_type=F32)


def _sigmoid(x):
    return 1.0 / (1.0 + jnp.exp(-x))


def _rope(a, cos, sin_signed):
    return a * cos + pltpu.roll(a, HEAD_DIM // 2, axis=a.ndim - 1) * sin_signed


def _interleave(*streams):
    n = max(len(s) for s in streams)
    for k in range(n):
        for s in streams:
            for thunk in s[k * len(s) // n:(k + 1) * len(s) // n]:
                thunk()


def _block_kind(b):
    if BLK_NKC <= b < BLK_NKC + N_SEG_BLOCKS:
        return "seg"
    return "rope_q" if b < BLK_MK else "rope" if b < N_ROPE_BLOCKS else "plain"


def _project_columns(h, w_ref, col0, kinds, cos, sin, store, store_segments=None):
    n = len(kinds)
    acc = jnp.dot(h, w_ref[:, col0:col0 + n * LANES], preferred_element_type=F32)
    n_seg = 0
    for c, kind in enumerate(kinds):
        a = acc[:, c * LANES:(c + 1) * LANES]
        if kind == "rope_q":
            a = _rope(a, cos * QK_PRESCALE, sin * QK_PRESCALE)
        elif kind == "rope":
            a = _rope(a, cos, sin)
        elif kind == "seg":
            store_segments(n_seg, a)
            n_seg += 1
        store(c, a.astype(BF16))


def _in_proj_kernel(x_ref, g_ref, w_ref, wg_ref, pos_ref, invf_ref, sign_ref,
                    out_ref, gate_ref, cos_ref, sin_ref, seg_ref, h_scr, seg_scr,
                    *, tile_patterns):
    j = pl.program_id(1)
    tm = x_ref.shape[0]
    first_pattern = tile_patterns[0][0]
    assert tile_patterns[0][1] == [0]

    @pl.when(j == 0)
    def _():
        ang = pos_ref[...].astype(F32) * invf_ref[...]
        cos_ref[...] = jnp.cos(ang)
        sin_ref[...] = jnp.sin(ang) * sign_ref[...]

        def norm_piece(r0):
            rows = slice(r0, r0 + NORM_ROWS)
            x = x_ref[rows, :]
            ms = jnp.mean(x * x, axis=-1, keepdims=True)
            h_scr[rows, :] = (x * lax.rsqrt(ms + EPS) * g_ref[...]).astype(BF16)

        def norm_pieces(m):
            return [functools.partial(norm_piece, r0)
                    for r0 in range(m * IN_SUB_ROWS, (m + 1) * IN_SUB_ROWS, NORM_ROWS)]

        def matmul_pieces(m):
            rows = slice(m * IN_SUB_ROWS, (m + 1) * IN_SUB_ROWS)

            def gates():
                acc = jnp.dot(h_scr[rows, :], wg_ref[...], preferred_element_type=F32)
                for g in range(NSA_GROUPS):
                    gate_ref[g, rows, :] = acc[:, g * LANES:(g + 1) * LANES]

            def columns(c0):
                def store(c, val):
                    out_ref[c0 + c, rows, :] = val
                per = MXU_COLS // LANES
                _project_columns(h_scr[rows, :], w_ref, c0 * LANES, first_pattern[c0:c0 + per],
                                 cos_ref[rows, :], sin_ref[rows, :], store)

            return [gates] + [functools.partial(columns, c0)
                              for c0 in range(0, len(first_pattern), MXU_COLS // LANES)]

        n_sub = tm // IN_SUB_ROWS
        _interleave(norm_pieces(0))
        for m in range(n_sub):
            _interleave(matmul_pieces(m), norm_pieces(m + 1) if m + 1 < n_sub else [])

    for pattern, tiles in tile_patterns[1:]:
        @pl.when(functools.reduce(jnp.logical_or, [j == t for t in tiles]))
        def _(pattern=pattern):
            def store(c, val):
                out_ref[c] = val

            def store_segments(k, a):
                seg_scr[...] = a
                for t in range(CMP_STRIDE):
                    piece = seg_scr[pl.ds(t, tm // CMP_STRIDE, stride=CMP_STRIDE), :]
                    seg_ref[k, :, t * HEAD_DIM:(t + 1) * HEAD_DIM] = piece.astype(BF16)

            _project_columns(h_scr[...], w_ref, 0, pattern, cos_ref[...], sin_ref[...], store,
                             store_segments)


def _in_proj(x2d, g_norm, w_tiles, w_gate, positions, *, tm, blocks_per_tile):
    T = x2d.shape[0]
    assert T % tm == 0 and tm % IN_SUB_ROWS == 0 and N_BLOCKS % blocks_per_tile == 0
    tn = blocks_per_tile * LANES
    n_tiles = N_BLOCKS // blocks_per_tile
    by_pattern = {}
    for t in range(n_tiles):
        pattern = tuple(_block_kind(t * blocks_per_tile + c) for c in range(blocks_per_tile))
        by_pattern.setdefault(pattern, []).append(t)
    half = HEAD_DIM // 2
    inv_freq = 1.0 / (ROPE_THETA ** (jnp.arange(0, HEAD_DIM, 2, dtype=F32) / HEAD_DIM))
    invf = jnp.concatenate([inv_freq, inv_freq]).reshape(1, HEAD_DIM)
    sign = jnp.concatenate([-jnp.ones((half,), F32), jnp.ones((half,), F32)]).reshape(1, HEAD_DIM)
    kern = functools.partial(_in_proj_kernel, tile_patterns=tuple(by_pattern.items()))
    row_table = pl.BlockSpec((tm, HEAD_DIM), lambda i, j: (i, 0))
    return pl.pallas_call(
        kern,
        grid=(T // tm, n_tiles),
        in_specs=[pl.BlockSpec((tm, D_MODEL), lambda i, j: (i, 0)),
                  pl.BlockSpec((1, D_MODEL), lambda i, j: (0, 0)),
                  pl.BlockSpec((None, D_MODEL, tn), lambda i, j: (j, 0, 0)),
                  pl.BlockSpec((D_MODEL, NSA_GROUPS * LANES), lambda i, j: (0, 0)),
                  pl.BlockSpec((tm, 1), lambda i, j: (i, 0)),
                  pl.BlockSpec((1, HEAD_DIM), lambda i, j: (0, 0)),
                  pl.BlockSpec((1, HEAD_DIM), lambda i, j: (0, 0))],
        out_specs=[pl.BlockSpec((blocks_per_tile, tm, LANES), lambda i, j: (j, i, 0)),
                   pl.BlockSpec((NSA_GROUPS, tm, LANES), lambda i, j: (0, i, 0)),
                   row_table, row_table,
                   pl.BlockSpec((N_SEG_BLOCKS, tm // CMP_STRIDE, CMP_STRIDE * HEAD_DIM),
                                lambda i, j: (0, i, 0))],
        out_shape=[jax.ShapeDtypeStruct((N_BLOCKS, T, LANES), BF16),
                   jax.ShapeDtypeStruct((NSA_GROUPS, T, LANES), F32),
                   jax.ShapeDtypeStruct((T, HEAD_DIM), F32),
                   jax.ShapeDtypeStruct((T, HEAD_DIM), F32),
                   jax.ShapeDtypeStruct((N_SEG_BLOCKS, T // CMP_STRIDE, CMP_STRIDE * HEAD_DIM),
                                        BF16)],
        scratch_shapes=[pltpu.VMEM((tm, D_MODEL), BF16), pltpu.VMEM((tm, HEAD_DIM), F32)],
        compiler_params=pltpu.CompilerParams(
            dimension_semantics=("parallel", "arbitrary"), vmem_limit_bytes=VMEM_LIMIT_BIG),
        name="in_proj",
    )(x2d, g_norm, w_tiles, w_gate, positions.reshape(T, 1), invf, sign)


def _compress_kernel(seg_ref, pe_ref, w1_ref, w2_ref, out_ref):
    half = CMP_STRIDE * HEAD_DIM
    seg = seg_ref[0].astype(F32)
    pe = pe_ref[0]
    top = (seg + pe[:, :half]).astype(BF16)
    bot = (seg + pe[:, half:]).astype(BF16)
    a = jnp.dot(top, w1_ref[0, :half, :], preferred_element_type=F32)
    b = jnp.dot(bot, w1_ref[0, half:, :], preferred_element_type=F32)
    rows = a.shape[0]
    h = a + pltpu.roll(b, rows - 1, axis=0)
    hid = h * _sigmoid(h)
    out_ref[0] = jnp.dot(hid.astype(BF16), w2_ref[0], preferred_element_type=F32).astype(BF16)


def _compress(seg, pe, w1, w2):
    _, R, half = seg.shape
    return pl.pallas_call(
        _compress_kernel,
        grid=(2,),
        in_specs=[pl.BlockSpec((1, R, half), lambda c: (c, 0, 0)),
                  pl.BlockSpec((1, 1, 2 * half), lambda c: (c, 0, 0)),
                  pl.BlockSpec((1, 2 * half, CMP_HIDDEN), lambda c: (c, 0, 0)),
                  pl.BlockSpec((1, CMP_HIDDEN, HEAD_DIM), lambda c: (c, 0, 0))],
        out_specs=pl.BlockSpec((1, R, HEAD_DIM), lambda c: (c, 0, 0)),
        out_shape=jax.ShapeDtypeStruct((2, R, HEAD_DIM), BF16),
        compiler_params=pltpu.CompilerParams(
            dimension_semantics=("arbitrary",), vmem_limit_bytes=VMEM_LIMIT),
        name="compress",
    )(seg, pe, w1, w2)


def _select_bias_t(score_t, n_rows, n_keep):
    sub = 8
    n_groups = score_t.shape[0] // sub
    groups = [score_t[g * sub:(g + 1) * sub, :] for g in range(n_groups)]
    jrow = lax.broadcasted_iota(jnp.int32, groups[0].shape, 0)
    cnts = [jnp.zeros(groups[0].shape, jnp.int32) for _ in range(n_groups)]
    for jp in range(n_rows):
        row = score_t[jp:jp + 1, :]
        for g, grp in enumerate(groups):
            if g * sub > jp:
                beats = row >= grp
            elif g * sub + sub - 1 <= jp:
                beats = row > grp
            else:
                beats = (row > grp) | ((row == grp) & (jrow + g * sub > jp))
            cnts[g] = cnts[g] + beats.astype(jnp.int32)
    cnt = cnts[0] if n_groups == 1 else jnp.concatenate(cnts, axis=0)
    return (cnt < n_keep) & (score_t > -jnp.inf)


def _bias_t(keep_t):
    return jnp.where(keep_t, 0.0, NEG_BIG).astype(F32)


def _bias_columns(bias_t):
    rows, q = bias_t.shape
    if rows < LANES:
        bias_t = jnp.concatenate([bias_t, jnp.zeros((LANES - rows, q), F32)], axis=0)
    return bias_t.T.astype(BF16)


def _softmax_rows(s_blocks, r0, *, tq, q0, k0, window=None):
    per_block = KEY_BLOCK // LANES
    n_tiles = len(s_blocks) * per_block
    rows = slice(r0, r0 + ROW_CHUNK)
    qlo = q0 + r0 % tq
    qhi = qlo + ROW_CHUNK - 1

    def tile(t):
        lane0 = (t % per_block) * LANES
        return s_blocks[t // per_block][rows, lane0:lane0 + LANES]

    kinds = []
    for t in range(n_tiles):
        klo = k0 + t * LANES
        khi = klo + LANES - 1
        none = klo > qhi or (window is not None and khi <= qlo - window)
        full = khi <= qlo and (window is None or klo > qhi - window)
        kinds.append("none" if none else "full" if full else "part")
    mx = None
    masked = {}
    for t, kind in enumerate(kinds):
        if kind == "none":
            continue
        x = tile(t)
        if kind == "part":
            qpos = qlo + lax.broadcasted_iota(jnp.int32, x.shape, 0)
            kpos = k0 + t * LANES + lax.broadcasted_iota(jnp.int32, x.shape, 1)
            ok = kpos <= qpos
            if window is not None:
                ok = ok & (kpos > qpos - window)
            x = jnp.where(ok, x, NEG_BIG)
            masked[t] = x
        mx = x if mx is None else jnp.maximum(mx, x)
    m = jnp.broadcast_to(jnp.max(mx, axis=-1, keepdims=True), mx.shape)
    p_tiles = []
    for t, kind in enumerate(kinds):
        if kind == "none":
            p_tiles.append(jnp.zeros((ROW_CHUNK, LANES), BF16))
            continue
        x = masked[t] if kind == "part" else tile(t)
        p_tiles.append(jnp.exp2(x - m).astype(BF16))
    return jnp.concatenate(p_tiles, axis=1)


def _pv_normalized(p, v_ones):
    o = jnp.dot(p, v_ones, preferred_element_type=F32)
    return o[:, :HEAD_DIM] / o[:, HEAD_DIM:]


class _AttnPass(NamedTuple):
    q: Callable[[], jax.Array]
    k_ref: Any
    v_ref: Any
    k0: int
    width: int
    n_rows: int
    tq: int
    q0: int
    window: Optional[int]
    pv_rows: int
    store: Callable[[int, jax.Array], None]


def _run_attention(passes):
    def score_thunks(ps):
        q = ps.q()
        return [lambda j=j: _nt_dot(
            q, ps.k_ref[ps.k0 + j * KEY_BLOCK:ps.k0 + (j + 1) * KEY_BLOCK, :])
                for j in range(ps.width // KEY_BLOCK)]

    s_blocks = [thunk() for thunk in score_thunks(passes[0])]
    pending_pv = None
    for t, ps in enumerate(passes):
        mxu_work = [] if pending_pv is None else [pending_pv]
        n_pv = len(mxu_work)
        if t + 1 < len(passes):
            mxu_work += score_thunks(passes[t + 1])
        row_starts = list(range(0, ps.n_rows, ROW_CHUNK))
        results, p_rows = [], []
        for k in range(max(len(mxu_work), len(row_starts))):
            if k < len(mxu_work):
                results.append(mxu_work[k]())
            if k < len(row_starts):
                p_rows.append(_softmax_rows(s_blocks, row_starts[k], tq=ps.tq, q0=ps.q0,
                                            k0=ps.k0, window=ps.window))
        s_blocks = results[n_pv:]

        def pending_pv(ps=ps, p_rows=p_rows):
            per = ps.pv_rows // ROW_CHUNK
            for k in range(ps.n_rows // ps.pv_rows):
                p = jnp.concatenate(p_rows[k * per:(k + 1) * per], axis=0)
                ps.store(k * ps.pv_rows, _pv_normalized(p, ps.v_ref[ps.k0:ps.k0 + ps.width, :]))
    pending_pv()


def _moba_kernel(q_ref, k_ref, v_ref, onehot_ref, o_ref, kaug_ref, qaug_ref, kmean_ref,
                 vaug_ref, *, n_blk, k_top):
    n_heads, S, _ = k_ref.shape
    tq = MOBA_TQ

    def prepare(h):
        kaug_ref[h, :, :HEAD_DIM] = k_ref[h]
        kaug_ref[h, :, HEAD_DIM:] = onehot_ref[...]
        vaug_ref[h, :, :HEAD_DIM] = v_ref[h]
        vaug_ref[h, :, HEAD_DIM:] = jnp.ones((S, HEAD_DIM), BF16)
        kmean_ref[h] = jnp.zeros(kmean_ref.shape[1:], F32)
        for j in range(n_blk):
            kb = k_ref[h, j * MOBA_BLOCK:(j + 1) * MOBA_BLOCK, :].astype(F32)
            kmean_ref[h, j:j + 1, :] = jnp.sum(kb, axis=0, keepdims=True) * (1.0 / MOBA_BLOCK)

        q = q_ref[h]
        gate_t = _nt_dot(kmean_ref[h].astype(BF16), q)
        jrow = lax.broadcasted_iota(jnp.int32, gate_t.shape, 0)
        own = lax.broadcasted_iota(jnp.int32, gate_t.shape, 1) // MOBA_BLOCK
        gate_t = jnp.where((jrow < own) & jnp.isfinite(gate_t), gate_t, -jnp.inf)
        bias_t = _bias_t(_select_bias_t(gate_t, n_blk, k_top) | (jrow == own))
        qaug_ref[h, :, :HEAD_DIM] = q
        for c in range(S // tq):
            qaug_ref[h, c * tq:(c + 1) * tq, HEAD_DIM:] = _bias_columns(
                bias_t[:, c * tq:(c + 1) * tq])

    def tile_pass(h, c):
        q_rows = slice(c * tq, (c + 1) * tq)

        def store(row0, o):
            o_ref[h, c * tq + row0:c * tq + row0 + o.shape[0], :] = o.astype(BF16)

        return _AttnPass(q=lambda: qaug_ref[h, q_rows, :], k_ref=kaug_ref.at[h],
                         v_ref=vaug_ref.at[h], k0=0, width=(c + 1) * tq, n_rows=tq, tq=tq,
                         q0=c * tq, window=None, pv_rows=MOBA_PV_ROWS, store=store)

    for h in range(n_heads):
        prepare(h)
    _run_attention([tile_pass(h, c) for c in range(S // tq) for h in range(n_heads)])


def _moba(proj, onehot, *, B, S):
    T = B * S
    assert S % MOBA_TQ == 0 and S % KEY_BLOCK == 0 and MOBA_HEADS % MOBA_HEADS_PER_STEP == 0
    n_blk = S // MOBA_BLOCK
    k_top = min(MOBA_TOPK, n_blk - 1)
    nb8 = -(-n_blk // 8) * 8
    hs = MOBA_HEADS_PER_STEP
    kern = functools.partial(_moba_kernel, n_blk=n_blk, k_top=k_top)
    head_spec = lambda blk: pl.BlockSpec((hs, S, LANES), lambda b, h: (blk // hs + h, b, 0))
    return pl.pallas_call(
        kern,
        grid=(B, MOBA_HEADS // hs),
        in_specs=[head_spec(BLK_MQ), head_spec(BLK_MK), head_spec(BLK_MV),
                  pl.BlockSpec((S, LANES), lambda b, h: (0, 0))],
        out_specs=head_spec(0),
        out_shape=jax.ShapeDtypeStruct((MOBA_HEADS, T, LANES), BF16),
        scratch_shapes=[pltpu.VMEM((hs, S, 2 * LANES), BF16), pltpu.VMEM((hs, S, 2 * LANES), BF16),
                        pltpu.VMEM((hs, nb8, HEAD_DIM), F32),
                        pltpu.VMEM((hs, S, 2 * LANES), BF16)],
        compiler_params=pltpu.CompilerParams(
            dimension_semantics=("parallel", "parallel"), vmem_limit_bytes=VMEM_LIMIT),
        name="moba_attn",
    )(proj, proj, proj, onehot)


def _nsa_select_kernel(q_ref, cos_ref, sin_ref, kc_ref, vc_ref, ovt_ref,
                       qrot_ref, bias_ref, oc_ref, *, n_cmp, n_sel_blk, n_top):
    i = pl.program_id(2)
    R, tq, _ = q_ref.shape
    q_raw = q_ref[...]
    cos = (cos_ref[...] * QK_PRESCALE)[None]
    sin = (sin_ref[...] * QK_PRESCALE)[None]
    qrot_ref[...] = _rope(q_raw.astype(F32), cos, sin).astype(BF16)

    n_seg = kc_ref.shape[0]
    s_c = (_nt_dot(q_raw.reshape(R * tq, HEAD_DIM), kc_ref[...]) * SCALE).reshape(R, tq, n_seg)
    n_idx = lax.broadcasted_iota(jnp.int32, s_c.shape, 2)
    pos3 = i * tq + lax.broadcasted_iota(jnp.int32, s_c.shape, 1)
    m_c = (n_idx * CMP_STRIDE + CMP_BLOCK - 1 <= pos3) & (n_idx < n_cmp)
    s_c = jnp.where(m_c, s_c, -jnp.inf)
    mx = jnp.max(s_c, axis=-1, keepdims=True)
    mx = jnp.where(jnp.isfinite(mx), mx, 0.0)
    e_c = jnp.where(m_c, jnp.exp(s_c - mx), 0.0)
    p_c = e_c * (1.0 / jnp.maximum(jnp.sum(e_c, axis=-1, keepdims=True), 1e-30))
    o_c = jnp.dot(p_c.reshape(R * tq, n_seg).astype(BF16), vc_ref[...],
                  preferred_element_type=F32).reshape(R, tq, HEAD_DIM)
    oc_ref[...] = o_c.astype(BF16)

    p_sum = jnp.sum(p_c, axis=0)
    p_hi = p_sum.astype(BF16)
    p_lo = (p_sum - p_hi.astype(F32)).astype(BF16)
    ovt = ovt_ref[...]
    imp_t = _nt_dot(ovt, p_hi) + _nt_dot(ovt, p_lo)
    jrow = lax.broadcasted_iota(jnp.int32, imp_t.shape, 0)
    posq = i * tq + lax.broadcasted_iota(jnp.int32, imp_t.shape, 1)
    own = posq // SEL_BLOCK
    forced = (jrow == 0) | (jrow == own) | (jrow == own - 1)
    future = jrow * SEL_BLOCK > posq
    score_t = jnp.where(future, -jnp.inf, jnp.where(forced, jnp.inf, imp_t))
    keep_t = _select_bias_t(score_t, n_sel_blk, n_top)
    bias_ref[...] = _bias_columns(_bias_t(keep_t))


def _nsa_select(proj, cos, sin, kvc, ovt, *, B, S, tq):
    T = B * S
    assert S % tq == 0
    nq = S // tq
    n_seg = S // CMP_STRIDE
    n_cmp = n_seg - CMP_BLOCK // CMP_STRIDE + 1
    n_sel_blk = S // SEL_BLOCK
    R = NSA_REP
    kern = functools.partial(_nsa_select_kernel, n_cmp=n_cmp, n_sel_blk=n_sel_blk,
                             n_top=min(SEL_TOPK, n_sel_blk))
    heads_spec = pl.BlockSpec((R, tq, LANES), lambda b, g, i: (g, b * nq + i, 0))
    return pl.pallas_call(
        kern,
        grid=(B, NSA_GROUPS, nq),
        in_specs=[pl.BlockSpec((R, tq, LANES), lambda b, g, i: (BLK_NQ // R + g, b * nq + i, 0)),
                  pl.BlockSpec((tq, LANES), lambda b, g, i: (b * nq + i, 0)),
                  pl.BlockSpec((tq, LANES), lambda b, g, i: (b * nq + i, 0)),
                  pl.BlockSpec((None, n_seg, LANES), lambda b, g, i: (0, g * B + b, 0)),
                  pl.BlockSpec((None, n_seg, LANES), lambda b, g, i: (1, g * B + b, 0)),
                  pl.BlockSpec(ovt.shape, lambda b, g, i: (0, 0))],
        out_specs=[heads_spec,
                   pl.BlockSpec((None, tq, LANES), lambda b, g, i: (g, b * nq + i, 0)),
                   heads_spec],
        out_shape=[jax.ShapeDtypeStruct((NSA_HEADS, T, LANES), BF16),
                   jax.ShapeDtypeStruct((NSA_GROUPS, T, LANES), BF16),
                   jax.ShapeDtypeStruct((NSA_HEADS, T, LANES), BF16)],
        compiler_params=pltpu.CompilerParams(
            dimension_semantics=("parallel", "parallel", "parallel"),
            vmem_limit_bytes=VMEM_LIMIT),
        name="nsa_select",
    )(proj, cos, sin, kvc, kvc, ovt)


def _nsa_sel_kernel(qrot_ref, bias_ref, ks_ref, vs_ref, onehot_ref, oc_ref, gate_ref, o_ref,
                    ksaug_ref, vaug_ref, *, tq):
    R, S, _ = qrot_ref.shape
    ksaug_ref[:, :HEAD_DIM] = ks_ref[...]
    ksaug_ref[:, HEAD_DIM:] = onehot_ref[...]
    vaug_ref[:, :HEAD_DIM] = vs_ref[...]
    vaug_ref[:, HEAD_DIM:] = jnp.ones((S, HEAD_DIM), BF16)

    def head_pass(c, heads):
        q_rows = slice(c * tq, (c + 1) * tq)

        def q_aug():
            bias = bias_ref[q_rows, :]
            return jnp.concatenate(
                [jnp.concatenate([qrot_ref[r, q_rows, :], bias], axis=1) for r in heads], axis=0)

        def store(row0, o):
            gt = _sigmoid(gate_ref[q_rows, :])
            for k, r in enumerate(heads):
                part = (gt[:, 3 * r:3 * r + 1] * oc_ref[r, q_rows, :].astype(F32)
                        + gt[:, 3 * r + 1:3 * r + 2] * o[k * tq:(k + 1) * tq, :])
                o_ref[r, q_rows, :] = part.astype(BF16)

        return _AttnPass(q=q_aug, k_ref=ksaug_ref, v_ref=vaug_ref, k0=0, width=(c + 1) * tq,
                         n_rows=len(heads) * tq, tq=tq, q0=c * tq, window=None,
                         pv_rows=len(heads) * tq, store=store)

    _run_attention([head_pass(c, range(r0, r0 + HEADS_PER_PASS))
                    for c in range(S // tq) for r0 in range(0, R, HEADS_PER_PASS)])


def _nsa_sel(q_rot, bias, proj, onehot, o_c, gates, *, B, S, tq):
    T = B * S
    assert S % tq == 0 and tq % KEY_BLOCK == 0
    R = NSA_REP
    heads_spec = pl.BlockSpec((R, S, LANES), lambda b, g: (g, b, 0))
    group_spec = pl.BlockSpec((None, S, LANES), lambda b, g: (g, b, 0))
    kv_spec = lambda blk: pl.BlockSpec((None, S, LANES), lambda b, g: (blk + g, b, 0))
    return pl.pallas_call(
        functools.partial(_nsa_sel_kernel, tq=tq),
        grid=(B, NSA_GROUPS),
        in_specs=[heads_spec, group_spec,
                  kv_spec(BLK_NKS), kv_spec(BLK_NVS),
                  pl.BlockSpec((S, LANES), lambda b, g: (0, 0)),
                  heads_spec, group_spec],
        out_specs=heads_spec,
        out_shape=jax.ShapeDtypeStruct((NSA_HEADS, T, LANES), BF16),
        scratch_shapes=[pltpu.VMEM((S, 2 * LANES), BF16), pltpu.VMEM((S, 2 * LANES), BF16)],
        compiler_params=pltpu.CompilerParams(
            dimension_semantics=("parallel", "parallel"), vmem_limit_bytes=VMEM_LIMIT_BIG),
        name="nsa_sel_attn",
    )(q_rot, bias, proj, proj, onehot, o_c, gates)


def _nsa_win_kernel(qrot_ref, kw_ref, vw_ref, part_ref, gate_ref, o_ref, vaug_ref, *, tq):
    R, S, _ = qrot_ref.shape
    vaug_ref[:, :HEAD_DIM] = vw_ref[...]
    vaug_ref[:, HEAD_DIM:] = jnp.ones((S, HEAD_DIM), BF16)

    def head_pass(c, heads):
        q_rows = slice(c * tq, (c + 1) * tq)
        k0 = max(0, c * tq - WINDOW)

        def store(row0, o_w):
            gt = _sigmoid(gate_ref[q_rows, :])
            for k, r in enumerate(heads):
                o = (part_ref[r, q_rows, :].astype(F32)
                     + gt[:, 3 * r + 2:3 * r + 3] * o_w[k * tq:(k + 1) * tq, :])
                o_ref[r, q_rows, :] = o.astype(BF16)

        return _AttnPass(
            q=lambda: jnp.concatenate([qrot_ref[r, q_rows, :] for r in heads], axis=0),
            k_ref=kw_ref, v_ref=vaug_ref, k0=k0, width=(c + 1) * tq - k0,
            n_rows=len(heads) * tq, tq=tq, q0=c * tq, window=WINDOW,
            pv_rows=len(heads) * tq, store=store)

    _run_attention([head_pass(c, range(r0, r0 + HEADS_PER_PASS))
                    for c in range(S // tq) for r0 in range(0, R, HEADS_PER_PASS)])


def _nsa_win(q_rot, proj, part, gates, *, B, S, tq):
    T = B * S
    assert S % tq == 0 and tq % KEY_BLOCK == 0 and WINDOW % KEY_BLOCK == 0
    R = NSA_REP
    heads_spec = pl.BlockSpec((R, S, LANES), lambda b, g: (g, b, 0))
    kv_spec = lambda blk: pl.BlockSpec((None, S, LANES), lambda b, g: (blk + g, b, 0))
    return pl.pallas_call(
        functools.partial(_nsa_win_kernel, tq=tq),
        grid=(B, NSA_GROUPS),
        in_specs=[heads_spec, kv_spec(BLK_NKW), kv_spec(BLK_NVW), heads_spec,
                  pl.BlockSpec((None, S, LANES), lambda b, g: (g, b, 0))],
        out_specs=heads_spec,
        out_shape=jax.ShapeDtypeStruct((NSA_HEADS, T, LANES), BF16),
        scratch_shapes=[pltpu.VMEM((S, 2 * LANES), BF16)],
        compiler_params=pltpu.CompilerParams(
            dimension_semantics=("parallel", "parallel"), vmem_limit_bytes=VMEM_LIMIT_BIG),
        name="nsa_win_attn",
    )(q_rot, proj, proj, part, gates)


def _nsa_attn_kernel(qrot_ref, bias_ref, ks_ref, vs_ref, kw_ref, vw_ref, onehot_ref, oc_ref,
                     gate_ref, o_ref, ksaug_ref, vsaug_ref, vwaug_ref, *, tq):
    R, S, _ = qrot_ref.shape
    ones = jnp.ones((S, HEAD_DIM), BF16)
    ksaug_ref[:, :HEAD_DIM] = ks_ref[...]
    ksaug_ref[:, HEAD_DIM:] = onehot_ref[...]
    vsaug_ref[:, :HEAD_DIM] = vs_ref[...]
    vsaug_ref[:, HEAD_DIM:] = ones
    vwaug_ref[:, :HEAD_DIM] = vw_ref[...]
    vwaug_ref[:, HEAD_DIM:] = ones
    selected = {}

    def passes(c, heads):
        q_rows = slice(c * tq, (c + 1) * tq)
        k0 = max(0, c * tq - WINDOW)
        n_rows = len(heads) * tq

        def q_sel():
            bias = bias_ref[q_rows, :]
            return jnp.concatenate(
                [jnp.concatenate([qrot_ref[r, q_rows, :], bias], axis=1) for r in heads], axis=0)

        def store_sel(row0, o):
            selected[c, heads[0]] = o

        def store_win(row0, o_w):
            o_s = selected.pop((c, heads[0]))
            gt = _sigmoid(gate_ref[q_rows, :])
            for k, r in enumerate(heads):
                rows = slice(k * tq, (k + 1) * tq)
                o = (gt[:, 3 * r:3 * r + 1] * oc_ref[r, q_rows, :].astype(F32)
                     + gt[:, 3 * r + 1:3 * r + 2] * o_s[rows, :]
                     + gt[:, 3 * r + 2:3 * r + 3] * o_w[rows, :])
                o_ref[r, q_rows, :] = o.astype(BF16)

        sel = _AttnPass(q=q_sel, k_ref=ksaug_ref, v_ref=vsaug_ref, k0=0, width=(c + 1) * tq,
                        n_rows=n_rows, tq=tq, q0=c * tq, window=None, pv_rows=n_rows,
                        store=store_sel)
        win = _AttnPass(
            q=lambda: jnp.concatenate([qrot_ref[r, q_rows, :] for r in heads], axis=0),
            k_ref=kw_ref, v_ref=vwaug_ref, k0=k0, width=(c + 1) * tq - k0, n_rows=n_rows,
            tq=tq, q0=c * tq, window=WINDOW, pv_rows=n_rows, store=store_win)
        return [sel, win]

    _run_attention([p for c in range(S // tq) for r0 in range(0, R, HEADS_PER_PASS)
                    for p in passes(c, tuple(range(r0, r0 + HEADS_PER_PASS)))])


def _nsa_attn(q_rot, bias, proj, onehot, o_c, gates, *, B, S, tq):
    T = B * S
    assert S % tq == 0 and tq % KEY_BLOCK == 0 and WINDOW % KEY_BLOCK == 0
    R = NSA_REP
    heads_spec = pl.BlockSpec((R, S, LANES), lambda b, g: (g, b, 0))
    group_spec = pl.BlockSpec((None, S, LANES), lambda b, g: (g, b, 0))
    kv_spec = lambda blk: pl.BlockSpec((None, S, LANES), lambda b, g: (blk + g, b, 0))
    aug = pltpu.VMEM((S, 2 * LANES), BF16)
    return pl.pallas_call(
        functools.partial(_nsa_attn_kernel, tq=tq),
        grid=(B, NSA_GROUPS),
        in_specs=[heads_spec, group_spec, kv_spec(BLK_NKS), kv_spec(BLK_NVS),
                  kv_spec(BLK_NKW), kv_spec(BLK_NVW),
                  pl.BlockSpec((S, LANES), lambda b, g: (0, 0)), heads_spec, group_spec],
        out_specs=heads_spec,
        out_shape=jax.ShapeDtypeStruct((NSA_HEADS, T, LANES), BF16),
        scratch_shapes=[aug, aug, aug],
        compiler_params=pltpu.CompilerParams(
            dimension_semantics=("parallel", "parallel"), vmem_limit_bytes=VMEM_LIMIT_BIG),
        name="nsa_attn",
    )(q_rot, bias, proj, proj, proj, proj, onehot, o_c, gates)


def _out_proj_kernel(om_ref, on_ref, zm_ref, zn_ref, gm_ref, gn_ref, x_ref, w_ref, gf_ref,
                     out_ref):
    def gated_norm(o_ref, z_ref, g_ref, rows):
        n_heads = o_ref.shape[0]
        acts = []
        ss = None
        for h in range(n_heads):
            z = z_ref[h, rows, :].astype(F32)
            a = o_ref[h, rows, :].astype(F32) * (z * _sigmoid(z))
            acts.append(a)
            ss = a * a if ss is None else ss + a * a
        ms = jnp.sum(ss, axis=-1, keepdims=True) * (1.0 / (n_heads * LANES))
        inv = lax.rsqrt(ms + EPS)
        return [(acts[h] * inv * g_ref[h]).astype(BF16) for h in range(n_heads)]

    for r0 in range(0, x_ref.shape[0], OUT_SUB_ROWS):
        rows = slice(r0, r0 + OUT_SUB_ROWS)
        y = jnp.concatenate(gated_norm(om_ref, zm_ref, gm_ref, rows)
                            + gated_norm(on_ref, zn_ref, gn_ref, rows), axis=1)
        r = x_ref[rows, :] + jnp.dot(y, w_ref[...], preferred_element_type=F32)
        ms = jnp.mean(r * r, axis=-1, keepdims=True)
        out_ref[rows, :] = r * lax.rsqrt(ms + EPS) * gf_ref[...]


def _out_proj(o_moba, o_nsa, proj, g_moba, g_nsa, x2d, w_out, g_final, *, tm):
    T = x2d.shape[0]
    assert T % tm == 0 and tm % OUT_SUB_ROWS == 0
    H = MOBA_HEADS
    return pl.pallas_call(
        _out_proj_kernel,
        grid=(T // tm,),
        in_specs=[pl.BlockSpec((H, tm, LANES), lambda i: (0, i, 0)),
                  pl.BlockSpec((H, tm, LANES), lambda i: (0, i, 0)),
                  pl.BlockSpec((H, tm, LANES), lambda i: (BLK_MZ // H, i, 0)),
                  pl.BlockSpec((H, tm, LANES), lambda i: (BLK_NZ // H, i, 0)),
                  pl.BlockSpec((H, 1, LANES), lambda i: (0, 0, 0)),
                  pl.BlockSpec((H, 1, LANES), lambda i: (0, 0, 0)),
                  pl.BlockSpec((tm, D_MODEL), lambda i: (i, 0)),
                  pl.BlockSpec((D_MODEL, D_MODEL), lambda i: (0, 0), pipeline_mode=pl.Buffered(1)),
                  pl.BlockSpec((1, D_MODEL), lambda i: (0, 0))],
        out_specs=pl.BlockSpec((tm, D_MODEL), lambda i: (i, 0)),
        out_shape=jax.ShapeDtypeStruct((T, D_MODEL), F32),
        compiler_params=pltpu.CompilerParams(
            dimension_semantics=("parallel",), vmem_limit_bytes=VMEM_LIMIT_BIG),
        name="out_proj",
    )(o_moba, o_nsa, proj, proj, g_moba, g_nsa, x2d, w_out, g_final)


def _w_in_offsets():
    mw, nw, kw = MOBA_HEADS * HEAD_DIM, NSA_HEADS * HEAD_DIM, NSA_GROUPS * HEAD_DIM
    sizes = [mw] * 4 + [nw] + [kw] * 6 + [3 * NSA_HEADS, nw]
    names = ["mq", "mk", "mv", "mz", "nq", "nkc", "nvc", "nks", "nvs", "nkw", "nvw", "ng", "nz"]
    offs = np.concatenate([[0], np.cumsum(sizes)])
    return {n: (int(offs[k]), int(sizes[k])) for k, n in enumerate(names)}


def _weight_tiles_kernel(w_ref, out_ref, *, src_cols, blocks_per_tile):
    for b, src in enumerate(src_cols):
        lane0 = (b % blocks_per_tile) * LANES
        out_ref[b // blocks_per_tile, :, lane0:lane0 + LANES] = (
            w_ref[:, src:src + LANES].astype(BF16))


def _weight_tiles(w_in, blocks_per_tile):
    offsets = _w_in_offsets()
    src_cols = []
    for name, n_blocks in COLUMN_ORDER:
        off, size = offsets[name]
        assert size == n_blocks * LANES
        src_cols += [off + k * LANES for k in range(n_blocks)]
    n_tiles = N_BLOCKS // blocks_per_tile
    rows = WEIGHT_PREP_ROWS
    kern = functools.partial(_weight_tiles_kernel, src_cols=tuple(src_cols),
                             blocks_per_tile=blocks_per_tile)
    return pl.pallas_call(
        kern,
        grid=(D_MODEL // rows,),
        in_specs=[pl.BlockSpec((None, rows, w_in.shape[2]), lambda r: (0, r, 0))],
        out_specs=pl.BlockSpec((n_tiles, rows, blocks_per_tile * LANES), lambda r: (0, r, 0)),
        out_shape=jax.ShapeDtypeStruct((n_tiles, D_MODEL, blocks_per_tile * LANES), BF16),
        compiler_params=pltpu.CompilerParams(
            dimension_semantics=("parallel",), vmem_limit_bytes=VMEM_LIMIT),
        name="weight_tiles",
    )(w_in)


def _gate_weight(w_in):
    off, size = _w_in_offsets()["ng"]
    per_group = 3 * NSA_REP
    wg = w_in[0, :, off:off + size].reshape(D_MODEL, NSA_GROUPS, per_group)
    wg = jnp.pad(wg, ((0, 0), (0, 0), (0, LANES - per_group)))
    return wg.reshape(D_MODEL, NSA_GROUPS * LANES).astype(BF16)


def _block_onehot(S, block):
    ids = np.arange(S)[:, None] // block
    return jnp.asarray((ids == np.arange(LANES)[None, :]).astype(np.float32), dtype=BF16)


def _overlap_t(n_seg, n_cmp, n_sel_blk):
    cs = np.arange(n_seg)[None, :] * CMP_STRIDE
    ss = np.arange(n_sel_blk)[:, None] * SEL_BLOCK
    ov = (cs < ss + SEL_BLOCK) & (cs + CMP_BLOCK > ss) & (np.arange(n_seg)[None, :] < n_cmp)
    return jnp.asarray(ov.astype(np.float32), dtype=BF16)


def _layer(x, positions, w_in, g_norm, pe_ck, pe_cv, w_ck1, w_ck2, w_cv1, w_cv2,
           g_out_moba, g_out_nsa, w_out, g_final, *, nsa_tq, tm_in, tm_out, blocks_per_tile):
    B, S, _ = x.shape
    T = B * S
    x2d = x.reshape(T, D_MODEL)
    w_tiles = _weight_tiles(w_in.astype(BF16), blocks_per_tile)
    w_gate = _gate_weight(w_in)
    proj, gates, cos, sin, seg = _in_proj(x2d, g_norm.reshape(1, D_MODEL), w_tiles, w_gate,
                                          positions, tm=tm_in, blocks_per_tile=blocks_per_tile)

    n_seg = S // CMP_STRIDE
    seg = seg.reshape(2, NSA_GROUPS * B * n_seg, CMP_STRIDE * HEAD_DIM)
    pe = jnp.stack([pe_ck.reshape(1, -1), pe_cv.reshape(1, -1)])
    w1 = jnp.stack([w_ck1, w_cv1]).astype(BF16)
    w2 = jnp.stack([w_ck2, w_cv2]).astype(BF16)
    kvc = _compress(seg, pe, w1, w2)

    o_moba = _moba(proj, _block_onehot(S, MOBA_BLOCK), B=B, S=S)
    n_cmp = n_seg - CMP_BLOCK // CMP_STRIDE + 1
    q_rot, bias, o_c = _nsa_select(proj, cos, sin, kvc, _overlap_t(n_seg, n_cmp, S // SEL_BLOCK),
                                   B=B, S=S, tq=4 * nsa_tq)
    o_nsa = _nsa_attn(q_rot, bias, proj, _block_onehot(S, SEL_BLOCK), o_c, gates,
                      B=B, S=S, tq=nsa_tq)
    out = _out_proj(o_moba, o_nsa, proj,
                    g_out_moba.reshape(MOBA_HEADS, 1, LANES), g_out_nsa.reshape(NSA_HEADS, 1, LANES),
                    x2d, w_out.astype(BF16), g_final.reshape(1, D_MODEL), tm=tm_out)
    return out.reshape(B, S, D_MODEL)


def kernel(x, positions, w_in, g_norm, pe_ck, pe_cv, w_ck1, w_ck2, w_cv1, w_cv2,
           g_out_moba, g_out_nsa, w_out, g_final):
    assert w_in.shape[0] == 1, "single-layer problem"
    return _layer(x, positions, w_in, g_norm[0], pe_ck[0], pe_cv[0], w_ck1[0], w_ck2[0],
                  w_cv1[0], w_cv2[0], g_out_moba[0], g_out_nsa[0], w_out[0], g_final,
                  nsa_tq=256, tm_in=1024, tm_out=512, blocks_per_tile=10)
```

```python
import functools
from typing import Any, Callable, NamedTuple, Optional

import numpy as np
import jax
import jax.numpy as jnp
from jax import lax
from jax.experimental import pallas as pl
from jax.experimental.pallas import tpu as pltpu

F32 = jnp.float32
BF16 = jnp.bfloat16

D_MODEL = 2048
HEAD_DIM = 128
MOBA_HEADS = 8
NSA_HEADS = 8
NSA_GROUPS = 2
NSA_REP = 4
MOBA_BLOCK = 256
MOBA_TOPK = 3
CMP_BLOCK = 32
CMP_STRIDE = 16
CMP_HIDDEN = 256
SEL_BLOCK = 64
SEL_TOPK = 8
WINDOW = 512
ROPE_THETA = 10000.0
EPS = 1e-6
SCALE = HEAD_DIM ** -0.5
QK_PRESCALE = SCALE * float(np.log2(np.e))
NEG_BIG = -(2.0 ** 100)

LANES = 128
VMEM_LIMIT = 48 * 1024 * 1024
VMEM_LIMIT_BIG = 58 * 1024 * 1024
COLUMN_ORDER = (("mq", 8), ("mk", 8), ("nks", 2), ("nkw", 2), ("nkc", 2), ("nvc", 2),
                ("mv", 8), ("mz", 8), ("nq", 8), ("nz", 8), ("nvs", 2), ("nvw", 2))
BLK_MQ, BLK_MK, BLK_NKS, BLK_NKW, BLK_NKC, BLK_NVC = 0, 8, 16, 18, 20, 22
BLK_MV, BLK_MZ, BLK_NQ, BLK_NZ, BLK_NVS, BLK_NVW = 24, 32, 40, 48, 56, 58
N_BLOCKS = 60
N_ROPE_BLOCKS = 20
N_SEG_BLOCKS = 4

ROW_CHUNK = 64
MOBA_PV_ROWS = 256
HEADS_PER_PASS = 2
NORM_ROWS = 16
IN_SUB_ROWS = 256
MXU_COLS = 256
OUT_SUB_ROWS = 512
MOBA_TQ = 256
MOBA_HEADS_PER_STEP = 4
KEY_BLOCK = 256
WEIGHT_PREP_ROWS = 256


def _nt_dot(a, b):
    return lax.dot_general(a, b, (((1,), (1,)), ((), ())), preferred_element_type=F32)


def _sigmoid(x):
    return 1.0 / (1.0 + jnp.exp(-x))


def _rope(a, cos, sin_signed):
    return a * cos + pltpu.roll(a, HEAD_DIM // 2, axis=a.ndim - 1) * sin_signed


def _interleave(*streams):
    n = max(len(s) for s in streams)
    for k in range(n):
        for s in streams:
            for thunk in s[k * len(s) // n:(k + 1) * len(s) // n]:
                thunk()


def _block_kind(b):
    if BLK_NKC <= b < BLK_NKC + N_SEG_BLOCKS:
        return "seg"
    return "rope_q" if b < BLK_MK else "rope" if b < N_ROPE_BLOCKS else "plain"


def _project_columns(h, w_ref, col0, kinds, cos, sin, store, store_segments=None):
    n = len(kinds)
    acc = jnp.dot(h, w_ref[:, col0:col0 + n * LANES], preferred_element_type=F32)
    n_seg = 0
    for c, kind in enumerate(kinds):
        a = acc[:, c * LANES:(c + 1) * LANES]
        if kind == "rope_q":
            a = _rope(a, cos * QK_PRESCALE, sin * QK_PRESCALE)
        elif kind == "rope":
            a = _rope(a, cos, sin)
        elif kind == "seg":
            store_segments(n_seg, a)
            n_seg += 1
        store(c, a.astype(BF16))


def _in_proj_kernel(x_ref, g_ref, w_ref, wg_ref, pos_ref, invf_ref, sign_ref,
                    out_ref, gate_ref, cos_ref, sin_ref, seg_ref, h_scr, seg_scr,
                    *, tile_patterns):
    j = pl.program_id(1)
    tm = x_ref.shape[0]
    first_pattern = tile_patterns[0][0]
    assert tile_patterns[0][1] == [0]

    @pl.when(j == 0)
    def _():
        ang = pos_ref[...].astype(F32) * invf_ref[...]
        cos_ref[...] = jnp.cos(ang)
        sin_ref[...] = jnp.sin(ang) * sign_ref[...]

        def norm_piece(r0):
            rows = slice(r0, r0 + NORM_ROWS)
            x = x_ref[rows, :]
            ms = jnp.mean(x * x, axis=-1, keepdims=True)
            h_scr[rows, :] = (x * lax.rsqrt(ms + EPS) * g_ref[...]).astype(BF16)

        def norm_pieces(m):
            return [functools.partial(norm_piece, r0)
                    for r0 in range(m * IN_SUB_ROWS, (m + 1) * IN_SUB_ROWS, NORM_ROWS)]

        def matmul_pieces(m):
            rows = slice(m * IN_SUB_ROWS, (m + 1) * IN_SUB_ROWS)

            def gates():
                acc = jnp.dot(h_scr[rows, :], wg_ref[...], preferred_element_type=F32)
                for g in range(NSA_GROUPS):
                    gate_ref[g, rows, :] = acc[:, g * LANES:(g + 1) * LANES]

            def columns(c0):
                def store(c, val):
                    out_ref[c0 + c, rows, :] = val
                per = MXU_COLS // LANES
                _project_columns(h_scr[rows, :], w_ref, c0 * LANES, first_pattern[c0:c0 + per],
                                 cos_ref[rows, :], sin_ref[rows, :], store)

            return [gates] + [functools.partial(columns, c0)
                              for c0 in range(0, len(first_pattern), MXU_COLS // LANES)]

        n_sub = tm // IN_SUB_ROWS
        _interleave(norm_pieces(0))
        for m in range(n_sub):
            _interleave(matmul_pieces(m), norm_pieces(m + 1) if m + 1 < n_sub else [])

    for pattern, tiles in tile_patterns[1:]:
        @pl.when(functools.reduce(jnp.logical_or, [j == t for t in tiles]))
        def _(pattern=pattern):
            def store(c, val):
                out_ref[c] = val

            def store_segments(k, a):
                seg_scr[...] = a
                for t in range(CMP_STRIDE):
                    piece = seg_scr[pl.ds(t, tm // CMP_STRIDE, stride=CMP_STRIDE), :]
                    seg_ref[k, :, t * HEAD_DIM:(t + 1) * HEAD_DIM] = piece.astype(BF16)

            _project_columns(h_scr[...], w_ref, 0, pattern, cos_ref[...], sin_ref[...], store,
                             store_segments)


def _in_proj(x2d, g_norm, w_tiles, w_gate, positions, *, tm, blocks_per_tile):
    T = x2d.shape[0]
    assert T % tm == 0 and tm % IN_SUB_ROWS == 0 and N_BLOCKS % blocks_per_tile == 0
    tn = blocks_per_tile * LANES
    n_tiles = N_BLOCKS // blocks_per_tile
    by_pattern = {}
    for t in range(n_tiles):
        pattern = tuple(_block_kind(t * blocks_per_tile + c) for c in range(blocks_per_tile))
        by_pattern.setdefault(pattern, []).append(t)
    half = HEAD_DIM // 2
    inv_freq = 1.0 / (ROPE_THETA ** (jnp.arange(0, HEAD_DIM, 2, dtype=F32) / HEAD_DIM))
    invf = jnp.concatenate([inv_freq, inv_freq]).reshape(1, HEAD_DIM)
    sign = jnp.concatenate([-jnp.ones((half,), F32), jnp.ones((half,), F32)]).reshape(1, HEAD_DIM)
    kern = functools.partial(_in_proj_kernel, tile_patterns=tuple(by_pattern.items()))
    row_table = pl.BlockSpec((tm, HEAD_DIM), lambda i, j: (i, 0))
    return pl.pallas_call(
        kern,
        grid=(T // tm, n_tiles),
        in_specs=[pl.BlockSpec((tm, D_MODEL), lambda i, j: (i, 0)),
                  pl.BlockSpec((1, D_MODEL), lambda i, j: (0, 0)),
                  pl.BlockSpec((None, D_MODEL, tn), lambda i, j: (j, 0, 0)),
                  pl.BlockSpec((D_MODEL, NSA_GROUPS * LANES), lambda i, j: (0, 0)),
                  pl.BlockSpec((tm, 1), lambda i, j: (i, 0)),
                  pl.BlockSpec((1, HEAD_DIM), lambda i, j: (0, 0)),
                  pl.BlockSpec((1, HEAD_DIM), lambda i, j: (0, 0))],
        out_specs=[pl.BlockSpec((blocks_per_tile, tm, LANES), lambda i, j: (j, i, 0)),
                   pl.BlockSpec((NSA_GROUPS, tm, LANES), lambda i, j: (0, i, 0)),
                   row_table, row_table,
                   pl.BlockSpec((N_SEG_BLOCKS, tm // CMP_STRIDE, CMP_STRIDE * HEAD_DIM),
                                lambda i, j: (0, i, 0))],
        out_shape=[jax.ShapeDtypeStruct((N_BLOCKS, T, LANES), BF16),
                   jax.ShapeDtypeStruct((NSA_GROUPS, T, LANES), F32),
                   jax.ShapeDtypeStruct((T, HEAD_DIM), F32),
                   jax.ShapeDtypeStruct((T, HEAD_DIM), F32),
                   jax.ShapeDtypeStruct((N_SEG_BLOCKS, T // CMP_STRIDE, CMP_STRIDE * HEAD_DIM),
                                        BF16)],
        scratch_shapes=[pltpu.VMEM((tm, D_MODEL), BF16), pltpu.VMEM((tm, HEAD_DIM), F32)],
        compiler_params=pltpu.CompilerParams(
            dimension_semantics=("parallel", "arbitrary"), vmem_limit_bytes=VMEM_LIMIT_BIG),
        name="in_proj",
    )(x2d, g_norm, w_tiles, w_gate, positions.reshape(T, 1), invf, sign)


def _compress_kernel(seg_ref, pe_ref, w1_ref, w2_ref, out_ref):
    half = CMP_STRIDE * HEAD_DIM
    seg = seg_ref[0].astype(F32)
    pe = pe_ref[0]
    top = (seg + pe[:, :half]).astype(BF16)
    bot = (seg + pe[:, half:]).astype(BF16)
    a = jnp.dot(top, w1_ref[0, :half, :], preferred_element_type=F32)
    b = jnp.dot(bot, w1_ref[0, half:, :], preferred_element_type=F32)
    rows = a.shape[0]
    h = a + pltpu.roll(b, rows - 1, axis=0)
    hid = h * _sigmoid(h)
    out_ref[0] = jnp.dot(hid.astype(BF16), w2_ref[0], preferred_element_type=F32).astype(BF16)


def _compress(seg, pe, w1, w2):
    _, R, half = seg.shape
    return pl.pallas_call(
        _compress_kernel,
        grid=(2,),
        in_specs=[pl.BlockSpec((1, R, half), lambda c: (c, 0, 0)),
                  pl.BlockSpec((1, 1, 2 * half), lambda c: (c, 0, 0)),
                  pl.BlockSpec((1, 2 * half, CMP_HIDDEN), lambda c: (c, 0, 0)),
                  pl.BlockSpec((1, CMP_HIDDEN, HEAD_DIM), lambda c: (c, 0, 0))],
        out_specs=pl.BlockSpec((1, R, HEAD_DIM), lambda c: (c, 0, 0)),
        out_shape=jax.ShapeDtypeStruct((2, R, HEAD_DIM), BF16),
        compiler_params=pltpu.CompilerParams(
            dimension_semantics=("arbitrary",), vmem_limit_bytes=VMEM_LIMIT),
        name="compress",
    )(seg, pe, w1, w2)


def _select_bias_t(score_t, n_rows, n_keep):
    sub = 8
    n_groups = score_t.shape[0] // sub
    groups = [score_t[g * sub:(g + 1) * sub, :] for g in range(n_groups)]
    jrow = lax.broadcasted_iota(jnp.int32, groups[0].shape, 0)
    cnts = [jnp.zeros(groups[0].shape, jnp.int32) for _ in range(n_groups)]
    for jp in range(n_rows):
        row = score_t[jp:jp + 1, :]
        for g, grp in enumerate(groups):
            if g * sub > jp:
                beats = row >= grp
            elif g * sub + sub - 1 <= jp:
                beats = row > grp
            else:
                beats = (row > grp) | ((row == grp) & (jrow + g * sub > jp))
            cnts[g] = cnts[g] + beats.astype(jnp.int32)
    cnt = cnts[0] if n_groups == 1 else jnp.concatenate(cnts, axis=0)
    return (cnt < n_keep) & (score_t > -jnp.inf)


def _bias_t(keep_t):
    return jnp.where(keep_t, 0.0, NEG_BIG).astype(F32)


def _bias_columns(bias_t):
    rows, q = bias_t.shape
    if rows < LANES:
        bias_t = jnp.concatenate([bias_t, jnp.zeros((LANES - rows, q), F32)], axis=0)
    return bias_t.T.astype(BF16)


def _softmax_rows(s_blocks, r0, *, tq, q0, k0, window=None):
    per_block = KEY_BLOCK // LANES
    n_tiles = len(s_blocks) * per_block
    rows = slice(r0, r0 + ROW_CHUNK)
    qlo = q0 + r0 % tq
    qhi = qlo + ROW_CHUNK - 1

    def tile(t):
        lane0 = (t % per_block) * LANES
        return s_blocks[t // per_block][rows, lane0:lane0 + LANES]

    kinds = []
    for t in range(n_tiles):
        klo = k0 + t * LANES
        khi = klo + LANES - 1
        none = klo > qhi or (window is not None and khi <= qlo - window)
        full = khi <= qlo and (window is None or klo > qhi - window)
        kinds.append("none" if none else "full" if full else "part")
    mx = None
    masked = {}
    for t, kind in enumerate(kinds):
        if kind == "none":
            continue
        x = tile(t)
        if kind == "part":
            qpos = qlo + lax.broadcasted_iota(jnp.int32, x.shape, 0)
            kpos = k0 + t * LANES + lax.broadcasted_iota(jnp.int32, x.shape, 1)
            ok = kpos <= qpos
            if window is not None:
                ok = ok & (kpos > qpos - window)
            x = jnp.where(ok, x, NEG_BIG)
            masked[t] = x
        mx = x if mx is None else jnp.maximum(mx, x)
    m = jnp.broadcast_to(jnp.max(mx, axis=-1, keepdims=True), mx.shape)
    p_tiles = []
    for t, kind in enumerate(kinds):
        if kind == "none":
            p_tiles.append(jnp.zeros((ROW_CHUNK, LANES), BF16))
            continue
        x = masked[t] if kind == "part" else tile(t)
        p_tiles.append(jnp.exp2(x - m).astype(BF16))
    return jnp.concatenate(p_tiles, axis=1)


def _pv_normalized(p, v_ones):
    o = jnp.dot(p, v_ones, preferred_element_type=F32)
    return o[:, :HEAD_DIM] / o[:, HEAD_DIM:]


class _AttnPass(NamedTuple):
    q: Callable[[], jax.Array]
    k_ref: Any
    v_ref: Any
    k0: int
    width: int
    n_rows: int
    tq: int
    q0: int
    window: Optional[int]
    pv_rows: int
    store: Callable[[int, jax.Array], None]


def _run_attention(passes):
    def score_thunks(ps):
        q = ps.q()
        return [lambda j=j: _nt_dot(
            q, ps.k_ref[ps.k0 + j * KEY_BLOCK:ps.k0 + (j + 1) * KEY_BLOCK, :])
                for j in range(ps.width // KEY_BLOCK)]

    s_blocks = [thunk() for thunk in score_thunks(passes[0])]
    pending_pv = None
    for t, ps in enumerate(passes):
        mxu_work = [] if pending_pv is None else [pending_pv]
        n_pv = len(mxu_work)
        if t + 1 < len(passes):
            mxu_work += score_thunks(passes[t + 1])
        row_starts = list(range(0, ps.n_rows, ROW_CHUNK))
        results, p_rows = [], []
        for k in range(max(len(mxu_work), len(row_starts))):
            if k < len(mxu_work):
                results.append(mxu_work[k]())
            if k < len(row_starts):
                p_rows.append(_softmax_rows(s_blocks, row_starts[k], tq=ps.tq, q0=ps.q0,
                                            k0=ps.k0, window=ps.window))
        s_blocks = results[n_pv:]

        def pending_pv(ps=ps, p_rows=p_rows):
            per = ps.pv_rows // ROW_CHUNK
            for k in range(ps.n_rows // ps.pv_rows):
                p = jnp.concatenate(p_rows[k * per:(k + 1) * per], axis=0)
                ps.store(k * ps.pv_rows, _pv_normalized(p, ps.v_ref[ps.k0:ps.k0 + ps.width, :]))
    pending_pv()


def _moba_kernel(q_ref, k_ref, v_ref, onehot_ref, o_ref, kaug_ref, qaug_ref, kmean_ref,
                 vaug_ref, *, n_blk, k_top):
    n_heads, S, _ = k_ref.shape
    tq = MOBA_TQ

    def prepare(h):
        kaug_ref[h, :, :HEAD_DIM] = k_ref[h]
        kaug_ref[h, :, HEAD_DIM:] = onehot_ref[...]
        vaug_ref[h, :, :HEAD_DIM] = v_ref[h]
        vaug_ref[h, :, HEAD_DIM:] = jnp.ones((S, HEAD_DIM), BF16)
        kmean_ref[h] = jnp.zeros(kmean_ref.shape[1:], F32)
        for j in range(n_blk):
            kb = k_ref[h, j * MOBA_BLOCK:(j + 1) * MOBA_BLOCK, :].astype(F32)
            kmean_ref[h, j:j + 1, :] = jnp.sum(kb, axis=0, keepdims=True) * (1.0 / MOBA_BLOCK)

        q = q_ref[h]
        gate_t = _nt_dot(kmean_ref[h].astype(BF16), q)
        jrow = lax.broadcasted_iota(jnp.int32, gate_t.shape, 0)
        own = lax.broadcasted_iota(jnp.int32, gate_t.shape, 1) // MOBA_BLOCK
        gate_t = jnp.where((jrow < own) & jnp.isfinite(gate_t), gate_t, -jnp.inf)
        bias_t = _bias_t(_select_bias_t(gate_t, n_blk, k_top) | (jrow == own))
        qaug_ref[h, :, :HEAD_DIM] = q
        for c in range(S // tq):
            qaug_ref[h, c * tq:(c + 1) * tq, HEAD_DIM:] = _bias_columns(
                bias_t[:, c * tq:(c + 1) * tq])

    def tile_pass(h, c):
        q_rows = slice(c * tq, (c + 1) * tq)

        def store(row0, o):
            o_ref[h, c * tq + row0:c * tq + row0 + o.shape[0], :] = o.astype(BF16)

        return _AttnPass(q=lambda: qaug_ref[h, q_rows, :], k_ref=kaug_ref.at[h],
                         v_ref=vaug_ref.at[h], k0=0, width=(c + 1) * tq, n_rows=tq, tq=tq,
                         q0=c * tq, window=None, pv_rows=MOBA_PV_ROWS, store=store)

    for h in range(n_heads):
        prepare(h)
    _run_attention([tile_pass(h, c) for c in range(S // tq) for h in range(n_heads)])


def _moba(proj, onehot, *, B, S):
    T = B * S
    assert S % MOBA_TQ == 0 and S % KEY_BLOCK == 0 and MOBA_HEADS % MOBA_HEADS_PER_STEP == 0
    n_blk = S // MOBA_BLOCK
    k_top = min(MOBA_TOPK, n_blk - 1)
    nb8 = -(-n_blk // 8) * 8
    hs = MOBA_HEADS_PER_STEP
    kern = functools.partial(_moba_kernel, n_blk=n_blk, k_top=k_top)
    head_spec = lambda blk: pl.BlockSpec((hs, S, LANES), lambda b, h: (blk // hs + h, b, 0))
    return pl.pallas_call(
        kern,
        grid=(B, MOBA_HEADS // hs),
        in_specs=[head_spec(BLK_MQ), head_spec(BLK_MK), head_spec(BLK_MV),
                  pl.BlockSpec((S, LANES), lambda b, h: (0, 0))],
        out_specs=head_spec(0),
        out_shape=jax.ShapeDtypeStruct((MOBA_HEADS, T, LANES), BF16),
        scratch_shapes=[pltpu.VMEM((hs, S, 2 * LANES), BF16), pltpu.VMEM((hs, S, 2 * LANES), BF16),
                        pltpu.VMEM((hs, nb8, HEAD_DIM), F32),
                        pltpu.VMEM((hs, S, 2 * LANES), BF16)],
        compiler_params=pltpu.CompilerParams(
            dimension_semantics=("parallel", "parallel"), vmem_limit_bytes=VMEM_LIMIT),
        name="moba_attn",
    )(proj, proj, proj, onehot)


def _nsa_select_kernel(q_ref, cos_ref, sin_ref, kc_ref, vc_ref, ovt_ref,
                       qrot_ref, bias_ref, oc_ref, *, n_cmp, n_sel_blk, n_top):
    i = pl.program_id(2)
    R, tq, _ = q_ref.shape
    q_raw = q_ref[...]
    cos = (cos_ref[...] * QK_PRESCALE)[None]
    sin = (sin_ref[...] * QK_PRESCALE)[None]
    qrot_ref[...] = _rope(q_raw.astype(F32), cos, sin).astype(BF16)

    n_seg = kc_ref.shape[0]
    s_c = (_nt_dot(q_raw.reshape(R * tq, HEAD_DIM), kc_ref[...]) * SCALE).reshape(R, tq, n_seg)
    n_idx = lax.broadcasted_iota(jnp.int32, s_c.shape, 2)
    pos3 = i * tq + lax.broadcasted_iota(jnp.int32, s_c.shape, 1)
    m_c = (n_idx * CMP_STRIDE + CMP_BLOCK - 1 <= pos3) & (n_idx < n_cmp)
    s_c = jnp.where(m_c, s_c, -jnp.inf)
    mx = jnp.max(s_c, axis=-1, keepdims=True)
    mx = jnp.where(jnp.isfinite(mx), mx, 0.0)
    e_c = jnp.where(m_c, jnp.exp(s_c - mx), 0.0)
    p_c = e_c * (1.0 / jnp.maximum(jnp.sum(e_c, axis=-1, keepdims=True), 1e-30))
    o_c = jnp.dot(p_c.reshape(R * tq, n_seg).astype(BF16), vc_ref[...],
                  preferred_element_type=F32).reshape(R, tq, HEAD_DIM)
    oc_ref[...] = o_c.astype(BF16)

    p_sum = jnp.sum(p_c, axis=0)
    p_hi = p_sum.astype(BF16)
    p_lo = (p_sum - p_hi.astype(F32)).astype(BF16)
    ovt = ovt_ref[...]
    imp_t = _nt_dot(ovt, p_hi) + _nt_dot(ovt, p_lo)
    jrow = lax.broadcasted_iota(jnp.int32, imp_t.shape, 0)
    posq = i * tq + lax.broadcasted_iota(jnp.int32, imp_t.shape, 1)
    own = posq // SEL_BLOCK
    forced = (jrow == 0) | (jrow == own) | (jrow == own - 1)
    future = jrow * SEL_BLOCK > posq
    score_t = jnp.where(future, -jnp.inf, jnp.where(forced, jnp.inf, imp_t))
    keep_t = _select_bias_t(score_t, n_sel_blk, n_top)
    bias_ref[...] = _bias_columns(_bias_t(keep_t))


def _nsa_select(proj, cos, sin, kvc, ovt, *, B, S, tq):
    T = B * S
    assert S % tq == 0
    nq = S // tq
    n_seg = S // CMP_STRIDE
    n_cmp = n_seg - CMP_BLOCK // CMP_STRIDE + 1
    n_sel_blk = S // SEL_BLOCK
    R = NSA_REP
    kern = functools.partial(_nsa_select_kernel, n_cmp=n_cmp, n_sel_blk=n_sel_blk,
                             n_top=min(SEL_TOPK, n_sel_blk))
    heads_spec = pl.BlockSpec((R, tq, LANES), lambda b, g, i: (g, b * nq + i, 0))
    return pl.pallas_call(
        kern,
        grid=(B, NSA_GROUPS, nq),
        in_specs=[pl.BlockSpec((R, tq, LANES), lambda b, g, i: (BLK_NQ // R + g, b * nq + i, 0)),
                  pl.BlockSpec((tq, LANES), lambda b, g, i: (b * nq + i, 0)),
                  pl.BlockSpec((tq, LANES), lambda b, g, i: (b * nq + i, 0)),
                  pl.BlockSpec((None, n_seg, LANES), lambda b, g, i: (0, g * B + b, 0)),
                  pl.BlockSpec((None, n_seg, LANES), lambda b, g, i: (1, g * B + b, 0)),
                  pl.BlockSpec(ovt.shape, lambda b, g, i: (0, 0))],
        out_specs=[heads_spec,
                   pl.BlockSpec((None, tq, LANES), lambda b, g, i: (g, b * nq + i, 0)),
                   heads_spec],
        out_shape=[jax.ShapeDtypeStruct((NSA_HEADS, T, LANES), BF16),
                   jax.ShapeDtypeStruct((NSA_GROUPS, T, LANES), BF16),
                   jax.ShapeDtypeStruct((NSA_HEADS, T, LANES), BF16)],
        compiler_params=pltpu.CompilerParams(
            dimension_semantics=("parallel", "parallel", "parallel"),
            vmem_limit_bytes=VMEM_LIMIT),
        name="nsa_select",
    )(proj, cos, sin, kvc, kvc, ovt)


def _nsa_sel_kernel(qrot_ref, bias_ref, ks_ref, vs_ref, onehot_ref, oc_ref, gate_ref, o_ref,
                    ksaug_ref, vaug_ref, *, tq):
    R, S, _ = qrot_ref.shape
    ksaug_ref[:, :HEAD_DIM] = ks_ref[...]
    ksaug_ref[:, HEAD_DIM:] = onehot_ref[...]
    vaug_ref[:, :HEAD_DIM] = vs_ref[...]
    vaug_ref[:, HEAD_DIM:] = jnp.ones((S, HEAD_DIM), BF16)

    def head_pass(c, heads):
        q_rows = slice(c * tq, (c + 1) * tq)

        def q_aug():
            bias = bias_ref[q_rows, :]
            return jnp.concatenate(
                [jnp.concatenate([qrot_ref[r, q_rows, :], bias], axis=1) for r in heads], axis=0)

        def store(row0, o):
            gt = _sigmoid(gate_ref[q_rows, :])
            for k, r in enumerate(heads):
                part = (gt[:, 3 * r:3 * r + 1] * oc_ref[r, q_rows, :].astype(F32)
                        + gt[:, 3 * r + 1:3 * r + 2] * o[k * tq:(k + 1) * tq, :])
                o_ref[r, q_rows, :] = part.astype(BF16)

        return _AttnPass(q=q_aug, k_ref=ksaug_ref, v_ref=vaug_ref, k0=0, width=(c + 1) * tq,
                         n_rows=len(heads) * tq, tq=tq, q0=c * tq, window=None,
                         pv_rows=len(heads) * tq, store=store)

    _run_attention([head_pass(c, range(r0, r0 + HEADS_PER_PASS))
                    for c in range(S // tq) for r0 in range(0, R, HEADS_PER_PASS)])


def _nsa_sel(q_rot, bias, proj, onehot, o_c, gates, *, B, S, tq):
    T = B * S
    assert S % tq == 0 and tq % KEY_BLOCK == 0
    R = NSA_REP
    heads_spec = pl.BlockSpec((R, S, LANES), lambda b, g: (g, b, 0))
    group_spec = pl.BlockSpec((None, S, LANES), lambda b, g: (g, b, 0))
    kv_spec = lambda blk: pl.BlockSpec((None, S, LANES), lambda b, g: (blk + g, b, 0))
    return pl.pallas_call(
        functools.partial(_nsa_sel_kernel, tq=tq),
        grid=(B, NSA_GROUPS),
        in_specs=[heads_spec, group_spec,
                  kv_spec(BLK_NKS), kv_spec(BLK_NVS),
                  pl.BlockSpec((S, LANES), lambda b, g: (0, 0)),
                  heads_spec, group_spec],
        out_specs=heads_spec,
        out_shape=jax.ShapeDtypeStruct((NSA_HEADS, T, LANES), BF16),
        scratch_shapes=[pltpu.VMEM((S, 2 * LANES), BF16), pltpu.VMEM((S, 2 * LANES), BF16)],
        compiler_params=pltpu.CompilerParams(
            dimension_semantics=("parallel", "parallel"), vmem_limit_bytes=VMEM_LIMIT_BIG),
        name="nsa_sel_attn",
    )(q_rot, bias, proj, proj, onehot, o_c, gates)


def _nsa_win_kernel(qrot_ref, kw_ref, vw_ref, part_ref, gate_ref, o_ref, vaug_ref, *, tq):
    R, S, _ = qrot_ref.shape
    vaug_ref[:, :HEAD_DIM] = vw_ref[...]
    vaug_ref[:, HEAD_DIM:] = jnp.ones((S, HEAD_DIM), BF16)

    def head_pass(c, heads):
        q_rows = slice(c * tq, (c + 1) * tq)
        k0 = max(0, c * tq - WINDOW)

        def store(row0, o_w):
            gt = _sigmoid(gate_ref[q_rows, :])
            for k, r in enumerate(heads):
                o = (part_ref[r, q_rows, :].astype(F32)
                     + gt[:, 3 * r + 2:3 * r + 3] * o_w[k * tq:(k + 1) * tq, :])
                o_ref[r, q_rows, :] = o.astype(BF16)

        return _AttnPass(
            q=lambda: jnp.concatenate([qrot_ref[r, q_rows, :] for r in heads], axis=0),
            k_ref=kw_ref, v_ref=vaug_ref, k0=k0, width=(c + 1) * tq - k0,
            n_rows=len(heads) * tq, tq=tq, q0=c * tq, window=WINDOW,
            pv_rows=len(heads) * tq, store=store)

    _run_attention([head_pass(c, range(r0, r0 + HEADS_PER_PASS))
                    for c in range(S // tq) for r0 in range(0, R, HEADS_PER_PASS)])


def _nsa_win(q_rot, proj, part, gates, *, B, S, tq):
    T = B * S
    assert S % tq == 0 and tq % KEY_BLOCK == 0 and WINDOW % KEY_BLOCK == 0
    R = NSA_REP
    heads_spec = pl.BlockSpec((R, S, LANES), lambda b, g: (g, b, 0))
    kv_spec = lambda blk: pl.BlockSpec((None, S, LANES), lambda b, g: (blk + g, b, 0))
    return pl.pallas_call(
        functools.partial(_nsa_win_kernel, tq=tq),
        grid=(B, NSA_GROUPS),
        in_specs=[heads_spec, kv_spec(BLK_NKW), kv_spec(BLK_NVW), heads_spec,
                  pl.BlockSpec((None, S, LANES), lambda b, g: (g, b, 0))],
        out_specs=heads_spec,
        out_shape=jax.ShapeDtypeStruct((NSA_HEADS, T, LANES), BF16),
        scratch_shapes=[pltpu.VMEM((S, 2 * LANES), BF16)],
        compiler_params=pltpu.CompilerParams(
            dimension_semantics=("parallel", "parallel"), vmem_limit_bytes=VMEM_LIMIT_BIG),
        name="nsa_win_attn",
    )(q_rot, proj, proj, part, gates)


def _out_proj_kernel(om_ref, on_ref, zm_ref, zn_ref, gm_ref, gn_ref, x_ref, w_ref, gf_ref,
                     out_ref, w_scr):
    @pl.when(pl.program_id(0) == 0)
    def _():
        w_scr[...] = w_ref[...].astype(BF16)

    def gated_norm(o_ref, z_ref, g_ref, rows):
        n_heads = o_ref.shape[0]
        acts = []
        ss = None
        for h in range(n_heads):
            z = z_ref[h, rows, :].astype(F32)
            a = o_ref[h, rows, :].astype(F32) * (z * _sigmoid(z))
            acts.append(a)
            ss = a * a if ss is None else ss + a * a
        ms = jnp.sum(ss, axis=-1, keepdims=True) * (1.0 / (n_heads * LANES))
        inv = lax.rsqrt(ms + EPS)
        return [(acts[h] * inv * g_ref[h]).astype(BF16) for h in range(n_heads)]

    for r0 in range(0, x_ref.shape[0], OUT_SUB_ROWS):
        rows = slice(r0, r0 + OUT_SUB_ROWS)
        y = jnp.concatenate(gated_norm(om_ref, zm_ref, gm_ref, rows)
                            + gated_norm(on_ref, zn_ref, gn_ref, rows), axis=1)
        r = x_ref[rows, :] + jnp.dot(y, w_scr[...], preferred_element_type=F32)
        ms = jnp.mean(r * r, axis=-1, keepdims=True)
        out_ref[rows, :] = r * lax.rsqrt(ms + EPS) * gf_ref[...]


def _out_proj(o_moba, o_nsa, proj, g_moba, g_nsa, x2d, w_out, g_final, *, tm):
    T = x2d.shape[0]
    assert T % tm == 0 and tm % OUT_SUB_ROWS == 0
    H = MOBA_HEADS
    return pl.pallas_call(
        _out_proj_kernel,
        grid=(T // tm,),
        in_specs=[pl.BlockSpec((H, tm, LANES), lambda i: (0, i, 0)),
                  pl.BlockSpec((H, tm, LANES), lambda i: (0, i, 0)),
                  pl.BlockSpec((H, tm, LANES), lambda i: (BLK_MZ // H, i, 0)),
                  pl.BlockSpec((H, tm, LANES), lambda i: (BLK_NZ // H, i, 0)),
                  pl.BlockSpec((H, 1, LANES), lambda i: (0, 0, 0)),
                  pl.BlockSpec((H, 1, LANES), lambda i: (0, 0, 0)),
                  pl.BlockSpec((tm, D_MODEL), lambda i: (i, 0)),
                  pl.BlockSpec((None, D_MODEL, D_MODEL), lambda i: (0, 0, 0),
                               pipeline_mode=pl.Buffered(1)),
                  pl.BlockSpec((1, D_MODEL), lambda i: (0, 0))],
        out_specs=pl.BlockSpec((tm, D_MODEL), lambda i: (i, 0)),
        out_shape=jax.ShapeDtypeStruct((T, D_MODEL), F32),
        scratch_shapes=[pltpu.VMEM((D_MODEL, D_MODEL), BF16)],
        compiler_params=pltpu.CompilerParams(
            dimension_semantics=("arbitrary",), vmem_limit_bytes=VMEM_LIMIT_BIG),
        name="out_proj",
    )(o_moba, o_nsa, proj, proj, g_moba, g_nsa, x2d, w_out, g_final)


def _w_in_offsets():
    mw, nw, kw = MOBA_HEADS * HEAD_DIM, NSA_HEADS * HEAD_DIM, NSA_GROUPS * HEAD_DIM
    sizes = [mw] * 4 + [nw] + [kw] * 6 + [3 * NSA_HEADS, nw]
    names = ["mq", "mk", "mv", "mz", "nq", "nkc", "nvc", "nks", "nvs", "nkw", "nvw", "ng", "nz"]
    offs = np.concatenate([[0], np.cumsum(sizes)])
    return {n: (int(offs[k]), int(sizes[k])) for k, n in enumerate(names)}


def _weight_tiles_kernel(w_ref, out_ref, *, src_cols, blocks_per_tile):
    for b, src in enumerate(src_cols):
        lane0 = (b % blocks_per_tile) * LANES
        out_ref[b // blocks_per_tile, :, lane0:lane0 + LANES] = (
            w_ref[:, src:src + LANES].astype(BF16))


def _weight_tiles(w_in, blocks_per_tile):
    offsets = _w_in_offsets()
    src_cols = []
    for name, n_blocks in COLUMN_ORDER:
        off, size = offsets[name]
        assert size == n_blocks * LANES
        src_cols += [off + k * LANES for k in range(n_blocks)]
    n_tiles = N_BLOCKS // blocks_per_tile
    rows = WEIGHT_PREP_ROWS
    kern = functools.partial(_weight_tiles_kernel, src_cols=tuple(src_cols),
                             blocks_per_tile=blocks_per_tile)
    return pl.pallas_call(
        kern,
        grid=(D_MODEL // rows,),
        in_specs=[pl.BlockSpec((None, rows, w_in.shape[2]), lambda r: (0, r, 0))],
        out_specs=pl.BlockSpec((n_tiles, rows, blocks_per_tile * LANES), lambda r: (0, r, 0)),
        out_shape=jax.ShapeDtypeStruct((n_tiles, D_MODEL, blocks_per_tile * LANES), BF16),
        compiler_params=pltpu.CompilerParams(
            dimension_semantics=("parallel",), vmem_limit_bytes=VMEM_LIMIT),
        name="weight_tiles",
    )(w_in)


def _gate_weight(w_in):
    off, size = _w_in_offsets()["ng"]
    per_group = 3 * NSA_REP
    wg = w_in[0, :, off:off + size].reshape(D_MODEL, NSA_GROUPS, per_group)
    wg = jnp.pad(wg, ((0, 0), (0, 0), (0, LANES - per_group)))
    return wg.reshape(D_MODEL, NSA_GROUPS * LANES).astype(BF16)


def _block_onehot(S, block):
    ids = np.arange(S)[:, None] // block
    return jnp.asarray((ids == np.arange(LANES)[None, :]).astype(np.float32), dtype=BF16)


def _overlap_t(n_seg, n_cmp, n_sel_blk):
    cs = np.arange(n_seg)[None, :] * CMP_STRIDE
    ss = np.arange(n_sel_blk)[:, None] * SEL_BLOCK
    ov = (cs < ss + SEL_BLOCK) & (cs + CMP_BLOCK > ss) & (np.arange(n_seg)[None, :] < n_cmp)
    return jnp.asarray(ov.astype(np.float32), dtype=BF16)


def _layer(x, positions, w_in, g_norm, pe_ck, pe_cv, w_ck1, w_ck2, w_cv1, w_cv2,
           g_out_moba, g_out_nsa, w_out, g_final, *, nsa_tq, tm_in, tm_out, blocks_per_tile):
    B, S, _ = x.shape
    T = B * S
    x2d = x.reshape(T, D_MODEL)
    w_tiles = _weight_tiles(w_in.astype(BF16), blocks_per_tile)
    w_gate = _gate_weight(w_in)
    proj, gates, cos, sin, seg = _in_proj(x2d, g_norm.reshape(1, D_MODEL), w_tiles, w_gate,
                                          positions, tm=tm_in, blocks_per_tile=blocks_per_tile)

    n_seg = S // CMP_STRIDE
    seg = seg.reshape(2, NSA_GROUPS * B * n_seg, CMP_STRIDE * HEAD_DIM)
    pe = jnp.stack([pe_ck.reshape(1, -1), pe_cv.reshape(1, -1)])
    w1 = jnp.stack([w_ck1, w_cv1]).astype(BF16)
    w2 = jnp.stack([w_ck2, w_cv2]).astype(BF16)
    kvc = _compress(seg, pe, w1, w2)

    o_moba = _moba(proj, _block_onehot(S, MOBA_BLOCK), B=B, S=S)
    n_cmp = n_seg - CMP_BLOCK // CMP_STRIDE + 1
    q_rot, bias, o_c = _nsa_select(proj, cos, sin, kvc, _overlap_t(n_seg, n_cmp, S // SEL_BLOCK),
                                   B=B, S=S, tq=4 * nsa_tq)
    part = _nsa_sel(q_rot, bias, proj, _block_onehot(S, SEL_BLOCK), o_c, gates,
                    B=B, S=S, tq=nsa_tq)
    o_nsa = _nsa_win(q_rot, proj, part, gates, B=B, S=S, tq=nsa_tq)
    out = _out_proj(o_moba, o_nsa, proj,
                    g_out_moba.reshape(MOBA_HEADS, 1, LANES), g_out_nsa.reshape(NSA_HEADS, 1, LANES),
                    x2d, w_out, g_final.reshape(1, D_MODEL), tm=tm_out)
    return out.reshape(B, S, D_MODEL)


def kernel(x, positions, w_in, g_norm, pe_ck, pe_cv, w_ck1, w_ck2, w_cv1, w_cv2,
           g_out_moba, g_out_nsa, w_out, g_final):
    assert w_in.shape[0] == 1, "single-layer problem"
    return _layer(x, positions, w_in, g_norm[0], pe_ck[0], pe_cv[0], w_ck1[0], w_ck2[0],
                  w_cv1[0], w_cv2[0], g_out_moba[0], g_out_nsa[0], w_out, g_final,
                  nsa_tq=256, tm_in=1024, tm_out=512, blocks_per_tile=10)
```
